```python
import math
import jax, jax.numpy as jnp
from jax import lax
import numpy as np

D_MODEL = 2048
BATCH = 2
SEQ = 4096
DEPTH = 1

CHUNK = 64
Q_BLOCK = 128
MIX_WIDTH = D_MODEL
MLA_HEADS = 8
MLA_Q_RANK = 512
MLA_KV_RANK = 256
MLA_NOPE_DIM = 128
MLA_ROPE_DIM = 64
MLA_V_DIM = 128
FOX_HEADS = 8
FOX_HEAD_DIM = 128
D_FF = 5632
MACARON_WEIGHT = 0.5
ROPE_THETA = 10000.0
EPS = 1e-6
N_MOD = 9
FOX_WIDTH = FOX_HEADS * FOX_HEAD_DIM
IN_SIZES = (MLA_Q_RANK, MLA_KV_RANK, MLA_ROPE_DIM, FOX_WIDTH, FOX_WIDTH, FOX_WIDTH, FOX_HEADS)
IN_WIDTH = sum(IN_SIZES)
IN_OFFSETS = tuple(int(v) for v in np.cumsum(IN_SIZES)[:-1])

kernel_name = "hybrid_mla_fox_macaron_adaln_block"


def rmsnorm(x, g):
    xf = x.astype(jnp.float32)
    xf = xf * lax.rsqrt(jnp.mean(xf * xf, axis=-1, keepdims=True) + EPS)
    return xf.astype(x.dtype) * g


def modulate(h, shift, scale):
    return h * (1 + scale[:, None, :]) + shift[:, None, :]


def swiglu(h, w_gate, w_up, w_down):
    return (jax.nn.silu(h @ w_gate) * (h @ w_up)) @ w_down


def rotary(x, pos):
    half = x.shape[-1] // 2
    inv = ROPE_THETA ** (-jnp.arange(half, dtype=jnp.float32) / half)
    ang = pos.astype(jnp.float32)[:, None] * inv[None, :]
    cos = jnp.cos(ang)[None, :, None, :].astype(x.dtype)
    sin = jnp.sin(ang)[None, :, None, :].astype(x.dtype)
    x1, x2 = x[..., :half], x[..., half:]
    return jnp.concatenate([x1 * cos - x2 * sin, x1 * sin + x2 * cos], axis=-1)


def blocked_attention(q, k, v, scale, logit_fn):
    S = q.shape[1]
    outs = []
    for i in range(S // Q_BLOCK):
        q0 = i * Q_BLOCK
        kv_len = q0 + Q_BLOCK
        s = jnp.einsum('bqhd,bkhd->bhqk', q[:, q0:kv_len], k[:, :kv_len]).astype(jnp.float32) * scale
        p = jax.nn.softmax(logit_fn(s, q0, kv_len), axis=-1)
        outs.append(jnp.einsum('bhqk,bkhd->bqhd', p.astype(v.dtype), v[:, :kv_len]))
    return jnp.concatenate(outs, axis=1)


def chunk_causal_logits(s, q0, kv_len):
    t = q0 + jnp.arange(Q_BLOCK)
    src = jnp.arange(kv_len)
    mask = (src[None, :] // CHUNK) <= (t[:, None] // CHUNK)
    return jnp.where(mask[None, None], s, -jnp.inf)


def hybrid_mixer(h, pos, w_in, b_forget, g_q_a, w_uq, g_kv_a, w_ukv, w_o):
    B, S, _ = h.shape
    proj = h @ w_in
    q_a, kv_a, k_pe, fq, fk, fv, f_logit = jnp.split(proj, IN_OFFSETS, axis=-1)

    q = (rmsnorm(q_a, g_q_a) @ w_uq).reshape(B, S, MLA_HEADS, MLA_NOPE_DIM + MLA_ROPE_DIM)
    kv = (rmsnorm(kv_a, g_kv_a) @ w_ukv).reshape(B, S, MLA_HEADS, MLA_NOPE_DIM + MLA_V_DIM)
    k_nope, v_mla = kv[..., :MLA_NOPE_DIM], kv[..., MLA_NOPE_DIM:]
    q_mla = jnp.concatenate([q[..., :MLA_NOPE_DIM], rotary(q[..., MLA_NOPE_DIM:], pos)], axis=-1)
    k_rope = rotary(k_pe[:, :, None, :], pos)
    k_mla = jnp.concatenate(
        [k_nope, jnp.broadcast_to(k_rope, (B, S, MLA_HEADS, MLA_ROPE_DIM))], axis=-1)
    o_mla = blocked_attention(q_mla, k_mla, v_mla,
                              1.0 / math.sqrt(MLA_NOPE_DIM + MLA_ROPE_DIM), chunk_causal_logits)

    log_f = jax.nn.log_sigmoid((f_logit + b_forget).astype(jnp.float32))
    cum = jnp.cumsum(log_f, axis=1).transpose(0, 2, 1)

    def fox_logits(s, q0, kv_len):
        bias = cum[:, :, q0:kv_len, None] - cum[:, :, None, :kv_len]
        t = q0 + jnp.arange(Q_BLOCK)
        mask = jnp.arange(kv_len)[None, :] <= t[:, None]
        return jnp.where(mask[None, None], s + bias, -jnp.inf)

    o_fox = blocked_attention(fq.reshape(B, S, FOX_HEADS, FOX_HEAD_DIM),
                              fk.reshape(B, S, FOX_HEADS, FOX_HEAD_DIM),
                              fv.reshape(B, S, FOX_HEADS, FOX_HEAD_DIM),
                              1.0 / math.sqrt(FOX_HEAD_DIM), fox_logits)

    o = jnp.concatenate([o_mla.reshape(B, S, MLA_HEADS * MLA_V_DIM),
                         o_fox.reshape(B, S, FOX_WIDTH)], axis=-1)
    return o @ w_o


def setup_inputs(seed: int = 0) -> dict:
    key = jax.random.key(seed)
    ks = jax.random.split(key, 32)
    f32 = jnp.float32
    L, D = DEPTH, D_MODEL

    def w(k, shape, fan_in, mult=1.0):
        return jax.random.normal(k, shape, f32) * (mult * fan_in ** -0.5)

    def gain(k, dim):
        return 1.0 + 0.05 * jax.random.normal(k, (L, dim), f32)

    return {
        "x": jax.random.normal(ks[0], (BATCH, SEQ, D), f32),
        "c": jax.random.normal(ks[1], (BATCH, D), f32),
        "w_ada": w(ks[2], (L, D, N_MOD * D), D, 0.5),
        "b_ada": 0.02 * jax.random.normal(ks[3], (L, N_MOD * D), f32),
        "g_ffn1_pre": gain(ks[4], D),
        "g_ffn1_post": gain(ks[5], D),
        "w1_gate": w(ks[6], (L, D, D_FF), D),
        "w1_up": w(ks[7], (L, D, D_FF), D),
        "w1_down": w(ks[8], (L, D_FF, D), D_FF),
        "g_mix_pre": gain(ks[9], D),
        "g_mix_post": gain(ks[10], D),
        "w_in": w(ks[11], (L, D, IN_WIDTH), D),
        "b_forget": jax.random.uniform(ks[12], (L, FOX_HEADS), f32, 1.0, 4.0),
        "g_q_a": gain(ks[13], MLA_Q_RANK),
        "w_uq": w(ks[14], (L, MLA_Q_RANK, MLA_HEADS * (MLA_NOPE_DIM + MLA_ROPE_DIM)), MLA_Q_RANK),
        "g_kv_a": gain(ks[15], MLA_KV_RANK),
        "w_ukv": w(ks[16], (L, MLA_KV_RANK, MLA_HEADS * (MLA_NOPE_DIM + MLA_V_DIM)), MLA_KV_RANK),
        "w_o": w(ks[17], (L, MIX_WIDTH, D), MIX_WIDTH),
        "g_ffn2_pre": gain(ks[18], D),
        "g_ffn2_post": gain(ks[19], D),
        "w2_gate": w(ks[20], (L, D, D_FF), D),
        "w2_up": w(ks[21], (L, D, D_FF), D),
        "w2_down": w(ks[22], (L, D_FF, D), D_FF),
    }


def reference(x, c, w_ada, b_ada, g_ffn1_pre, g_ffn1_post, w1_gate, w1_up, w1_down,
              g_mix_pre, g_mix_post, w_in, b_forget, g_q_a, w_uq, g_kv_a, w_ukv, w_o,
              g_ffn2_pre, g_ffn2_post, w2_gate, w2_up, w2_down):
    pos = jnp.arange(x.shape[1])
    cond = jax.nn.silu(c)
    for l in range(DEPTH):
        mod = cond @ w_ada[l] + b_ada[l]
        sh1, sc1, gt1, sh2, sc2, gt2, sh3, sc3, gt3 = jnp.split(mod, N_MOD, axis=-1)
        h = modulate(rmsnorm(x, g_ffn1_pre[l]), sh1, sc1)
        y = rmsnorm(swiglu(h, w1_gate[l], w1_up[l], w1_down[l]), g_ffn1_post[l])
        x = x + MACARON_WEIGHT * gt1[:, None, :] * y
        h = modulate(rmsnorm(x, g_mix_pre[l]), sh2, sc2)
        y = hybrid_mixer(h, pos, w_in[l], b_forget[l], g_q_a[l], w_uq[l], g_kv_a[l], w_ukv[l], w_o[l])
        x = x + gt2[:, None, :] * rmsnorm(y, g_mix_post[l])
        h = modulate(rmsnorm(x, g_ffn2_pre[l]), sh3, sc3)
        y = rmsnorm(swiglu(h, w2_gate[l], w2_up[l], w2_down[l]), g_ffn2_post[l])
        x = x + MACARON_WEIGHT * gt3[:, None, :] * y
    return x
```

```python
import functools
import math

import jax
import jax.numpy as jnp
from jax import lax
from jax.experimental import pallas as pl
from jax.experimental.pallas import tpu as pltpu

F32 = jnp.float32
BF16 = jnp.bfloat16

V7X_VMEM_BYTES = 64 * 1024 * 1024
LANES = 128

EPS = 1e-6
ROPE_THETA = 10000.0
CHUNK = 64
MLA_HEADS = 8
MLA_Q_RANK = 512
MLA_KV_RANK = 256
MLA_NOPE = 128
MLA_ROPE = 64
MLA_V = 128
MLA_QK_PAD = 256
FOX_HEADS = 8
FOX_DIM = 128
FOX_WIDTH = FOX_HEADS * FOX_DIM
LATENT_WIDTH = 1024
KPE_COL = MLA_Q_RANK + MLA_KV_RANK
FLOGIT_COL = KPE_COL + LANES
NEG_BIG = -1e30


def _params(vmem_bytes):
    limit = min(int(vmem_bytes * 1.25) + (4 << 20), V7X_VMEM_BYTES - (4 << 20))
    return pltpu.CompilerParams(vmem_limit_bytes=limit)


def _rms(x, g):
    return x * lax.rsqrt(jnp.mean(x * x, axis=-1, keepdims=True) + EPS) * g


def _ada_kernel(c_ref, w_ref, b_ref, o_ref):
    c = c_ref[...]
    cond = (c * jax.nn.sigmoid(c)).astype(BF16)
    o_ref[...] = jnp.dot(cond, w_ref[...].astype(BF16), preferred_element_type=F32) + b_ref[...]


def _ada(c_pad, w, b, *, tn):
    m, d = c_pad.shape
    n = w.shape[1]
    vmem = 2 * d * tn * 4 + d * tn * 2 + 4 * m * tn * 4
    return pl.pallas_call(
        _ada_kernel,
        grid=(n // tn,),
        in_specs=[pl.BlockSpec((m, d), lambda j: (0, 0)),
                  pl.BlockSpec((d, tn), lambda j: (0, j)),
                  pl.BlockSpec((1, tn), lambda j: (0, j))],
        out_specs=pl.BlockSpec((m, tn), lambda j: (0, j)),
        out_shape=jax.ShapeDtypeStruct((m, n), F32),
        compiler_params=_params(vmem),
        name="ada",
    )(c_pad, w, b)


def _ffn_kernel(x_ref, sh_ref, sc_ref, gt_ref, gpre_ref, gpost_ref, wg_ref, wu_ref, wd_ref,
                o_ref, h_ref, acc_ref, *, res_weight):
    f = pl.program_id(1)

    @pl.when(f == 0)
    def _():
        h = _rms(x_ref[...], gpre_ref[...]) * (1.0 + sc_ref[0]) + sh_ref[0]
        h_ref[...] = h.astype(BF16)
        acc_ref[...] = jnp.zeros_like(acc_ref)

    h = h_ref[...]
    g = jnp.dot(h, wg_ref[...], preferred_element_type=F32)
    u = jnp.dot(h, wu_ref[...], preferred_element_type=F32)
    a = (g * jax.nn.sigmoid(g) * u).astype(BF16)
    acc_ref[...] += jnp.dot(a, wd_ref[...], preferred_element_type=F32)

    @pl.when(f == pl.num_programs(1) - 1)
    def _():
        y = _rms(acc_ref[...], gpost_ref[...])
        o_ref[...] = x_ref[...] + res_weight * gt_ref[0] * y


def _ffn(x, sh, sc, gt, g_pre, g_post, wg, wu, wd, *, seq, tm, tf, res_weight):
    t, d = x.shape
    ff = wg.shape[1]
    tpb = seq // tm
    vmem = (4 * tm * d * 4 + tm * d * 2 + tm * d * 4 + 6 * d * tf * 2 + 3 * tm * tf * 4)
    mod_spec = pl.BlockSpec((1, 1, d), lambda i, f: (i // tpb, 0, 0))
    gain_spec = pl.BlockSpec((1, d), lambda i, f: (0, 0))
    return pl.pallas_call(
        functools.partial(_ffn_kernel, res_weight=res_weight),
        grid=(t // tm, ff // tf),
        in_specs=[pl.BlockSpec((tm, d), lambda i, f: (i, 0)),
                  mod_spec, mod_spec, mod_spec, gain_spec, gain_spec,
                  pl.BlockSpec((d, tf), lambda i, f: (0, f)),
                  pl.BlockSpec((d, tf), lambda i, f: (0, f)),
                  pl.BlockSpec((tf, d), lambda i, f: (f, 0))],
        out_specs=pl.BlockSpec((tm, d), lambda i, f: (i, 0)),
        out_shape=jax.ShapeDtypeStruct((t, d), F32),
        scratch_shapes=[pltpu.VMEM((tm, d), BF16), pltpu.VMEM((tm, d), F32)],
        compiler_params=_params(vmem),
        name="ffn",
    )(x, sh, sc, gt, g_pre, g_post, wg, wu, wd)


def _inproj_kernel(x_ref, sh_ref, sc_ref, gpre_ref, w_ref, lat_ref, fox_ref, h_ref):
    j = pl.program_id(1)

    @pl.when(j == 0)
    def _():
        h = _rms(x_ref[...], gpre_ref[...]) * (1.0 + sc_ref[0]) + sh_ref[0]
        h_ref[...] = h.astype(BF16)

    p = jnp.dot(h_ref[...], w_ref[...], preferred_element_type=F32)

    @pl.when(j == 0)
    def _():
        lat_ref[...] = p

    @pl.when(j > 0)
    def _():
        fox_ref[...] = p.astype(BF16)


def _inproj(x, sh, sc, g_pre, w, *, seq, tm, tn):
    t, d = x.shape
    n = w.shape[1]
    assert tn == LATENT_WIDTH
    tpb = seq // tm
    vmem = 2 * tm * d * 4 + tm * d * 2 + 2 * d * tn * 2 + 2 * tm * tn * 4 + 2 * tm * tn * 2 + tm * tn * 4
    mod_spec = pl.BlockSpec((1, 1, d), lambda i, j: (i // tpb, 0, 0))
    return pl.pallas_call(
        _inproj_kernel,
        grid=(t // tm, n // tn),
        in_specs=[pl.BlockSpec((tm, d), lambda i, j: (i, 0)),
                  mod_spec, mod_spec,
                  pl.BlockSpec((1, d), lambda i, j: (0, 0)),
                  pl.BlockSpec((d, tn), lambda i, j: (0, j))],
        out_specs=[pl.BlockSpec((tm, tn), lambda i, j: (i, 0)),
                   pl.BlockSpec((tm, tn), lambda i, j: (i, jnp.maximum(j - 1, 0)))],
        out_shape=[jax.ShapeDtypeStruct((t, LATENT_WIDTH), F32),
                   jax.ShapeDtypeStruct((t, n - LATENT_WIDTH), BF16)],
        scratch_shapes=[pltpu.VMEM((tm, d), BF16)],
        compiler_params=_params(vmem),
        name="in_proj",
    )(x, sh, sc, g_pre, w)


def _rot(r, cos_t, sin_lo, sin_hi):
    return r * cos_t + pltpu.roll(r, 96, 1) * sin_lo + pltpu.roll(r, 32, 1) * sin_hi


def _prep_kernel(lat_ref, gq_ref, gkv_ref, wuq_ref, wukv_ref, bf_ref, cos_ref, slo_ref, shi_ref,
                 q_ref, k_ref, v_ref, cum_ref, cumt_ref, carry_ref, *, tiles_per_batch):
    i = pl.program_id(0)
    tm = lat_ref.shape[0]
    cos_t, sin_lo, sin_hi = cos_ref[...], slo_ref[...], shi_ref[...]

    qn = _rms(lat_ref[:, :MLA_Q_RANK], gq_ref[...]).astype(BF16)
    q = jnp.dot(qn, wuq_ref[...], preferred_element_type=F32)
    kvn = _rms(lat_ref[:, MLA_Q_RANK:KPE_COL], gkv_ref[...]).astype(BF16)
    kv = jnp.dot(kvn, wukv_ref[...], preferred_element_type=F32)
    k_rope = _rot(lat_ref[:, KPE_COL:KPE_COL + LANES], cos_t, sin_lo, sin_hi).astype(BF16)
    for h in range(MLA_HEADS):
        c0 = h * MLA_QK_PAD
        q_ref[:, c0:c0 + MLA_NOPE] = q[:, c0:c0 + MLA_NOPE].astype(BF16)
        q_ref[:, c0 + MLA_NOPE:c0 + MLA_QK_PAD] = _rot(
            q[:, c0 + MLA_NOPE:c0 + MLA_QK_PAD], cos_t, sin_lo, sin_hi).astype(BF16)
        k_ref[:, c0:c0 + MLA_NOPE] = kv[:, h * MLA_NOPE:(h + 1) * MLA_NOPE].astype(BF16)
        k_ref[:, c0 + MLA_NOPE:c0 + MLA_QK_PAD] = k_rope
    v_ref[...] = kv[:, MLA_HEADS * MLA_NOPE:].astype(BF16)

    @pl.when(i % tiles_per_batch == 0)
    def _():
        carry_ref[...] = jnp.zeros_like(carry_ref)

    z = lat_ref[:, FLOGIT_COL:FLOGIT_COL + LANES] + bf_ref[...]
    log_f = jnp.minimum(z, 0.0) - jnp.log1p(jnp.exp(-jnp.abs(z)))
    row = lax.broadcasted_iota(jnp.int32, (tm, tm), 0)
    col = lax.broadcasted_iota(jnp.int32, (tm, tm), 1)
    tri = (col <= row).astype(F32)
    cum = jnp.dot(tri, log_f, precision=lax.Precision.HIGHEST,
                  preferred_element_type=F32) + carry_ref[...]
    cum_ref[...] = cum
    cumt_ref[0] = cum.T[:FOX_HEADS, :]
    carry_ref[...] = cum[tm - 1:tm, :]


def _prep(lat, g_q, g_kv, wuq, wukv, b_forget, cos_t, sin_lo, sin_hi, *, batch, seq, tm):
    t = lat.shape[0]
    tpb = seq // tm
    qk_w = MLA_HEADS * MLA_QK_PAD
    v_w = MLA_HEADS * MLA_V
    vmem = (2 * tm * LATENT_WIDTH * 4 + 2 * (wuq.size + wukv.size) * 2 + 6 * tm * LANES * 4
            + 2 * (2 * tm * qk_w + tm * v_w) * 2 + 4 * tm * LANES * 4
            + tm * (qk_w + 2 * v_w) * 4 + 3 * tm * tm * 4)
    full = lambda a: pl.BlockSpec(a.shape, lambda i: (0, 0))
    tab_spec = pl.BlockSpec((tm, LANES), lambda i: (i % tpb, 0))
    return pl.pallas_call(
        functools.partial(_prep_kernel, tiles_per_batch=tpb),
        grid=(t // tm,),
        in_specs=[pl.BlockSpec((tm, LATENT_WIDTH), lambda i: (i, 0)),
                  full(g_q), full(g_kv), full(wuq), full(wukv), full(b_forget),
                  tab_spec, tab_spec, tab_spec],
        out_specs=[pl.BlockSpec((tm, qk_w), lambda i: (i, 0)),
                   pl.BlockSpec((tm, qk_w), lambda i: (i, 0)),
                   pl.BlockSpec((tm, v_w), lambda i: (i, 0)),
                   pl.BlockSpec((tm, LANES), lambda i: (i, 0)),
                   pl.BlockSpec((1, FOX_HEADS, tm), lambda i: (i // tpb, 0, i % tpb))],
        out_shape=[jax.ShapeDtypeStruct((t, qk_w), BF16),
                   jax.ShapeDtypeStruct((t, qk_w), BF16),
                   jax.ShapeDtypeStruct((t, v_w), BF16),
                   jax.ShapeDtypeStruct((t, LANES), F32),
                   jax.ShapeDtypeStruct((batch, FOX_HEADS, seq), F32)],
        scratch_shapes=[pltpu.VMEM((1, LANES), F32)],
        compiler_params=_params(vmem),
        name="mla_fox_prep",
    )(lat, g_q, g_kv, wuq, wukv, b_forget, cos_t, sin_lo, sin_hi)


def _attn_kernel(*refs, scale, mask_chunk, fox):
    if fox:
        q_ref, k_ref, v_ref, cumq_ref, cumk_ref, o_ref, m_ref, l_ref, acc_ref = refs
    else:
        q_ref, k_ref, v_ref, o_ref, m_ref, l_ref, acc_ref = refs
    head = pl.program_id(1)
    i = pl.program_id(2)
    tq = q_ref.shape[0]
    q = q_ref[...]
    m_ref[...] = jnp.full_like(m_ref, NEG_BIG)
    l_ref[...] = jnp.zeros_like(l_ref)
    acc_ref[...] = jnp.zeros_like(acc_ref)
    if fox:
        lane = lax.broadcasted_iota(jnp.int32, cumq_ref.shape, 1)
        cum_q = jnp.sum(jnp.where(lane == head, cumq_ref[...], 0.0), axis=-1, keepdims=True)

    def step(j, diagonal):
        start = pl.multiple_of(j * tq, tq)
        k = k_ref[pl.ds(start, tq), :]
        v = v_ref[pl.ds(start, tq), :]
        s = lax.dot_general(q, k, (((1,), (1,)), ((), ())), preferred_element_type=F32) * scale
        if fox:
            s = s + (cum_q - cumk_ref[0, :, pl.ds(start, tq)])
        if diagonal:
            row = lax.broadcasted_iota(jnp.int32, (tq, tq), 0) // mask_chunk
            col = lax.broadcasted_iota(jnp.int32, (tq, tq), 1) // mask_chunk
            s = jnp.where(col <= row, s, NEG_BIG)
        m_prev = m_ref[...]
        m_new = jnp.maximum(m_prev, jnp.max(s, axis=-1, keepdims=True))
        alpha = jnp.exp(m_prev - m_new)
        p = jnp.exp(s - m_new)
        l_ref[...] = alpha * l_ref[...] + jnp.sum(p, axis=-1, keepdims=True)
        acc_ref[...] = alpha * acc_ref[...] + jnp.dot(p.astype(BF16), v, preferred_element_type=F32)
        m_ref[...] = m_new

    def body(j, carry):
        step(j, False)
        return carry

    lax.fori_loop(0, i, body, 0)
    step(i, True)
    o_ref[...] = (acc_ref[...] / l_ref[...]).astype(o_ref.dtype)


def _attention(q_arr, k_arr, v_arr, cum=None, cum_t=None, *, batch, seq, heads, dk, dv,
               q_col, k_col, v_col, tq, scale, mask_chunk):
    fox = cum is not None
    nq = seq // tq
    in_specs = [pl.BlockSpec((tq, dk), lambda b, h, i: (b * nq + i, q_col + h)),
                pl.BlockSpec((seq, dk), lambda b, h, i: (b, k_col + h)),
                pl.BlockSpec((seq, dv), lambda b, h, i: (b, v_col + h))]
    args = [q_arr, k_arr, v_arr]
    if fox:
        in_specs += [pl.BlockSpec((tq, LANES), lambda b, h, i: (b * nq + i, 0)),
                     pl.BlockSpec((1, 1, seq), lambda b, h, i: (b * heads + h, 0, 0))]
        args += [cum, cum_t]
    vmem = (2 * tq * dk * 2 + 2 * seq * (dk + dv) * 2 + 2 * tq * dv * 2 + 2 * tq * LANES * 4
            + 8 * seq * 4 * 2 + 2 * tq * LANES * 4 + tq * dv * 4 + 6 * tq * tq * 4)
    return pl.pallas_call(
        functools.partial(_attn_kernel, scale=scale, mask_chunk=mask_chunk, fox=fox),
        grid=(batch, heads, nq),
        in_specs=in_specs,
        out_specs=pl.BlockSpec((tq, dv), lambda b, h, i: (b * nq + i, h)),
        out_shape=jax.ShapeDtypeStruct((batch * seq, heads * dv), BF16),
        scratch_shapes=[pltpu.VMEM((tq, 1), F32), pltpu.VMEM((tq, 1), F32),
                        pltpu.VMEM((tq, dv), F32)],
        compiler_params=_params(vmem),
        name="fox_attn" if fox else "mla_attn",
    )(*args)


def _outproj_kernel(x_ref, oa_ref, ob_ref, wa_ref, wb_ref, gpost_ref, gt_ref, o_ref):
    y = jnp.dot(oa_ref[...], wa_ref[...], preferred_element_type=F32)
    y = y + jnp.dot(ob_ref[...], wb_ref[...], preferred_element_type=F32)
    o_ref[...] = x_ref[...] + gt_ref[0] * _rms(y, gpost_ref[...])


def _outproj(x, o_a, o_b, w_a, w_b, g_post, gt, *, seq, tm):
    t, d = x.shape
    ka, kb = o_a.shape[1], o_b.shape[1]
    tpb = seq // tm
    vmem = 4 * tm * d * 4 + 2 * tm * (ka + kb) * 2 + 2 * (ka + kb) * d * 2 + 2 * tm * d * 4
    return pl.pallas_call(
        _outproj_kernel,
        grid=(t // tm,),
        in_specs=[pl.BlockSpec((tm, d), lambda i: (i, 0)),
                  pl.BlockSpec((tm, ka), lambda i: (i, 0)),
                  pl.BlockSpec((tm, kb), lambda i: (i, 0)),
                  pl.BlockSpec((ka, d), lambda i: (0, 0)),
                  pl.BlockSpec((kb, d), lambda i: (0, 0)),
                  pl.BlockSpec((1, d), lambda i: (0, 0)),
                  pl.BlockSpec((1, 1, d), lambda i: (i // tpb, 0, 0))],
        out_specs=pl.BlockSpec((tm, d), lambda i: (i, 0)),
        out_shape=jax.ShapeDtypeStruct((t, d), F32),
        compiler_params=_params(vmem),
        name="out_proj",
    )(x, o_a, o_b, w_a, w_b, g_post, gt)


def _rope_tables(seq):
    half = MLA_ROPE // 2
    inv = ROPE_THETA ** (-jnp.arange(half, dtype=F32) / half)
    ang = jnp.arange(seq).astype(F32)[:, None] * inv[None, :]
    cos, sin = jnp.cos(ang), jnp.sin(ang)
    zero = jnp.zeros_like(cos)
    pad = jnp.zeros((seq, LANES - 2 * half), F32)
    return (jnp.concatenate([cos, cos, pad], axis=1),
            jnp.concatenate([-sin, zero, pad], axis=1),
            jnp.concatenate([zero, sin, pad], axis=1))


def kernel(x, c, w_ada, b_ada, g_ffn1_pre, g_ffn1_post, w1_gate, w1_up, w1_down, g_mix_pre,
           g_mix_post, w_in, b_forget, g_q_a, w_uq, g_kv_a, w_ukv, w_o, g_ffn2_pre, g_ffn2_post,
           w2_gate, w2_up, w2_down):
    batch, seq, d = x.shape
    depth = w_ada.shape[0]
    t = batch * seq
    xt = x.reshape(t, d)
    c_pad = jnp.pad(c, ((0, 8 - batch), (0, 0)))
    cos_t, sin_lo, sin_hi = _rope_tables(seq)

    for l in range(depth):
        mod = _ada(c_pad, w_ada[l], b_ada[l:l + 1], tn=1024)[:batch]
        sh1, sc1, gt1, sh2, sc2, gt2, sh3, sc3, gt3 = [
            mod[:, n * d:(n + 1) * d].reshape(batch, 1, d) for n in range(9)]

        xt = _ffn(xt, sh1, sc1, gt1, g_ffn1_pre[l:l + 1], g_ffn1_post[l:l + 1],
                  w1_gate[l].astype(BF16), w1_up[l].astype(BF16), w1_down[l].astype(BF16),
                  seq=seq, tm=512, tf=512, res_weight=0.5)

        wi = w_in[l]
        fox0 = KPE_COL + MLA_ROPE
        fl0 = fox0 + 3 * FOX_WIDTH
        w_in_p = jnp.concatenate(
            [wi[:, :fox0], jnp.zeros((d, LANES - MLA_ROPE), F32),
             wi[:, fl0:], jnp.zeros((d, LANES - FOX_HEADS), F32),
             wi[:, fox0:fl0]], axis=1).astype(BF16)
        lat, fqkv = _inproj(xt, sh2, sc2, g_mix_pre[l:l + 1], w_in_p, seq=seq, tm=1024, tn=LATENT_WIDTH)

        wuq_p = jnp.pad(w_uq[l].reshape(MLA_Q_RANK, MLA_HEADS, MLA_NOPE + MLA_ROPE),
                        ((0, 0), (0, 0), (0, MLA_QK_PAD - MLA_NOPE - MLA_ROPE))
                        ).reshape(MLA_Q_RANK, MLA_HEADS * MLA_QK_PAD).astype(BF16)
        wukv = w_ukv[l].reshape(MLA_KV_RANK, MLA_HEADS, MLA_NOPE + MLA_V)
        wukv_p = jnp.concatenate(
            [wukv[:, :, :MLA_NOPE].reshape(MLA_KV_RANK, -1),
             wukv[:, :, MLA_NOPE:].reshape(MLA_KV_RANK, -1)], axis=1).astype(BF16)
        bf_pad = jnp.pad(b_forget[l:l + 1], ((0, 0), (0, LANES - FOX_HEADS)))
        q_mla, k_mla, v_mla, cum, cum_t = _prep(
            lat, g_q_a[l:l + 1], g_kv_a[l:l + 1], wuq_p, wukv_p, bf_pad, cos_t, sin_lo, sin_hi,
            batch=batch, seq=seq, tm=512)

        o_mla = _attention(q_mla, k_mla, v_mla, batch=batch, seq=seq, heads=MLA_HEADS,
                           dk=MLA_QK_PAD, dv=MLA_V, q_col=0, k_col=0, v_col=0, tq=512,
                           scale=1.0 / math.sqrt(MLA_NOPE + MLA_ROPE), mask_chunk=CHUNK)
        o_fox = _attention(fqkv, fqkv, fqkv, cum, cum_t.reshape(batch * FOX_HEADS, 1, seq),
                           batch=batch, seq=seq, heads=FOX_HEADS, dk=FOX_DIM, dv=FOX_DIM,
                           q_col=0, k_col=FOX_HEADS, v_col=2 * FOX_HEADS, tq=512,
                           scale=1.0 / math.sqrt(FOX_DIM), mask_chunk=1)

        wo = w_o[l].astype(BF16)
        xt = _outproj(xt, o_mla, o_fox, wo[:MLA_HEADS * MLA_V], wo[MLA_HEADS * MLA_V:],
                      g_mix_post[l:l + 1], gt2, seq=seq, tm=512)

        xt = _ffn(xt, sh3, sc3, gt3, g_ffn2_pre[l:l + 1], g_ffn2_post[l:l + 1],
                  w2_gate[l].astype(BF16), w2_up[l].astype(BF16), w2_down[l].astype(BF16),
                  seq=seq, tm=512, tf=512, res_weight=0.5)

    return xt.reshape(batch, seq, d)
```

```python
import functools
import math

import jax
import jax.numpy as jnp
from jax import lax
from jax.experimental import pallas as pl
from jax.experimental.pallas import tpu as pltpu

F32 = jnp.float32
BF16 = jnp.bfloat16

V7X_VMEM_BYTES = 64 * 1024 * 1024
LANES = 128

EPS = 1e-6
ROPE_THETA = 10000.0
CHUNK = 64
MLA_HEADS = 8
MLA_Q_RANK = 512
MLA_KV_RANK = 256
MLA_NOPE = 128
MLA_ROPE = 64
MLA_V = 128
MLA_QK_PAD = 256
FOX_HEADS = 8
FOX_DIM = 128
FOX_WIDTH = FOX_HEADS * FOX_DIM
LATENT_WIDTH = 1024
KPE_COL = MLA_Q_RANK + MLA_KV_RANK
FLOGIT_COL = KPE_COL + LANES
ATTN_TILE = 512
LOG2E = math.log2(math.e)
NEG_BIG = -1e30

_NT = (((1,), (1,)), ((), ()))


def _params(vmem_bytes):
    limit = min(int(vmem_bytes * 1.25) + (4 << 20), V7X_VMEM_BYTES - (4 << 20))
    return pltpu.CompilerParams(vmem_limit_bytes=limit)


def _rms(x, g):
    return x * lax.rsqrt(jnp.mean(x * x, axis=-1, keepdims=True) + EPS) * g


def _ada_kernel(c_ref, w_ref, b_ref, o_ref):
    c = c_ref[...]
    cond = (c * jax.nn.sigmoid(c)).astype(BF16)
    o_ref[...] = jnp.dot(cond, w_ref[...].astype(BF16), preferred_element_type=F32) + b_ref[...]


def _ada(c_pad, w, b, *, tn):
    m, d = c_pad.shape
    n = w.shape[1]
    vmem = 2 * d * tn * 4 + d * tn * 2 + 4 * m * tn * 4
    return pl.pallas_call(
        _ada_kernel,
        grid=(n // tn,),
        in_specs=[pl.BlockSpec((m, d), lambda j: (0, 0)),
                  pl.BlockSpec((d, tn), lambda j: (0, j)),
                  pl.BlockSpec((1, tn), lambda j: (0, j))],
        out_specs=pl.BlockSpec((m, tn), lambda j: (0, j)),
        out_shape=jax.ShapeDtypeStruct((m, n), F32),
        compiler_params=_params(vmem),
        name="ada",
    )(c_pad, w, b)


def _ffn_kernel(x_ref, sh_ref, sc_ref, gt_ref, gpre_ref, gpost_ref, wg_ref, wu_ref, wd_ref,
                o_ref, h_ref, acc_ref, *, res_weight):
    f = pl.program_id(1)

    @pl.when(f == 0)
    def _():
        h = _rms(x_ref[...], gpre_ref[...]) * (1.0 + sc_ref[0]) + sh_ref[0]
        h_ref[...] = h.astype(BF16)
        acc_ref[...] = jnp.zeros_like(acc_ref)

    h = h_ref[...]
    g = jnp.dot(h, wg_ref[...], preferred_element_type=F32)
    u = jnp.dot(h, wu_ref[...], preferred_element_type=F32)
    a = (g * jax.nn.sigmoid(g) * u).astype(BF16)
    acc_ref[...] += jnp.dot(a, wd_ref[...], preferred_element_type=F32)

    @pl.when(f == pl.num_programs(1) - 1)
    def _():
        y = _rms(acc_ref[...], gpost_ref[...])
        o_ref[...] = x_ref[...] + res_weight * gt_ref[0] * y


def _ffn(x, sh, sc, gt, g_pre, g_post, wg, wu, wd, *, seq, tm, tf, res_weight):
    t, d = x.shape
    ff = wg.shape[1]
    tpb = seq // tm
    vmem = (4 * tm * d * 4 + tm * d * 2 + tm * d * 4 + 6 * d * tf * 2 + 3 * tm * tf * 4)
    mod_spec = pl.BlockSpec((1, 1, d), lambda i, f: (i // tpb, 0, 0))
    gain_spec = pl.BlockSpec((1, d), lambda i, f: (0, 0))
    return pl.pallas_call(
        functools.partial(_ffn_kernel, res_weight=res_weight),
        grid=(t // tm, ff // tf),
        in_specs=[pl.BlockSpec((tm, d), lambda i, f: (i, 0)),
                  mod_spec, mod_spec, mod_spec, gain_spec, gain_spec,
                  pl.BlockSpec((d, tf), lambda i, f: (0, f)),
                  pl.BlockSpec((d, tf), lambda i, f: (0, f)),
                  pl.BlockSpec((tf, d), lambda i, f: (f, 0))],
        out_specs=pl.BlockSpec((tm, d), lambda i, f: (i, 0)),
        out_shape=jax.ShapeDtypeStruct((t, d), F32),
        scratch_shapes=[pltpu.VMEM((tm, d), BF16), pltpu.VMEM((tm, d), F32)],
        compiler_params=_params(vmem),
        name="ffn",
    )(x, sh, sc, gt, g_pre, g_post, wg, wu, wd)


def _inproj_kernel(x_ref, sh_ref, sc_ref, gpre_ref, w_ref, wvt_ref, lat_ref, fqk_ref, fvt_ref, h_ref):
    j = pl.program_id(1)

    @pl.when(j == 0)
    def _():
        h = _rms(x_ref[...], gpre_ref[...]) * (1.0 + sc_ref[0]) + sh_ref[0]
        h_ref[...] = h.astype(BF16)

    @pl.when(j == 0)
    def _():
        lat_ref[...] = jnp.dot(h_ref[...], w_ref[...], preferred_element_type=F32)

    @pl.when(jnp.logical_and(j > 0, j < 3))
    def _():
        fqk_ref[...] = jnp.dot(h_ref[...], w_ref[...], preferred_element_type=F32).astype(BF16)

    @pl.when(j == 3)
    def _():
        fvt_ref[0, 0] = lax.dot_general(wvt_ref[...], h_ref[...], _NT,
                                        preferred_element_type=F32).astype(BF16)


def _inproj(x, sh, sc, g_pre, w, wvt, *, batch, seq):
    t, d = x.shape
    tm = ATTN_TILE
    tn = LATENT_WIDTH
    tpb = seq // tm
    assert w.shape[1] == 3 * tn and wvt.shape == (FOX_WIDTH, d)
    vmem = (2 * tm * d * 4 + tm * d * 2 + 2 * d * tn * 2 + 2 * FOX_WIDTH * d * 2 + 2 * tm * tn * 4
            + 4 * tm * tn * 2 + tm * tn * 4)
    mod_spec = pl.BlockSpec((1, 1, d), lambda i, j: (i // tpb, 0, 0))
    return pl.pallas_call(
        _inproj_kernel,
        grid=(t // tm, 4),
        in_specs=[pl.BlockSpec((tm, d), lambda i, j: (i, 0)),
                  mod_spec, mod_spec,
                  pl.BlockSpec((1, d), lambda i, j: (0, 0)),
                  pl.BlockSpec((d, tn), lambda i, j: (0, jnp.minimum(j, 2))),
                  pl.BlockSpec((FOX_WIDTH, d), lambda i, j: (0, 0))],
        out_specs=[pl.BlockSpec((tm, tn), lambda i, j: (i, 0)),
                   pl.BlockSpec((tm, tn), lambda i, j: (i, jnp.clip(j - 1, 0, 1))),
                   pl.BlockSpec((1, 1, FOX_WIDTH, tm), lambda i, j: (i // tpb, i % tpb, 0, 0))],
        out_shape=[jax.ShapeDtypeStruct((t, tn), F32),
                   jax.ShapeDtypeStruct((t, 2 * tn), BF16),
                   jax.ShapeDtypeStruct((batch, tpb, FOX_WIDTH, tm), BF16)],
        scratch_shapes=[pltpu.VMEM((tm, d), BF16)],
        compiler_params=_params(vmem),
        name="in_proj",
    )(x, sh, sc, g_pre, w, wvt)


def _rot(r, cos_t, sin_lo, sin_hi):
    return r * cos_t + pltpu.roll(r, 96, 1) * sin_lo + pltpu.roll(r, 32, 1) * sin_hi


def _prep_kernel(lat_ref, gq_ref, gkv_ref, wuq_ref, wuk_ref, wuvt_ref, bf_ref, cos_ref, slo_ref, shi_ref,
                 q_ref, k_ref, vt_ref, ck_ref, cq_ref, carry_ref, *, tiles_per_batch):
    i = pl.program_id(0)
    tm = lat_ref.shape[0]
    cos_t, sin_lo, sin_hi = cos_ref[...], slo_ref[...], shi_ref[...]

    qn = _rms(lat_ref[:, :MLA_Q_RANK], gq_ref[...]).astype(BF16)
    q = jnp.dot(qn, wuq_ref[...], preferred_element_type=F32)
    kvn = _rms(lat_ref[:, MLA_Q_RANK:KPE_COL], gkv_ref[...]).astype(BF16)
    k_nope = jnp.dot(kvn, wuk_ref[...], preferred_element_type=F32)
    vt_ref[0, 0] = lax.dot_general(wuvt_ref[...], kvn, _NT, preferred_element_type=F32).astype(BF16)
    k_rope = _rot(lat_ref[:, KPE_COL:KPE_COL + LANES], cos_t, sin_lo, sin_hi).astype(BF16)
    for h in range(MLA_HEADS):
        c0 = h * MLA_QK_PAD
        q_ref[:, c0:c0 + MLA_NOPE] = q[:, c0:c0 + MLA_NOPE].astype(BF16)
        q_ref[:, c0 + MLA_NOPE:c0 + MLA_QK_PAD] = _rot(
            q[:, c0 + MLA_NOPE:c0 + MLA_QK_PAD], cos_t, sin_lo, sin_hi).astype(BF16)
        k_ref[:, c0:c0 + MLA_NOPE] = k_nope[:, h * MLA_NOPE:(h + 1) * MLA_NOPE].astype(BF16)
        k_ref[:, c0 + MLA_NOPE:c0 + MLA_QK_PAD] = k_rope

    @pl.when(i % tiles_per_batch == 0)
    def _():
        carry_ref[...] = jnp.zeros_like(carry_ref)

    z = lat_ref[:, FLOGIT_COL:FLOGIT_COL + LANES] + bf_ref[...]
    log_f = jnp.minimum(z, 0.0) - jnp.log1p(jnp.exp(-jnp.abs(z)))
    row = lax.broadcasted_iota(jnp.int32, (tm, tm), 0)
    col = lax.broadcasted_iota(jnp.int32, (tm, tm), 1)
    tri = (col <= row).astype(F32)
    cum = jnp.dot(tri, log_f, precision=lax.Precision.HIGHEST,
                  preferred_element_type=F32) + carry_ref[...]
    carry_ref[...] = cum[tm - 1:tm, :]
    cum2 = cum * LOG2E
    cq_ref[0, 0] = cum2.T[:FOX_HEADS, :]
    for h in range(FOX_HEADS):
        ck_ref[0, h] = jnp.broadcast_to(cum2[:, h:h + 1], (tm, LANES))


def _prep(lat, g_q, g_kv, wuq, wuk, wuvt, b_forget, cos_t, sin_lo, sin_hi, *, batch, seq):
    t = lat.shape[0]
    tm = ATTN_TILE
    tpb = seq // tm
    qk_w = MLA_HEADS * MLA_QK_PAD
    v_w = MLA_HEADS * MLA_V
    vmem = (2 * tm * LATENT_WIDTH * 4 + 2 * (wuq.size + wuk.size + wuvt.size) * 2 + 6 * tm * LANES * 4
            + 2 * (2 * tm * qk_w + tm * v_w) * 2 + 2 * FOX_HEADS * tm * LANES * 4
            + tm * (qk_w + 2 * v_w) * 4 + 3 * tm * tm * 4)
    full = lambda a: pl.BlockSpec(a.shape, lambda i: (0, 0))
    tab_spec = pl.BlockSpec((tm, LANES), lambda i: (i % tpb, 0))
    return pl.pallas_call(
        functools.partial(_prep_kernel, tiles_per_batch=tpb),
        grid=(t // tm,),
        in_specs=[pl.BlockSpec((tm, LATENT_WIDTH), lambda i: (i, 0)),
                  full(g_q), full(g_kv), full(wuq), full(wuk), full(wuvt), full(b_forget),
                  tab_spec, tab_spec, tab_spec],
        out_specs=[pl.BlockSpec((tm, qk_w), lambda i: (i, 0)),
                   pl.BlockSpec((tm, qk_w), lambda i: (i, 0)),
                   pl.BlockSpec((1, 1, v_w, tm), lambda i: (i // tpb, i % tpb, 0, 0)),
                   pl.BlockSpec((1, FOX_HEADS, tm, LANES), lambda i: (i // tpb, 0, i % tpb, 0)),
                   pl.BlockSpec((1, 1, FOX_HEADS, tm), lambda i: (i // tpb, i % tpb, 0, 0))],
        out_shape=[jax.ShapeDtypeStruct((t, qk_w), BF16),
                   jax.ShapeDtypeStruct((t, qk_w), BF16),
                   jax.ShapeDtypeStruct((batch, tpb, v_w, tm), BF16),
                   jax.ShapeDtypeStruct((batch, FOX_HEADS, seq, LANES), F32),
                   jax.ShapeDtypeStruct((batch, tpb, FOX_HEADS, tm), F32)],
        scratch_shapes=[pltpu.VMEM((1, LANES), F32)],
        compiler_params=_params(vmem),
        name="mla_fox_prep",
    )(lat, g_q, g_kv, wuq, wuk, wuvt, b_forget, cos_t, sin_lo, sin_hi)


def _attn_kernel(*refs, scale_log2, mask_chunk, fox):
    if fox:
        q_ref, k_ref, vt_ref, ck_ref, cq_ref, o_ref, s0_ref, s1_ref, m_ref, l_ref, acc_ref = refs
    else:
        q_ref, k_ref, vt_ref, o_ref, s0_ref, s1_ref, m_ref, l_ref, acc_ref = refs
    head = pl.program_id(1)
    i = pl.program_id(2)
    tq = q_ref.shape[0]
    q = q_ref[...]
    m_ref[...] = jnp.full_like(m_ref, NEG_BIG)
    l_ref[...] = jnp.zeros_like(l_ref)
    acc_ref[...] = jnp.zeros_like(acc_ref)
    if fox:
        cum_q = cq_ref[0, i, pl.ds(head, 1), :]

    def scores(j, s_ref):
        start = pl.multiple_of(j * tq, tq)
        k = k_ref[pl.ds(start, tq), :]
        s = lax.dot_general(k, q, _NT, preferred_element_type=F32) * scale_log2
        if fox:
            cum_k = ck_ref[0, 0, pl.ds(start, tq), :]
            s = s + (cum_q - jnp.concatenate([cum_k] * (tq // LANES), axis=1))
        s_ref[...] = s

    def accumulate(j, s_ref, diagonal):
        s = s_ref[...]
        if diagonal:
            src = lax.broadcasted_iota(jnp.int32, (tq, tq), 0) // mask_chunk
            dst = lax.broadcasted_iota(jnp.int32, (tq, tq), 1) // mask_chunk
            s = jnp.where(src <= dst, s, NEG_BIG)
        m_prev = m_ref[...]
        m_new = jnp.maximum(m_prev, jnp.max(s, axis=0, keepdims=True))
        alpha = jnp.exp2(m_prev - m_new)
        p = jnp.exp2(s - m_new)
        l_ref[...] = alpha * l_ref[...] + jnp.sum(p, axis=0, keepdims=True)
        acc_ref[...] = alpha * acc_ref[...] + jnp.dot(vt_ref[0, j], p.astype(BF16),
                                                      preferred_element_type=F32)
        m_ref[...] = m_new

    def body(jj, carry):
        j = 2 * jj
        scores(j + 1, s1_ref)
        accumulate(j, s0_ref, False)
        scores(j + 2, s0_ref)
        accumulate(j + 1, s1_ref, False)
        return carry

    scores(0, s0_ref)
    lax.fori_loop(0, i // 2, body, 0)

    @pl.when(i % 2 == 0)
    def _():
        accumulate(i, s0_ref, True)

    @pl.when(i % 2 == 1)
    def _():
        scores(i, s1_ref)
        accumulate(i - 1, s0_ref, False)
        accumulate(i, s1_ref, True)

    o_ref[...] = (acc_ref[...] / l_ref[...]).T.astype(o_ref.dtype)


def _attention(q_arr, k_arr, vt_arr, cum_k=None, cum_q=None, *, batch, seq, heads, dk, dv,
               q_col, k_col, scale, mask_chunk):
    fox = cum_k is not None
    tq = ATTN_TILE
    nq = seq // tq
    in_specs = [pl.BlockSpec((tq, dk), lambda b, h, i: (b * nq + i, q_col + h)),
                pl.BlockSpec((seq, dk), lambda b, h, i: (b, k_col + h)),
                pl.BlockSpec((1, nq, dv, tq), lambda b, h, i: (b, 0, h, 0))]
    args = [q_arr, k_arr, vt_arr]
    if fox:
        in_specs += [pl.BlockSpec((1, 1, seq, LANES), lambda b, h, i: (b, h, 0, 0)),
                     pl.BlockSpec((1, nq, heads, tq), lambda b, h, i: (b, 0, 0, 0))]
        args += [cum_k, cum_q]
    vmem = (2 * tq * dk * 2 + 2 * seq * (dk + dv) * 2 + 2 * tq * dv * 2 + 2 * seq * LANES * 4
            + 2 * seq * heads * 4 + tq * dv * 4 + 6 * tq * tq * 4)
    return pl.pallas_call(
        functools.partial(_attn_kernel, scale_log2=scale * LOG2E, mask_chunk=mask_chunk, fox=fox),
        grid=(batch, heads, nq),
        in_specs=in_specs,
        out_specs=pl.BlockSpec((tq, dv), lambda b, h, i: (b * nq + i, h)),
        out_shape=jax.ShapeDtypeStruct((batch * seq, heads * dv), BF16),
        scratch_shapes=[pltpu.VMEM((tq, tq), F32), pltpu.VMEM((tq, tq), F32),
                        pltpu.VMEM((1, tq), F32), pltpu.VMEM((1, tq), F32), pltpu.VMEM((dv, tq), F32)],
        compiler_params=_params(vmem),
        name="fox_attn" if fox else "mla_attn",
    )(*args)


def _outproj_kernel(x_ref, oa_ref, ob_ref, wa_ref, wb_ref, gpost_ref, gt_ref, o_ref):
    y = jnp.dot(oa_ref[...], wa_ref[...], preferred_element_type=F32)
    y = y + jnp.dot(ob_ref[...], wb_ref[...], preferred_element_type=F32)
    o_ref[...] = x_ref[...] + gt_ref[0] * _rms(y, gpost_ref[...])


def _outproj(x, o_a, o_b, w_a, w_b, g_post, gt, *, seq, tm):
    t, d = x.shape
    ka, kb = o_a.shape[1], o_b.shape[1]
    tpb = seq // tm
    vmem = 4 * tm * d * 4 + 2 * tm * (ka + kb) * 2 + 2 * (ka + kb) * d * 2 + 2 * tm * d * 4
    return pl.pallas_call(
        _outproj_kernel,
        grid=(t // tm,),
        in_specs=[pl.BlockSpec((tm, d), lambda i: (i, 0)),
                  pl.BlockSpec((tm, ka), lambda i: (i, 0)),
                  pl.BlockSpec((tm, kb), lambda i: (i, 0)),
                  pl.BlockSpec((ka, d), lambda i: (0, 0)),
                  pl.BlockSpec((kb, d), lambda i: (0, 0)),
                  pl.BlockSpec((1, d), lambda i: (0, 0)),
                  pl.BlockSpec((1, 1, d), lambda i: (i // tpb, 0, 0))],
        out_specs=pl.BlockSpec((tm, d), lambda i: (i, 0)),
        out_shape=jax.ShapeDtypeStruct((t, d), F32),
        compiler_params=_params(vmem),
        name="out_proj",
    )(x, o_a, o_b, w_a, w_b, g_post, gt)


def _rope_tables(seq):
    half = MLA_ROPE // 2
    inv = ROPE_THETA ** (-jnp.arange(half, dtype=F32) / half)
    ang = jnp.arange(seq).astype(F32)[:, None] * inv[None, :]
    cos, sin = jnp.cos(ang), jnp.sin(ang)
    zero = jnp.zeros_like(cos)
    pad = jnp.zeros((seq, LANES - 2 * half), F32)
    return (jnp.concatenate([cos, cos, pad], axis=1),
            jnp.concatenate([-sin, zero, pad], axis=1),
            jnp.concatenate([zero, sin, pad], axis=1))


def kernel(x, c, w_ada, b_ada, g_ffn1_pre, g_ffn1_post, w1_gate, w1_up, w1_down, g_mix_pre,
           g_mix_post, w_in, b_forget, g_q_a, w_uq, g_kv_a, w_ukv, w_o, g_ffn2_pre, g_ffn2_post,
           w2_gate, w2_up, w2_down):
    batch, seq, d = x.shape
    depth = w_ada.shape[0]
    t = batch * seq
    xt = x.reshape(t, d)
    c_pad = jnp.pad(c, ((0, 8 - batch), (0, 0)))
    cos_t, sin_lo, sin_hi = _rope_tables(seq)

    for l in range(depth):
        mod = _ada(c_pad, w_ada[l], b_ada[l:l + 1], tn=1024)[:batch]
        sh1, sc1, gt1, sh2, sc2, gt2, sh3, sc3, gt3 = [
            mod[:, n * d:(n + 1) * d].reshape(batch, 1, d) for n in range(9)]

        xt = _ffn(xt, sh1, sc1, gt1, g_ffn1_pre[l:l + 1], g_ffn1_post[l:l + 1],
                  w1_gate[l].astype(BF16), w1_up[l].astype(BF16), w1_down[l].astype(BF16),
                  seq=seq, tm=512, tf=512, res_weight=0.5)

        wi = w_in[l]
        fox0 = KPE_COL + MLA_ROPE
        fv0 = fox0 + 2 * FOX_WIDTH
        fl0 = fox0 + 3 * FOX_WIDTH
        w_in_p = jnp.concatenate(
            [wi[:, :fox0], jnp.zeros((d, LANES - MLA_ROPE), F32),
             wi[:, fl0:], jnp.zeros((d, LANES - FOX_HEADS), F32),
             wi[:, fox0:fv0]], axis=1).astype(BF16)
        w_fvt = wi[:, fv0:fl0].T.astype(BF16)
        lat, fqk, fvt = _inproj(xt, sh2, sc2, g_mix_pre[l:l + 1], w_in_p, w_fvt, batch=batch, seq=seq)

        wuq_p = jnp.pad(w_uq[l].reshape(MLA_Q_RANK, MLA_HEADS, MLA_NOPE + MLA_ROPE),
                        ((0, 0), (0, 0), (0, MLA_QK_PAD - MLA_NOPE - MLA_ROPE))
                        ).reshape(MLA_Q_RANK, MLA_HEADS * MLA_QK_PAD).astype(BF16)
        wukv = w_ukv[l].reshape(MLA_KV_RANK, MLA_HEADS, MLA_NOPE + MLA_V)
        wuk = wukv[:, :, :MLA_NOPE].reshape(MLA_KV_RANK, -1).astype(BF16)
        wuvt = wukv[:, :, MLA_NOPE:].reshape(MLA_KV_RANK, -1).T.astype(BF16)
        bf_pad = jnp.pad(b_forget[l:l + 1], ((0, 0), (0, LANES - FOX_HEADS)))
        q_mla, k_mla, vt_mla, cum_k, cum_q = _prep(
            lat, g_q_a[l:l + 1], g_kv_a[l:l + 1], wuq_p, wuk, wuvt, bf_pad, cos_t, sin_lo, sin_hi,
            batch=batch, seq=seq)

        o_mla = _attention(q_mla, k_mla, vt_mla, batch=batch, seq=seq, heads=MLA_HEADS,
                           dk=MLA_QK_PAD, dv=MLA_V, q_col=0, k_col=0,
                           scale=1.0 / math.sqrt(MLA_NOPE + MLA_ROPE), mask_chunk=CHUNK)
        o_fox = _attention(fqk, fqk, fvt, cum_k, cum_q, batch=batch, seq=seq, heads=FOX_HEADS,
                           dk=FOX_DIM, dv=FOX_DIM, q_col=0, k_col=FOX_HEADS,
                           scale=1.0 / math.sqrt(FOX_DIM), mask_chunk=1)

        wo = w_o[l].astype(BF16)
        xt = _outproj(xt, o_mla, o_fox, wo[:MLA_HEADS * MLA_V], wo[MLA_HEADS * MLA_V:],
                      g_mix_post[l:l + 1], gt2, seq=seq, tm=512)

        xt = _ffn(xt, sh3, sc3, gt3, g_ffn2_pre[l:l + 1], g_ffn2_post[l:l + 1],
                  w2_gate[l].astype(BF16), w2_up[l].astype(BF16), w2_down[l].astype(BF16),
                  seq=seq, tm=512, tf=512, res_weight=0.5)

    return xt.reshape(batch, seq, d)
```

```python
import functools
import math

import jax
import jax.numpy as jnp
from jax import lax
from jax.experimental import pallas as pl
from jax.experimental.pallas import tpu as pltpu

F32 = jnp.float32
BF16 = jnp.bfloat16

V7X_VMEM_BYTES = 64 * 1024 * 1024
LANES = 128

EPS = 1e-6
ROPE_THETA = 10000.0
CHUNK = 64
MLA_HEADS = 8
MLA_Q_RANK = 512
MLA_KV_RANK = 256
MLA_NOPE = 128
MLA_ROPE = 64
MLA_V = 128
MLA_QK_PAD = 256
FOX_HEADS = 8
FOX_DIM = 128
FOX_WIDTH = FOX_HEADS * FOX_DIM
LATENT_WIDTH = 1024
KPE_COL = MLA_Q_RANK + MLA_KV_RANK
FLOGIT_COL = KPE_COL + LANES
ATTN_TILE = 512
PHASE_B_UNROLL = 4
LOG2E = math.log2(math.e)
MLA_Q_SCALE = LOG2E / math.sqrt(MLA_NOPE + MLA_ROPE)
FOX_Q_SCALE = LOG2E / math.sqrt(FOX_DIM)
NEG_BIG = -1e30

_NT = (((1,), (1,)), ((), ()))


def _params(vmem_bytes):
    limit = min(int(vmem_bytes * 1.25) + (4 << 20), V7X_VMEM_BYTES - (4 << 20))
    return pltpu.CompilerParams(vmem_limit_bytes=limit)


def _rms(x, g):
    return x * lax.rsqrt(jnp.mean(x * x, axis=-1, keepdims=True) + EPS) * g


def _ada_kernel(c_ref, w_ref, b_ref, o_ref):
    c = c_ref[...]
    cond = (c * jax.nn.sigmoid(c)).astype(BF16)
    o_ref[...] = jnp.dot(cond, w_ref[...].astype(BF16), preferred_element_type=F32) + b_ref[...]


def _ada(c_pad, w, b, *, tn):
    m, d = c_pad.shape
    n = w.shape[1]
    vmem = 2 * d * tn * 4 + d * tn * 2 + 4 * m * tn * 4
    return pl.pallas_call(
        _ada_kernel,
        grid=(n // tn,),
        in_specs=[pl.BlockSpec((m, d), lambda j: (0, 0)),
                  pl.BlockSpec((d, tn), lambda j: (0, j)),
                  pl.BlockSpec((1, tn), lambda j: (0, j))],
        out_specs=pl.BlockSpec((m, tn), lambda j: (0, j)),
        out_shape=jax.ShapeDtypeStruct((m, n), F32),
        compiler_params=_params(vmem),
        name="ada",
    )(c_pad, w, b)


def _ffn_kernel(x_ref, sh_ref, sc_ref, gt_ref, gpre_ref, gpost_ref, wg_ref, wu_ref, wd_ref,
                o_ref, h_ref, *, res_weight):
    f = pl.program_id(1)

    @pl.when(f == 0)
    def _():
        h = _rms(x_ref[...], gpre_ref[...]) * (1.0 + sc_ref[0]) + sh_ref[0]
        h_ref[...] = h.astype(BF16)
        o_ref[...] = jnp.zeros_like(o_ref)

    h = h_ref[...]
    g = jnp.dot(h, wg_ref[...], preferred_element_type=F32)
    u = jnp.dot(h, wu_ref[...], preferred_element_type=F32)
    a = (g * jax.nn.sigmoid(g) * u).astype(BF16)
    o_ref[...] += jnp.dot(a, wd_ref[...], preferred_element_type=F32)

    @pl.when(f == pl.num_programs(1) - 1)
    def _():
        o_ref[...] = x_ref[...] + res_weight * gt_ref[0] * _rms(o_ref[...], gpost_ref[...])


def _ffn(x, sh, sc, gt, g_pre, g_post, wg, wu, wd, *, seq, tm, tf, res_weight):
    t, d = x.shape
    ff = wg.shape[1]
    tpb = seq // tm
    vmem = 4 * tm * d * 4 + tm * d * 2 + 6 * d * tf * 2 + 3 * tm * tf * 4
    mod_spec = pl.BlockSpec((1, 1, d), lambda i, f: (i // tpb, 0, 0))
    gain_spec = pl.BlockSpec((1, d), lambda i, f: (0, 0))
    return pl.pallas_call(
        functools.partial(_ffn_kernel, res_weight=res_weight),
        grid=(t // tm, ff // tf),
        in_specs=[pl.BlockSpec((tm, d), lambda i, f: (i, 0)),
                  mod_spec, mod_spec, mod_spec, gain_spec, gain_spec,
                  pl.BlockSpec((d, tf), lambda i, f: (0, f)),
                  pl.BlockSpec((d, tf), lambda i, f: (0, f)),
                  pl.BlockSpec((tf, d), lambda i, f: (f, 0))],
        out_specs=pl.BlockSpec((tm, d), lambda i, f: (i, 0)),
        out_shape=jax.ShapeDtypeStruct((t, d), F32),
        scratch_shapes=[pltpu.VMEM((tm, d), BF16)],
        compiler_params=_params(vmem),
        name="ffn",
    )(x, sh, sc, gt, g_pre, g_post, wg, wu, wd)


def _inproj_kernel(x_ref, sh_ref, sc_ref, gpre_ref, w_ref, wvt_ref, lat_ref, fqk_ref, fvt_ref, h_ref):
    j = pl.program_id(1)

    @pl.when(j == 0)
    def _():
        h = _rms(x_ref[...], gpre_ref[...]) * (1.0 + sc_ref[0]) + sh_ref[0]
        h_ref[...] = h.astype(BF16)

    @pl.when(j == 0)
    def _():
        lat_ref[...] = jnp.dot(h_ref[...], w_ref[...], preferred_element_type=F32)

    @pl.when(j == 1)
    def _():
        fq = jnp.dot(h_ref[...], w_ref[...], preferred_element_type=F32) * FOX_Q_SCALE
        fqk_ref[...] = fq.astype(BF16)

    @pl.when(j == 2)
    def _():
        fqk_ref[...] = jnp.dot(h_ref[...], w_ref[...], preferred_element_type=F32).astype(BF16)

    @pl.when(j == 3)
    def _():
        fvt_ref[0, 0] = lax.dot_general(wvt_ref[...], h_ref[...], _NT,
                                        preferred_element_type=F32).astype(BF16)


def _inproj(x, sh, sc, g_pre, w, wvt, *, batch, seq):
    t, d = x.shape
    tm = ATTN_TILE
    tn = LATENT_WIDTH
    tpb = seq // tm
    assert w.shape[1] == 3 * tn and wvt.shape == (FOX_WIDTH, d)
    vmem = (2 * tm * d * 4 + tm * d * 2 + 2 * d * tn * 2 + 2 * FOX_WIDTH * d * 2 + 2 * tm * tn * 4
            + 4 * tm * tn * 2 + tm * tn * 4)
    mod_spec = pl.BlockSpec((1, 1, d), lambda i, j: (i // tpb, 0, 0))
    return pl.pallas_call(
        _inproj_kernel,
        grid=(t // tm, 4),
        in_specs=[pl.BlockSpec((tm, d), lambda i, j: (i, 0)),
                  mod_spec, mod_spec,
                  pl.BlockSpec((1, d), lambda i, j: (0, 0)),
                  pl.BlockSpec((d, tn), lambda i, j: (0, jnp.minimum(j, 2))),
                  pl.BlockSpec((FOX_WIDTH, d), lambda i, j: (0, 0))],
        out_specs=[pl.BlockSpec((tm, tn), lambda i, j: (i, 0)),
                   pl.BlockSpec((tm, tn), lambda i, j: (i, jnp.clip(j - 1, 0, 1))),
                   pl.BlockSpec((1, 1, FOX_WIDTH, tm), lambda i, j: (i // tpb, i % tpb, 0, 0))],
        out_shape=[jax.ShapeDtypeStruct((t, tn), F32),
                   jax.ShapeDtypeStruct((t, 2 * tn), BF16),
                   jax.ShapeDtypeStruct((batch, tpb, FOX_WIDTH, tm), BF16)],
        scratch_shapes=[pltpu.VMEM((tm, d), BF16)],
        compiler_params=_params(vmem),
        name="in_proj",
    )(x, sh, sc, g_pre, w, wvt)


def _rot(r, cos_t, sin_lo, sin_hi):
    return r * cos_t + pltpu.roll(r, 96, 1) * sin_lo + pltpu.roll(r, 32, 1) * sin_hi


def _prep_kernel(lat_ref, gq_ref, gkv_ref, wuq_ref, wuk_ref, wuvt_ref, bf_ref, cos_ref, slo_ref, shi_ref,
                 q_ref, k_ref, vt_ref, ck_ref, cq_ref, carry_ref, *, tiles_per_batch):
    i = pl.program_id(0)
    tm = lat_ref.shape[0]
    cos_t, sin_lo, sin_hi = cos_ref[...], slo_ref[...], shi_ref[...]

    qn = _rms(lat_ref[:, :MLA_Q_RANK], gq_ref[...]).astype(BF16)
    q = jnp.dot(qn, wuq_ref[...], preferred_element_type=F32) * MLA_Q_SCALE
    kvn = _rms(lat_ref[:, MLA_Q_RANK:KPE_COL], gkv_ref[...]).astype(BF16)
    k_nope = jnp.dot(kvn, wuk_ref[...], preferred_element_type=F32)
    vt_ref[0, 0] = lax.dot_general(wuvt_ref[...], kvn, _NT, preferred_element_type=F32).astype(BF16)
    k_rope = _rot(lat_ref[:, KPE_COL:KPE_COL + LANES], cos_t, sin_lo, sin_hi).astype(BF16)
    for h in range(MLA_HEADS):
        c0 = h * MLA_QK_PAD
        q_ref[:, c0:c0 + MLA_NOPE] = q[:, c0:c0 + MLA_NOPE].astype(BF16)
        q_ref[:, c0 + MLA_NOPE:c0 + MLA_QK_PAD] = _rot(
            q[:, c0 + MLA_NOPE:c0 + MLA_QK_PAD], cos_t, sin_lo, sin_hi).astype(BF16)
        k_ref[:, c0:c0 + MLA_NOPE] = k_nope[:, h * MLA_NOPE:(h + 1) * MLA_NOPE].astype(BF16)
        k_ref[:, c0 + MLA_NOPE:c0 + MLA_QK_PAD] = k_rope

    @pl.when(i % tiles_per_batch == 0)
    def _():
        carry_ref[...] = jnp.zeros_like(carry_ref)

    z = lat_ref[:, FLOGIT_COL:FLOGIT_COL + LANES] + bf_ref[...]
    log_f = jnp.minimum(z, 0.0) - jnp.log1p(jnp.exp(-jnp.abs(z)))
    row = lax.broadcasted_iota(jnp.int32, (tm, tm), 0)
    col = lax.broadcasted_iota(jnp.int32, (tm, tm), 1)
    tri = (col <= row).astype(F32)
    cum = jnp.dot(tri, log_f, precision=lax.Precision.HIGHEST,
                  preferred_element_type=F32) + carry_ref[...]
    carry_ref[...] = cum[tm - 1:tm, :]
    cum2 = cum * LOG2E
    cq_ref[0, 0] = cum2.T[:FOX_HEADS, :]
    for h in range(FOX_HEADS):
        ck_ref[0, h] = jnp.broadcast_to(cum2[:, h:h + 1], (tm, LANES))


def _prep(lat, g_q, g_kv, wuq, wuk, wuvt, b_forget, cos_t, sin_lo, sin_hi, *, batch, seq):
    t = lat.shape[0]
    tm = ATTN_TILE
    tpb = seq // tm
    qk_w = MLA_HEADS * MLA_QK_PAD
    v_w = MLA_HEADS * MLA_V
    vmem = (2 * tm * LATENT_WIDTH * 4 + 2 * (wuq.size + wuk.size + wuvt.size) * 2 + 6 * tm * LANES * 4
            + 2 * (2 * tm * qk_w + tm * v_w) * 2 + 2 * FOX_HEADS * tm * LANES * 4
            + tm * (qk_w + 2 * v_w) * 4 + 3 * tm * tm * 4)
    full = lambda a: pl.BlockSpec(a.shape, lambda i: (0, 0))
    tab_spec = pl.BlockSpec((tm, LANES), lambda i: (i % tpb, 0))
    return pl.pallas_call(
        functools.partial(_prep_kernel, tiles_per_batch=tpb),
        grid=(t // tm,),
        in_specs=[pl.BlockSpec((tm, LATENT_WIDTH), lambda i: (i, 0)),
                  full(g_q), full(g_kv), full(wuq), full(wuk), full(wuvt), full(b_forget),
                  tab_spec, tab_spec, tab_spec],
        out_specs=[pl.BlockSpec((tm, qk_w), lambda i: (i, 0)),
                   pl.BlockSpec((tm, qk_w), lambda i: (i, 0)),
                   pl.BlockSpec((1, 1, v_w, tm), lambda i: (i // tpb, i % tpb, 0, 0)),
                   pl.BlockSpec((1, FOX_HEADS, tm, LANES), lambda i: (i // tpb, 0, i % tpb, 0)),
                   pl.BlockSpec((1, 1, FOX_HEADS, tm), lambda i: (i // tpb, i % tpb, 0, 0))],
        out_shape=[jax.ShapeDtypeStruct((t, qk_w), BF16),
                   jax.ShapeDtypeStruct((t, qk_w), BF16),
                   jax.ShapeDtypeStruct((batch, tpb, v_w, tm), BF16),
                   jax.ShapeDtypeStruct((batch, FOX_HEADS, seq, LANES), F32),
                   jax.ShapeDtypeStruct((batch, tpb, FOX_HEADS, tm), F32)],
        scratch_shapes=[pltpu.VMEM((1, LANES), F32)],
        compiler_params=_params(vmem),
        name="mla_fox_prep",
    )(lat, g_q, g_kv, wuq, wuk, wuvt, b_forget, cos_t, sin_lo, sin_hi)


def _attn_kernel(*refs, mask_chunk, fox):
    if fox:
        q_ref, k_ref, vt_ref, ck_ref, cq_ref, o_ref, s0_ref, s1_ref, m_ref, l_ref, acc_ref = refs
    else:
        q_ref, k_ref, vt_ref, o_ref, s0_ref, s1_ref, m_ref, l_ref, acc_ref = refs
    head = pl.program_id(1)
    tq = ATTN_TILE
    nq = q_ref.shape[0] // tq
    n_low = nq * (nq - 1) // 2
    assert PHASE_B_UNROLL % 2 == 0 and n_low % PHASE_B_UNROLL == 0
    bufs = (s0_ref, s1_ref)

    def rows(t):
        return pl.ds(pl.multiple_of(t * tq, tq), tq)

    def scores(i, j, s_ref):
        s = lax.dot_general(k_ref[rows(j), :], q_ref[rows(i), :], _NT, preferred_element_type=F32)
        if fox:
            cum_q = cq_ref[0, i, pl.ds(head, 1), :]
            cum_k = ck_ref[0, 0, rows(j), :]
            s = s + (cum_q - jnp.concatenate([cum_k] * (tq // LANES), axis=1))
        s_ref[...] = s

    def accumulate_diagonal(i, s_ref):
        src = lax.broadcasted_iota(jnp.int32, (tq, tq), 0) // mask_chunk
        dst = lax.broadcasted_iota(jnp.int32, (tq, tq), 1) // mask_chunk
        s = jnp.where(src <= dst, s_ref[...], NEG_BIG)
        m = jnp.max(s, axis=0, keepdims=True)
        p = jnp.exp2(s - m)
        m_ref[i] = m
        l_ref[i] = jnp.sum(p, axis=0, keepdims=True)
        acc_ref[i] = jnp.dot(vt_ref[0, i], p.astype(BF16), preferred_element_type=F32)

    def accumulate(i, j, s_ref):
        s = s_ref[...]
        m_prev = m_ref[i]
        m_new = jnp.maximum(m_prev, jnp.max(s, axis=0, keepdims=True))
        alpha = jnp.exp2(m_prev - m_new)
        p = jnp.exp2(s - m_new)
        l_ref[i] = alpha * l_ref[i] + jnp.sum(p, axis=0, keepdims=True)
        acc_ref[i] = alpha * acc_ref[i] + jnp.dot(vt_ref[0, j], p.astype(BF16),
                                                  preferred_element_type=F32)
        m_ref[i] = m_new

    scores(0, 0, bufs[0])
    for i in range(nq):
        if i + 1 < nq:
            scores(i + 1, i + 1, bufs[(i + 1) % 2])
        else:
            scores(1, 0, bufs[(i + 1) % 2])
        accumulate_diagonal(i, bufs[i % 2])

    def advance(i, j):
        wraps = j + 1 == i
        i_nxt = jnp.minimum(jnp.where(wraps, i + 1, i), nq - 1)
        return i_nxt, jnp.where(wraps, 0, j + 1)

    def body(_, carry):
        i, j = carry
        for u in range(PHASE_B_UNROLL):
            i_nxt, j_nxt = advance(i, j)
            scores(i_nxt, j_nxt, bufs[(nq + u + 1) % 2])
            accumulate(i, j, bufs[(nq + u) % 2])
            i, j = i_nxt, j_nxt
        return i, j

    lax.fori_loop(0, n_low // PHASE_B_UNROLL, body, (jnp.int32(1), jnp.int32(0)))

    for i in range(nq):
        o_ref[i * tq:(i + 1) * tq, :] = (acc_ref[i] / l_ref[i]).T.astype(o_ref.dtype)


def _attention(q_arr, k_arr, vt_arr, cum_k=None, cum_q=None, *, batch, seq, heads, dk, dv,
               q_col, k_col, mask_chunk):
    fox = cum_k is not None
    tq = ATTN_TILE
    nq = seq // tq
    in_specs = [pl.BlockSpec((seq, dk), lambda b, h: (b, q_col + h)),
                pl.BlockSpec((seq, dk), lambda b, h: (b, k_col + h)),
                pl.BlockSpec((1, nq, dv, tq), lambda b, h: (b, 0, h, 0))]
    args = [q_arr, k_arr, vt_arr]
    if fox:
        in_specs += [pl.BlockSpec((1, 1, seq, LANES), lambda b, h: (b, h, 0, 0)),
                     pl.BlockSpec((1, nq, heads, tq), lambda b, h: (b, 0, 0, 0))]
        args += [cum_k, cum_q]
    vmem = (2 * seq * (2 * dk + 2 * dv) * 2 + 2 * seq * LANES * 4 + 2 * seq * heads * 4
            + seq * dv * 4 + 8 * tq * tq * 4)
    return pl.pallas_call(
        functools.partial(_attn_kernel, mask_chunk=mask_chunk, fox=fox),
        grid=(batch, heads),
        in_specs=in_specs,
        out_specs=pl.BlockSpec((seq, dv), lambda b, h: (b, h)),
        out_shape=jax.ShapeDtypeStruct((batch * seq, heads * dv), BF16),
        scratch_shapes=[pltpu.VMEM((tq, tq), F32), pltpu.VMEM((tq, tq), F32),
                        pltpu.VMEM((nq, 1, tq), F32), pltpu.VMEM((nq, 1, tq), F32),
                        pltpu.VMEM((nq, dv, tq), F32)],
        compiler_params=_params(vmem),
        name="fox_attn" if fox else "mla_attn",
    )(*args)


def _outproj_kernel(x_ref, oa_ref, ob_ref, wa_ref, wb_ref, gpost_ref, gt_ref, o_ref):
    y = jnp.dot(oa_ref[...], wa_ref[...], preferred_element_type=F32)
    y = y + jnp.dot(ob_ref[...], wb_ref[...], preferred_element_type=F32)
    o_ref[...] = x_ref[...] + gt_ref[0] * _rms(y, gpost_ref[...])


def _outproj(x, o_a, o_b, w_a, w_b, g_post, gt, *, seq, tm):
    t, d = x.shape
    ka, kb = o_a.shape[1], o_b.shape[1]
    tpb = seq // tm
    vmem = 4 * tm * d * 4 + 2 * tm * (ka + kb) * 2 + 2 * (ka + kb) * d * 2 + 2 * tm * d * 4
    return pl.pallas_call(
        _outproj_kernel,
        grid=(t // tm,),
        in_specs=[pl.BlockSpec((tm, d), lambda i: (i, 0)),
                  pl.BlockSpec((tm, ka), lambda i: (i, 0)),
                  pl.BlockSpec((tm, kb), lambda i: (i, 0)),
                  pl.BlockSpec((ka, d), lambda i: (0, 0)),
                  pl.BlockSpec((kb, d), lambda i: (0, 0)),
                  pl.BlockSpec((1, d), lambda i: (0, 0)),
                  pl.BlockSpec((1, 1, d), lambda i: (i // tpb, 0, 0))],
        out_specs=pl.BlockSpec((tm, d), lambda i: (i, 0)),
        out_shape=jax.ShapeDtypeStruct((t, d), F32),
        compiler_params=_params(vmem),
        name="out_proj",
    )(x, o_a, o_b, w_a, w_b, g_post, gt)


def _rope_tables(seq):
    half = MLA_ROPE // 2
    inv = ROPE_THETA ** (-jnp.arange(half, dtype=F32) / half)
    ang = jnp.arange(seq).astype(F32)[:, None] * inv[None, :]
    cos, sin = jnp.cos(ang), jnp.sin(ang)
    zero = jnp.zeros_like(cos)
    pad = jnp.zeros((seq, LANES - 2 * half), F32)
    return (jnp.concatenate([cos, cos, pad], axis=1),
            jnp.concatenate([-sin, zero, pad], axis=1),
            jnp.concatenate([zero, sin, pad], axis=1))


def kernel(x, c, w_ada, b_ada, g_ffn1_pre, g_ffn1_post, w1_gate, w1_up, w1_down, g_mix_pre,
           g_mix_post, w_in, b_forget, g_q_a, w_uq, g_kv_a, w_ukv, w_o, g_ffn2_pre, g_ffn2_post,
           w2_gate, w2_up, w2_down):
    batch, seq, d = x.shape
    depth = w_ada.shape[0]
    t = batch * seq
    xt = x.reshape(t, d)
    c_pad = jnp.pad(c, ((0, 8 - batch), (0, 0)))
    cos_t, sin_lo, sin_hi = _rope_tables(seq)

    for l in range(depth):
        mod = _ada(c_pad, w_ada[l], b_ada[l:l + 1], tn=1024)[:batch]
        sh1, sc1, gt1, sh2, sc2, gt2, sh3, sc3, gt3 = [
            mod[:, n * d:(n + 1) * d].reshape(batch, 1, d) for n in range(9)]

        xt = _ffn(xt, sh1, sc1, gt1, g_ffn1_pre[l:l + 1], g_ffn1_post[l:l + 1],
                  w1_gate[l].astype(BF16), w1_up[l].astype(BF16), w1_down[l].astype(BF16),
                  seq=seq, tm=512, tf=512, res_weight=0.5)

        wi = w_in[l]
        fox0 = KPE_COL + MLA_ROPE
        fv0 = fox0 + 2 * FOX_WIDTH
        fl0 = fox0 + 3 * FOX_WIDTH
        w_in_p = jnp.concatenate(
            [wi[:, :fox0], jnp.zeros((d, LANES - MLA_ROPE), F32),
             wi[:, fl0:], jnp.zeros((d, LANES - FOX_HEADS), F32),
             wi[:, fox0:fv0]], axis=1).astype(BF16)
        w_fvt = wi[:, fv0:fl0].T.astype(BF16)
        lat, fqk, fvt = _inproj(xt, sh2, sc2, g_mix_pre[l:l + 1], w_in_p, w_fvt, batch=batch, seq=seq)

        wuq_p = jnp.pad(w_uq[l].reshape(MLA_Q_RANK, MLA_HEADS, MLA_NOPE + MLA_ROPE),
                        ((0, 0), (0, 0), (0, MLA_QK_PAD - MLA_NOPE - MLA_ROPE))
                        ).reshape(MLA_Q_RANK, MLA_HEADS * MLA_QK_PAD).astype(BF16)
        wukv = w_ukv[l].reshape(MLA_KV_RANK, MLA_HEADS, MLA_NOPE + MLA_V)
        wuk = wukv[:, :, :MLA_NOPE].reshape(MLA_KV_RANK, -1).astype(BF16)
        wuvt = wukv[:, :, MLA_NOPE:].reshape(MLA_KV_RANK, -1).T.astype(BF16)
        bf_pad = jnp.pad(b_forget[l:l + 1], ((0, 0), (0, LANES - FOX_HEADS)))
        q_mla, k_mla, vt_mla, cum_k, cum_q = _prep(
            lat, g_q_a[l:l + 1], g_kv_a[l:l + 1], wuq_p, wuk, wuvt, bf_pad, cos_t, sin_lo, sin_hi,
            batch=batch, seq=seq)

        o_mla = _attention(q_mla, k_mla, vt_mla, batch=batch, seq=seq, heads=MLA_HEADS,
                           dk=MLA_QK_PAD, dv=MLA_V, q_col=0, k_col=0, mask_chunk=CHUNK)
        o_fox = _attention(fqk, fqk, fvt, cum_k, cum_q, batch=batch, seq=seq, heads=FOX_HEADS,
                           dk=FOX_DIM, dv=FOX_DIM, q_col=0, k_col=FOX_HEADS, mask_chunk=1)

        wo = w_o[l].astype(BF16)
        xt = _outproj(xt, o_mla, o_fox, wo[:MLA_HEADS * MLA_V], wo[MLA_HEADS * MLA_V:],
                      g_mix_post[l:l + 1], gt2, seq=seq, tm=512)

        xt = _ffn(xt, sh3, sc3, gt3, g_ffn2_pre[l:l + 1], g_ffn2_post[l:l + 1],
                  w2_gate[l].astype(BF16), w2_up[l].astype(BF16), w2_down[l].astype(BF16),
                  seq=seq, tm=512, tf=512, res_weight=0.5)

    return xt.reshape(batch, seq, d)
```

```python
import functools
import math

import jax
import jax.numpy as jnp
from jax import lax
from jax.experimental import pallas as pl
from jax.experimental.pallas import tpu as pltpu

F32 = jnp.float32
BF16 = jnp.bfloat16

V7X_VMEM_BYTES = 64 * 1024 * 1024
LANES = 128

EPS = 1e-6
ROPE_THETA = 10000.0
CHUNK = 64
MLA_HEADS = 8
MLA_Q_RANK = 512
MLA_KV_RANK = 256
MLA_NOPE = 128
MLA_ROPE = 64
MLA_V = 128
MLA_QK_PAD = 256
FOX_HEADS = 8
FOX_DIM = 128
FOX_WIDTH = FOX_HEADS * FOX_DIM
LATENT_WIDTH = 1024
KPE_COL = MLA_Q_RANK + MLA_KV_RANK
FLOGIT_COL = KPE_COL + LANES
ATTN_TILE = 512
PHASE_B_UNROLL = 4
LOG2E = math.log2(math.e)
MLA_Q_SCALE = LOG2E / math.sqrt(MLA_NOPE + MLA_ROPE)
FOX_Q_SCALE = LOG2E / math.sqrt(FOX_DIM)
NEG_BIG = -1e30

_NT = (((1,), (1,)), ((), ()))


def _params(vmem_bytes):
    limit = min(int(vmem_bytes * 1.25) + (4 << 20), V7X_VMEM_BYTES - (8 << 20))
    return pltpu.CompilerParams(vmem_limit_bytes=limit)


def _rms(x, g):
    return x * lax.rsqrt(jnp.mean(x * x, axis=-1, keepdims=True) + EPS) * g


STAT_ROWS = 8
APPLY_ROWS = 16


def _rms_stats(src_ref, stat_ref):
    tm, d = src_ref.shape

    def step(c, carry):
        rows = pl.ds(pl.multiple_of(c * STAT_ROWS, STAT_ROWS), STAT_ROWS)
        x = src_ref[rows, :]
        ms = jnp.sum(x * x, axis=-1, keepdims=True) * (1.0 / d)
        stat_ref[rows, :] = jnp.broadcast_to(lax.rsqrt(ms + EPS), (STAT_ROWS, LANES))
        return carry

    lax.fori_loop(0, tm // STAT_ROWS, step, 0, unroll=32)


def _for_row_blocks(tm, fn):
    def step(c, carry):
        fn(pl.ds(pl.multiple_of(c * APPLY_ROWS, APPLY_ROWS), APPLY_ROWS))
        return carry

    lax.fori_loop(0, tm // APPLY_ROWS, step, 0, unroll=4)


def _modulated_norm(x_ref, stat_ref, coef_ref, gpre_ref, sc_ref, sh_ref, h_ref):
    tm, d = x_ref.shape
    coef_ref[0:1, :] = gpre_ref[...] * (1.0 + sc_ref[0])
    coef_ref[1:2, :] = sh_ref[0]
    _rms_stats(x_ref, stat_ref)

    def apply(rows):
        r = stat_ref[rows, :]
        for k in range(d // LANES):
            cols = slice(k * LANES, (k + 1) * LANES)
            h = x_ref[rows, cols] * r * coef_ref[0:1, cols] + coef_ref[1:2, cols]
            h_ref[rows, cols] = h.astype(BF16)

    _for_row_blocks(tm, apply)


def _gated_norm_residual(x_ref, o_ref, stat_ref, coef_ref, gpost_ref, gt_ref, res_weight):
    tm, d = o_ref.shape
    coef_ref[0:1, :] = res_weight * gt_ref[0] * gpost_ref[...]
    _rms_stats(o_ref, stat_ref)

    def apply(rows):
        r = stat_ref[rows, :]
        for k in range(d // LANES):
            cols = slice(k * LANES, (k + 1) * LANES)
            o_ref[rows, cols] = x_ref[rows, cols] + o_ref[rows, cols] * r * coef_ref[0:1, cols]

    _for_row_blocks(tm, apply)


def _ada_kernel(c_ref, w_ref, b_ref, o_ref):
    c = c_ref[...]
    cond = (c * jax.nn.sigmoid(c)).astype(BF16)
    o_ref[...] = jnp.dot(cond, w_ref[...].astype(BF16), preferred_element_type=F32) + b_ref[...]


def _ada(c_pad, w, b, *, tn):
    m, d = c_pad.shape
    n = w.shape[1]
    vmem = 2 * d * tn * 4 + d * tn * 2 + 4 * m * tn * 4
    return pl.pallas_call(
        _ada_kernel,
        grid=(n // tn,),
        in_specs=[pl.BlockSpec((m, d), lambda j: (0, 0)),
                  pl.BlockSpec((d, tn), lambda j: (0, j)),
                  pl.BlockSpec((1, tn), lambda j: (0, j))],
        out_specs=pl.BlockSpec((m, tn), lambda j: (0, j)),
        out_shape=jax.ShapeDtypeStruct((m, n), F32),
        compiler_params=_params(vmem),
        name="ada",
    )(c_pad, w, b)


def _ffn_kernel(x_ref, sh_ref, sc_ref, gt_ref, gpre_ref, gpost_ref, wg_ref, wu_ref, wd_ref,
                o_ref, h_ref, stat_ref, coef_ref, *, res_weight):
    f = pl.program_id(1)

    @pl.when(f == 0)
    def _():
        _modulated_norm(x_ref, stat_ref, coef_ref, gpre_ref, sc_ref, sh_ref, h_ref)
        o_ref[...] = jnp.zeros_like(o_ref)

    h = h_ref[...]
    g = jnp.dot(h, wg_ref[...], preferred_element_type=F32)
    u = jnp.dot(h, wu_ref[...], preferred_element_type=F32)
    a = (g * jax.nn.sigmoid(g) * u).astype(BF16)
    o_ref[...] += jnp.dot(a, wd_ref[...], preferred_element_type=F32)

    @pl.when(f == pl.num_programs(1) - 1)
    def _():
        _gated_norm_residual(x_ref, o_ref, stat_ref, coef_ref, gpost_ref, gt_ref, res_weight)


def _ffn(x, sh, sc, gt, g_pre, g_post, wg, wu, wd, *, seq, tm, tf, res_weight):
    t, d = x.shape
    ff = wg.shape[1]
    tpb = seq // tm
    vmem = 4 * tm * d * 4 + tm * d * 2 + 6 * d * tf * 2 + 3 * tm * tf * 4
    mod_spec = pl.BlockSpec((1, 1, d), lambda i, f: (i // tpb, 0, 0))
    gain_spec = pl.BlockSpec((1, d), lambda i, f: (0, 0))
    return pl.pallas_call(
        functools.partial(_ffn_kernel, res_weight=res_weight),
        grid=(t // tm, ff // tf),
        in_specs=[pl.BlockSpec((tm, d), lambda i, f: (i, 0)),
                  mod_spec, mod_spec, mod_spec, gain_spec, gain_spec,
                  pl.BlockSpec((d, tf), lambda i, f: (0, f)),
                  pl.BlockSpec((d, tf), lambda i, f: (0, f)),
                  pl.BlockSpec((tf, d), lambda i, f: (f, 0))],
        out_specs=pl.BlockSpec((tm, d), lambda i, f: (i, 0)),
        out_shape=jax.ShapeDtypeStruct((t, d), F32),
        scratch_shapes=[pltpu.VMEM((tm, d), BF16), pltpu.VMEM((tm, LANES), F32), pltpu.VMEM((8, d), F32)],
        compiler_params=_params(vmem),
        name="ffn",
    )(x, sh, sc, gt, g_pre, g_post, wg, wu, wd)


def _inproj_kernel(x_ref, sh_ref, sc_ref, gpre_ref, w_ref, wvt_ref, lat_ref, fqk_ref, fvt_ref,
                   h_ref, stat_ref, coef_ref):
    j = pl.program_id(1)

    @pl.when(j == 0)
    def _():
        _modulated_norm(x_ref, stat_ref, coef_ref, gpre_ref, sc_ref, sh_ref, h_ref)

    @pl.when(j == 0)
    def _():
        lat_ref[...] = jnp.dot(h_ref[...], w_ref[...], preferred_element_type=F32)

    @pl.when(j == 1)
    def _():
        fq = jnp.dot(h_ref[...], w_ref[...], preferred_element_type=F32) * FOX_Q_SCALE
        fqk_ref[...] = fq.astype(BF16)

    @pl.when(j == 2)
    def _():
        fqk_ref[...] = jnp.dot(h_ref[...], w_ref[...], preferred_element_type=F32).astype(BF16)

    @pl.when(j == 3)
    def _():
        fvt_ref[0, 0] = lax.dot_general(wvt_ref[...], h_ref[...], _NT,
                                        preferred_element_type=F32).astype(BF16)


def _inproj(x, sh, sc, g_pre, w, wvt, *, batch, seq):
    t, d = x.shape
    tm = ATTN_TILE
    tn = LATENT_WIDTH
    tpb = seq // tm
    assert w.shape[1] == 3 * tn and wvt.shape == (FOX_WIDTH, d)
    vmem = (2 * tm * d * 4 + tm * d * 2 + 2 * d * tn * 2 + 2 * FOX_WIDTH * d * 2 + 2 * tm * tn * 4
            + 4 * tm * tn * 2 + tm * tn * 4)
    mod_spec = pl.BlockSpec((1, 1, d), lambda i, j: (i // tpb, 0, 0))
    return pl.pallas_call(
        _inproj_kernel,
        grid=(t // tm, 4),
        in_specs=[pl.BlockSpec((tm, d), lambda i, j: (i, 0)),
                  mod_spec, mod_spec,
                  pl.BlockSpec((1, d), lambda i, j: (0, 0)),
                  pl.BlockSpec((d, tn), lambda i, j: (0, jnp.minimum(j, 2))),
                  pl.BlockSpec((FOX_WIDTH, d), lambda i, j: (0, 0))],
        out_specs=[pl.BlockSpec((tm, tn), lambda i, j: (i, 0)),
                   pl.BlockSpec((tm, tn), lambda i, j: (i, jnp.clip(j - 1, 0, 1))),
                   pl.BlockSpec((1, 1, FOX_WIDTH, tm), lambda i, j: (i // tpb, i % tpb, 0, 0))],
        out_shape=[jax.ShapeDtypeStruct((t, tn), F32),
                   jax.ShapeDtypeStruct((t, 2 * tn), BF16),
                   jax.ShapeDtypeStruct((batch, tpb, FOX_WIDTH, tm), BF16)],
        scratch_shapes=[pltpu.VMEM((tm, d), BF16), pltpu.VMEM((tm, LANES), F32), pltpu.VMEM((8, d), F32)],
        compiler_params=_params(vmem),
        name="in_proj",
    )(x, sh, sc, g_pre, w, wvt)


def _rot(r, cos_t, sin_lo, sin_hi):
    return r * cos_t + pltpu.roll(r, 96, 1) * sin_lo + pltpu.roll(r, 32, 1) * sin_hi


def _prep_kernel(lat_ref, gq_ref, gkv_ref, wuq_ref, wuk_ref, wuvt_ref, bf_ref, cos_ref, slo_ref, shi_ref,
                 q_ref, k_ref, vt_ref, ck_ref, cq_ref, carry_ref, *, tiles_per_batch):
    i = pl.program_id(0)
    tm = lat_ref.shape[0]
    cos_t, sin_lo, sin_hi = cos_ref[...], slo_ref[...], shi_ref[...]

    qn = _rms(lat_ref[:, :MLA_Q_RANK], gq_ref[...]).astype(BF16)
    q = jnp.dot(qn, wuq_ref[...], preferred_element_type=F32) * MLA_Q_SCALE
    kvn = _rms(lat_ref[:, MLA_Q_RANK:KPE_COL], gkv_ref[...]).astype(BF16)
    k_nope = jnp.dot(kvn, wuk_ref[...], preferred_element_type=F32)
    vt_ref[0, 0] = lax.dot_general(wuvt_ref[...], kvn, _NT, preferred_element_type=F32).astype(BF16)
    k_rope = _rot(lat_ref[:, KPE_COL:KPE_COL + LANES], cos_t, sin_lo, sin_hi).astype(BF16)
    for h in range(MLA_HEADS):
        c0 = h * MLA_QK_PAD
        q_ref[:, c0:c0 + MLA_NOPE] = q[:, c0:c0 + MLA_NOPE].astype(BF16)
        q_ref[:, c0 + MLA_NOPE:c0 + MLA_QK_PAD] = _rot(
            q[:, c0 + MLA_NOPE:c0 + MLA_QK_PAD], cos_t, sin_lo, sin_hi).astype(BF16)
        k_ref[:, c0:c0 + MLA_NOPE] = k_nope[:, h * MLA_NOPE:(h + 1) * MLA_NOPE].astype(BF16)
        k_ref[:, c0 + MLA_NOPE:c0 + MLA_QK_PAD] = k_rope

    @pl.when(i % tiles_per_batch == 0)
    def _():
        carry_ref[...] = jnp.zeros_like(carry_ref)

    z = lat_ref[:, FLOGIT_COL:FLOGIT_COL + LANES] + bf_ref[...]
    log_f = jnp.minimum(z, 0.0) - jnp.log1p(jnp.exp(-jnp.abs(z)))
    row = lax.broadcasted_iota(jnp.int32, (tm, tm), 0)
    col = lax.broadcasted_iota(jnp.int32, (tm, tm), 1)
    tri = (col <= row).astype(F32)
    cum = jnp.dot(tri, log_f, precision=lax.Precision.HIGHEST,
                  preferred_element_type=F32) + carry_ref[...]
    carry_ref[...] = cum[tm - 1:tm, :]
    cum2 = cum * LOG2E
    cq_ref[0, 0] = cum2.T[:FOX_HEADS, :]
    for h in range(FOX_HEADS):
        ck_ref[0, h] = jnp.broadcast_to(cum2[:, h:h + 1], (tm, LANES))


def _prep(lat, g_q, g_kv, wuq, wuk, wuvt, b_forget, cos_t, sin_lo, sin_hi, *, batch, seq):
    t = lat.shape[0]
    tm = ATTN_TILE
    tpb = seq // tm
    qk_w = MLA_HEADS * MLA_QK_PAD
    v_w = MLA_HEADS * MLA_V
    vmem = (2 * tm * LATENT_WIDTH * 4 + 2 * (wuq.size + wuk.size + wuvt.size) * 2 + 6 * tm * LANES * 4
            + 2 * (2 * tm * qk_w + tm * v_w) * 2 + 2 * FOX_HEADS * tm * LANES * 4
            + tm * (qk_w + 2 * v_w) * 4 + 3 * tm * tm * 4)
    full = lambda a: pl.BlockSpec(a.shape, lambda i: (0, 0))
    tab_spec = pl.BlockSpec((tm, LANES), lambda i: (i % tpb, 0))
    return pl.pallas_call(
        functools.partial(_prep_kernel, tiles_per_batch=tpb),
        grid=(t // tm,),
        in_specs=[pl.BlockSpec((tm, LATENT_WIDTH), lambda i: (i, 0)),
                  full(g_q), full(g_kv), full(wuq), full(wuk), full(wuvt), full(b_forget),
                  tab_spec, tab_spec, tab_spec],
        out_specs=[pl.BlockSpec((tm, qk_w), lambda i: (i, 0)),
                   pl.BlockSpec((tm, qk_w), lambda i: (i, 0)),
                   pl.BlockSpec((1, 1, v_w, tm), lambda i: (i // tpb, i % tpb, 0, 0)),
                   pl.BlockSpec((1, FOX_HEADS, tm, LANES), lambda i: (i // tpb, 0, i % tpb, 0)),
                   pl.BlockSpec((1, 1, FOX_HEADS, tm), lambda i: (i // tpb, i % tpb, 0, 0))],
        out_shape=[jax.ShapeDtypeStruct((t, qk_w), BF16),
                   jax.ShapeDtypeStruct((t, qk_w), BF16),
                   jax.ShapeDtypeStruct((batch, tpb, v_w, tm), BF16),
                   jax.ShapeDtypeStruct((batch, FOX_HEADS, seq, LANES), F32),
                   jax.ShapeDtypeStruct((batch, tpb, FOX_HEADS, tm), F32)],
        scratch_shapes=[pltpu.VMEM((1, LANES), F32)],
        compiler_params=_params(vmem),
        name="mla_fox_prep",
    )(lat, g_q, g_kv, wuq, wuk, wuvt, b_forget, cos_t, sin_lo, sin_hi)


def _attn_kernel(*refs, mask_chunk, fox):
    n_in = 5 if fox else 3
    n_cast = (len(refs) - n_in - 6) // 2
    cast_src = refs[n_in:n_in + n_cast]
    cast_dst = refs[n_in + n_cast + 1:n_in + 2 * n_cast + 1]
    refs = refs[:n_in] + (refs[n_in + n_cast],) + refs[n_in + 2 * n_cast + 1:]
    if fox:
        q_ref, k_ref, vt_ref, ck_ref, cq_ref, o_ref, s0_ref, s1_ref, m_ref, l_ref, acc_ref = refs
    else:
        q_ref, k_ref, vt_ref, o_ref, s0_ref, s1_ref, m_ref, l_ref, acc_ref = refs
    for src, dst in zip(cast_src, cast_dst):
        dst[...] = src[...].astype(BF16)
    head = pl.program_id(1)
    tq = ATTN_TILE
    nq = q_ref.shape[0] // tq
    n_low = nq * (nq - 1) // 2
    assert PHASE_B_UNROLL % 2 == 0 and n_low % PHASE_B_UNROLL == 0
    bufs = (s0_ref, s1_ref)

    def rows(t):
        return pl.ds(pl.multiple_of(t * tq, tq), tq)

    def scores(i, j, s_ref):
        s = lax.dot_general(k_ref[rows(j), :], q_ref[rows(i), :], _NT, preferred_element_type=F32)
        if fox:
            cum_q = cq_ref[0, i, pl.ds(head, 1), :]
            cum_k = ck_ref[0, 0, rows(j), :]
            s = s + (cum_q - jnp.concatenate([cum_k] * (tq // LANES), axis=1))
        s_ref[...] = s

    def accumulate_diagonal(i, s_ref):
        src = lax.broadcasted_iota(jnp.int32, (tq, tq), 0) // mask_chunk
        dst = lax.broadcasted_iota(jnp.int32, (tq, tq), 1) // mask_chunk
        s = jnp.where(src <= dst, s_ref[...], NEG_BIG)
        m = jnp.max(s, axis=0, keepdims=True)
        p = jnp.exp2(s - m)
        m_ref[i] = m
        l_ref[i] = jnp.sum(p, axis=0, keepdims=True)
        acc_ref[i] = jnp.dot(vt_ref[0, i], p.astype(BF16), preferred_element_type=F32)

    def accumulate(i, j, s_ref):
        s = s_ref[...]
        m_prev = m_ref[i]
        m_new = jnp.maximum(m_prev, jnp.max(s, axis=0, keepdims=True))
        alpha = jnp.exp2(m_prev - m_new)
        p = jnp.exp2(s - m_new)
        l_ref[i] = alpha * l_ref[i] + jnp.sum(p, axis=0, keepdims=True)
        acc_ref[i] = alpha * acc_ref[i] + jnp.dot(vt_ref[0, j], p.astype(BF16),
                                                  preferred_element_type=F32)
        m_ref[i] = m_new

    scores(0, 0, bufs[0])
    for i in range(nq):
        if i + 1 < nq:
            scores(i + 1, i + 1, bufs[(i + 1) % 2])
        else:
            scores(1, 0, bufs[(i + 1) % 2])
        accumulate_diagonal(i, bufs[i % 2])

    def advance(i, j):
        wraps = j + 1 == i
        i_nxt = jnp.minimum(jnp.where(wraps, i + 1, i), nq - 1)
        return i_nxt, jnp.where(wraps, 0, j + 1)

    def body(_, carry):
        i, j = carry
        for u in range(PHASE_B_UNROLL):
            i_nxt, j_nxt = advance(i, j)
            scores(i_nxt, j_nxt, bufs[(nq + u + 1) % 2])
            accumulate(i, j, bufs[(nq + u) % 2])
            i, j = i_nxt, j_nxt
        return i, j

    lax.fori_loop(0, n_low // PHASE_B_UNROLL, body, (jnp.int32(1), jnp.int32(0)))

    for i in range(nq):
        o_ref[i * tq:(i + 1) * tq, :] = (acc_ref[i] / l_ref[i]).T.astype(o_ref.dtype)


def _attention(q_arr, k_arr, vt_arr, cum_k=None, cum_q=None, *, batch, seq, heads, dk, dv,
               q_col, k_col, mask_chunk, riders=()):
    fox = cum_k is not None
    tq = ATTN_TILE
    nq = seq // tq
    steps = batch * heads
    in_specs = [pl.BlockSpec((seq, dk), lambda b, h: (b, q_col + h)),
                pl.BlockSpec((seq, dk), lambda b, h: (b, k_col + h)),
                pl.BlockSpec((1, nq, dv, tq), lambda b, h: (b, 0, h, 0))]
    args = [q_arr, k_arr, vt_arr]
    if fox:
        in_specs += [pl.BlockSpec((1, 1, seq, LANES), lambda b, h: (b, h, 0, 0)),
                     pl.BlockSpec((1, nq, heads, tq), lambda b, h: (b, 0, 0, 0))]
        args += [cum_k, cum_q]
    slab_specs = [pl.BlockSpec((w.shape[0] // steps, w.shape[1]), lambda b, h: (b * heads + h, 0))
                  for w in riders]
    vmem = (2 * seq * (2 * dk + 2 * dv) * 2 + 2 * seq * LANES * 4 + 2 * seq * heads * 4
            + seq * dv * 4 + 8 * tq * tq * 4 + sum(2 * 6 * w.size // steps for w in riders))
    return pl.pallas_call(
        functools.partial(_attn_kernel, mask_chunk=mask_chunk, fox=fox),
        grid=(batch, heads),
        in_specs=in_specs + slab_specs,
        out_specs=[pl.BlockSpec((seq, dv), lambda b, h: (b, h))] + slab_specs,
        out_shape=[jax.ShapeDtypeStruct((batch * seq, heads * dv), BF16)]
                  + [jax.ShapeDtypeStruct(w.shape, BF16) for w in riders],
        scratch_shapes=[pltpu.VMEM((tq, tq), F32), pltpu.VMEM((tq, tq), F32),
                        pltpu.VMEM((nq, 1, tq), F32), pltpu.VMEM((nq, 1, tq), F32),
                        pltpu.VMEM((nq, dv, tq), F32)],
        compiler_params=_params(vmem),
        name="fox_attn" if fox else "mla_attn",
    )(*args, *riders)


def _outproj_kernel(x_ref, oa_ref, ob_ref, wa_ref, wb_ref, gpost_ref, gt_ref, o_ref, stat_ref, coef_ref):
    y = jnp.dot(oa_ref[...], wa_ref[...], preferred_element_type=F32)
    o_ref[...] = y + jnp.dot(ob_ref[...], wb_ref[...], preferred_element_type=F32)
    _gated_norm_residual(x_ref, o_ref, stat_ref, coef_ref, gpost_ref, gt_ref, 1.0)


def _outproj(x, o_a, o_b, w_a, w_b, g_post, gt, *, seq, tm):
    t, d = x.shape
    ka, kb = o_a.shape[1], o_b.shape[1]
    tpb = seq // tm
    vmem = 4 * tm * d * 4 + 2 * tm * (ka + kb) * 2 + 2 * (ka + kb) * d * 2 + 2 * tm * d * 4
    return pl.pallas_call(
        _outproj_kernel,
        grid=(t // tm,),
        in_specs=[pl.BlockSpec((tm, d), lambda i: (i, 0)),
                  pl.BlockSpec((tm, ka), lambda i: (i, 0)),
                  pl.BlockSpec((tm, kb), lambda i: (i, 0)),
                  pl.BlockSpec((ka, d), lambda i: (0, 0)),
                  pl.BlockSpec((kb, d), lambda i: (0, 0)),
                  pl.BlockSpec((1, d), lambda i: (0, 0)),
                  pl.BlockSpec((1, 1, d), lambda i: (i // tpb, 0, 0))],
        out_specs=pl.BlockSpec((tm, d), lambda i: (i, 0)),
        out_shape=jax.ShapeDtypeStruct((t, d), F32),
        scratch_shapes=[pltpu.VMEM((tm, LANES), F32), pltpu.VMEM((8, d), F32)],
        compiler_params=_params(vmem),
        name="out_proj",
    )(x, o_a, o_b, w_a, w_b, g_post, gt)


def _rope_tables(seq):
    half = MLA_ROPE // 2
    inv = ROPE_THETA ** (-jnp.arange(half, dtype=F32) / half)
    ang = jnp.arange(seq).astype(F32)[:, None] * inv[None, :]
    cos, sin = jnp.cos(ang), jnp.sin(ang)
    zero = jnp.zeros_like(cos)
    pad = jnp.zeros((seq, LANES - 2 * half), F32)
    return (jnp.concatenate([cos, cos, pad], axis=1),
            jnp.concatenate([-sin, zero, pad], axis=1),
            jnp.concatenate([zero, sin, pad], axis=1))


def kernel(x, c, w_ada, b_ada, g_ffn1_pre, g_ffn1_post, w1_gate, w1_up, w1_down, g_mix_pre,
           g_mix_post, w_in, b_forget, g_q_a, w_uq, g_kv_a, w_ukv, w_o, g_ffn2_pre, g_ffn2_post,
           w2_gate, w2_up, w2_down):
    batch, seq, d = x.shape
    depth = w_ada.shape[0]
    t = batch * seq
    xt = x.reshape(t, d)
    c_pad = jnp.pad(c, ((0, 8 - batch), (0, 0)))
    cos_t, sin_lo, sin_hi = _rope_tables(seq)

    for l in range(depth):
        mod = _ada(c_pad, w_ada[l], b_ada[l:l + 1], tn=1024)[:batch]
        sh1, sc1, gt1, sh2, sc2, gt2, sh3, sc3, gt3 = [
            mod[:, n * d:(n + 1) * d].reshape(batch, 1, d) for n in range(9)]

        xt = _ffn(xt, sh1, sc1, gt1, g_ffn1_pre[l:l + 1], g_ffn1_post[l:l + 1],
                  w1_gate[l].astype(BF16), w1_up[l].astype(BF16), w1_down[l].astype(BF16),
                  seq=seq, tm=512, tf=512, res_weight=0.5)

        wi = w_in[l]
        fox0 = KPE_COL + MLA_ROPE
        fv0 = fox0 + 2 * FOX_WIDTH
        fl0 = fox0 + 3 * FOX_WIDTH
        w_in_p = jnp.concatenate(
            [wi[:, :fox0], jnp.zeros((d, LANES - MLA_ROPE), F32),
             wi[:, fl0:], jnp.zeros((d, LANES - FOX_HEADS), F32),
             wi[:, fox0:fv0]], axis=1).astype(BF16)
        w_fvt = wi[:, fv0:fl0].T.astype(BF16)
        lat, fqk, fvt = _inproj(xt, sh2, sc2, g_mix_pre[l:l + 1], w_in_p, w_fvt, batch=batch, seq=seq)

        wuq_p = jnp.pad(w_uq[l].reshape(MLA_Q_RANK, MLA_HEADS, MLA_NOPE + MLA_ROPE),
                        ((0, 0), (0, 0), (0, MLA_QK_PAD - MLA_NOPE - MLA_ROPE))
                        ).reshape(MLA_Q_RANK, MLA_HEADS * MLA_QK_PAD).astype(BF16)
        wukv = w_ukv[l].reshape(MLA_KV_RANK, MLA_HEADS, MLA_NOPE + MLA_V)
        wuk = wukv[:, :, :MLA_NOPE].reshape(MLA_KV_RANK, -1).astype(BF16)
        wuvt = wukv[:, :, MLA_NOPE:].reshape(MLA_KV_RANK, -1).T.astype(BF16)
        bf_pad = jnp.pad(b_forget[l:l + 1], ((0, 0), (0, LANES - FOX_HEADS)))
        q_mla, k_mla, vt_mla, cum_k, cum_q = _prep(
            lat, g_q_a[l:l + 1], g_kv_a[l:l + 1], wuq_p, wuk, wuvt, bf_pad, cos_t, sin_lo, sin_hi,
            batch=batch, seq=seq)

        o_mla, w2g, w2u = _attention(q_mla, k_mla, vt_mla, batch=batch, seq=seq, heads=MLA_HEADS,
                                     dk=MLA_QK_PAD, dv=MLA_V, q_col=0, k_col=0, mask_chunk=CHUNK,
                                     riders=(w2_gate[l], w2_up[l]))
        o_fox, w2d = _attention(fqk, fqk, fvt, cum_k, cum_q, batch=batch, seq=seq, heads=FOX_HEADS,
                                dk=FOX_DIM, dv=FOX_DIM, q_col=0, k_col=FOX_HEADS, mask_chunk=1,
                                riders=(w2_down[l],))

        wo = w_o[l].astype(BF16)
        xt = _outproj(xt, o_mla, o_fox, wo[:MLA_HEADS * MLA_V], wo[MLA_HEADS * MLA_V:],
                      g_mix_post[l:l + 1], gt2, seq=seq, tm=512)

        xt = _ffn(xt, sh3, sc3, gt3, g_ffn2_pre[l:l + 1], g_ffn2_post[l:l + 1],
                  w2g, w2u, w2d, seq=seq, tm=512, tf=512, res_weight=0.5)

    return xt.reshape(batch, seq, d)
```

```python
import functools
import math

import jax
import jax.numpy as jnp
from jax import lax
from jax.experimental import pallas as pl
from jax.experimental.pallas import tpu as pltpu

F32 = jnp.float32
BF16 = jnp.bfloat16

V7X_VMEM_BYTES = 64 * 1024 * 1024
LANES = 128

EPS = 1e-6
ROPE_THETA = 10000.0
CHUNK = 64
MLA_HEADS = 8
MLA_Q_RANK = 512
MLA_KV_RANK = 256
MLA_NOPE = 128
MLA_ROPE = 64
MLA_V = 128
MLA_QK_PAD = 256
FOX_HEADS = 8
FOX_DIM = 128
FOX_WIDTH = FOX_HEADS * FOX_DIM
LATENT_WIDTH = 1024
KPE_COL = MLA_Q_RANK + MLA_KV_RANK
FLOGIT_COL = KPE_COL + LANES
ATTN_TILE = 512
PHASE_B_UNROLL = 4
LOG2E = math.log2(math.e)
MLA_Q_SCALE = LOG2E / math.sqrt(MLA_NOPE + MLA_ROPE)
FOX_Q_SCALE = LOG2E / math.sqrt(FOX_DIM)
NEG_BIG = -1e30

_NT = (((1,), (1,)), ((), ()))


def _params(vmem_bytes):
    limit = min(int(vmem_bytes * 1.25) + (4 << 20), V7X_VMEM_BYTES - (8 << 20))
    return pltpu.CompilerParams(vmem_limit_bytes=limit)


def _rms(x, g):
    return x * lax.rsqrt(jnp.mean(x * x, axis=-1, keepdims=True) + EPS) * g


STAT_ROWS = 8
APPLY_ROWS = 16


def _rms_stats(src_ref, stat_ref):
    tm, d = src_ref.shape

    def step(c, carry):
        rows = pl.ds(pl.multiple_of(c * STAT_ROWS, STAT_ROWS), STAT_ROWS)
        x = src_ref[rows, :]
        ms = jnp.sum(x * x, axis=-1, keepdims=True) * (1.0 / d)
        stat_ref[rows, :] = jnp.broadcast_to(lax.rsqrt(ms + EPS), (STAT_ROWS, LANES))
        return carry

    lax.fori_loop(0, tm // STAT_ROWS, step, 0, unroll=32)


def _for_row_blocks(tm, fn):
    def step(c, carry):
        fn(pl.ds(pl.multiple_of(c * APPLY_ROWS, APPLY_ROWS), APPLY_ROWS))
        return carry

    lax.fori_loop(0, tm // APPLY_ROWS, step, 0, unroll=4)


def _modulated_norm(x_ref, stat_ref, coef_ref, gpre_ref, sc_ref, sh_ref, h_ref):
    tm, d = x_ref.shape
    coef_ref[0:1, :] = gpre_ref[...] * (1.0 + sc_ref[0])
    coef_ref[1:2, :] = sh_ref[0]
    _rms_stats(x_ref, stat_ref)

    def apply(rows):
        r = stat_ref[rows, :]
        for k in range(d // LANES):
            cols = slice(k * LANES, (k + 1) * LANES)
            h = x_ref[rows, cols] * r * coef_ref[0:1, cols] + coef_ref[1:2, cols]
            h_ref[rows, cols] = h.astype(BF16)

    _for_row_blocks(tm, apply)


def _gated_norm_residual(x_ref, o_ref, stat_ref, coef_ref, gpost_ref, gt_ref, res_weight):
    tm, d = o_ref.shape
    coef_ref[0:1, :] = res_weight * gt_ref[0] * gpost_ref[...]
    _rms_stats(o_ref, stat_ref)

    def apply(rows):
        r = stat_ref[rows, :]
        for k in range(d // LANES):
            cols = slice(k * LANES, (k + 1) * LANES)
            o_ref[rows, cols] = x_ref[rows, cols] + o_ref[rows, cols] * r * coef_ref[0:1, cols]

    _for_row_blocks(tm, apply)


def _ada_kernel(c_ref, w_ref, b_ref, o_ref):
    c = c_ref[...]
    cond = (c * jax.nn.sigmoid(c)).astype(BF16)
    o_ref[...] = jnp.dot(cond, w_ref[...].astype(BF16), preferred_element_type=F32) + b_ref[...]


def _ada(c_pad, w, b, *, tn):
    m, d = c_pad.shape
    n = w.shape[1]
    vmem = 2 * d * tn * 4 + d * tn * 2 + 4 * m * tn * 4
    return pl.pallas_call(
        _ada_kernel,
        grid=(n // tn,),
        in_specs=[pl.BlockSpec((m, d), lambda j: (0, 0)),
                  pl.BlockSpec((d, tn), lambda j: (0, j)),
                  pl.BlockSpec((1, tn), lambda j: (0, j))],
        out_specs=pl.BlockSpec((m, tn), lambda j: (0, j)),
        out_shape=jax.ShapeDtypeStruct((m, n), F32),
        compiler_params=_params(vmem),
        name="ada",
    )(c_pad, w, b)


def _ffn_kernel(x_ref, sh_ref, sc_ref, gt_ref, gpre_ref, gpost_ref, wg_ref, wu_ref, wd_ref,
                o_ref, h_ref, stat_ref, coef_ref, *, res_weight):
    f = pl.program_id(1)

    def swiglu_down(h):
        g = jnp.dot(h, wg_ref[...], preferred_element_type=F32)
        u = jnp.dot(h, wu_ref[...], preferred_element_type=F32)
        a = (g * jax.nn.sigmoid(g) * u).astype(BF16)
        return jnp.dot(a, wd_ref[...], preferred_element_type=F32)

    @pl.when(f == 0)
    def _():
        h = (_rms(x_ref[...], gpre_ref[...]) * (1.0 + sc_ref[0]) + sh_ref[0]).astype(BF16)
        h_ref[...] = h
        o_ref[...] = swiglu_down(h)

    @pl.when(f > 0)
    def _():
        o_ref[...] += swiglu_down(h_ref[...])

    @pl.when(f == pl.num_programs(1) - 1)
    def _():
        _gated_norm_residual(x_ref, o_ref, stat_ref, coef_ref, gpost_ref, gt_ref, res_weight)


def _ffn(x, sh, sc, gt, g_pre, g_post, wg, wu, wd, *, seq, tm, tf, res_weight):
    t, d = x.shape
    ff = wg.shape[1]
    tpb = seq // tm
    vmem = 4 * tm * d * 4 + tm * d * 2 + 6 * d * tf * 2 + 3 * tm * tf * 4
    mod_spec = pl.BlockSpec((1, 1, d), lambda i, f: (i // tpb, 0, 0))
    gain_spec = pl.BlockSpec((1, d), lambda i, f: (0, 0))
    return pl.pallas_call(
        functools.partial(_ffn_kernel, res_weight=res_weight),
        grid=(t // tm, ff // tf),
        in_specs=[pl.BlockSpec((tm, d), lambda i, f: (i, 0)),
                  mod_spec, mod_spec, mod_spec, gain_spec, gain_spec,
                  pl.BlockSpec((d, tf), lambda i, f: (0, f)),
                  pl.BlockSpec((d, tf), lambda i, f: (0, f)),
                  pl.BlockSpec((tf, d), lambda i, f: (f, 0))],
        out_specs=pl.BlockSpec((tm, d), lambda i, f: (i, 0)),
        out_shape=jax.ShapeDtypeStruct((t, d), F32),
        scratch_shapes=[pltpu.VMEM((tm, d), BF16), pltpu.VMEM((tm, LANES), F32), pltpu.VMEM((8, d), F32)],
        compiler_params=_params(vmem),
        name="ffn",
    )(x, sh, sc, gt, g_pre, g_post, wg, wu, wd)


def _inproj_kernel(x_ref, sh_ref, sc_ref, gpre_ref, w_ref, wvt_ref, lat_ref, fqk_ref, fvt_ref, h_ref):
    j = pl.program_id(1)

    @pl.when(j == 0)
    def _():
        h = (_rms(x_ref[...], gpre_ref[...]) * (1.0 + sc_ref[0]) + sh_ref[0]).astype(BF16)
        h_ref[...] = h
        lat_ref[...] = jnp.dot(h, w_ref[...], preferred_element_type=F32)

    @pl.when(j == 1)
    def _():
        fq = jnp.dot(h_ref[...], w_ref[...], preferred_element_type=F32) * FOX_Q_SCALE
        fqk_ref[...] = fq.astype(BF16)

    @pl.when(j == 2)
    def _():
        fqk_ref[...] = jnp.dot(h_ref[...], w_ref[...], preferred_element_type=F32).astype(BF16)

    @pl.when(j == 3)
    def _():
        fvt_ref[0, 0] = lax.dot_general(wvt_ref[...], h_ref[...], _NT,
                                        preferred_element_type=F32).astype(BF16)


def _inproj(x, sh, sc, g_pre, w, wvt, *, batch, seq):
    t, d = x.shape
    tm = ATTN_TILE
    tn = LATENT_WIDTH
    tpb = seq // tm
    assert w.shape[1] == 3 * tn and wvt.shape == (FOX_WIDTH, d)
    vmem = (2 * tm * d * 4 + tm * d * 2 + 2 * d * tn * 2 + 2 * FOX_WIDTH * d * 2 + 2 * tm * tn * 4
            + 4 * tm * tn * 2 + tm * tn * 4)
    mod_spec = pl.BlockSpec((1, 1, d), lambda i, j: (i // tpb, 0, 0))
    return pl.pallas_call(
        _inproj_kernel,
        grid=(t // tm, 4),
        in_specs=[pl.BlockSpec((tm, d), lambda i, j: (i, 0)),
                  mod_spec, mod_spec,
                  pl.BlockSpec((1, d), lambda i, j: (0, 0)),
                  pl.BlockSpec((d, tn), lambda i, j: (0, jnp.minimum(j, 2))),
                  pl.BlockSpec((FOX_WIDTH, d), lambda i, j: (0, 0))],
        out_specs=[pl.BlockSpec((tm, tn), lambda i, j: (i, 0)),
                   pl.BlockSpec((tm, tn), lambda i, j: (i, jnp.clip(j - 1, 0, 1))),
                   pl.BlockSpec((1, 1, FOX_WIDTH, tm), lambda i, j: (i // tpb, i % tpb, 0, 0))],
        out_shape=[jax.ShapeDtypeStruct((t, tn), F32),
                   jax.ShapeDtypeStruct((t, 2 * tn), BF16),
                   jax.ShapeDtypeStruct((batch, tpb, FOX_WIDTH, tm), BF16)],
        scratch_shapes=[pltpu.VMEM((tm, d), BF16)],
        compiler_params=_params(vmem),
        name="in_proj",
    )(x, sh, sc, g_pre, w, wvt)


def _rot(r, cos_t, sin_lo, sin_hi):
    return r * cos_t + pltpu.roll(r, 96, 1) * sin_lo + pltpu.roll(r, 32, 1) * sin_hi


def _prep_kernel(lat_ref, gq_ref, gkv_ref, wuq_ref, wuk_ref, wuvt_ref, bf_ref, cos_ref, slo_ref, shi_ref,
                 q_ref, k_ref, vt_ref, ck_ref, cq_ref, carry_ref, *, tiles_per_batch):
    i = pl.program_id(0)
    tm = lat_ref.shape[0]
    cos_t, sin_lo, sin_hi = cos_ref[...], slo_ref[...], shi_ref[...]

    qn = _rms(lat_ref[:, :MLA_Q_RANK], gq_ref[...]).astype(BF16)
    q = jnp.dot(qn, wuq_ref[...], preferred_element_type=F32) * MLA_Q_SCALE
    kvn = _rms(lat_ref[:, MLA_Q_RANK:KPE_COL], gkv_ref[...]).astype(BF16)
    k_nope = jnp.dot(kvn, wuk_ref[...], preferred_element_type=F32)
    vt_ref[0, 0] = lax.dot_general(wuvt_ref[...], kvn, _NT, preferred_element_type=F32).astype(BF16)
    k_rope = _rot(lat_ref[:, KPE_COL:KPE_COL + LANES], cos_t, sin_lo, sin_hi).astype(BF16)
    for h in range(MLA_HEADS):
        c0 = h * MLA_QK_PAD
        q_ref[:, c0:c0 + MLA_NOPE] = q[:, c0:c0 + MLA_NOPE].astype(BF16)
        q_ref[:, c0 + MLA_NOPE:c0 + MLA_QK_PAD] = _rot(
            q[:, c0 + MLA_NOPE:c0 + MLA_QK_PAD], cos_t, sin_lo, sin_hi).astype(BF16)
        k_ref[:, c0:c0 + MLA_NOPE] = k_nope[:, h * MLA_NOPE:(h + 1) * MLA_NOPE].astype(BF16)
        k_ref[:, c0 + MLA_NOPE:c0 + MLA_QK_PAD] = k_rope

    @pl.when(i % tiles_per_batch == 0)
    def _():
        carry_ref[...] = jnp.zeros_like(carry_ref)

    z = lat_ref[:, FLOGIT_COL:FLOGIT_COL + LANES] + bf_ref[...]
    log_f = jnp.minimum(z, 0.0) - jnp.log1p(jnp.exp(-jnp.abs(z)))
    row = lax.broadcasted_iota(jnp.int32, (tm, tm), 0)
    col = lax.broadcasted_iota(jnp.int32, (tm, tm), 1)
    tri = (col <= row).astype(F32)
    cum = jnp.dot(tri, log_f, precision=lax.Precision.HIGHEST,
                  preferred_element_type=F32) + carry_ref[...]
    carry_ref[...] = cum[tm - 1:tm, :]
    cum2 = cum * LOG2E
    cq_ref[0, 0] = cum2.T[:FOX_HEADS, :]
    for h in range(FOX_HEADS):
        ck_ref[0, h] = jnp.broadcast_to(cum2[:, h:h + 1], (tm, LANES))


def _prep(lat, g_q, g_kv, wuq, wuk, wuvt, b_forget, cos_t, sin_lo, sin_hi, *, batch, seq):
    t = lat.shape[0]
    tm = ATTN_TILE
    tpb = seq // tm
    qk_w = MLA_HEADS * MLA_QK_PAD
    v_w = MLA_HEADS * MLA_V
    vmem = (2 * tm * LATENT_WIDTH * 4 + 2 * (wuq.size + wuk.size + wuvt.size) * 2 + 6 * tm * LANES * 4
            + 2 * (2 * tm * qk_w + tm * v_w) * 2 + 2 * FOX_HEADS * tm * LANES * 4
            + tm * (qk_w + 2 * v_w) * 4 + 3 * tm * tm * 4)
    full = lambda a: pl.BlockSpec(a.shape, lambda i: (0, 0))
    tab_spec = pl.BlockSpec((tm, LANES), lambda i: (i % tpb, 0))
    return pl.pallas_call(
        functools.partial(_prep_kernel, tiles_per_batch=tpb),
        grid=(t // tm,),
        in_specs=[pl.BlockSpec((tm, LATENT_WIDTH), lambda i: (i, 0)),
                  full(g_q), full(g_kv), full(wuq), full(wuk), full(wuvt), full(b_forget),
                  tab_spec, tab_spec, tab_spec],
        out_specs=[pl.BlockSpec((tm, qk_w), lambda i: (i, 0)),
                   pl.BlockSpec((tm, qk_w), lambda i: (i, 0)),
                   pl.BlockSpec((1, 1, v_w, tm), lambda i: (i // tpb, i % tpb, 0, 0)),
                   pl.BlockSpec((1, FOX_HEADS, tm, LANES), lambda i: (i // tpb, 0, i % tpb, 0)),
                   pl.BlockSpec((1, 1, FOX_HEADS, tm), lambda i: (i // tpb, i % tpb, 0, 0))],
        out_shape=[jax.ShapeDtypeStruct((t, qk_w), BF16),
                   jax.ShapeDtypeStruct((t, qk_w), BF16),
                   jax.ShapeDtypeStruct((batch, tpb, v_w, tm), BF16),
                   jax.ShapeDtypeStruct((batch, FOX_HEADS, seq, LANES), F32),
                   jax.ShapeDtypeStruct((batch, tpb, FOX_HEADS, tm), F32)],
        scratch_shapes=[pltpu.VMEM((1, LANES), F32)],
        compiler_params=_params(vmem),
        name="mla_fox_prep",
    )(lat, g_q, g_kv, wuq, wuk, wuvt, b_forget, cos_t, sin_lo, sin_hi)


def _attn_kernel(*refs, mask_chunk, fox):
    n_in = 5 if fox else 3
    n_cast = (len(refs) - n_in - 6) // 2
    cast_src = refs[n_in:n_in + n_cast]
    cast_dst = refs[n_in + n_cast + 1:n_in + 2 * n_cast + 1]
    refs = refs[:n_in] + (refs[n_in + n_cast],) + refs[n_in + 2 * n_cast + 1:]
    if fox:
        q_ref, k_ref, vt_ref, ck_ref, cq_ref, o_ref, s0_ref, s1_ref, m_ref, l_ref, acc_ref = refs
    else:
        q_ref, k_ref, vt_ref, o_ref, s0_ref, s1_ref, m_ref, l_ref, acc_ref = refs
    for src, dst in zip(cast_src, cast_dst):
        dst[...] = src[...].astype(BF16)
    head = pl.program_id(1)
    tq = ATTN_TILE
    nq = q_ref.shape[0] // tq
    n_low = nq * (nq - 1) // 2
    assert PHASE_B_UNROLL % 2 == 0 and n_low % PHASE_B_UNROLL == 0
    bufs = (s0_ref, s1_ref)

    def rows(t):
        return pl.ds(pl.multiple_of(t * tq, tq), tq)

    def scores(i, j, s_ref):
        s = lax.dot_general(k_ref[rows(j), :], q_ref[rows(i), :], _NT, preferred_element_type=F32)
        if fox:
            cum_q = cq_ref[0, i, pl.ds(head, 1), :]
            cum_k = ck_ref[0, 0, rows(j), :]
            s = s + (cum_q - jnp.concatenate([cum_k] * (tq // LANES), axis=1))
        s_ref[...] = s

    def accumulate_diagonal(i, s_ref):
        src = lax.broadcasted_iota(jnp.int32, (tq, tq), 0) // mask_chunk
        dst = lax.broadcasted_iota(jnp.int32, (tq, tq), 1) // mask_chunk
        s = jnp.where(src <= dst, s_ref[...], NEG_BIG)
        m = jnp.max(s, axis=0, keepdims=True)
        p = jnp.exp2(s - m)
        m_ref[i] = m
        l_ref[i] = jnp.sum(p, axis=0, keepdims=True)
        acc_ref[i] = jnp.dot(vt_ref[0, i], p.astype(BF16), preferred_element_type=F32)

    def accumulate(i, j, s_ref):
        s = s_ref[...]
        m_prev = m_ref[i]
        m_new = jnp.maximum(m_prev, jnp.max(s, axis=0, keepdims=True))
        alpha = jnp.exp2(m_prev - m_new)
        p = jnp.exp2(s - m_new)
        l_ref[i] = alpha * l_ref[i] + jnp.sum(p, axis=0, keepdims=True)
        acc_ref[i] = alpha * acc_ref[i] + jnp.dot(vt_ref[0, j], p.astype(BF16),
                                                  preferred_element_type=F32)
        m_ref[i] = m_new

    scores(0, 0, bufs[0])
    for i in range(nq):
        if i + 1 < nq:
            scores(i + 1, i + 1, bufs[(i + 1) % 2])
        else:
            scores(1, 0, bufs[(i + 1) % 2])
        accumulate_diagonal(i, bufs[i % 2])

    def advance(i, j):
        wraps = j + 1 == i
        i_nxt = jnp.minimum(jnp.where(wraps, i + 1, i), nq - 1)
        return i_nxt, jnp.where(wraps, 0, j + 1)

    def body(_, carry):
        i, j = carry
        for u in range(PHASE_B_UNROLL):
            i_nxt, j_nxt = advance(i, j)
            scores(i_nxt, j_nxt, bufs[(nq + u + 1) % 2])
            accumulate(i, j, bufs[(nq + u) % 2])
            i, j = i_nxt, j_nxt
        return i, j

    lax.fori_loop(0, n_low // PHASE_B_UNROLL, body, (jnp.int32(1), jnp.int32(0)))

    for i in range(nq):
        o_ref[i * tq:(i + 1) * tq, :] = (acc_ref[i] / l_ref[i]).T.astype(o_ref.dtype)


def _attention(q_arr, k_arr, vt_arr, cum_k=None, cum_q=None, *, batch, seq, heads, dk, dv,
               q_col, k_col, mask_chunk, riders=()):
    fox = cum_k is not None
    tq = ATTN_TILE
    nq = seq // tq
    steps = batch * heads
    in_specs = [pl.BlockSpec((seq, dk), lambda b, h: (b, q_col + h)),
                pl.BlockSpec((seq, dk), lambda b, h: (b, k_col + h)),
                pl.BlockSpec((1, nq, dv, tq), lambda b, h: (b, 0, h, 0))]
    args = [q_arr, k_arr, vt_arr]
    if fox:
        in_specs += [pl.BlockSpec((1, 1, seq, LANES), lambda b, h: (b, h, 0, 0)),
                     pl.BlockSpec((1, nq, heads, tq), lambda b, h: (b, 0, 0, 0))]
        args += [cum_k, cum_q]
    slab_specs = [pl.BlockSpec((w.shape[0] // steps, w.shape[1]), lambda b, h: (b * heads + h, 0))
                  for w in riders]
    vmem = (2 * seq * (2 * dk + 2 * dv) * 2 + 2 * seq * LANES * 4 + 2 * seq * heads * 4
            + seq * dv * 4 + 8 * tq * tq * 4 + sum(2 * 6 * w.size // steps for w in riders))
    return pl.pallas_call(
        functools.partial(_attn_kernel, mask_chunk=mask_chunk, fox=fox),
        grid=(batch, heads),
        in_specs=in_specs + slab_specs,
        out_specs=[pl.BlockSpec((seq, dv), lambda b, h: (b, h))] + slab_specs,
        out_shape=[jax.ShapeDtypeStruct((batch * seq, heads * dv), BF16)]
                  + [jax.ShapeDtypeStruct(w.shape, BF16) for w in riders],
        scratch_shapes=[pltpu.VMEM((tq, tq), F32), pltpu.VMEM((tq, tq), F32),
                        pltpu.VMEM((nq, 1, tq), F32), pltpu.VMEM((nq, 1, tq), F32),
                        pltpu.VMEM((nq, dv, tq), F32)],
        compiler_params=_params(vmem),
        name="fox_attn" if fox else "mla_attn",
    )(*args, *riders)


def _outproj_kernel(x_ref, oa_ref, ob_ref, wa_ref, wb_ref, gpost_ref, gt_ref, o_ref):
    y = jnp.dot(oa_ref[...], wa_ref[...], preferred_element_type=F32)
    y = y + jnp.dot(ob_ref[...], wb_ref[...], preferred_element_type=F32)
    o_ref[...] = x_ref[...] + gt_ref[0] * _rms(y, gpost_ref[...])


def _outproj(x, o_a, o_b, w_a, w_b, g_post, gt, *, seq, tm):
    t, d = x.shape
    ka, kb = o_a.shape[1], o_b.shape[1]
    tpb = seq // tm
    vmem = 4 * tm * d * 4 + 2 * tm * (ka + kb) * 2 + 2 * (ka + kb) * d * 2 + 2 * tm * d * 4
    return pl.pallas_call(
        _outproj_kernel,
        grid=(t // tm,),
        in_specs=[pl.BlockSpec((tm, d), lambda i: (i, 0)),
                  pl.BlockSpec((tm, ka), lambda i: (i, 0)),
                  pl.BlockSpec((tm, kb), lambda i: (i, 0)),
                  pl.BlockSpec((ka, d), lambda i: (0, 0)),
                  pl.BlockSpec((kb, d), lambda i: (0, 0)),
                  pl.BlockSpec((1, d), lambda i: (0, 0)),
                  pl.BlockSpec((1, 1, d), lambda i: (i // tpb, 0, 0))],
        out_specs=pl.BlockSpec((tm, d), lambda i: (i, 0)),
        out_shape=jax.ShapeDtypeStruct((t, d), F32),
        compiler_params=_params(vmem),
        name="out_proj",
    )(x, o_a, o_b, w_a, w_b, g_post, gt)


def _rope_tables(seq):
    half = MLA_ROPE // 2
    inv = ROPE_THETA ** (-jnp.arange(half, dtype=F32) / half)
    ang = jnp.arange(seq).astype(F32)[:, None] * inv[None, :]
    cos, sin = jnp.cos(ang), jnp.sin(ang)
    zero = jnp.zeros_like(cos)
    pad = jnp.zeros((seq, LANES - 2 * half), F32)
    return (jnp.concatenate([cos, cos, pad], axis=1),
            jnp.concatenate([-sin, zero, pad], axis=1),
            jnp.concatenate([zero, sin, pad], axis=1))


def kernel(x, c, w_ada, b_ada, g_ffn1_pre, g_ffn1_post, w1_gate, w1_up, w1_down, g_mix_pre,
           g_mix_post, w_in, b_forget, g_q_a, w_uq, g_kv_a, w_ukv, w_o, g_ffn2_pre, g_ffn2_post,
           w2_gate, w2_up, w2_down):
    batch, seq, d = x.shape
    depth = w_ada.shape[0]
    t = batch * seq
    xt = x.reshape(t, d)
    c_pad = jnp.pad(c, ((0, 8 - batch), (0, 0)))
    cos_t, sin_lo, sin_hi = _rope_tables(seq)

    for l in range(depth):
        mod = _ada(c_pad, w_ada[l], b_ada[l:l + 1], tn=1024)[:batch]
        sh1, sc1, gt1, sh2, sc2, gt2, sh3, sc3, gt3 = [
            mod[:, n * d:(n + 1) * d].reshape(batch, 1, d) for n in range(9)]

        xt = _ffn(xt, sh1, sc1, gt1, g_ffn1_pre[l:l + 1], g_ffn1_post[l:l + 1],
                  w1_gate[l].astype(BF16), w1_up[l].astype(BF16), w1_down[l].astype(BF16),
                  seq=seq, tm=1024, tf=512, res_weight=0.5)

        wi = w_in[l]
        fox0 = KPE_COL + MLA_ROPE
        fv0 = fox0 + 2 * FOX_WIDTH
        fl0 = fox0 + 3 * FOX_WIDTH
        w_in_p = jnp.concatenate(
            [wi[:, :fox0], jnp.zeros((d, LANES - MLA_ROPE), F32),
             wi[:, fl0:], jnp.zeros((d, LANES - FOX_HEADS), F32),
             wi[:, fox0:fv0]], axis=1).astype(BF16)
        w_fvt = wi[:, fv0:fl0].T.astype(BF16)
        lat, fqk, fvt = _inproj(xt, sh2, sc2, g_mix_pre[l:l + 1], w_in_p, w_fvt, batch=batch, seq=seq)

        wuq_p = jnp.pad(w_uq[l].reshape(MLA_Q_RANK, MLA_HEADS, MLA_NOPE + MLA_ROPE),
                        ((0, 0), (0, 0), (0, MLA_QK_PAD - MLA_NOPE - MLA_ROPE))
                        ).reshape(MLA_Q_RANK, MLA_HEADS * MLA_QK_PAD).astype(BF16)
        wukv = w_ukv[l].reshape(MLA_KV_RANK, MLA_HEADS, MLA_NOPE + MLA_V)
        wuk = wukv[:, :, :MLA_NOPE].reshape(MLA_KV_RANK, -1).astype(BF16)
        wuvt = wukv[:, :, MLA_NOPE:].reshape(MLA_KV_RANK, -1).T.astype(BF16)
        bf_pad = jnp.pad(b_forget[l:l + 1], ((0, 0), (0, LANES - FOX_HEADS)))
        q_mla, k_mla, vt_mla, cum_k, cum_q = _prep(
            lat, g_q_a[l:l + 1], g_kv_a[l:l + 1], wuq_p, wuk, wuvt, bf_pad, cos_t, sin_lo, sin_hi,
            batch=batch, seq=seq)

        o_mla, w2g, w2u = _attention(q_mla, k_mla, vt_mla, batch=batch, seq=seq, heads=MLA_HEADS,
                                     dk=MLA_QK_PAD, dv=MLA_V, q_col=0, k_col=0, mask_chunk=CHUNK,
                                     riders=(w2_gate[l], w2_up[l]))
        o_fox, w2d = _attention(fqk, fqk, fvt, cum_k, cum_q, batch=batch, seq=seq, heads=FOX_HEADS,
                                dk=FOX_DIM, dv=FOX_DIM, q_col=0, k_col=FOX_HEADS, mask_chunk=1,
                                riders=(w2_down[l],))

        wo = w_o[l].astype(BF16)
        xt = _outproj(xt, o_mla, o_fox, wo[:MLA_HEADS * MLA_V], wo[MLA_HEADS * MLA_V:],
                      g_mix_post[l:l + 1], gt2, seq=seq, tm=512)

        xt = _ffn(xt, sh3, sc3, gt3, g_ffn2_pre[l:l + 1], g_ffn2_post[l:l + 1],
                  w2g, w2u, w2d, seq=seq, tm=1024, tf=512, res_weight=0.5)

    return xt.reshape(batch, seq, d)
```

```python
import functools
import math

import jax
import jax.numpy as jnp
from jax import lax
from jax.experimental import pallas as pl
from jax.experimental.pallas import tpu as pltpu

F32 = jnp.float32
BF16 = jnp.bfloat16

V7X_VMEM_BYTES = 64 * 1024 * 1024
LANES = 128

EPS = 1e-6
ROPE_THETA = 10000.0
CHUNK = 64
MLA_HEADS = 8
MLA_Q_RANK = 512
MLA_KV_RANK = 256
MLA_NOPE = 128
MLA_ROPE = 64
MLA_V = 128
MLA_QK_PAD = 256
FOX_HEADS = 8
FOX_DIM = 128
FOX_WIDTH = FOX_HEADS * FOX_DIM
LATENT_WIDTH = 1024
KPE_COL = MLA_Q_RANK + MLA_KV_RANK
FLOGIT_COL = KPE_COL + LANES
ATTN_TILE = 512
PHASE_B_UNROLL = 14
LOG2E = math.log2(math.e)
MLA_Q_SCALE = LOG2E / math.sqrt(MLA_NOPE + MLA_ROPE)
FOX_Q_SCALE = LOG2E / math.sqrt(FOX_DIM)
NEG_BIG = -1e30

_NT = (((1,), (1,)), ((), ()))


def _params(vmem_bytes):
    limit = min(int(vmem_bytes * 1.25) + (4 << 20), V7X_VMEM_BYTES - (8 << 20))
    return pltpu.CompilerParams(vmem_limit_bytes=limit)


def _rms(x, g):
    return x * lax.rsqrt(jnp.mean(x * x, axis=-1, keepdims=True) + EPS) * g


STAT_ROWS = 8
APPLY_ROWS = 16


def _rms_stats(src_ref, stat_ref):
    tm, d = src_ref.shape

    def step(c, carry):
        rows = pl.ds(pl.multiple_of(c * STAT_ROWS, STAT_ROWS), STAT_ROWS)
        x = src_ref[rows, :]
        ms = jnp.sum(x * x, axis=-1, keepdims=True) * (1.0 / d)
        stat_ref[rows, :] = jnp.broadcast_to(lax.rsqrt(ms + EPS), (STAT_ROWS, LANES))
        return carry

    lax.fori_loop(0, tm // STAT_ROWS, step, 0, unroll=32)


def _for_row_blocks(tm, fn):
    def step(c, carry):
        fn(pl.ds(pl.multiple_of(c * APPLY_ROWS, APPLY_ROWS), APPLY_ROWS))
        return carry

    lax.fori_loop(0, tm // APPLY_ROWS, step, 0, unroll=4)


def _modulated_norm(x_ref, stat_ref, coef_ref, gpre_ref, sc_ref, sh_ref, h_ref):
    tm, d = x_ref.shape
    coef_ref[0:1, :] = gpre_ref[...] * (1.0 + sc_ref[0])
    coef_ref[1:2, :] = sh_ref[0]
    _rms_stats(x_ref, stat_ref)

    def apply(rows):
        r = stat_ref[rows, :]
        for k in range(d // LANES):
            cols = slice(k * LANES, (k + 1) * LANES)
            h = x_ref[rows, cols] * r * coef_ref[0:1, cols] + coef_ref[1:2, cols]
            h_ref[rows, cols] = h.astype(BF16)

    _for_row_blocks(tm, apply)


def _gated_norm_residual(x_ref, o_ref, stat_ref, coef_ref, gpost_ref, gt_ref, res_weight):
    tm, d = o_ref.shape
    coef_ref[0:1, :] = res_weight * gt_ref[0] * gpost_ref[...]
    _rms_stats(o_ref, stat_ref)

    def apply(rows):
        r = stat_ref[rows, :]
        for k in range(d // LANES):
            cols = slice(k * LANES, (k + 1) * LANES)
            o_ref[rows, cols] = x_ref[rows, cols] + o_ref[rows, cols] * r * coef_ref[0:1, cols]

    _for_row_blocks(tm, apply)


def _ada_kernel(c_ref, w_ref, b_ref, o_ref):
    c = c_ref[...]
    cond = (c * jax.nn.sigmoid(c)).astype(BF16)
    o_ref[...] = jnp.dot(cond, w_ref[...].astype(BF16), preferred_element_type=F32) + b_ref[...]


def _ada(c_pad, w, b, *, tn):
    m, d = c_pad.shape
    n = w.shape[1]
    vmem = 2 * d * tn * 4 + d * tn * 2 + 4 * m * tn * 4
    return pl.pallas_call(
        _ada_kernel,
        grid=(n // tn,),
        in_specs=[pl.BlockSpec((m, d), lambda j: (0, 0)),
                  pl.BlockSpec((d, tn), lambda j: (0, j)),
                  pl.BlockSpec((1, tn), lambda j: (0, j))],
        out_specs=pl.BlockSpec((m, tn), lambda j: (0, j)),
        out_shape=jax.ShapeDtypeStruct((m, n), F32),
        compiler_params=_params(vmem),
        name="ada",
    )(c_pad, w, b)


def _ffn_kernel(x_ref, sh_ref, sc_ref, gt_ref, gpre_ref, gpost_ref, wg_ref, wu_ref, wd_ref,
                o_ref, h_ref, stat_ref, coef_ref, *, res_weight):
    f = pl.program_id(1)

    def swiglu_down(h):
        g = jnp.dot(h, wg_ref[...], preferred_element_type=F32)
        u = jnp.dot(h, wu_ref[...], preferred_element_type=F32)
        a = (g * jax.nn.sigmoid(g) * u).astype(BF16)
        return jnp.dot(a, wd_ref[...], preferred_element_type=F32)

    @pl.when(f == 0)
    def _():
        h = (_rms(x_ref[...], gpre_ref[...]) * (1.0 + sc_ref[0]) + sh_ref[0]).astype(BF16)
        h_ref[...] = h
        o_ref[...] = swiglu_down(h)

    @pl.when(f > 0)
    def _():
        o_ref[...] += swiglu_down(h_ref[...])

    @pl.when(f == pl.num_programs(1) - 1)
    def _():
        _gated_norm_residual(x_ref, o_ref, stat_ref, coef_ref, gpost_ref, gt_ref, res_weight)


def _ffn(x, sh, sc, gt, g_pre, g_post, wg, wu, wd, *, seq, tm, tf, res_weight):
    t, d = x.shape
    ff = wg.shape[1]
    tpb = seq // tm
    vmem = 4 * tm * d * 4 + tm * d * 2 + 6 * d * tf * 2 + 3 * tm * tf * 4
    mod_spec = pl.BlockSpec((1, 1, d), lambda i, f: (i // tpb, 0, 0))
    gain_spec = pl.BlockSpec((1, d), lambda i, f: (0, 0))
    return pl.pallas_call(
        functools.partial(_ffn_kernel, res_weight=res_weight),
        grid=(t // tm, ff // tf),
        in_specs=[pl.BlockSpec((tm, d), lambda i, f: (i, 0)),
                  mod_spec, mod_spec, mod_spec, gain_spec, gain_spec,
                  pl.BlockSpec((d, tf), lambda i, f: (0, f)),
                  pl.BlockSpec((d, tf), lambda i, f: (0, f)),
                  pl.BlockSpec((tf, d), lambda i, f: (f, 0))],
        out_specs=pl.BlockSpec((tm, d), lambda i, f: (i, 0)),
        out_shape=jax.ShapeDtypeStruct((t, d), F32),
        scratch_shapes=[pltpu.VMEM((tm, d), BF16), pltpu.VMEM((tm, LANES), F32), pltpu.VMEM((8, d), F32)],
        compiler_params=_params(vmem),
        name="ffn",
    )(x, sh, sc, gt, g_pre, g_post, wg, wu, wd)


def _inproj_kernel(x_ref, sh_ref, sc_ref, gpre_ref, w_ref, wvt_ref, lat_ref, fqk_ref, fvt_ref):
    tn = LATENT_WIDTH
    h = (_rms(x_ref[...], gpre_ref[...]) * (1.0 + sc_ref[0]) + sh_ref[0]).astype(BF16)
    lat_ref[...] = jnp.dot(h, w_ref[:, :tn], preferred_element_type=F32)
    fq = jnp.dot(h, w_ref[:, tn:2 * tn], preferred_element_type=F32) * FOX_Q_SCALE
    fqk_ref[:, :tn] = fq.astype(BF16)
    fqk_ref[:, tn:] = jnp.dot(h, w_ref[:, 2 * tn:], preferred_element_type=F32).astype(BF16)
    fvt_ref[0, 0] = lax.dot_general(wvt_ref[...], h, _NT, preferred_element_type=F32).astype(BF16)


def _inproj(x, sh, sc, g_pre, w, wvt, *, batch, seq):
    t, d = x.shape
    tm = ATTN_TILE
    tn = LATENT_WIDTH
    tpb = seq // tm
    assert w.shape == (d, 3 * tn) and wvt.shape == (FOX_WIDTH, d)
    vmem = (2 * tm * d * 4 + tm * d * 2 + (w.size + wvt.size) * 2 + 2 * tm * tn * 4
            + 2 * tm * 2 * tn * 2 + 2 * tm * FOX_WIDTH * 2 + 2 * tm * tn * 4)
    mod_spec = pl.BlockSpec((1, 1, d), lambda i: (i // tpb, 0, 0))
    resident = dict(pipeline_mode=pl.Buffered(1))
    return pl.pallas_call(
        _inproj_kernel,
        grid=(t // tm,),
        in_specs=[pl.BlockSpec((tm, d), lambda i: (i, 0)),
                  mod_spec, mod_spec,
                  pl.BlockSpec((1, d), lambda i: (0, 0)),
                  pl.BlockSpec(w.shape, lambda i: (0, 0), **resident),
                  pl.BlockSpec(wvt.shape, lambda i: (0, 0), **resident)],
        out_specs=[pl.BlockSpec((tm, tn), lambda i: (i, 0)),
                   pl.BlockSpec((tm, 2 * tn), lambda i: (i, 0)),
                   pl.BlockSpec((1, 1, FOX_WIDTH, tm), lambda i: (i // tpb, i % tpb, 0, 0))],
        out_shape=[jax.ShapeDtypeStruct((t, tn), F32),
                   jax.ShapeDtypeStruct((t, 2 * tn), BF16),
                   jax.ShapeDtypeStruct((batch, tpb, FOX_WIDTH, tm), BF16)],
        compiler_params=_params(vmem),
        name="in_proj",
    )(x, sh, sc, g_pre, w, wvt)


def _rot(r, cos_t, sin_lo, sin_hi):
    return r * cos_t + pltpu.roll(r, 96, 1) * sin_lo + pltpu.roll(r, 32, 1) * sin_hi


def _prep_kernel(lat_ref, gq_ref, gkv_ref, wuq_ref, wuk_ref, wuvt_ref, bf_ref, cos_ref, slo_ref, shi_ref,
                 q_ref, k_ref, vt_ref, ck_ref, cq_ref, carry_ref, *, tiles_per_batch):
    i = pl.program_id(0)
    tm = lat_ref.shape[0]
    cos_t, sin_lo, sin_hi = cos_ref[...], slo_ref[...], shi_ref[...]

    qn = _rms(lat_ref[:, :MLA_Q_RANK], gq_ref[...]).astype(BF16)
    q = jnp.dot(qn, wuq_ref[...], preferred_element_type=F32) * MLA_Q_SCALE
    kvn = _rms(lat_ref[:, MLA_Q_RANK:KPE_COL], gkv_ref[...]).astype(BF16)
    k_nope = jnp.dot(kvn, wuk_ref[...], preferred_element_type=F32)
    vt_ref[0, 0] = lax.dot_general(wuvt_ref[...], kvn, _NT, preferred_element_type=F32).astype(BF16)
    k_rope = _rot(lat_ref[:, KPE_COL:KPE_COL + LANES], cos_t, sin_lo, sin_hi).astype(BF16)
    for h in range(MLA_HEADS):
        c0 = h * MLA_QK_PAD
        q_ref[:, c0:c0 + MLA_NOPE] = q[:, c0:c0 + MLA_NOPE].astype(BF16)
        q_ref[:, c0 + MLA_NOPE:c0 + MLA_QK_PAD] = _rot(
            q[:, c0 + MLA_NOPE:c0 + MLA_QK_PAD], cos_t, sin_lo, sin_hi).astype(BF16)
        k_ref[:, c0:c0 + MLA_NOPE] = k_nope[:, h * MLA_NOPE:(h + 1) * MLA_NOPE].astype(BF16)
        k_ref[:, c0 + MLA_NOPE:c0 + MLA_QK_PAD] = k_rope

    @pl.when(i % tiles_per_batch == 0)
    def _():
        carry_ref[...] = jnp.zeros_like(carry_ref)

    z = lat_ref[:, FLOGIT_COL:FLOGIT_COL + LANES] + bf_ref[...]
    lane = lax.broadcasted_iota(jnp.int32, (tm, LANES), 1)
    log_f = jnp.where(lane < FOX_HEADS, jnp.minimum(z, 0.0) - jnp.log1p(jnp.exp(-jnp.abs(z))), 0.0)
    hi = log_f.astype(BF16).astype(F32)
    mid = (log_f - hi).astype(BF16).astype(F32)
    lo = (log_f - hi - mid).astype(BF16).astype(F32)
    pieces = hi + pltpu.roll(mid, FOX_HEADS, 1) + pltpu.roll(lo, 2 * FOX_HEADS, 1)
    row = lax.broadcasted_iota(jnp.int32, (tm, tm), 0)
    col = lax.broadcasted_iota(jnp.int32, (tm, tm), 1)
    tri = (col <= row).astype(BF16)
    part = jnp.dot(tri, pieces.astype(BF16), preferred_element_type=F32)
    part = part + pltpu.roll(part, LANES - FOX_HEADS, 1) + pltpu.roll(part, LANES - 2 * FOX_HEADS, 1)
    cum = jnp.where(lane < FOX_HEADS, part, 0.0) + carry_ref[...]
    carry_ref[...] = cum[tm - 1:tm, :]
    cum2 = cum * LOG2E
    cq_ref[0, 0] = cum2.T[:FOX_HEADS, :]
    for h in range(FOX_HEADS):
        ck_ref[0, h] = jnp.broadcast_to(cum2[:, h:h + 1], (tm, LANES))


def _prep(lat, g_q, g_kv, wuq, wuk, wuvt, b_forget, cos_t, sin_lo, sin_hi, *, batch, seq):
    t = lat.shape[0]
    tm = ATTN_TILE
    tpb = seq // tm
    qk_w = MLA_HEADS * MLA_QK_PAD
    v_w = MLA_HEADS * MLA_V
    vmem = (2 * tm * LATENT_WIDTH * 4 + 2 * (wuq.size + wuk.size + wuvt.size) * 2 + 6 * tm * LANES * 4
            + 2 * (2 * tm * qk_w + tm * v_w) * 2 + 2 * FOX_HEADS * tm * LANES * 4
            + tm * (qk_w + 2 * v_w) * 4 + 3 * tm * tm * 4)
    full = lambda a: pl.BlockSpec(a.shape, lambda i: (0, 0))
    tab_spec = pl.BlockSpec((tm, LANES), lambda i: (i % tpb, 0))
    return pl.pallas_call(
        functools.partial(_prep_kernel, tiles_per_batch=tpb),
        grid=(t // tm,),
        in_specs=[pl.BlockSpec((tm, LATENT_WIDTH), lambda i: (i, 0)),
                  full(g_q), full(g_kv), full(wuq), full(wuk), full(wuvt), full(b_forget),
                  tab_spec, tab_spec, tab_spec],
        out_specs=[pl.BlockSpec((tm, qk_w), lambda i: (i, 0)),
                   pl.BlockSpec((tm, qk_w), lambda i: (i, 0)),
                   pl.BlockSpec((1, 1, v_w, tm), lambda i: (i // tpb, i % tpb, 0, 0)),
                   pl.BlockSpec((1, FOX_HEADS, tm, LANES), lambda i: (i // tpb, 0, i % tpb, 0)),
                   pl.BlockSpec((1, 1, FOX_HEADS, tm), lambda i: (i // tpb, i % tpb, 0, 0))],
        out_shape=[jax.ShapeDtypeStruct((t, qk_w), BF16),
                   jax.ShapeDtypeStruct((t, qk_w), BF16),
                   jax.ShapeDtypeStruct((batch, tpb, v_w, tm), BF16),
                   jax.ShapeDtypeStruct((batch, FOX_HEADS, seq, LANES), F32),
                   jax.ShapeDtypeStruct((batch, tpb, FOX_HEADS, tm), F32)],
        scratch_shapes=[pltpu.VMEM((1, LANES), F32)],
        compiler_params=_params(vmem),
        name="mla_fox_prep",
    )(lat, g_q, g_kv, wuq, wuk, wuvt, b_forget, cos_t, sin_lo, sin_hi)


def _attn_kernel(*refs, mask_chunk, fox):
    n_in = 5 if fox else 3
    n_cast = (len(refs) - n_in - 6) // 2
    cast_src = refs[n_in:n_in + n_cast]
    cast_dst = refs[n_in + n_cast + 1:n_in + 2 * n_cast + 1]
    refs = refs[:n_in] + (refs[n_in + n_cast],) + refs[n_in + 2 * n_cast + 1:]
    if fox:
        q_ref, k_ref, vt_ref, ck_ref, cq_ref, o_ref, s0_ref, s1_ref, m_ref, l_ref, acc_ref = refs
    else:
        q_ref, k_ref, vt_ref, o_ref, s0_ref, s1_ref, m_ref, l_ref, acc_ref = refs
    for src, dst in zip(cast_src, cast_dst):
        dst[...] = src[...].astype(BF16)
    head = pl.program_id(1)
    tq = ATTN_TILE
    nq = q_ref.shape[0] // tq
    n_low = nq * (nq - 1) // 2
    assert PHASE_B_UNROLL % 2 == 0 and n_low % PHASE_B_UNROLL == 0
    bufs = (s0_ref, s1_ref)

    def rows(t):
        return pl.ds(pl.multiple_of(t * tq, tq), tq)

    def scores(i, j, s_ref):
        s = lax.dot_general(k_ref[rows(j), :], q_ref[rows(i), :], _NT, preferred_element_type=F32)
        if fox:
            cum_q = cq_ref[0, i, pl.ds(head, 1), :]
            cum_k = ck_ref[0, 0, rows(j), :]
            s = s + (cum_q - jnp.concatenate([cum_k] * (tq // LANES), axis=1))
        s_ref[...] = s

    def accumulate_diagonal(i, s_ref):
        src = lax.broadcasted_iota(jnp.int32, (tq, tq), 0) // mask_chunk
        dst = lax.broadcasted_iota(jnp.int32, (tq, tq), 1) // mask_chunk
        s = jnp.where(src <= dst, s_ref[...], NEG_BIG)
        m = jnp.max(s, axis=0, keepdims=True)
        p = jnp.exp2(s - m)
        m_ref[i] = m
        l_ref[i] = jnp.sum(p, axis=0, keepdims=True)
        acc_ref[i] = jnp.dot(vt_ref[0, i], p.astype(BF16), preferred_element_type=F32)

    def accumulate(i, j, s_ref):
        s = s_ref[...]
        m_prev = m_ref[i]
        m_new = jnp.maximum(m_prev, jnp.max(s, axis=0, keepdims=True))
        alpha = jnp.exp2(m_prev - m_new)
        p = jnp.exp2(s - m_new)
        l_ref[i] = alpha * l_ref[i] + jnp.sum(p, axis=0, keepdims=True)
        acc_ref[i] = alpha * acc_ref[i] + jnp.dot(vt_ref[0, j], p.astype(BF16),
                                                  preferred_element_type=F32)
        m_ref[i] = m_new

    scores(0, 0, bufs[0])
    for i in range(nq):
        if i + 1 < nq:
            scores(i + 1, i + 1, bufs[(i + 1) % 2])
        else:
            scores(1, 0, bufs[(i + 1) % 2])
        accumulate_diagonal(i, bufs[i % 2])

    def advance(i, j):
        wraps = j + 1 == i
        i_nxt = jnp.minimum(jnp.where(wraps, i + 1, i), nq - 1)
        return i_nxt, jnp.where(wraps, 0, j + 1)

    def body(_, carry):
        i, j = carry
        for u in range(PHASE_B_UNROLL):
            i_nxt, j_nxt = advance(i, j)
            scores(i_nxt, j_nxt, bufs[(nq + u + 1) % 2])
            accumulate(i, j, bufs[(nq + u) % 2])
            i, j = i_nxt, j_nxt
        return i, j

    lax.fori_loop(0, n_low // PHASE_B_UNROLL, body, (jnp.int32(1), jnp.int32(0)))

    for i in range(nq):
        o_ref[i * tq:(i + 1) * tq, :] = (acc_ref[i] / l_ref[i]).T.astype(o_ref.dtype)


def _attention(q_arr, k_arr, vt_arr, cum_k=None, cum_q=None, *, batch, seq, heads, dk, dv,
               q_col, k_col, mask_chunk, riders=()):
    fox = cum_k is not None
    tq = ATTN_TILE
    nq = seq // tq
    steps = batch * heads
    in_specs = [pl.BlockSpec((seq, dk), lambda b, h: (b, q_col + h)),
                pl.BlockSpec((seq, dk), lambda b, h: (b, k_col + h)),
                pl.BlockSpec((1, nq, dv, tq), lambda b, h: (b, 0, h, 0))]
    args = [q_arr, k_arr, vt_arr]
    if fox:
        in_specs += [pl.BlockSpec((1, 1, seq, LANES), lambda b, h: (b, h, 0, 0)),
                     pl.BlockSpec((1, nq, heads, tq), lambda b, h: (b, 0, 0, 0))]
        args += [cum_k, cum_q]
    slab_specs = [pl.BlockSpec((w.shape[0] // steps, w.shape[1]), lambda b, h: (b * heads + h, 0))
                  for w in riders]
    vmem = (2 * seq * (2 * dk + 2 * dv) * 2 + 2 * seq * LANES * 4 + 2 * seq * heads * 4
            + seq * dv * 4 + 8 * tq * tq * 4 + sum(2 * 6 * w.size // steps for w in riders))
    return pl.pallas_call(
        functools.partial(_attn_kernel, mask_chunk=mask_chunk, fox=fox),
        grid=(batch, heads),
        in_specs=in_specs + slab_specs,
        out_specs=[pl.BlockSpec((seq, dv), lambda b, h: (b, h))] + slab_specs,
        out_shape=[jax.ShapeDtypeStruct((batch * seq, heads * dv), BF16)]
                  + [jax.ShapeDtypeStruct(w.shape, BF16) for w in riders],
        scratch_shapes=[pltpu.VMEM((tq, tq), F32), pltpu.VMEM((tq, tq), F32),
                        pltpu.VMEM((nq, 1, tq), F32), pltpu.VMEM((nq, 1, tq), F32),
                        pltpu.VMEM((nq, dv, tq), F32)],
        compiler_params=_params(vmem),
        name="fox_attn" if fox else "mla_attn",
    )(*args, *riders)


def _outproj_kernel(x_ref, oa_ref, ob_ref, wa_ref, wb_ref, gpost_ref, gt_ref, o_ref):
    y = jnp.dot(oa_ref[...], wa_ref[...], preferred_element_type=F32)
    y = y + jnp.dot(ob_ref[...], wb_ref[...], preferred_element_type=F32)
    o_ref[...] = x_ref[...] + gt_ref[0] * _rms(y, gpost_ref[...])


def _outproj(x, o_a, o_b, w_a, w_b, g_post, gt, *, seq, tm):
    t, d = x.shape
    ka, kb = o_a.shape[1], o_b.shape[1]
    tpb = seq // tm
    vmem = 4 * tm * d * 4 + 2 * tm * (ka + kb) * 2 + 2 * (ka + kb) * d * 2 + 2 * tm * d * 4
    return pl.pallas_call(
        _outproj_kernel,
        grid=(t // tm,),
        in_specs=[pl.BlockSpec((tm, d), lambda i: (i, 0)),
                  pl.BlockSpec((tm, ka), lambda i: (i, 0)),
                  pl.BlockSpec((tm, kb), lambda i: (i, 0)),
                  pl.BlockSpec((ka, d), lambda i: (0, 0)),
                  pl.BlockSpec((kb, d), lambda i: (0, 0)),
                  pl.BlockSpec((1, d), lambda i: (0, 0)),
                  pl.BlockSpec((1, 1, d), lambda i: (i // tpb, 0, 0))],
        out_specs=pl.BlockSpec((tm, d), lambda i: (i, 0)),
        out_shape=jax.ShapeDtypeStruct((t, d), F32),
        compiler_params=_params(vmem),
        name="out_proj",
    )(x, o_a, o_b, w_a, w_b, g_post, gt)


def _rope_tables(seq):
    half = MLA_ROPE // 2
    inv = ROPE_THETA ** (-jnp.arange(half, dtype=F32) / half)
    ang = jnp.arange(seq).astype(F32)[:, None] * inv[None, :]
    cos, sin = jnp.cos(ang), jnp.sin(ang)
    zero = jnp.zeros_like(cos)
    pad = jnp.zeros((seq, LANES - 2 * half), F32)
    return (jnp.concatenate([cos, cos, pad], axis=1),
            jnp.concatenate([-sin, zero, pad], axis=1),
            jnp.concatenate([zero, sin, pad], axis=1))


def kernel(x, c, w_ada, b_ada, g_ffn1_pre, g_ffn1_post, w1_gate, w1_up, w1_down, g_mix_pre,
           g_mix_post, w_in, b_forget, g_q_a, w_uq, g_kv_a, w_ukv, w_o, g_ffn2_pre, g_ffn2_post,
           w2_gate, w2_up, w2_down):
    batch, seq, d = x.shape
    depth = w_ada.shape[0]
    t = batch * seq
    xt = x.reshape(t, d)
    c_pad = jnp.pad(c, ((0, 8 - batch), (0, 0)))
    cos_t, sin_lo, sin_hi = _rope_tables(seq)

    for l in range(depth):
        mod = _ada(c_pad, w_ada[l], b_ada[l:l + 1], tn=1024)[:batch]
        sh1, sc1, gt1, sh2, sc2, gt2, sh3, sc3, gt3 = [
            mod[:, n * d:(n + 1) * d].reshape(batch, 1, d) for n in range(9)]

        xt = _ffn(xt, sh1, sc1, gt1, g_ffn1_pre[l:l + 1], g_ffn1_post[l:l + 1],
                  w1_gate[l].astype(BF16), w1_up[l].astype(BF16), w1_down[l].astype(BF16),
                  seq=seq, tm=1024, tf=512, res_weight=0.5)

        wi = w_in[l]
        fox0 = KPE_COL + MLA_ROPE
        fv0 = fox0 + 2 * FOX_WIDTH
        fl0 = fox0 + 3 * FOX_WIDTH
        w_in_p = jnp.concatenate(
            [wi[:, :fox0], jnp.zeros((d, LANES - MLA_ROPE), F32),
             wi[:, fl0:], jnp.zeros((d, LANES - FOX_HEADS), F32),
             wi[:, fox0:fv0]], axis=1).astype(BF16)
        w_fvt = wi[:, fv0:fl0].T.astype(BF16)
        lat, fqk, fvt = _inproj(xt, sh2, sc2, g_mix_pre[l:l + 1], w_in_p, w_fvt, batch=batch, seq=seq)

        wuq_p = jnp.pad(w_uq[l].reshape(MLA_Q_RANK, MLA_HEADS, MLA_NOPE + MLA_ROPE),
                        ((0, 0), (0, 0), (0, MLA_QK_PAD - MLA_NOPE - MLA_ROPE))
                        ).reshape(MLA_Q_RANK, MLA_HEADS * MLA_QK_PAD).astype(BF16)
        wukv = w_ukv[l].reshape(MLA_KV_RANK, MLA_HEADS, MLA_NOPE + MLA_V)
        wuk = wukv[:, :, :MLA_NOPE].reshape(MLA_KV_RANK, -1).astype(BF16)
        wuvt = wukv[:, :, MLA_NOPE:].reshape(MLA_KV_RANK, -1).T.astype(BF16)
        bf_pad = jnp.pad(b_forget[l:l + 1], ((0, 0), (0, LANES - FOX_HEADS)))
        q_mla, k_mla, vt_mla, cum_k, cum_q = _prep(
            lat, g_q_a[l:l + 1], g_kv_a[l:l + 1], wuq_p, wuk, wuvt, bf_pad, cos_t, sin_lo, sin_hi,
            batch=batch, seq=seq)

        o_mla, w2g, w2u = _attention(q_mla, k_mla, vt_mla, batch=batch, seq=seq, heads=MLA_HEADS,
                                     dk=MLA_QK_PAD, dv=MLA_V, q_col=0, k_col=0, mask_chunk=CHUNK,
                                     riders=(w2_gate[l], w2_up[l]))
        o_fox, w2d = _attention(fqk, fqk, fvt, cum_k, cum_q, batch=batch, seq=seq, heads=FOX_HEADS,
                                dk=FOX_DIM, dv=FOX_DIM, q_col=0, k_col=FOX_HEADS, mask_chunk=1,
                                riders=(w2_down[l],))

        wo = w_o[l].astype(BF16)
        xt = _outproj(xt, o_mla, o_fox, wo[:MLA_HEADS * MLA_V], wo[MLA_HEADS * MLA_V:],
                      g_mix_post[l:l + 1], gt2, seq=seq, tm=512)

        xt = _ffn(xt, sh3, sc3, gt3, g_ffn2_pre[l:l + 1], g_ffn2_post[l:l + 1],
                  w2g, w2u, w2d, seq=seq, tm=1024, tf=512, res_weight=0.5)

    return xt.reshape(batch, seq, d)
```

```python
import functools
import math

import jax
import jax.numpy as jnp
from jax import lax
from jax.experimental import pallas as pl
from jax.experimental.pallas import tpu as pltpu

F32 = jnp.float32
BF16 = jnp.bfloat16

V7X_VMEM_BYTES = 64 * 1024 * 1024
LANES = 128

EPS = 1e-6
ROPE_THETA = 10000.0
CHUNK = 64
MLA_HEADS = 8
MLA_Q_RANK = 512
MLA_KV_RANK = 256
MLA_NOPE = 128
MLA_ROPE = 64
MLA_V = 128
MLA_QK_PAD = 256
FOX_HEADS = 8
FOX_DIM = 128
FOX_WIDTH = FOX_HEADS * FOX_DIM
LATENT_WIDTH = 1024
KPE_COL = MLA_Q_RANK + MLA_KV_RANK
FLOGIT_COL = KPE_COL + LANES
ATTN_TILE = 512
PHASE_B_UNROLL = 14
LOG2E = math.log2(math.e)
MLA_Q_SCALE = LOG2E / math.sqrt(MLA_NOPE + MLA_ROPE)
FOX_Q_SCALE = LOG2E / math.sqrt(FOX_DIM)
NEG_BIG = -1e30

_NT = (((1,), (1,)), ((), ()))


def _params(vmem_bytes):
    limit = min(int(vmem_bytes * 1.25) + (4 << 20), V7X_VMEM_BYTES - (8 << 20))
    return pltpu.CompilerParams(vmem_limit_bytes=limit)


def _rms(x, g):
    return x * lax.rsqrt(jnp.mean(x * x, axis=-1, keepdims=True) + EPS) * g


STAT_ROWS = 8
APPLY_ROWS = 16


def _rms_stats(src_ref, stat_ref):
    tm, d = src_ref.shape

    def step(c, carry):
        rows = pl.ds(pl.multiple_of(c * STAT_ROWS, STAT_ROWS), STAT_ROWS)
        x = src_ref[rows, :]
        ms = jnp.sum(x * x, axis=-1, keepdims=True) * (1.0 / d)
        stat_ref[rows, :] = jnp.broadcast_to(lax.rsqrt(ms + EPS), (STAT_ROWS, LANES))
        return carry

    lax.fori_loop(0, tm // STAT_ROWS, step, 0, unroll=32)


def _for_row_blocks(tm, fn):
    def step(c, carry):
        fn(pl.ds(pl.multiple_of(c * APPLY_ROWS, APPLY_ROWS), APPLY_ROWS))
        return carry

    lax.fori_loop(0, tm // APPLY_ROWS, step, 0, unroll=4)


def _modulated_norm(x_ref, stat_ref, coef_ref, gpre_ref, sc_ref, sh_ref, h_ref):
    tm, d = x_ref.shape
    coef_ref[0:1, :] = gpre_ref[...] * (1.0 + sc_ref[0])
    coef_ref[1:2, :] = sh_ref[0]
    _rms_stats(x_ref, stat_ref)

    def apply(rows):
        r = stat_ref[rows, :]
        for k in range(d // LANES):
            cols = slice(k * LANES, (k + 1) * LANES)
            h = x_ref[rows, cols] * r * coef_ref[0:1, cols] + coef_ref[1:2, cols]
            h_ref[rows, cols] = h.astype(BF16)

    _for_row_blocks(tm, apply)


def _gated_norm_residual(x_ref, o_ref, stat_ref, coef_ref, gpost_ref, gt_ref, res_weight):
    tm, d = o_ref.shape
    coef_ref[0:1, :] = res_weight * gt_ref[0] * gpost_ref[...]
    _rms_stats(o_ref, stat_ref)

    def apply(rows):
        r = stat_ref[rows, :]
        for k in range(d // LANES):
            cols = slice(k * LANES, (k + 1) * LANES)
            o_ref[rows, cols] = x_ref[rows, cols] + o_ref[rows, cols] * r * coef_ref[0:1, cols]

    _for_row_blocks(tm, apply)


def _ada_kernel(c_ref, w_ref, b_ref, o_ref):
    c = c_ref[...]
    cond = (c * jax.nn.sigmoid(c)).astype(BF16)
    o_ref[...] = jnp.dot(cond, w_ref[...].astype(BF16), preferred_element_type=F32) + b_ref[...]


def _ada(c_pad, w, b, *, tn):
    m, d = c_pad.shape
    n = w.shape[1]
    vmem = 2 * d * tn * 4 + d * tn * 2 + 4 * m * tn * 4
    return pl.pallas_call(
        _ada_kernel,
        grid=(n // tn,),
        in_specs=[pl.BlockSpec((m, d), lambda j: (0, 0)),
                  pl.BlockSpec((d, tn), lambda j: (0, j)),
                  pl.BlockSpec((1, tn), lambda j: (0, j))],
        out_specs=pl.BlockSpec((m, tn), lambda j: (0, j)),
        out_shape=jax.ShapeDtypeStruct((m, n), F32),
        compiler_params=_params(vmem),
        name="ada",
    )(c_pad, w, b)


def _ffn_kernel(x_ref, sh_ref, sc_ref, gt_ref, gpre_ref, gpost_ref, wg_ref, wu_ref, wd_ref,
                o_ref, h_ref, stat_ref, coef_ref, *, res_weight):
    f = pl.program_id(1)

    def swiglu_down(h):
        g = jnp.dot(h, wg_ref[...], preferred_element_type=F32)
        u = jnp.dot(h, wu_ref[...], preferred_element_type=F32)
        a = (g * jax.nn.sigmoid(g) * u).astype(BF16)
        return jnp.dot(a, wd_ref[...], preferred_element_type=F32)

    @pl.when(f == 0)
    def _():
        h = (_rms(x_ref[...], gpre_ref[...]) * (1.0 + sc_ref[0]) + sh_ref[0]).astype(BF16)
        h_ref[...] = h
        o_ref[...] = swiglu_down(h)

    @pl.when(f > 0)
    def _():
        o_ref[...] += swiglu_down(h_ref[...])

    @pl.when(f == pl.num_programs(1) - 1)
    def _():
        _gated_norm_residual(x_ref, o_ref, stat_ref, coef_ref, gpost_ref, gt_ref, res_weight)


def _ffn(x, sh, sc, gt, g_pre, g_post, wg, wu, wd, *, seq, tm, tf, res_weight):
    t, d = x.shape
    ff = wg.shape[1]
    tpb = seq // tm
    vmem = 4 * tm * d * 4 + tm * d * 2 + 6 * d * tf * 2 + 3 * tm * tf * 4
    mod_spec = pl.BlockSpec((1, 1, d), lambda i, f: (i // tpb, 0, 0))
    gain_spec = pl.BlockSpec((1, d), lambda i, f: (0, 0))
    return pl.pallas_call(
        functools.partial(_ffn_kernel, res_weight=res_weight),
        grid=(t // tm, ff // tf),
        in_specs=[pl.BlockSpec((tm, d), lambda i, f: (i, 0)),
                  mod_spec, mod_spec, mod_spec, gain_spec, gain_spec,
                  pl.BlockSpec((d, tf), lambda i, f: (0, f)),
                  pl.BlockSpec((d, tf), lambda i, f: (0, f)),
                  pl.BlockSpec((tf, d), lambda i, f: (f, 0))],
        out_specs=pl.BlockSpec((tm, d), lambda i, f: (i, 0)),
        out_shape=jax.ShapeDtypeStruct((t, d), F32),
        scratch_shapes=[pltpu.VMEM((tm, d), BF16), pltpu.VMEM((tm, LANES), F32), pltpu.VMEM((8, d), F32)],
        compiler_params=_params(vmem),
        name="ffn",
    )(x, sh, sc, gt, g_pre, g_post, wg, wu, wd)


def _rot(r, cos_t, sin_lo, sin_hi):
    return r * cos_t + pltpu.roll(r, 96, 1) * sin_lo + pltpu.roll(r, 32, 1) * sin_hi


def _mla_fox_prep(lat, gq_ref, gkv_ref, wuq_ref, wuk_ref, wuvt_ref, bf_ref, cos_ref, slo_ref, shi_ref,
                  q_ref, k_ref, vt_ref, ck_ref, cq_ref, carry_ref):
    tm = lat.shape[0]
    cos_t, sin_lo, sin_hi = cos_ref[...], slo_ref[...], shi_ref[...]

    qn = _rms(lat[:, :MLA_Q_RANK], gq_ref[...]).astype(BF16)
    q = jnp.dot(qn, wuq_ref[...], preferred_element_type=F32) * MLA_Q_SCALE
    kvn = _rms(lat[:, MLA_Q_RANK:KPE_COL], gkv_ref[...]).astype(BF16)
    k_nope = jnp.dot(kvn, wuk_ref[...], preferred_element_type=F32)
    vt_ref[0, 0] = lax.dot_general(wuvt_ref[...], kvn, _NT, preferred_element_type=F32).astype(BF16)
    k_rope = _rot(lat[:, KPE_COL:KPE_COL + LANES], cos_t, sin_lo, sin_hi).astype(BF16)
    for h in range(MLA_HEADS):
        c0 = h * MLA_QK_PAD
        q_ref[:, c0:c0 + MLA_NOPE] = q[:, c0:c0 + MLA_NOPE].astype(BF16)
        q_ref[:, c0 + MLA_NOPE:c0 + MLA_QK_PAD] = _rot(
            q[:, c0 + MLA_NOPE:c0 + MLA_QK_PAD], cos_t, sin_lo, sin_hi).astype(BF16)
        k_ref[:, c0:c0 + MLA_NOPE] = k_nope[:, h * MLA_NOPE:(h + 1) * MLA_NOPE].astype(BF16)
        k_ref[:, c0 + MLA_NOPE:c0 + MLA_QK_PAD] = k_rope

    z = lat[:, FLOGIT_COL:FLOGIT_COL + LANES] + bf_ref[...]
    lane = lax.broadcasted_iota(jnp.int32, (tm, LANES), 1)
    log_f = jnp.where(lane < FOX_HEADS, jnp.minimum(z, 0.0) - jnp.log1p(jnp.exp(-jnp.abs(z))), 0.0)
    hi = log_f.astype(BF16).astype(F32)
    mid = (log_f - hi).astype(BF16).astype(F32)
    lo = (log_f - hi - mid).astype(BF16).astype(F32)
    pieces = hi + pltpu.roll(mid, FOX_HEADS, 1) + pltpu.roll(lo, 2 * FOX_HEADS, 1)
    row = lax.broadcasted_iota(jnp.int32, (tm, tm), 0)
    col = lax.broadcasted_iota(jnp.int32, (tm, tm), 1)
    tri = (col <= row).astype(BF16)
    part = jnp.dot(tri, pieces.astype(BF16), preferred_element_type=F32)
    part = part + pltpu.roll(part, LANES - FOX_HEADS, 1) + pltpu.roll(part, LANES - 2 * FOX_HEADS, 1)
    cum = jnp.where(lane < FOX_HEADS, part, 0.0) + carry_ref[...]
    carry_ref[...] = cum[tm - 1:tm, :]
    cum2 = cum * LOG2E
    cq_ref[0, 0] = cum2.T[:FOX_HEADS, :]
    for h in range(FOX_HEADS):
        ck_ref[0, h] = jnp.broadcast_to(cum2[:, h:h + 1], (tm, LANES))


def _mixer_in_kernel(x_ref, sh_ref, sc_ref, gpre_ref, w_ref, wvt_ref,
                     gq_ref, gkv_ref, wuq_ref, wuk_ref, wuvt_ref, bf_ref, cos_ref, slo_ref, shi_ref,
                     fqk_ref, fvt_ref, q_ref, k_ref, vt_ref, ck_ref, cq_ref, carry_ref, *, tiles_per_batch):
    @pl.when(pl.program_id(0) % tiles_per_batch == 0)
    def _():
        carry_ref[...] = jnp.zeros_like(carry_ref)

    tn = LATENT_WIDTH
    h = (_rms(x_ref[...], gpre_ref[...]) * (1.0 + sc_ref[0]) + sh_ref[0]).astype(BF16)
    lat = jnp.dot(h, w_ref[:, :tn], preferred_element_type=F32)
    fq = jnp.dot(h, w_ref[:, tn:2 * tn], preferred_element_type=F32) * FOX_Q_SCALE
    fqk_ref[:, :tn] = fq.astype(BF16)
    fqk_ref[:, tn:] = jnp.dot(h, w_ref[:, 2 * tn:], preferred_element_type=F32).astype(BF16)
    fvt_ref[0, 0] = lax.dot_general(wvt_ref[...], h, _NT, preferred_element_type=F32).astype(BF16)
    _mla_fox_prep(lat, gq_ref, gkv_ref, wuq_ref, wuk_ref, wuvt_ref, bf_ref, cos_ref, slo_ref, shi_ref,
                  q_ref, k_ref, vt_ref, ck_ref, cq_ref, carry_ref)


def _mixer_in(x, sh, sc, g_pre, w, wvt, g_q, g_kv, wuq, wuk, wuvt, b_forget, cos_t, sin_lo, sin_hi,
              *, batch, seq):
    t, d = x.shape
    tm = ATTN_TILE
    tn = LATENT_WIDTH
    tpb = seq // tm
    qk_w = MLA_HEADS * MLA_QK_PAD
    v_w = MLA_HEADS * MLA_V
    assert w.shape == (d, 3 * tn) and wvt.shape == (FOX_WIDTH, d)
    resident_bytes = (w.size + wvt.size + wuq.size + wuk.size + wuvt.size) * 2
    vmem = (2 * tm * d * 4 + tm * d * 2 + resident_bytes + 2 * tm * (2 * tn + FOX_WIDTH) * 2
            + 2 * tm * (2 * qk_w + v_w) * 2 + 2 * FOX_HEADS * tm * LANES * 4 + 6 * tm * LANES * 4
            + tm * (tn + qk_w) * 4)
    mod_spec = pl.BlockSpec((1, 1, d), lambda i: (i // tpb, 0, 0))
    resident = lambda a: pl.BlockSpec(a.shape, lambda i: (0, 0), pipeline_mode=pl.Buffered(1))
    small = lambda a: pl.BlockSpec(a.shape, lambda i: (0, 0))
    tab_spec = pl.BlockSpec((tm, LANES), lambda i: (i % tpb, 0))
    return pl.pallas_call(
        functools.partial(_mixer_in_kernel, tiles_per_batch=tpb),
        grid=(t // tm,),
        in_specs=[pl.BlockSpec((tm, d), lambda i: (i, 0)),
                  mod_spec, mod_spec, small(g_pre), resident(w), resident(wvt),
                  small(g_q), small(g_kv), resident(wuq), resident(wuk), resident(wuvt), small(b_forget),
                  tab_spec, tab_spec, tab_spec],
        out_specs=[pl.BlockSpec((tm, 2 * tn), lambda i: (i, 0)),
                   pl.BlockSpec((1, 1, FOX_WIDTH, tm), lambda i: (i // tpb, i % tpb, 0, 0)),
                   pl.BlockSpec((tm, qk_w), lambda i: (i, 0)),
                   pl.BlockSpec((tm, qk_w), lambda i: (i, 0)),
                   pl.BlockSpec((1, 1, v_w, tm), lambda i: (i // tpb, i % tpb, 0, 0)),
                   pl.BlockSpec((1, FOX_HEADS, tm, LANES), lambda i: (i // tpb, 0, i % tpb, 0)),
                   pl.BlockSpec((1, 1, FOX_HEADS, tm), lambda i: (i // tpb, i % tpb, 0, 0))],
        out_shape=[jax.ShapeDtypeStruct((t, 2 * tn), BF16),
                   jax.ShapeDtypeStruct((batch, tpb, FOX_WIDTH, tm), BF16),
                   jax.ShapeDtypeStruct((t, qk_w), BF16),
                   jax.ShapeDtypeStruct((t, qk_w), BF16),
                   jax.ShapeDtypeStruct((batch, tpb, v_w, tm), BF16),
                   jax.ShapeDtypeStruct((batch, FOX_HEADS, seq, LANES), F32),
                   jax.ShapeDtypeStruct((batch, tpb, FOX_HEADS, tm), F32)],
        scratch_shapes=[pltpu.VMEM((1, LANES), F32)],
        compiler_params=_params(vmem),
        name="mixer_in",
    )(x, sh, sc, g_pre, w, wvt, g_q, g_kv, wuq, wuk, wuvt, b_forget, cos_t, sin_lo, sin_hi)


def _attn_kernel(*refs, mask_chunk, fox):
    n_in = 5 if fox else 3
    n_cast = (len(refs) - n_in - 6) // 2
    cast_src = refs[n_in:n_in + n_cast]
    cast_dst = refs[n_in + n_cast + 1:n_in + 2 * n_cast + 1]
    refs = refs[:n_in] + (refs[n_in + n_cast],) + refs[n_in + 2 * n_cast + 1:]
    if fox:
        q_ref, k_ref, vt_ref, ck_ref, cq_ref, o_ref, s0_ref, s1_ref, m_ref, l_ref, acc_ref = refs
    else:
        q_ref, k_ref, vt_ref, o_ref, s0_ref, s1_ref, m_ref, l_ref, acc_ref = refs
    for src, dst in zip(cast_src, cast_dst):
        dst[...] = src[...].astype(BF16)
    head = pl.program_id(1)
    tq = ATTN_TILE
    nq = q_ref.shape[0] // tq
    n_low = nq * (nq - 1) // 2
    assert PHASE_B_UNROLL % 2 == 0 and n_low % PHASE_B_UNROLL == 0
    bufs = (s0_ref, s1_ref)

    def rows(t):
        return pl.ds(pl.multiple_of(t * tq, tq), tq)

    def scores(i, j, s_ref):
        s = lax.dot_general(k_ref[rows(j), :], q_ref[rows(i), :], _NT, preferred_element_type=F32)
        if fox:
            cum_q = cq_ref[0, i, pl.ds(head, 1), :]
            cum_k = ck_ref[0, 0, rows(j), :]
            s = s + (cum_q - jnp.concatenate([cum_k] * (tq // LANES), axis=1))
        s_ref[...] = s

    half = tq // 2

    def scores_diagonal(i, s_ref):
        lo, hi = i * tq, i * tq + half
        s_a = lax.dot_general(k_ref[lo:hi, :], q_ref[lo:lo + tq, :], _NT, preferred_element_type=F32)
        s_b = lax.dot_general(k_ref[hi:hi + half, :], q_ref[hi:hi + half, :], _NT,
                              preferred_element_type=F32)
        if fox:
            cum_q = cq_ref[0, i, pl.ds(head, 1), :]
            s_a = s_a + (cum_q - jnp.concatenate([ck_ref[0, 0, lo:hi, :]] * (tq // LANES), axis=1))
            s_b = s_b + (cum_q[:, half:]
                         - jnp.concatenate([ck_ref[0, 0, hi:hi + half, :]] * (half // LANES), axis=1))
        src = lax.broadcasted_iota(jnp.int32, (half, half), 0) // mask_chunk
        dst = lax.broadcasted_iota(jnp.int32, (half, half), 1) // mask_chunk
        s_ref[:half, :half] = jnp.where(src <= dst, s_a[:, :half], NEG_BIG)
        s_ref[:half, half:] = s_a[:, half:]
        s_ref[half:, half:] = jnp.where(src <= dst, s_b, NEG_BIG)

    def accumulate_diagonal(i, s_ref):
        s_a = s_ref[:half, :]
        s_b = s_ref[half:, half:]
        m_a = jnp.max(s_a, axis=0, keepdims=True)
        m_hi = jnp.maximum(m_a[:, half:], jnp.max(s_b, axis=0, keepdims=True))
        m = jnp.concatenate([m_a[:, :half], m_hi], axis=1)
        p_a = jnp.exp2(s_a - m)
        p_b = jnp.exp2(s_b - m_hi)
        l_a = jnp.sum(p_a, axis=0, keepdims=True)
        m_ref[i] = m
        l_ref[i] = jnp.concatenate([l_a[:, :half], l_a[:, half:] + jnp.sum(p_b, axis=0, keepdims=True)],
                                   axis=1)
        acc_a = jnp.dot(vt_ref[0, i, :, :half], p_a.astype(BF16), preferred_element_type=F32)
        acc_b = jnp.dot(vt_ref[0, i, :, half:], p_b.astype(BF16), preferred_element_type=F32)
        acc_ref[i, :, :half] = acc_a[:, :half]
        acc_ref[i, :, half:] = acc_a[:, half:] + acc_b

    def accumulate(i, j, s_ref):
        s = s_ref[...]
        m_prev = m_ref[i]
        m_new = jnp.maximum(m_prev, jnp.max(s, axis=0, keepdims=True))
        alpha = jnp.exp2(m_prev - m_new)
        p = jnp.exp2(s - m_new)
        l_ref[i] = alpha * l_ref[i] + jnp.sum(p, axis=0, keepdims=True)
        acc_ref[i] = alpha * acc_ref[i] + jnp.dot(vt_ref[0, j], p.astype(BF16),
                                                  preferred_element_type=F32)
        m_ref[i] = m_new

    scores_diagonal(0, bufs[0])
    for i in range(nq):
        if i + 1 < nq:
            scores_diagonal(i + 1, bufs[(i + 1) % 2])
        else:
            scores(1, 0, bufs[(i + 1) % 2])
        accumulate_diagonal(i, bufs[i % 2])

    def advance(i, j):
        wraps = j + 1 == i
        i_nxt = jnp.minimum(jnp.where(wraps, i + 1, i), nq - 1)
        return i_nxt, jnp.where(wraps, 0, j + 1)

    def body(_, carry):
        i, j = carry
        for u in range(PHASE_B_UNROLL):
            i_nxt, j_nxt = advance(i, j)
            scores(i_nxt, j_nxt, bufs[(nq + u + 1) % 2])
            accumulate(i, j, bufs[(nq + u) % 2])
            i, j = i_nxt, j_nxt
        return i, j

    lax.fori_loop(0, n_low // PHASE_B_UNROLL, body, (jnp.int32(1), jnp.int32(0)))

    for i in range(nq):
        o_ref[i * tq:(i + 1) * tq, :] = (acc_ref[i] / l_ref[i]).T.astype(o_ref.dtype)


def _attention(q_arr, k_arr, vt_arr, cum_k=None, cum_q=None, *, batch, seq, heads, dk, dv,
               q_col, k_col, mask_chunk, riders=()):
    fox = cum_k is not None
    tq = ATTN_TILE
    nq = seq // tq
    steps = batch * heads
    in_specs = [pl.BlockSpec((seq, dk), lambda b, h: (b, q_col + h)),
                pl.BlockSpec((seq, dk), lambda b, h: (b, k_col + h)),
                pl.BlockSpec((1, nq, dv, tq), lambda b, h: (b, 0, h, 0))]
    args = [q_arr, k_arr, vt_arr]
    if fox:
        in_specs += [pl.BlockSpec((1, 1, seq, LANES), lambda b, h: (b, h, 0, 0)),
                     pl.BlockSpec((1, nq, heads, tq), lambda b, h: (b, 0, 0, 0))]
        args += [cum_k, cum_q]
    slab_specs = [pl.BlockSpec((w.shape[0] // steps, w.shape[1]), lambda b, h: (b * heads + h, 0))
                  for w in riders]
    vmem = (2 * seq * (2 * dk + 2 * dv) * 2 + 2 * seq * LANES * 4 + 2 * seq * heads * 4
            + seq * dv * 4 + 8 * tq * tq * 4 + sum(2 * 6 * w.size // steps for w in riders))
    return pl.pallas_call(
        functools.partial(_attn_kernel, mask_chunk=mask_chunk, fox=fox),
        grid=(batch, heads),
        in_specs=in_specs + slab_specs,
        out_specs=[pl.BlockSpec((seq, dv), lambda b, h: (b, h))] + slab_specs,
        out_shape=[jax.ShapeDtypeStruct((batch * seq, heads * dv), BF16)]
                  + [jax.ShapeDtypeStruct(w.shape, BF16) for w in riders],
        scratch_shapes=[pltpu.VMEM((tq, tq), F32), pltpu.VMEM((tq, tq), F32),
                        pltpu.VMEM((nq, 1, tq), F32), pltpu.VMEM((nq, 1, tq), F32),
                        pltpu.VMEM((nq, dv, tq), F32)],
        compiler_params=_params(vmem),
        name="fox_attn" if fox else "mla_attn",
    )(*args, *riders)


def _outproj_kernel(x_ref, oa_ref, ob_ref, wa_ref, wb_ref, gpost_ref, gt_ref, o_ref):
    y = jnp.dot(oa_ref[...], wa_ref[...], preferred_element_type=F32)
    y = y + jnp.dot(ob_ref[...], wb_ref[...], preferred_element_type=F32)
    o_ref[...] = x_ref[...] + gt_ref[0] * _rms(y, gpost_ref[...])


def _outproj(x, o_a, o_b, w_a, w_b, g_post, gt, *, seq, tm):
    t, d = x.shape
    ka, kb = o_a.shape[1], o_b.shape[1]
    tpb = seq // tm
    vmem = 4 * tm * d * 4 + 2 * tm * (ka + kb) * 2 + 2 * (ka + kb) * d * 2 + 2 * tm * d * 4
    return pl.pallas_call(
        _outproj_kernel,
        grid=(t // tm,),
        in_specs=[pl.BlockSpec((tm, d), lambda i: (i, 0)),
                  pl.BlockSpec((tm, ka), lambda i: (i, 0)),
                  pl.BlockSpec((tm, kb), lambda i: (i, 0)),
                  pl.BlockSpec((ka, d), lambda i: (0, 0)),
                  pl.BlockSpec((kb, d), lambda i: (0, 0)),
                  pl.BlockSpec((1, d), lambda i: (0, 0)),
                  pl.BlockSpec((1, 1, d), lambda i: (i // tpb, 0, 0))],
        out_specs=pl.BlockSpec((tm, d), lambda i: (i, 0)),
        out_shape=jax.ShapeDtypeStruct((t, d), F32),
        compiler_params=_params(vmem),
        name="out_proj",
    )(x, o_a, o_b, w_a, w_b, g_post, gt)


def _rope_tables(seq):
    half = MLA_ROPE // 2
    inv = ROPE_THETA ** (-jnp.arange(half, dtype=F32) / half)
    ang = jnp.arange(seq).astype(F32)[:, None] * inv[None, :]
    cos, sin = jnp.cos(ang), jnp.sin(ang)
    zero = jnp.zeros_like(cos)
    pad = jnp.zeros((seq, LANES - 2 * half), F32)
    return (jnp.concatenate([cos, cos, pad], axis=1),
            jnp.concatenate([-sin, zero, pad], axis=1),
            jnp.concatenate([zero, sin, pad], axis=1))


def kernel(x, c, w_ada, b_ada, g_ffn1_pre, g_ffn1_post, w1_gate, w1_up, w1_down, g_mix_pre,
           g_mix_post, w_in, b_forget, g_q_a, w_uq, g_kv_a, w_ukv, w_o, g_ffn2_pre, g_ffn2_post,
           w2_gate, w2_up, w2_down):
    batch, seq, d = x.shape
    depth = w_ada.shape[0]
    t = batch * seq
    xt = x.reshape(t, d)
    c_pad = jnp.pad(c, ((0, 8 - batch), (0, 0)))
    cos_t, sin_lo, sin_hi = _rope_tables(seq)

    for l in range(depth):
        mod = _ada(c_pad, w_ada[l], b_ada[l:l + 1], tn=1024)[:batch]
        sh1, sc1, gt1, sh2, sc2, gt2, sh3, sc3, gt3 = [
            mod[:, n * d:(n + 1) * d].reshape(batch, 1, d) for n in range(9)]

        xt = _ffn(xt, sh1, sc1, gt1, g_ffn1_pre[l:l + 1], g_ffn1_post[l:l + 1],
                  w1_gate[l].astype(BF16), w1_up[l].astype(BF16), w1_down[l].astype(BF16),
                  seq=seq, tm=1024, tf=512, res_weight=0.5)

        wi = w_in[l]
        fox0 = KPE_COL + MLA_ROPE
        fv0 = fox0 + 2 * FOX_WIDTH
        fl0 = fox0 + 3 * FOX_WIDTH
        w_in_p = jnp.concatenate(
            [wi[:, :fox0], jnp.zeros((d, LANES - MLA_ROPE), F32),
             wi[:, fl0:], jnp.zeros((d, LANES - FOX_HEADS), F32),
             wi[:, fox0:fv0]], axis=1).astype(BF16)
        w_fvt = wi[:, fv0:fl0].T.astype(BF16)

        wuq_p = jnp.pad(w_uq[l].reshape(MLA_Q_RANK, MLA_HEADS, MLA_NOPE + MLA_ROPE),
                        ((0, 0), (0, 0), (0, MLA_QK_PAD - MLA_NOPE - MLA_ROPE))
                        ).reshape(MLA_Q_RANK, MLA_HEADS * MLA_QK_PAD).astype(BF16)
        wukv = w_ukv[l].reshape(MLA_KV_RANK, MLA_HEADS, MLA_NOPE + MLA_V)
        wuk = wukv[:, :, :MLA_NOPE].reshape(MLA_KV_RANK, -1).astype(BF16)
        wuvt = wukv[:, :, MLA_NOPE:].reshape(MLA_KV_RANK, -1).T.astype(BF16)
        bf_pad = jnp.pad(b_forget[l:l + 1], ((0, 0), (0, LANES - FOX_HEADS)))
        fqk, fvt, q_mla, k_mla, vt_mla, cum_k, cum_q = _mixer_in(
            xt, sh2, sc2, g_mix_pre[l:l + 1], w_in_p, w_fvt, g_q_a[l:l + 1], g_kv_a[l:l + 1],
            wuq_p, wuk, wuvt, bf_pad, cos_t, sin_lo, sin_hi, batch=batch, seq=seq)

        o_mla, w2g, w2u = _attention(q_mla, k_mla, vt_mla, batch=batch, seq=seq, heads=MLA_HEADS,
                                     dk=MLA_QK_PAD, dv=MLA_V, q_col=0, k_col=0, mask_chunk=CHUNK,
                                     riders=(w2_gate[l], w2_up[l]))
        o_fox, w2d = _attention(fqk, fqk, fvt, cum_k, cum_q, batch=batch, seq=seq, heads=FOX_HEADS,
                                dk=FOX_DIM, dv=FOX_DIM, q_col=0, k_col=FOX_HEADS, mask_chunk=1,
                                riders=(w2_down[l],))

        wo = w_o[l].astype(BF16)
        xt = _outproj(xt, o_mla, o_fox, wo[:MLA_HEADS * MLA_V], wo[MLA_HEADS * MLA_V:],
                      g_mix_post[l:l + 1], gt2, seq=seq, tm=512)

        xt = _ffn(xt, sh3, sc3, gt3, g_ffn2_pre[l:l + 1], g_ffn2_post[l:l + 1],
                  w2g, w2u, w2d, seq=seq, tm=1024, tf=512, res_weight=0.5)

    return xt.reshape(batch, seq, d)
```

```python
import functools
import math

import jax
import jax.numpy as jnp
from jax import lax
from jax.experimental import pallas as pl
from jax.experimental.pallas import tpu as pltpu

F32 = jnp.float32
BF16 = jnp.bfloat16

V7X_VMEM_BYTES = 64 * 1024 * 1024
LANES = 128

EPS = 1e-6
ROPE_THETA = 10000.0
CHUNK = 64
MLA_HEADS = 8
MLA_Q_RANK = 512
MLA_KV_RANK = 256
MLA_NOPE = 128
MLA_ROPE = 64
MLA_V = 128
MLA_QK_PAD = 256
FOX_HEADS = 8
FOX_DIM = 128
FOX_WIDTH = FOX_HEADS * FOX_DIM
LATENT_WIDTH = 1024
KPE_COL = MLA_Q_RANK + MLA_KV_RANK
FLOGIT_COL = KPE_COL + LANES
ATTN_TILE = 512
PHASE_B_UNROLL = 14
LOG2E = math.log2(math.e)
MLA_Q_SCALE = LOG2E / math.sqrt(MLA_NOPE + MLA_ROPE)
FOX_Q_SCALE = LOG2E / math.sqrt(FOX_DIM)
NEG_BIG = -1e30

_NT = (((1,), (1,)), ((), ()))


def _params(vmem_bytes):
    limit = min(int(vmem_bytes * 1.25) + (4 << 20), V7X_VMEM_BYTES - (8 << 20))
    return pltpu.CompilerParams(vmem_limit_bytes=limit)


def _rms(x, g):
    return x * lax.rsqrt(jnp.mean(x * x, axis=-1, keepdims=True) + EPS) * g


STAT_ROWS = 8
APPLY_ROWS = 16


def _rms_stats(src_ref, stat_ref):
    tm, d = src_ref.shape

    def step(c, carry):
        rows = pl.ds(pl.multiple_of(c * STAT_ROWS, STAT_ROWS), STAT_ROWS)
        x = src_ref[rows, :]
        ms = jnp.sum(x * x, axis=-1, keepdims=True) * (1.0 / d)
        stat_ref[rows, :] = jnp.broadcast_to(lax.rsqrt(ms + EPS), (STAT_ROWS, LANES))
        return carry

    lax.fori_loop(0, tm // STAT_ROWS, step, 0, unroll=32)


def _for_row_blocks(tm, fn):
    def step(c, carry):
        fn(pl.ds(pl.multiple_of(c * APPLY_ROWS, APPLY_ROWS), APPLY_ROWS))
        return carry

    lax.fori_loop(0, tm // APPLY_ROWS, step, 0, unroll=4)


def _modulated_norm(x_ref, stat_ref, coef_ref, gpre_ref, sc_ref, sh_ref, h_ref):
    tm, d = x_ref.shape
    coef_ref[0:1, :] = gpre_ref[...] * (1.0 + sc_ref[0])
    coef_ref[1:2, :] = sh_ref[0]
    _rms_stats(x_ref, stat_ref)

    def apply(rows):
        r = stat_ref[rows, :]
        for k in range(d // LANES):
            cols = slice(k * LANES, (k + 1) * LANES)
            h = x_ref[rows, cols] * r * coef_ref[0:1, cols] + coef_ref[1:2, cols]
            h_ref[rows, cols] = h.astype(BF16)

    _for_row_blocks(tm, apply)


def _gated_norm_residual(x_ref, o_ref, stat_ref, coef_ref, gpost_ref, gt_ref, res_weight):
    tm, d = o_ref.shape
    coef_ref[0:1, :] = res_weight * gt_ref[0] * gpost_ref[...]
    _rms_stats(o_ref, stat_ref)

    def apply(rows):
        r = stat_ref[rows, :]
        for k in range(d // LANES):
            cols = slice(k * LANES, (k + 1) * LANES)
            o_ref[rows, cols] = x_ref[rows, cols] + o_ref[rows, cols] * r * coef_ref[0:1, cols]

    _for_row_blocks(tm, apply)


def _ada_kernel(c_ref, w_ref, b_ref, o_ref):
    c = c_ref[...]
    cond = (c * jax.nn.sigmoid(c)).astype(BF16)
    o_ref[...] = jnp.dot(cond, w_ref[...].astype(BF16), preferred_element_type=F32) + b_ref[...]


def _ada(c_pad, w, b, *, tn):
    m, d = c_pad.shape
    n = w.shape[1]
    vmem = 2 * d * tn * 4 + d * tn * 2 + 4 * m * tn * 4
    return pl.pallas_call(
        _ada_kernel,
        grid=(n // tn,),
        in_specs=[pl.BlockSpec((m, d), lambda j: (0, 0)),
                  pl.BlockSpec((d, tn), lambda j: (0, j)),
                  pl.BlockSpec((1, tn), lambda j: (0, j))],
        out_specs=pl.BlockSpec((m, tn), lambda j: (0, j)),
        out_shape=jax.ShapeDtypeStruct((m, n), F32),
        compiler_params=_params(vmem),
        name="ada",
    )(c_pad, w, b)


def _ffn_kernel(x_ref, sh_ref, sc_ref, gt_ref, gpre_ref, gpost_ref, wg_ref, wu_ref, wd_ref,
                o_ref, h_ref, stat_ref, coef_ref, *, res_weight):
    f = pl.program_id(1)

    def swiglu_down(h):
        g = jnp.dot(h, wg_ref[...], preferred_element_type=F32)
        u = jnp.dot(h, wu_ref[...], preferred_element_type=F32)
        a = (g * jax.nn.sigmoid(g) * u).astype(BF16)
        return jnp.dot(a, wd_ref[...], preferred_element_type=F32)

    @pl.when(f == 0)
    def _():
        h = (_rms(x_ref[...], gpre_ref[...]) * (1.0 + sc_ref[0]) + sh_ref[0]).astype(BF16)
        h_ref[...] = h
        o_ref[...] = swiglu_down(h)

    @pl.when(f > 0)
    def _():
        o_ref[...] += swiglu_down(h_ref[...])

    @pl.when(f == pl.num_programs(1) - 1)
    def _():
        _gated_norm_residual(x_ref, o_ref, stat_ref, coef_ref, gpost_ref, gt_ref, res_weight)


def _ffn(x, sh, sc, gt, g_pre, g_post, wg, wu, wd, *, seq, tm, tf, res_weight):
    t, d = x.shape
    ff = wg.shape[1]
    tpb = seq // tm
    vmem = 4 * tm * d * 4 + tm * d * 2 + 6 * d * tf * 2 + 3 * tm * tf * 4
    mod_spec = pl.BlockSpec((1, 1, d), lambda i, f: (i // tpb, 0, 0))
    gain_spec = pl.BlockSpec((1, d), lambda i, f: (0, 0))
    return pl.pallas_call(
        functools.partial(_ffn_kernel, res_weight=res_weight),
        grid=(t // tm, ff // tf),
        in_specs=[pl.BlockSpec((tm, d), lambda i, f: (i, 0)),
                  mod_spec, mod_spec, mod_spec, gain_spec, gain_spec,
                  pl.BlockSpec((d, tf), lambda i, f: (0, f)),
                  pl.BlockSpec((d, tf), lambda i, f: (0, f)),
                  pl.BlockSpec((tf, d), lambda i, f: (f, 0))],
        out_specs=pl.BlockSpec((tm, d), lambda i, f: (i, 0)),
        out_shape=jax.ShapeDtypeStruct((t, d), F32),
        scratch_shapes=[pltpu.VMEM((tm, d), BF16), pltpu.VMEM((tm, LANES), F32), pltpu.VMEM((8, d), F32)],
        compiler_params=_params(vmem),
        name="ffn",
    )(x, sh, sc, gt, g_pre, g_post, wg, wu, wd)


def _rot(r, cos_t, sin_lo, sin_hi):
    return r * cos_t + pltpu.roll(r, 96, 1) * sin_lo + pltpu.roll(r, 32, 1) * sin_hi


def _mla_fox_prep(lat, gq_ref, gkv_ref, wuq_ref, wuk_ref, wuvt_ref, bf_ref, cos_ref, slo_ref, shi_ref,
                  q_ref, k_ref, vt_ref, ck_ref, cq_ref, carry_ref):
    tm = lat.shape[0]
    cos_t, sin_lo, sin_hi = cos_ref[...], slo_ref[...], shi_ref[...]

    qn = _rms(lat[:, :MLA_Q_RANK], gq_ref[...]).astype(BF16)
    q = jnp.dot(qn, wuq_ref[...], preferred_element_type=F32) * MLA_Q_SCALE
    kvn = _rms(lat[:, MLA_Q_RANK:KPE_COL], gkv_ref[...]).astype(BF16)
    k_nope = jnp.dot(kvn, wuk_ref[...], preferred_element_type=F32)
    vt_ref[0, 0] = lax.dot_general(wuvt_ref[...], kvn, _NT, preferred_element_type=F32).astype(BF16)
    k_rope = _rot(lat[:, KPE_COL:KPE_COL + LANES], cos_t, sin_lo, sin_hi).astype(BF16)
    for h in range(MLA_HEADS):
        c0 = h * MLA_QK_PAD
        q_ref[:, c0:c0 + MLA_NOPE] = q[:, c0:c0 + MLA_NOPE].astype(BF16)
        q_ref[:, c0 + MLA_NOPE:c0 + MLA_QK_PAD] = _rot(
            q[:, c0 + MLA_NOPE:c0 + MLA_QK_PAD], cos_t, sin_lo, sin_hi).astype(BF16)
        k_ref[:, c0:c0 + MLA_NOPE] = k_nope[:, h * MLA_NOPE:(h + 1) * MLA_NOPE].astype(BF16)
        k_ref[:, c0 + MLA_NOPE:c0 + MLA_QK_PAD] = k_rope

    z = lat[:, FLOGIT_COL:FLOGIT_COL + LANES] + bf_ref[...]
    lane = lax.broadcasted_iota(jnp.int32, (tm, LANES), 1)
    log_f = jnp.where(lane < FOX_HEADS, jnp.minimum(z, 0.0) - jnp.log1p(jnp.exp(-jnp.abs(z))), 0.0)
    hi = log_f.astype(BF16).astype(F32)
    mid = (log_f - hi).astype(BF16).astype(F32)
    lo = (log_f - hi - mid).astype(BF16).astype(F32)
    pieces = hi + pltpu.roll(mid, FOX_HEADS, 1) + pltpu.roll(lo, 2 * FOX_HEADS, 1)
    row = lax.broadcasted_iota(jnp.int32, (tm, tm), 0)
    col = lax.broadcasted_iota(jnp.int32, (tm, tm), 1)
    tri = (col <= row).astype(BF16)
    part = jnp.dot(tri, pieces.astype(BF16), preferred_element_type=F32)
    part = part + pltpu.roll(part, LANES - FOX_HEADS, 1) + pltpu.roll(part, LANES - 2 * FOX_HEADS, 1)
    cum = jnp.where(lane < FOX_HEADS, part, 0.0) + carry_ref[...]
    carry_ref[...] = cum[tm - 1:tm, :]
    cum2 = cum * LOG2E
    cq_ref[0, 0] = cum2.T[:FOX_HEADS, :]
    for h in range(FOX_HEADS):
        ck_ref[0, h] = jnp.broadcast_to(cum2[:, h:h + 1], (tm, LANES))


def _mixer_in_kernel(x_ref, sh_ref, sc_ref, gpre_ref, w_ref, wvt_ref,
                     gq_ref, gkv_ref, wuq_ref, wuk_ref, wuvt_ref, bf_ref, cos_ref, slo_ref, shi_ref,
                     fqk_ref, fvt_ref, q_ref, k_ref, vt_ref, ck_ref, cq_ref, carry_ref, *, tiles_per_batch):
    @pl.when(pl.program_id(0) % tiles_per_batch == 0)
    def _():
        carry_ref[...] = jnp.zeros_like(carry_ref)

    tn = LATENT_WIDTH
    h = (_rms(x_ref[...], gpre_ref[...]) * (1.0 + sc_ref[0]) + sh_ref[0]).astype(BF16)
    lat = jnp.dot(h, w_ref[:, :tn], preferred_element_type=F32)
    fq = jnp.dot(h, w_ref[:, tn:2 * tn], preferred_element_type=F32) * FOX_Q_SCALE
    fqk_ref[:, :tn] = fq.astype(BF16)
    fqk_ref[:, tn:] = jnp.dot(h, w_ref[:, 2 * tn:], preferred_element_type=F32).astype(BF16)
    fvt_ref[0, 0] = lax.dot_general(wvt_ref[...], h, _NT, preferred_element_type=F32).astype(BF16)
    _mla_fox_prep(lat, gq_ref, gkv_ref, wuq_ref, wuk_ref, wuvt_ref, bf_ref, cos_ref, slo_ref, shi_ref,
                  q_ref, k_ref, vt_ref, ck_ref, cq_ref, carry_ref)


def _mixer_in(x, sh, sc, g_pre, w, wvt, g_q, g_kv, wuq, wuk, wuvt, b_forget, cos_t, sin_lo, sin_hi,
              *, batch, seq):
    t, d = x.shape
    tm = ATTN_TILE
    tn = LATENT_WIDTH
    tpb = seq // tm
    qk_w = MLA_HEADS * MLA_QK_PAD
    v_w = MLA_HEADS * MLA_V
    assert w.shape == (d, 3 * tn) and wvt.shape == (FOX_WIDTH, d)
    resident_bytes = (w.size + wvt.size + wuq.size + wuk.size + wuvt.size) * 2
    vmem = (2 * tm * d * 4 + tm * d * 2 + resident_bytes + 2 * tm * (2 * tn + FOX_WIDTH) * 2
            + 2 * tm * (2 * qk_w + v_w) * 2 + 2 * FOX_HEADS * tm * LANES * 4 + 6 * tm * LANES * 4
            + tm * (tn + qk_w) * 4)
    mod_spec = pl.BlockSpec((1, 1, d), lambda i: (i // tpb, 0, 0))
    resident = lambda a: pl.BlockSpec(a.shape, lambda i: (0, 0), pipeline_mode=pl.Buffered(1))
    small = lambda a: pl.BlockSpec(a.shape, lambda i: (0, 0))
    tab_spec = pl.BlockSpec((tm, LANES), lambda i: (i % tpb, 0))
    return pl.pallas_call(
        functools.partial(_mixer_in_kernel, tiles_per_batch=tpb),
        grid=(t // tm,),
        in_specs=[pl.BlockSpec((tm, d), lambda i: (i, 0)),
                  mod_spec, mod_spec, small(g_pre), resident(w), resident(wvt),
                  small(g_q), small(g_kv), resident(wuq), resident(wuk), resident(wuvt), small(b_forget),
                  tab_spec, tab_spec, tab_spec],
        out_specs=[pl.BlockSpec((tm, 2 * tn), lambda i: (i, 0)),
                   pl.BlockSpec((1, 1, FOX_WIDTH, tm), lambda i: (i // tpb, i % tpb, 0, 0)),
                   pl.BlockSpec((tm, qk_w), lambda i: (i, 0)),
                   pl.BlockSpec((tm, qk_w), lambda i: (i, 0)),
                   pl.BlockSpec((1, 1, v_w, tm), lambda i: (i // tpb, i % tpb, 0, 0)),
                   pl.BlockSpec((1, FOX_HEADS, tm, LANES), lambda i: (i // tpb, 0, i % tpb, 0)),
                   pl.BlockSpec((1, 1, FOX_HEADS, tm), lambda i: (i // tpb, i % tpb, 0, 0))],
        out_shape=[jax.ShapeDtypeStruct((t, 2 * tn), BF16),
                   jax.ShapeDtypeStruct((batch, tpb, FOX_WIDTH, tm), BF16),
                   jax.ShapeDtypeStruct((t, qk_w), BF16),
                   jax.ShapeDtypeStruct((t, qk_w), BF16),
                   jax.ShapeDtypeStruct((batch, tpb, v_w, tm), BF16),
                   jax.ShapeDtypeStruct((batch, FOX_HEADS, seq, LANES), F32),
                   jax.ShapeDtypeStruct((batch, tpb, FOX_HEADS, tm), F32)],
        scratch_shapes=[pltpu.VMEM((1, LANES), F32)],
        compiler_params=_params(vmem),
        name="mixer_in",
    )(x, sh, sc, g_pre, w, wvt, g_q, g_kv, wuq, wuk, wuvt, b_forget, cos_t, sin_lo, sin_hi)


def _attn_kernel(*refs, mask_chunk, fox):
    n_in = 5 if fox else 3
    n_cast = (len(refs) - n_in - 8) // 2
    cast_src = refs[n_in:n_in + n_cast]
    cast_dst = refs[n_in + n_cast + 1:n_in + 2 * n_cast + 1]
    refs = refs[:n_in] + (refs[n_in + n_cast],) + refs[n_in + 2 * n_cast + 1:]
    if fox:
        q_ref, k_ref, vt_ref, ck_ref, cq_ref, o_ref, s0_ref, s1_ref, t0_ref, t1_ref, m_ref, l_ref, acc_ref = refs
    else:
        q_ref, k_ref, vt_ref, o_ref, s0_ref, s1_ref, t0_ref, t1_ref, m_ref, l_ref, acc_ref = refs
    for src, dst in zip(cast_src, cast_dst):
        dst[...] = src[...].astype(BF16)
    head = pl.program_id(1)
    tq = ATTN_TILE
    nq = q_ref.shape[0] // tq
    n_low = nq * (nq - 1) // 2
    assert PHASE_B_UNROLL % 2 == 0 and n_low % PHASE_B_UNROLL == 0
    bufs = (s0_ref, s1_ref)
    maxes = (t0_ref, t1_ref)

    def rows(t):
        return pl.ds(pl.multiple_of(t * tq, tq), tq)

    def scores(i, j, s_ref, max_ref):
        s = lax.dot_general(k_ref[rows(j), :], q_ref[rows(i), :], _NT, preferred_element_type=F32)
        if fox:
            cum_q = cq_ref[0, i, pl.ds(head, 1), :]
            cum_k = ck_ref[0, 0, rows(j), :]
            s = s + (cum_q - jnp.concatenate([cum_k] * (tq // LANES), axis=1))
        s_ref[...] = s
        max_ref[...] = jnp.max(s, axis=0, keepdims=True)

    half = tq // 2

    def scores_diagonal(i, s_ref):
        lo, hi = i * tq, i * tq + half
        s_a = lax.dot_general(k_ref[lo:hi, :], q_ref[lo:lo + tq, :], _NT, preferred_element_type=F32)
        s_b = lax.dot_general(k_ref[hi:hi + half, :], q_ref[hi:hi + half, :], _NT,
                              preferred_element_type=F32)
        if fox:
            cum_q = cq_ref[0, i, pl.ds(head, 1), :]
            s_a = s_a + (cum_q - jnp.concatenate([ck_ref[0, 0, lo:hi, :]] * (tq // LANES), axis=1))
            s_b = s_b + (cum_q[:, half:]
                         - jnp.concatenate([ck_ref[0, 0, hi:hi + half, :]] * (half // LANES), axis=1))
        src = lax.broadcasted_iota(jnp.int32, (half, half), 0) // mask_chunk
        dst = lax.broadcasted_iota(jnp.int32, (half, half), 1) // mask_chunk
        s_ref[:half, :half] = jnp.where(src <= dst, s_a[:, :half], NEG_BIG)
        s_ref[:half, half:] = s_a[:, half:]
        s_ref[half:, half:] = jnp.where(src <= dst, s_b, NEG_BIG)

    def accumulate_diagonal(i, s_ref):
        s_a = s_ref[:half, :]
        s_b = s_ref[half:, half:]
        m_a = jnp.max(s_a, axis=0, keepdims=True)
        m_hi = jnp.maximum(m_a[:, half:], jnp.max(s_b, axis=0, keepdims=True))
        m = jnp.concatenate([m_a[:, :half], m_hi], axis=1)
        p_a = jnp.exp2(s_a - m)
        p_b = jnp.exp2(s_b - m_hi)
        l_a = jnp.sum(p_a, axis=0, keepdims=True)
        m_ref[i] = m
        l_ref[i] = jnp.concatenate([l_a[:, :half], l_a[:, half:] + jnp.sum(p_b, axis=0, keepdims=True)],
                                   axis=1)
        acc_a = jnp.dot(vt_ref[0, i, :, :half], p_a.astype(BF16), preferred_element_type=F32)
        acc_b = jnp.dot(vt_ref[0, i, :, half:], p_b.astype(BF16), preferred_element_type=F32)
        acc_ref[i, :, :half] = acc_a[:, :half]
        acc_ref[i, :, half:] = acc_a[:, half:] + acc_b

    def accumulate(i, j, s_ref, max_ref):
        s = s_ref[...]
        m_prev = m_ref[i]
        m_new = jnp.maximum(m_prev, max_ref[...])
        alpha = jnp.exp2(m_prev - m_new)
        p = jnp.exp2(s - m_new)
        l_ref[i] = alpha * l_ref[i] + jnp.sum(p, axis=0, keepdims=True)
        acc_ref[i] = alpha * acc_ref[i] + jnp.dot(vt_ref[0, j], p.astype(BF16),
                                                  preferred_element_type=F32)
        m_ref[i] = m_new

    scores_diagonal(0, bufs[0])
    for i in range(nq):
        if i + 1 < nq:
            scores_diagonal(i + 1, bufs[(i + 1) % 2])
        else:
            scores(1, 0, bufs[(i + 1) % 2], maxes[(i + 1) % 2])
        accumulate_diagonal(i, bufs[i % 2])

    def advance(i, j):
        wraps = j + 1 == i
        i_nxt = jnp.minimum(jnp.where(wraps, i + 1, i), nq - 1)
        return i_nxt, jnp.where(wraps, 0, j + 1)

    def body(_, carry):
        i, j = carry
        for u in range(PHASE_B_UNROLL):
            i_nxt, j_nxt = advance(i, j)
            scores(i_nxt, j_nxt, bufs[(nq + u + 1) % 2], maxes[(nq + u + 1) % 2])
            accumulate(i, j, bufs[(nq + u) % 2], maxes[(nq + u) % 2])
            i, j = i_nxt, j_nxt
        return i, j

    lax.fori_loop(0, n_low // PHASE_B_UNROLL, body, (jnp.int32(1), jnp.int32(0)))

    for i in range(nq):
        o_ref[i * tq:(i + 1) * tq, :] = (acc_ref[i] / l_ref[i]).T.astype(o_ref.dtype)


def _attention(q_arr, k_arr, vt_arr, cum_k=None, cum_q=None, *, batch, seq, heads, dk, dv,
               q_col, k_col, mask_chunk, riders=()):
    fox = cum_k is not None
    tq = ATTN_TILE
    nq = seq // tq
    steps = batch * heads
    in_specs = [pl.BlockSpec((seq, dk), lambda b, h: (b, q_col + h)),
                pl.BlockSpec((seq, dk), lambda b, h: (b, k_col + h)),
                pl.BlockSpec((1, nq, dv, tq), lambda b, h: (b, 0, h, 0))]
    args = [q_arr, k_arr, vt_arr]
    if fox:
        in_specs += [pl.BlockSpec((1, 1, seq, LANES), lambda b, h: (b, h, 0, 0)),
                     pl.BlockSpec((1, nq, heads, tq), lambda b, h: (b, 0, 0, 0))]
        args += [cum_k, cum_q]
    slab_specs = [pl.BlockSpec((w.shape[0] // steps, w.shape[1]), lambda b, h: (b * heads + h, 0))
                  for w in riders]
    vmem = (2 * seq * (2 * dk + 2 * dv) * 2 + 2 * seq * LANES * 4 + 2 * seq * heads * 4
            + seq * dv * 4 + 8 * tq * tq * 4 + sum(2 * 6 * w.size // steps for w in riders))
    return pl.pallas_call(
        functools.partial(_attn_kernel, mask_chunk=mask_chunk, fox=fox),
        grid=(batch, heads),
        in_specs=in_specs + slab_specs,
        out_specs=[pl.BlockSpec((seq, dv), lambda b, h: (b, h))] + slab_specs,
        out_shape=[jax.ShapeDtypeStruct((batch * seq, heads * dv), BF16)]
                  + [jax.ShapeDtypeStruct(w.shape, BF16) for w in riders],
        scratch_shapes=[pltpu.VMEM((tq, tq), F32), pltpu.VMEM((tq, tq), F32),
                        pltpu.VMEM((1, tq), F32), pltpu.VMEM((1, tq), F32),
                        pltpu.VMEM((nq, 1, tq), F32), pltpu.VMEM((nq, 1, tq), F32),
                        pltpu.VMEM((nq, dv, tq), F32)],
        compiler_params=_params(vmem),
        name="fox_attn" if fox else "mla_attn",
    )(*args, *riders)


def _outproj_kernel(x_ref, oa_ref, ob_ref, wa_ref, wb_ref, gpost_ref, gt_ref, o_ref):
    y = jnp.dot(oa_ref[...], wa_ref[...], preferred_element_type=F32)
    y = y + jnp.dot(ob_ref[...], wb_ref[...], preferred_element_type=F32)
    o_ref[...] = x_ref[...] + gt_ref[0] * _rms(y, gpost_ref[...])


def _outproj(x, o_a, o_b, w_a, w_b, g_post, gt, *, seq, tm):
    t, d = x.shape
    ka, kb = o_a.shape[1], o_b.shape[1]
    tpb = seq // tm
    vmem = 4 * tm * d * 4 + 2 * tm * (ka + kb) * 2 + 2 * (ka + kb) * d * 2 + 2 * tm * d * 4
    return pl.pallas_call(
        _outproj_kernel,
        grid=(t // tm,),
        in_specs=[pl.BlockSpec((tm, d), lambda i: (i, 0)),
                  pl.BlockSpec((tm, ka), lambda i: (i, 0)),
                  pl.BlockSpec((tm, kb), lambda i: (i, 0)),
                  pl.BlockSpec((ka, d), lambda i: (0, 0)),
                  pl.BlockSpec((kb, d), lambda i: (0, 0)),
                  pl.BlockSpec((1, d), lambda i: (0, 0)),
                  pl.BlockSpec((1, 1, d), lambda i: (i // tpb, 0, 0))],
        out_specs=pl.BlockSpec((tm, d), lambda i: (i, 0)),
        out_shape=jax.ShapeDtypeStruct((t, d), F32),
        compiler_params=_params(vmem),
        name="out_proj",
    )(x, o_a, o_b, w_a, w_b, g_post, gt)


def _rope_tables(seq):
    half = MLA_ROPE // 2
    inv = ROPE_THETA ** (-jnp.arange(half, dtype=F32) / half)
    ang = jnp.arange(seq).astype(F32)[:, None] * inv[None, :]
    cos, sin = jnp.cos(ang), jnp.sin(ang)
    zero = jnp.zeros_like(cos)
    pad = jnp.zeros((seq, LANES - 2 * half), F32)
    return (jnp.concatenate([cos, cos, pad], axis=1),
            jnp.concatenate([-sin, zero, pad], axis=1),
            jnp.concatenate([zero, sin, pad], axis=1))


def kernel(x, c, w_ada, b_ada, g_ffn1_pre, g_ffn1_post, w1_gate, w1_up, w1_down, g_mix_pre,
           g_mix_post, w_in, b_forget, g_q_a, w_uq, g_kv_a, w_ukv, w_o, g_ffn2_pre, g_ffn2_post,
           w2_gate, w2_up, w2_down):
    batch, seq, d = x.shape
    depth = w_ada.shape[0]
    t = batch * seq
    xt = x.reshape(t, d)
    c_pad = jnp.pad(c, ((0, 8 - batch), (0, 0)))
    cos_t, sin_lo, sin_hi = _rope_tables(seq)

    for l in range(depth):
        mod = _ada(c_pad, w_ada[l], b_ada[l:l + 1], tn=1024)[:batch]
        sh1, sc1, gt1, sh2, sc2, gt2, sh3, sc3, gt3 = [
            mod[:, n * d:(n + 1) * d].reshape(batch, 1, d) for n in range(9)]

        xt = _ffn(xt, sh1, sc1, gt1, g_ffn1_pre[l:l + 1], g_ffn1_post[l:l + 1],
                  w1_gate[l].astype(BF16), w1_up[l].astype(BF16), w1_down[l].astype(BF16),
                  seq=seq, tm=1024, tf=512, res_weight=0.5)

        wi = w_in[l]
        fox0 = KPE_COL + MLA_ROPE
        fv0 = fox0 + 2 * FOX_WIDTH
        fl0 = fox0 + 3 * FOX_WIDTH
        w_in_p = jnp.concatenate(
            [wi[:, :fox0], jnp.zeros((d, LANES - MLA_ROPE), F32),
             wi[:, fl0:], jnp.zeros((d, LANES - FOX_HEADS), F32),
             wi[:, fox0:fv0]], axis=1).astype(BF16)
        w_fvt = wi[:, fv0:fl0].T.astype(BF16)

        wuq_p = jnp.pad(w_uq[l].reshape(MLA_Q_RANK, MLA_HEADS, MLA_NOPE + MLA_ROPE),
                        ((0, 0), (0, 0), (0, MLA_QK_PAD - MLA_NOPE - MLA_ROPE))
                        ).reshape(MLA_Q_RANK, MLA_HEADS * MLA_QK_PAD).astype(BF16)
        wukv = w_ukv[l].reshape(MLA_KV_RANK, MLA_HEADS, MLA_NOPE + MLA_V)
        wuk = wukv[:, :, :MLA_NOPE].reshape(MLA_KV_RANK, -1).astype(BF16)
        wuvt = wukv[:, :, MLA_NOPE:].reshape(MLA_KV_RANK, -1).T.astype(BF16)
        bf_pad = jnp.pad(b_forget[l:l + 1], ((0, 0), (0, LANES - FOX_HEADS)))
        fqk, fvt, q_mla, k_mla, vt_mla, cum_k, cum_q = _mixer_in(
            xt, sh2, sc2, g_mix_pre[l:l + 1], w_in_p, w_fvt, g_q_a[l:l + 1], g_kv_a[l:l + 1],
            wuq_p, wuk, wuvt, bf_pad, cos_t, sin_lo, sin_hi, batch=batch, seq=seq)

        o_mla, w2g, w2u = _attention(q_mla, k_mla, vt_mla, batch=batch, seq=seq, heads=MLA_HEADS,
                                     dk=MLA_QK_PAD, dv=MLA_V, q_col=0, k_col=0, mask_chunk=CHUNK,
                                     riders=(w2_gate[l], w2_up[l]))
        o_fox, w2d = _attention(fqk, fqk, fvt, cum_k, cum_q, batch=batch, seq=seq, heads=FOX_HEADS,
                                dk=FOX_DIM, dv=FOX_DIM, q_col=0, k_col=FOX_HEADS, mask_chunk=1,
                                riders=(w2_down[l],))

        wo = w_o[l].astype(BF16)
        xt = _outproj(xt, o_mla, o_fox, wo[:MLA_HEADS * MLA_V], wo[MLA_HEADS * MLA_V:],
                      g_mix_post[l:l + 1], gt2, seq=seq, tm=512)

        xt = _ffn(xt, sh3, sc3, gt3, g_ffn2_pre[l:l + 1], g_ffn2_post[l:l + 1],
                  w2g, w2u, w2d, seq=seq, tm=1024, tf=512, res_weight=0.5)

    return xt.reshape(batch, seq, d)
```

```python
import functools
import math

import jax
import jax.numpy as jnp
from jax import lax
from jax.experimental import pallas as pl
from jax.experimental.pallas import tpu as pltpu

F32 = jnp.float32
BF16 = jnp.bfloat16

V7X_VMEM_BYTES = 64 * 1024 * 1024
LANES = 128

EPS = 1e-6
ROPE_THETA = 10000.0
CHUNK = 64
MLA_HEADS = 8
MLA_Q_RANK = 512
MLA_KV_RANK = 256
MLA_NOPE = 128
MLA_ROPE = 64
MLA_V = 128
MLA_QK_PAD = 256
FOX_HEADS = 8
FOX_DIM = 128
FOX_WIDTH = FOX_HEADS * FOX_DIM
LATENT_WIDTH = 1024
KPE_COL = MLA_Q_RANK + MLA_KV_RANK
FLOGIT_COL = KPE_COL + LANES
ATTN_TILE = 512
PHASE_B_UNROLL = 14
LOG2E = math.log2(math.e)
MLA_Q_SCALE = LOG2E / math.sqrt(MLA_NOPE + MLA_ROPE)
FOX_Q_SCALE = LOG2E / math.sqrt(FOX_DIM)
NEG_BIG = -1e30

_NT = (((1,), (1,)), ((), ()))


def _params(vmem_bytes):
    limit = min(int(vmem_bytes * 1.25) + (4 << 20), V7X_VMEM_BYTES - (8 << 20))
    return pltpu.CompilerParams(vmem_limit_bytes=limit)


def _rms(x, g):
    return x * lax.rsqrt(jnp.mean(x * x, axis=-1, keepdims=True) + EPS) * g


STAT_ROWS = 8
APPLY_ROWS = 16


def _rms_stats(src_ref, stat_ref):
    tm, d = src_ref.shape

    def step(c, carry):
        rows = pl.ds(pl.multiple_of(c * STAT_ROWS, STAT_ROWS), STAT_ROWS)
        x = src_ref[rows, :]
        ms = jnp.sum(x * x, axis=-1, keepdims=True) * (1.0 / d)
        stat_ref[rows, :] = jnp.broadcast_to(lax.rsqrt(ms + EPS), (STAT_ROWS, LANES))
        return carry

    lax.fori_loop(0, tm // STAT_ROWS, step, 0, unroll=32)


def _for_row_blocks(tm, fn):
    def step(c, carry):
        fn(pl.ds(pl.multiple_of(c * APPLY_ROWS, APPLY_ROWS), APPLY_ROWS))
        return carry

    lax.fori_loop(0, tm // APPLY_ROWS, step, 0, unroll=4)


def _modulated_norm(x_ref, stat_ref, coef_ref, gpre_ref, sc_ref, sh_ref, h_ref):
    tm, d = x_ref.shape
    coef_ref[0:1, :] = gpre_ref[...] * (1.0 + sc_ref[0])
    coef_ref[1:2, :] = sh_ref[0]
    _rms_stats(x_ref, stat_ref)

    def apply(rows):
        r = stat_ref[rows, :]
        for k in range(d // LANES):
            cols = slice(k * LANES, (k + 1) * LANES)
            h = x_ref[rows, cols] * r * coef_ref[0:1, cols] + coef_ref[1:2, cols]
            h_ref[rows, cols] = h.astype(BF16)

    _for_row_blocks(tm, apply)


def _gated_norm_residual(x_ref, o_ref, stat_ref, coef_ref, gpost_ref, gt_ref, res_weight):
    tm, d = o_ref.shape
    coef_ref[0:1, :] = res_weight * gt_ref[0] * gpost_ref[...]
    _rms_stats(o_ref, stat_ref)

    def apply(rows):
        r = stat_ref[rows, :]
        for k in range(d // LANES):
            cols = slice(k * LANES, (k + 1) * LANES)
            o_ref[rows, cols] = x_ref[rows, cols] + o_ref[rows, cols] * r * coef_ref[0:1, cols]

    _for_row_blocks(tm, apply)


def _ada_kernel(c_ref, w_ref, b_ref, o_ref):
    c = c_ref[...]
    cond = (c * jax.nn.sigmoid(c)).astype(BF16)
    o_ref[...] = jnp.dot(cond, w_ref[...].astype(BF16), preferred_element_type=F32) + b_ref[...]


def _ada(c_pad, w, b, *, tn):
    m, d = c_pad.shape
    n = w.shape[1]
    vmem = 2 * d * tn * 4 + d * tn * 2 + 4 * m * tn * 4
    return pl.pallas_call(
        _ada_kernel,
        grid=(n // tn,),
        in_specs=[pl.BlockSpec((m, d), lambda j: (0, 0)),
                  pl.BlockSpec((d, tn), lambda j: (0, j)),
                  pl.BlockSpec((1, tn), lambda j: (0, j))],
        out_specs=pl.BlockSpec((m, tn), lambda j: (0, j)),
        out_shape=jax.ShapeDtypeStruct((m, n), F32),
        compiler_params=_params(vmem),
        name="ada",
    )(c_pad, w, b)


def _ffn_kernel(*refs, res_weight, mode):
    x_ref, sh_ref, sc_ref, gt_ref, gpre_ref, gpost_ref, wg_ref, wu_ref, wd_ref = refs[:9]
    if mode == "head":
        o_ref, wg_out, wu_out, wd_out, h_ref, stat_ref, coef_ref = refs[9:]
    else:
        o_ref, h_ref, stat_ref, coef_ref = refs[9:]
    f = pl.program_id(1)
    active = pl.program_id(0) > 0 if mode == "tail" else True

    def swiglu_down(h):
        wg, wu, wd = wg_ref[...], wu_ref[...], wd_ref[...]
        if mode == "head":
            wg, wu, wd = wg.astype(BF16), wu.astype(BF16), wd.astype(BF16)
            wg_out[...], wu_out[...], wd_out[...] = wg, wu, wd
        g = jnp.dot(h, wg, preferred_element_type=F32)
        u = jnp.dot(h, wu, preferred_element_type=F32)
        a = (g * jax.nn.sigmoid(g) * u).astype(BF16)
        return jnp.dot(a, wd, preferred_element_type=F32)

    @pl.when(jnp.logical_and(active, f == 0))
    def _():
        h = (_rms(x_ref[...], gpre_ref[...]) * (1.0 + sc_ref[0]) + sh_ref[0]).astype(BF16)
        h_ref[...] = h
        o_ref[...] = swiglu_down(h)

    @pl.when(jnp.logical_and(active, f > 0))
    def _():
        o_ref[...] += swiglu_down(h_ref[...])

    @pl.when(jnp.logical_and(active, f == pl.num_programs(1) - 1))
    def _():
        _gated_norm_residual(x_ref, o_ref, stat_ref, coef_ref, gpost_ref, gt_ref, res_weight)

    if mode == "tail":
        @pl.when(jnp.logical_and(jnp.logical_not(active), f == 0))
        def _():
            o_ref[...] = jnp.zeros_like(o_ref)


def _ffn(x, sh, sc, gt, g_pre, g_post, wg, wu, wd, *, seq, tm, tf, res_weight, mode="all"):
    t, d = x.shape
    ff = wg.shape[1]
    tpb = seq // tm
    head = mode == "head"
    n_tiles = 1 if head else t // tm
    w_bytes = 4 if head else 2
    x_bufs = 1 if head else 2
    vmem = (2 * x_bufs * tm * d * 4 + tm * d * 2 + 6 * d * tf * w_bytes + 3 * tm * tf * 4
            + (9 * d * tf * 2 if head else 0))
    mod_spec = pl.BlockSpec((1, 1, d), lambda i, f: (i // tpb, 0, 0))
    gain_spec = pl.BlockSpec((1, d), lambda i, f: (0, 0))
    up_spec = pl.BlockSpec((d, tf), lambda i, f: (0, f))
    down_spec = pl.BlockSpec((tf, d), lambda i, f: (f, 0))
    tile_kw = dict(pipeline_mode=pl.Buffered(1)) if head else {}
    out_specs = pl.BlockSpec((tm, d), lambda i, f: (i, 0), **tile_kw)
    out_shape = jax.ShapeDtypeStruct((n_tiles * tm, d), F32)
    if head:
        out_specs = [out_specs, up_spec, up_spec, down_spec]
        out_shape = [out_shape, jax.ShapeDtypeStruct(wg.shape, BF16), jax.ShapeDtypeStruct(wu.shape, BF16),
                     jax.ShapeDtypeStruct(wd.shape, BF16)]
    return pl.pallas_call(
        functools.partial(_ffn_kernel, res_weight=res_weight, mode=mode),
        grid=(n_tiles, ff // tf),
        in_specs=[pl.BlockSpec((tm, d), lambda i, f: (i, 0), **tile_kw),
                  mod_spec, mod_spec, mod_spec, gain_spec, gain_spec, up_spec, up_spec, down_spec],
        out_specs=out_specs,
        out_shape=out_shape,
        scratch_shapes=[pltpu.VMEM((tm, d), BF16), pltpu.VMEM((tm, LANES), F32), pltpu.VMEM((8, d), F32)],
        compiler_params=_params(vmem),
        name="ffn_" + mode,
    )(x, sh, sc, gt, g_pre, g_post, wg, wu, wd)


def _rot(r, cos_t, sin_lo, sin_hi):
    return r * cos_t + pltpu.roll(r, 96, 1) * sin_lo + pltpu.roll(r, 32, 1) * sin_hi


def _mla_fox_prep(lat, gq_ref, gkv_ref, wuq_ref, wuk_ref, wuvt_ref, bf_ref, cos_ref, slo_ref, shi_ref,
                  q_ref, k_ref, vt_ref, ck_ref, cq_ref, carry_ref):
    tm = lat.shape[0]
    cos_t, sin_lo, sin_hi = cos_ref[...], slo_ref[...], shi_ref[...]

    qn = _rms(lat[:, :MLA_Q_RANK], gq_ref[...]).astype(BF16)
    q = jnp.dot(qn, wuq_ref[...], preferred_element_type=F32) * MLA_Q_SCALE
    kvn = _rms(lat[:, MLA_Q_RANK:KPE_COL], gkv_ref[...]).astype(BF16)
    k_nope = jnp.dot(kvn, wuk_ref[...], preferred_element_type=F32)
    vt_ref[0, 0] = lax.dot_general(wuvt_ref[...], kvn, _NT, preferred_element_type=F32).astype(BF16)
    k_rope = _rot(lat[:, KPE_COL:KPE_COL + LANES], cos_t, sin_lo, sin_hi).astype(BF16)
    for h in range(MLA_HEADS):
        c0 = h * MLA_QK_PAD
        q_ref[:, c0:c0 + MLA_NOPE] = q[:, c0:c0 + MLA_NOPE].astype(BF16)
        q_ref[:, c0 + MLA_NOPE:c0 + MLA_QK_PAD] = _rot(
            q[:, c0 + MLA_NOPE:c0 + MLA_QK_PAD], cos_t, sin_lo, sin_hi).astype(BF16)
        k_ref[:, c0:c0 + MLA_NOPE] = k_nope[:, h * MLA_NOPE:(h + 1) * MLA_NOPE].astype(BF16)
        k_ref[:, c0 + MLA_NOPE:c0 + MLA_QK_PAD] = k_rope

    z = lat[:, FLOGIT_COL:FLOGIT_COL + LANES] + bf_ref[...]
    lane = lax.broadcasted_iota(jnp.int32, (tm, LANES), 1)
    log_f = jnp.where(lane < FOX_HEADS, jnp.minimum(z, 0.0) - jnp.log1p(jnp.exp(-jnp.abs(z))), 0.0)
    hi = log_f.astype(BF16).astype(F32)
    mid = (log_f - hi).astype(BF16).astype(F32)
    lo = (log_f - hi - mid).astype(BF16).astype(F32)
    pieces = hi + pltpu.roll(mid, FOX_HEADS, 1) + pltpu.roll(lo, 2 * FOX_HEADS, 1)
    row = lax.broadcasted_iota(jnp.int32, (tm, tm), 0)
    col = lax.broadcasted_iota(jnp.int32, (tm, tm), 1)
    tri = (col <= row).astype(BF16)
    part = jnp.dot(tri, pieces.astype(BF16), preferred_element_type=F32)
    part = part + pltpu.roll(part, LANES - FOX_HEADS, 1) + pltpu.roll(part, LANES - 2 * FOX_HEADS, 1)
    cum = jnp.where(lane < FOX_HEADS, part, 0.0) + carry_ref[...]
    carry_ref[...] = cum[tm - 1:tm, :]
    cum2 = cum * LOG2E
    cq_ref[0, 0] = cum2.T[:FOX_HEADS, :]
    for h in range(FOX_HEADS):
        ck_ref[0, h] = jnp.broadcast_to(cum2[:, h:h + 1], (tm, LANES))


def _mixer_in_kernel(x_ref, sh_ref, sc_ref, gpre_ref, w_ref, wvt_ref,
                     gq_ref, gkv_ref, wuq_ref, wuk_ref, wuvt_ref, bf_ref, cos_ref, slo_ref, shi_ref,
                     fqk_ref, fvt_ref, q_ref, k_ref, vt_ref, ck_ref, cq_ref, carry_ref, *, tiles_per_batch):
    @pl.when(pl.program_id(0) % tiles_per_batch == 0)
    def _():
        carry_ref[...] = jnp.zeros_like(carry_ref)

    tn = LATENT_WIDTH
    h = (_rms(x_ref[...], gpre_ref[...]) * (1.0 + sc_ref[0]) + sh_ref[0]).astype(BF16)
    lat = jnp.dot(h, w_ref[:, :tn], preferred_element_type=F32)
    fq = jnp.dot(h, w_ref[:, tn:2 * tn], preferred_element_type=F32) * FOX_Q_SCALE
    fqk_ref[:, :tn] = fq.astype(BF16)
    fqk_ref[:, tn:] = jnp.dot(h, w_ref[:, 2 * tn:], preferred_element_type=F32).astype(BF16)
    fvt_ref[0, 0] = lax.dot_general(wvt_ref[...], h, _NT, preferred_element_type=F32).astype(BF16)
    _mla_fox_prep(lat, gq_ref, gkv_ref, wuq_ref, wuk_ref, wuvt_ref, bf_ref, cos_ref, slo_ref, shi_ref,
                  q_ref, k_ref, vt_ref, ck_ref, cq_ref, carry_ref)


def _mixer_in(x, sh, sc, g_pre, w, wvt, g_q, g_kv, wuq, wuk, wuvt, b_forget, cos_t, sin_lo, sin_hi,
              *, batch, seq):
    t, d = x.shape
    tm = ATTN_TILE
    tn = LATENT_WIDTH
    tpb = seq // tm
    qk_w = MLA_HEADS * MLA_QK_PAD
    v_w = MLA_HEADS * MLA_V
    assert w.shape == (d, 3 * tn) and wvt.shape == (FOX_WIDTH, d)
    resident_bytes = (w.size + wvt.size + wuq.size + wuk.size + wuvt.size) * 2
    vmem = (2 * tm * d * 4 + tm * d * 2 + resident_bytes + 2 * tm * (2 * tn + FOX_WIDTH) * 2
            + 2 * tm * (2 * qk_w + v_w) * 2 + 2 * FOX_HEADS * tm * LANES * 4 + 6 * tm * LANES * 4
            + tm * (tn + qk_w) * 4)
    mod_spec = pl.BlockSpec((1, 1, d), lambda i: (i // tpb, 0, 0))
    resident = lambda a: pl.BlockSpec(a.shape, lambda i: (0, 0), pipeline_mode=pl.Buffered(1))
    small = lambda a: pl.BlockSpec(a.shape, lambda i: (0, 0))
    tab_spec = pl.BlockSpec((tm, LANES), lambda i: (i % tpb, 0))
    return pl.pallas_call(
        functools.partial(_mixer_in_kernel, tiles_per_batch=tpb),
        grid=(t // tm,),
        in_specs=[pl.BlockSpec((tm, d), lambda i: (i, 0)),
                  mod_spec, mod_spec, small(g_pre), resident(w), resident(wvt),
                  small(g_q), small(g_kv), resident(wuq), resident(wuk), resident(wuvt), small(b_forget),
                  tab_spec, tab_spec, tab_spec],
        out_specs=[pl.BlockSpec((tm, 2 * tn), lambda i: (i, 0)),
                   pl.BlockSpec((1, 1, FOX_WIDTH, tm), lambda i: (i // tpb, i % tpb, 0, 0)),
                   pl.BlockSpec((tm, qk_w), lambda i: (i, 0)),
                   pl.BlockSpec((tm, qk_w), lambda i: (i, 0)),
                   pl.BlockSpec((1, 1, v_w, tm), lambda i: (i // tpb, i % tpb, 0, 0)),
                   pl.BlockSpec((1, FOX_HEADS, tm, LANES), lambda i: (i // tpb, 0, i % tpb, 0)),
                   pl.BlockSpec((1, 1, FOX_HEADS, tm), lambda i: (i // tpb, i % tpb, 0, 0))],
        out_shape=[jax.ShapeDtypeStruct((t, 2 * tn), BF16),
                   jax.ShapeDtypeStruct((batch, tpb, FOX_WIDTH, tm), BF16),
                   jax.ShapeDtypeStruct((t, qk_w), BF16),
                   jax.ShapeDtypeStruct((t, qk_w), BF16),
                   jax.ShapeDtypeStruct((batch, tpb, v_w, tm), BF16),
                   jax.ShapeDtypeStruct((batch, FOX_HEADS, seq, LANES), F32),
                   jax.ShapeDtypeStruct((batch, tpb, FOX_HEADS, tm), F32)],
        scratch_shapes=[pltpu.VMEM((1, LANES), F32)],
        compiler_params=_params(vmem),
        name="mixer_in",
    )(x, sh, sc, g_pre, w, wvt, g_q, g_kv, wuq, wuk, wuvt, b_forget, cos_t, sin_lo, sin_hi)


def _attn_kernel(*refs, mask_chunk, fox):
    n_in = 5 if fox else 3
    n_cast = (len(refs) - n_in - 6) // 2
    cast_src = refs[n_in:n_in + n_cast]
    cast_dst = refs[n_in + n_cast + 1:n_in + 2 * n_cast + 1]
    refs = refs[:n_in] + (refs[n_in + n_cast],) + refs[n_in + 2 * n_cast + 1:]
    if fox:
        q_ref, k_ref, vt_ref, ck_ref, cq_ref, o_ref, s0_ref, s1_ref, m_ref, l_ref, acc_ref = refs
    else:
        q_ref, k_ref, vt_ref, o_ref, s0_ref, s1_ref, m_ref, l_ref, acc_ref = refs
    for src, dst in zip(cast_src, cast_dst):
        dst[...] = src[...].astype(BF16)
    head = pl.program_id(1)
    tq = ATTN_TILE
    nq = q_ref.shape[0] // tq
    n_low = nq * (nq - 1) // 2
    assert PHASE_B_UNROLL % 2 == 0 and n_low % PHASE_B_UNROLL == 0
    bufs = (s0_ref, s1_ref)

    def rows(t):
        return pl.ds(pl.multiple_of(t * tq, tq), tq)

    def scores(i, j, s_ref):
        s = lax.dot_general(k_ref[rows(j), :], q_ref[rows(i), :], _NT, preferred_element_type=F32)
        if fox:
            cum_q = cq_ref[0, i, pl.ds(head, 1), :]
            cum_k = ck_ref[0, 0, rows(j), :]
            s = s + (cum_q - jnp.concatenate([cum_k] * (tq // LANES), axis=1))
        s_ref[...] = s

    half = tq // 2

    def scores_diagonal(i, s_ref):
        lo, hi = i * tq, i * tq + half
        s_a = lax.dot_general(k_ref[lo:hi, :], q_ref[lo:lo + tq, :], _NT, preferred_element_type=F32)
        s_b = lax.dot_general(k_ref[hi:hi + half, :], q_ref[hi:hi + half, :], _NT,
                              preferred_element_type=F32)
        if fox:
            cum_q = cq_ref[0, i, pl.ds(head, 1), :]
            s_a = s_a + (cum_q - jnp.concatenate([ck_ref[0, 0, lo:hi, :]] * (tq // LANES), axis=1))
            s_b = s_b + (cum_q[:, half:]
                         - jnp.concatenate([ck_ref[0, 0, hi:hi + half, :]] * (half // LANES), axis=1))
        src = lax.broadcasted_iota(jnp.int32, (half, half), 0) // mask_chunk
        dst = lax.broadcasted_iota(jnp.int32, (half, half), 1) // mask_chunk
        s_ref[:half, :half] = jnp.where(src <= dst, s_a[:, :half], NEG_BIG)
        s_ref[:half, half:] = s_a[:, half:]
        s_ref[half:, half:] = jnp.where(src <= dst, s_b, NEG_BIG)

    def accumulate_diagonal(i, s_ref):
        s_a = s_ref[:half, :]
        s_b = s_ref[half:, half:]
        m_a = jnp.max(s_a, axis=0, keepdims=True)
        m_hi = jnp.maximum(m_a[:, half:], jnp.max(s_b, axis=0, keepdims=True))
        m = jnp.concatenate([m_a[:, :half], m_hi], axis=1)
        p_a = jnp.exp2(s_a - m)
        p_b = jnp.exp2(s_b - m_hi)
        l_a = jnp.sum(p_a, axis=0, keepdims=True)
        m_ref[i] = m
        l_ref[i] = jnp.concatenate([l_a[:, :half], l_a[:, half:] + jnp.sum(p_b, axis=0, keepdims=True)],
                                   axis=1)
        acc_a = jnp.dot(vt_ref[0, i, :, :half], p_a.astype(BF16), preferred_element_type=F32)
        acc_b = jnp.dot(vt_ref[0, i, :, half:], p_b.astype(BF16), preferred_element_type=F32)
        acc_ref[i, :, :half] = acc_a[:, :half]
        acc_ref[i, :, half:] = acc_a[:, half:] + acc_b

    def accumulate(i, j, s_ref):
        s = s_ref[...]
        m_prev = m_ref[i]
        m_new = jnp.maximum(m_prev, jnp.max(s, axis=0, keepdims=True))
        alpha = jnp.exp2(m_prev - m_new)
        p = jnp.exp2(s - m_new)
        l_ref[i] = alpha * l_ref[i] + jnp.sum(p, axis=0, keepdims=True)
        acc_ref[i] = alpha * acc_ref[i] + jnp.dot(vt_ref[0, j], p.astype(BF16),
                                                  preferred_element_type=F32)
        m_ref[i] = m_new

    scores_diagonal(0, bufs[0])
    for i in range(nq):
        if i + 1 < nq:
            scores_diagonal(i + 1, bufs[(i + 1) % 2])
        else:
            scores(1, 0, bufs[(i + 1) % 2])
        accumulate_diagonal(i, bufs[i % 2])

    def advance(i, j):
        wraps = j + 1 == i
        i_nxt = jnp.minimum(jnp.where(wraps, i + 1, i), nq - 1)
        return i_nxt, jnp.where(wraps, 0, j + 1)

    def body(_, carry):
        i, j = carry
        for u in range(PHASE_B_UNROLL):
            i_nxt, j_nxt = advance(i, j)
            scores(i_nxt, j_nxt, bufs[(nq + u + 1) % 2])
            accumulate(i, j, bufs[(nq + u) % 2])
            i, j = i_nxt, j_nxt
        return i, j

    lax.fori_loop(0, n_low // PHASE_B_UNROLL, body, (jnp.int32(1), jnp.int32(0)))

    for i in range(nq):
        o_ref[i * tq:(i + 1) * tq, :] = (acc_ref[i] / l_ref[i]).T.astype(o_ref.dtype)


def _attention(q_arr, k_arr, vt_arr, cum_k=None, cum_q=None, *, batch, seq, heads, dk, dv,
               q_col, k_col, mask_chunk, riders=()):
    fox = cum_k is not None
    tq = ATTN_TILE
    nq = seq // tq
    steps = batch * heads
    in_specs = [pl.BlockSpec((seq, dk), lambda b, h: (b, q_col + h)),
                pl.BlockSpec((seq, dk), lambda b, h: (b, k_col + h)),
                pl.BlockSpec((1, nq, dv, tq), lambda b, h: (b, 0, h, 0))]
    args = [q_arr, k_arr, vt_arr]
    if fox:
        in_specs += [pl.BlockSpec((1, 1, seq, LANES), lambda b, h: (b, h, 0, 0)),
                     pl.BlockSpec((1, nq, heads, tq), lambda b, h: (b, 0, 0, 0))]
        args += [cum_k, cum_q]
    slab_specs = [pl.BlockSpec((w.shape[0] // steps, w.shape[1]), lambda b, h: (b * heads + h, 0))
                  for w in riders]
    vmem = (2 * seq * (2 * dk + 2 * dv) * 2 + 2 * seq * LANES * 4 + 2 * seq * heads * 4
            + seq * dv * 4 + 8 * tq * tq * 4 + sum(2 * 6 * w.size // steps for w in riders))
    return pl.pallas_call(
        functools.partial(_attn_kernel, mask_chunk=mask_chunk, fox=fox),
        grid=(batch, heads),
        in_specs=in_specs + slab_specs,
        out_specs=[pl.BlockSpec((seq, dv), lambda b, h: (b, h))] + slab_specs,
        out_shape=[jax.ShapeDtypeStruct((batch * seq, heads * dv), BF16)]
                  + [jax.ShapeDtypeStruct(w.shape, BF16) for w in riders],
        scratch_shapes=[pltpu.VMEM((tq, tq), F32), pltpu.VMEM((tq, tq), F32),
                        pltpu.VMEM((nq, 1, tq), F32), pltpu.VMEM((nq, 1, tq), F32),
                        pltpu.VMEM((nq, dv, tq), F32)],
        compiler_params=_params(vmem),
        name="fox_attn" if fox else "mla_attn",
    )(*args, *riders)


def _outproj_kernel(x_ref, oa_ref, ob_ref, wa_ref, wb_ref, gpost_ref, gt_ref, o_ref):
    y = jnp.dot(oa_ref[...], wa_ref[...], preferred_element_type=F32)
    y = y + jnp.dot(ob_ref[...], wb_ref[...], preferred_element_type=F32)
    o_ref[...] = x_ref[...] + gt_ref[0] * _rms(y, gpost_ref[...])


def _outproj(x, o_a, o_b, w_a, w_b, g_post, gt, *, seq, tm):
    t, d = x.shape
    ka, kb = o_a.shape[1], o_b.shape[1]
    tpb = seq // tm
    vmem = 4 * tm * d * 4 + 2 * tm * (ka + kb) * 2 + 2 * (ka + kb) * d * 2 + 2 * tm * d * 4
    return pl.pallas_call(
        _outproj_kernel,
        grid=(t // tm,),
        in_specs=[pl.BlockSpec((tm, d), lambda i: (i, 0)),
                  pl.BlockSpec((tm, ka), lambda i: (i, 0)),
                  pl.BlockSpec((tm, kb), lambda i: (i, 0)),
                  pl.BlockSpec((ka, d), lambda i: (0, 0)),
                  pl.BlockSpec((kb, d), lambda i: (0, 0)),
                  pl.BlockSpec((1, d), lambda i: (0, 0)),
                  pl.BlockSpec((1, 1, d), lambda i: (i // tpb, 0, 0))],
        out_specs=pl.BlockSpec((tm, d), lambda i: (i, 0)),
        out_shape=jax.ShapeDtypeStruct((t, d), F32),
        compiler_params=_params(vmem),
        name="out_proj",
    )(x, o_a, o_b, w_a, w_b, g_post, gt)


def _rope_tables(seq):
    half = MLA_ROPE // 2
    inv = ROPE_THETA ** (-jnp.arange(half, dtype=F32) / half)
    ang = jnp.arange(seq).astype(F32)[:, None] * inv[None, :]
    cos, sin = jnp.cos(ang), jnp.sin(ang)
    zero = jnp.zeros_like(cos)
    pad = jnp.zeros((seq, LANES - 2 * half), F32)
    return (jnp.concatenate([cos, cos, pad], axis=1),
            jnp.concatenate([-sin, zero, pad], axis=1),
            jnp.concatenate([zero, sin, pad], axis=1))


def kernel(x, c, w_ada, b_ada, g_ffn1_pre, g_ffn1_post, w1_gate, w1_up, w1_down, g_mix_pre,
           g_mix_post, w_in, b_forget, g_q_a, w_uq, g_kv_a, w_ukv, w_o, g_ffn2_pre, g_ffn2_post,
           w2_gate, w2_up, w2_down):
    batch, seq, d = x.shape
    depth = w_ada.shape[0]
    t = batch * seq
    xt = x.reshape(t, d)
    c_pad = jnp.pad(c, ((0, 8 - batch), (0, 0)))
    cos_t, sin_lo, sin_hi = _rope_tables(seq)

    for l in range(depth):
        mod = _ada(c_pad, w_ada[l], b_ada[l:l + 1], tn=1024)[:batch]
        sh1, sc1, gt1, sh2, sc2, gt2, sh3, sc3, gt3 = [
            mod[:, n * d:(n + 1) * d].reshape(batch, 1, d) for n in range(9)]

        ffn1 = functools.partial(_ffn, xt, sh1, sc1, gt1, g_ffn1_pre[l:l + 1], g_ffn1_post[l:l + 1],
                                 seq=seq, tm=1024, res_weight=0.5)
        y_head, w1g, w1u, w1d = ffn1(w1_gate[l], w1_up[l], w1_down[l], tf=256, mode="head")
        xt = lax.dynamic_update_slice(ffn1(w1g, w1u, w1d, tf=512, mode="tail"), y_head, (0, 0))

        wi = w_in[l]
        fox0 = KPE_COL + MLA_ROPE
        fv0 = fox0 + 2 * FOX_WIDTH
        fl0 = fox0 + 3 * FOX_WIDTH
        w_in_p = jnp.concatenate(
            [wi[:, :fox0], jnp.zeros((d, LANES - MLA_ROPE), F32),
             wi[:, fl0:], jnp.zeros((d, LANES - FOX_HEADS), F32),
             wi[:, fox0:fv0]], axis=1).astype(BF16)
        w_fvt = wi[:, fv0:fl0].T.astype(BF16)

        wuq_p = jnp.pad(w_uq[l].reshape(MLA_Q_RANK, MLA_HEADS, MLA_NOPE + MLA_ROPE),
                        ((0, 0), (0, 0), (0, MLA_QK_PAD - MLA_NOPE - MLA_ROPE))
                        ).reshape(MLA_Q_RANK, MLA_HEADS * MLA_QK_PAD).astype(BF16)
        wukv = w_ukv[l].reshape(MLA_KV_RANK, MLA_HEADS, MLA_NOPE + MLA_V)
        wuk = wukv[:, :, :MLA_NOPE].reshape(MLA_KV_RANK, -1).astype(BF16)
        wuvt = wukv[:, :, MLA_NOPE:].reshape(MLA_KV_RANK, -1).T.astype(BF16)
        bf_pad = jnp.pad(b_forget[l:l + 1], ((0, 0), (0, LANES - FOX_HEADS)))
        fqk, fvt, q_mla, k_mla, vt_mla, cum_k, cum_q = _mixer_in(
            xt, sh2, sc2, g_mix_pre[l:l + 1], w_in_p, w_fvt, g_q_a[l:l + 1], g_kv_a[l:l + 1],
            wuq_p, wuk, wuvt, bf_pad, cos_t, sin_lo, sin_hi, batch=batch, seq=seq)

        o_mla, w2g, w2u = _attention(q_mla, k_mla, vt_mla, batch=batch, seq=seq, heads=MLA_HEADS,
                                     dk=MLA_QK_PAD, dv=MLA_V, q_col=0, k_col=0, mask_chunk=CHUNK,
                                     riders=(w2_gate[l], w2_up[l]))
        o_fox, w2d = _attention(fqk, fqk, fvt, cum_k, cum_q, batch=batch, seq=seq, heads=FOX_HEADS,
                                dk=FOX_DIM, dv=FOX_DIM, q_col=0, k_col=FOX_HEADS, mask_chunk=1,
                                riders=(w2_down[l],))

        wo = w_o[l].astype(BF16)
        xt = _outproj(xt, o_mla, o_fox, wo[:MLA_HEADS * MLA_V], wo[MLA_HEADS * MLA_V:],
                      g_mix_post[l:l + 1], gt2, seq=seq, tm=512)

        xt = _ffn(xt, sh3, sc3, gt3, g_ffn2_pre[l:l + 1], g_ffn2_post[l:l + 1],
                  w2g, w2u, w2d, seq=seq, tm=1024, tf=512, res_weight=0.5)

    return xt.reshape(batch, seq, d)
```

```python
import functools
import math

import jax
import jax.numpy as jnp
from jax import lax
from jax.experimental import pallas as pl
from jax.experimental.pallas import tpu as pltpu

F32 = jnp.float32
BF16 = jnp.bfloat16

V7X_VMEM_BYTES = 64 * 1024 * 1024
LANES = 128

EPS = 1e-6
ROPE_THETA = 10000.0
CHUNK = 64
MLA_HEADS = 8
MLA_Q_RANK = 512
MLA_KV_RANK = 256
MLA_NOPE = 128
MLA_ROPE = 64
MLA_V = 128
MLA_QK_PAD = 256
FOX_HEADS = 8
FOX_DIM = 128
FOX_WIDTH = FOX_HEADS * FOX_DIM
LATENT_WIDTH = 1024
KPE_COL = MLA_Q_RANK + MLA_KV_RANK
FLOGIT_COL = KPE_COL + LANES
ATTN_TILE = 512
PHASE_B_UNROLL = 14
LOG2E = math.log2(math.e)
MLA_Q_SCALE = LOG2E / math.sqrt(MLA_NOPE + MLA_ROPE)
FOX_Q_SCALE = LOG2E / math.sqrt(FOX_DIM)
NEG_BIG = -1e30

_NT = (((1,), (1,)), ((), ()))


def _params(vmem_bytes):
    limit = min(int(vmem_bytes * 1.25) + (4 << 20), V7X_VMEM_BYTES - (8 << 20))
    return pltpu.CompilerParams(vmem_limit_bytes=limit)


def _rms(x, g):
    return x * lax.rsqrt(jnp.mean(x * x, axis=-1, keepdims=True) + EPS) * g


STAT_ROWS = 8
APPLY_ROWS = 16


def _rms_stats(src_ref, stat_ref):
    tm, d = src_ref.shape

    def step(c, carry):
        rows = pl.ds(pl.multiple_of(c * STAT_ROWS, STAT_ROWS), STAT_ROWS)
        x = src_ref[rows, :]
        ms = jnp.sum(x * x, axis=-1, keepdims=True) * (1.0 / d)
        stat_ref[rows, :] = jnp.broadcast_to(lax.rsqrt(ms + EPS), (STAT_ROWS, LANES))
        return carry

    lax.fori_loop(0, tm // STAT_ROWS, step, 0, unroll=32)


def _for_row_blocks(tm, fn):
    def step(c, carry):
        fn(pl.ds(pl.multiple_of(c * APPLY_ROWS, APPLY_ROWS), APPLY_ROWS))
        return carry

    lax.fori_loop(0, tm // APPLY_ROWS, step, 0, unroll=4)


def _modulated_norm(x_ref, stat_ref, coef_ref, gpre_ref, sc_ref, sh_ref, h_ref):
    tm, d = x_ref.shape
    coef_ref[0:1, :] = gpre_ref[...] * (1.0 + sc_ref[0])
    coef_ref[1:2, :] = sh_ref[0]
    _rms_stats(x_ref, stat_ref)

    def apply(rows):
        r = stat_ref[rows, :]
        for k in range(d // LANES):
            cols = slice(k * LANES, (k + 1) * LANES)
            h = x_ref[rows, cols] * r * coef_ref[0:1, cols] + coef_ref[1:2, cols]
            h_ref[rows, cols] = h.astype(BF16)

    _for_row_blocks(tm, apply)


def _gated_norm_residual(x_ref, o_ref, stat_ref, coef_ref, gpost_ref, gt_ref, res_weight):
    tm, d = o_ref.shape
    coef_ref[0:1, :] = res_weight * gt_ref[0] * gpost_ref[...]
    _rms_stats(o_ref, stat_ref)

    def apply(rows):
        r = stat_ref[rows, :]
        for k in range(d // LANES):
            cols = slice(k * LANES, (k + 1) * LANES)
            o_ref[rows, cols] = x_ref[rows, cols] + o_ref[rows, cols] * r * coef_ref[0:1, cols]

    _for_row_blocks(tm, apply)


def _ada_kernel(c_ref, w_ref, b_ref, o_ref):
    c = c_ref[...]
    cond = (c * jax.nn.sigmoid(c)).astype(BF16)
    o_ref[...] = jnp.dot(cond, w_ref[...].astype(BF16), preferred_element_type=F32) + b_ref[...]


def _ada(c_pad, w, b, *, tn):
    m, d = c_pad.shape
    n = w.shape[1]
    vmem = 2 * d * tn * 4 + d * tn * 2 + 4 * m * tn * 4
    return pl.pallas_call(
        _ada_kernel,
        grid=(n // tn,),
        in_specs=[pl.BlockSpec((m, d), lambda j: (0, 0)),
                  pl.BlockSpec((d, tn), lambda j: (0, j)),
                  pl.BlockSpec((1, tn), lambda j: (0, j))],
        out_specs=pl.BlockSpec((m, tn), lambda j: (0, j)),
        out_shape=jax.ShapeDtypeStruct((m, n), F32),
        compiler_params=_params(vmem),
        name="ada",
    )(c_pad, w, b)


def _ffn_kernel(*refs, res_weight, mode):
    x_ref, sh_ref, sc_ref, gt_ref, gpre_ref, gpost_ref, wg_ref, wu_ref, wd_ref = refs[:9]
    if mode == "head":
        o_ref, wg_out, wu_out, wd_out, h_ref, stat_ref, coef_ref = refs[9:]
    else:
        o_ref, h_ref, stat_ref, coef_ref = refs[9:]
    f = pl.program_id(1)
    active = pl.program_id(0) > 0 if mode == "tail" else True

    def swiglu_down(h):
        wg, wu, wd = wg_ref[...], wu_ref[...], wd_ref[...]
        if mode == "head":
            wg, wu, wd = wg.astype(BF16), wu.astype(BF16), wd.astype(BF16)
            wg_out[...], wu_out[...], wd_out[...] = wg, wu, wd
        g = jnp.dot(h, wg, preferred_element_type=F32)
        u = jnp.dot(h, wu, preferred_element_type=F32)
        a = (g * jax.nn.sigmoid(g) * u).astype(BF16)
        return jnp.dot(a, wd, preferred_element_type=F32)

    @pl.when(jnp.logical_and(active, f == 0))
    def _():
        h = (_rms(x_ref[...], gpre_ref[...]) * (1.0 + sc_ref[0]) + sh_ref[0]).astype(BF16)
        h_ref[...] = h
        o_ref[...] = swiglu_down(h)

    @pl.when(jnp.logical_and(active, f > 0))
    def _():
        o_ref[...] += swiglu_down(h_ref[...])

    @pl.when(jnp.logical_and(active, f == pl.num_programs(1) - 1))
    def _():
        _gated_norm_residual(x_ref, o_ref, stat_ref, coef_ref, gpost_ref, gt_ref, res_weight)

    if mode == "tail":
        @pl.when(jnp.logical_and(jnp.logical_not(active), f == 0))
        def _():
            o_ref[...] = jnp.zeros_like(o_ref)


def _ffn(x, sh, sc, gt, g_pre, g_post, wg, wu, wd, *, seq, tm, tf, res_weight, mode="all"):
    t, d = x.shape
    ff = wg.shape[1]
    tpb = seq // tm
    head = mode == "head"
    n_tiles = 1 if head else t // tm
    w_bytes = 4 if head else 2
    x_bufs = 1 if head else 2
    vmem = (2 * x_bufs * tm * d * 4 + tm * d * 2 + 6 * d * tf * w_bytes + 3 * tm * tf * 4
            + (9 * d * tf * 2 if head else 0))
    mod_spec = pl.BlockSpec((1, 1, d), lambda i, f: (i // tpb, 0, 0))
    gain_spec = pl.BlockSpec((1, d), lambda i, f: (0, 0))
    if mode == "tail":
        f_of = lambda i, f: jnp.where(i == 0, 0, f)
        x_spec = pl.BlockSpec((tm, d), lambda i, f: (jnp.maximum(i, 1), 0))
    else:
        f_of = lambda i, f: f
        x_spec = pl.BlockSpec((tm, d), lambda i, f: (i, 0), **(dict(pipeline_mode=pl.Buffered(1)) if head else {}))
    up_spec = pl.BlockSpec((d, tf), lambda i, f: (0, f_of(i, f)))
    down_spec = pl.BlockSpec((tf, d), lambda i, f: (f_of(i, f), 0))
    tile_kw = dict(pipeline_mode=pl.Buffered(1)) if head else {}
    out_specs = pl.BlockSpec((tm, d), lambda i, f: (i, 0), **tile_kw)
    out_shape = jax.ShapeDtypeStruct((n_tiles * tm, d), F32)
    if head:
        out_specs = [out_specs, up_spec, up_spec, down_spec]
        out_shape = [out_shape, jax.ShapeDtypeStruct(wg.shape, BF16), jax.ShapeDtypeStruct(wu.shape, BF16),
                     jax.ShapeDtypeStruct(wd.shape, BF16)]
    return pl.pallas_call(
        functools.partial(_ffn_kernel, res_weight=res_weight, mode=mode),
        grid=(n_tiles, ff // tf),
        in_specs=[x_spec, mod_spec, mod_spec, mod_spec, gain_spec, gain_spec, up_spec, up_spec, down_spec],
        out_specs=out_specs,
        out_shape=out_shape,
        scratch_shapes=[pltpu.VMEM((tm, d), BF16), pltpu.VMEM((tm, LANES), F32), pltpu.VMEM((8, d), F32)],
        compiler_params=_params(vmem),
        name="ffn_" + mode,
    )(x, sh, sc, gt, g_pre, g_post, wg, wu, wd)


def _rot(r, cos_t, sin_lo, sin_hi):
    return r * cos_t + pltpu.roll(r, 96, 1) * sin_lo + pltpu.roll(r, 32, 1) * sin_hi


def _mla_fox_prep(lat, gq_ref, gkv_ref, wuq_ref, wuk_ref, wuvt_ref, bf_ref, cos_ref, slo_ref, shi_ref,
                  q_ref, k_ref, vt_ref, ck_ref, cq_ref, carry_ref):
    tm = lat.shape[0]
    cos_t, sin_lo, sin_hi = cos_ref[...], slo_ref[...], shi_ref[...]

    qn = _rms(lat[:, :MLA_Q_RANK], gq_ref[...]).astype(BF16)
    q = jnp.dot(qn, wuq_ref[...], preferred_element_type=F32) * MLA_Q_SCALE
    kvn = _rms(lat[:, MLA_Q_RANK:KPE_COL], gkv_ref[...]).astype(BF16)
    k_nope = jnp.dot(kvn, wuk_ref[...], preferred_element_type=F32)
    vt_ref[0, 0] = lax.dot_general(wuvt_ref[...], kvn, _NT, preferred_element_type=F32).astype(BF16)
    k_rope = _rot(lat[:, KPE_COL:KPE_COL + LANES], cos_t, sin_lo, sin_hi).astype(BF16)
    for h in range(MLA_HEADS):
        c0 = h * MLA_QK_PAD
        q_ref[:, c0:c0 + MLA_NOPE] = q[:, c0:c0 + MLA_NOPE].astype(BF16)
        q_ref[:, c0 + MLA_NOPE:c0 + MLA_QK_PAD] = _rot(
            q[:, c0 + MLA_NOPE:c0 + MLA_QK_PAD], cos_t, sin_lo, sin_hi).astype(BF16)
        k_ref[:, c0:c0 + MLA_NOPE] = k_nope[:, h * MLA_NOPE:(h + 1) * MLA_NOPE].astype(BF16)
        k_ref[:, c0 + MLA_NOPE:c0 + MLA_QK_PAD] = k_rope

    z = lat[:, FLOGIT_COL:FLOGIT_COL + LANES] + bf_ref[...]
    lane = lax.broadcasted_iota(jnp.int32, (tm, LANES), 1)
    log_f = jnp.where(lane < FOX_HEADS, jnp.minimum(z, 0.0) - jnp.log1p(jnp.exp(-jnp.abs(z))), 0.0)
    hi = log_f.astype(BF16).astype(F32)
    mid = (log_f - hi).astype(BF16).astype(F32)
    lo = (log_f - hi - mid).astype(BF16).astype(F32)
    pieces = hi + pltpu.roll(mid, FOX_HEADS, 1) + pltpu.roll(lo, 2 * FOX_HEADS, 1)
    row = lax.broadcasted_iota(jnp.int32, (tm, tm), 0)
    col = lax.broadcasted_iota(jnp.int32, (tm, tm), 1)
    tri = (col <= row).astype(BF16)
    part = jnp.dot(tri, pieces.astype(BF16), preferred_element_type=F32)
    part = part + pltpu.roll(part, LANES - FOX_HEADS, 1) + pltpu.roll(part, LANES - 2 * FOX_HEADS, 1)
    cum = jnp.where(lane < FOX_HEADS, part, 0.0) + carry_ref[...]
    carry_ref[...] = cum[tm - 1:tm, :]
    cum2 = cum * LOG2E
    cq_ref[0, 0] = cum2.T[:FOX_HEADS, :]
    for h in range(FOX_HEADS):
        ck_ref[0, h] = jnp.broadcast_to(cum2[:, h:h + 1], (tm, LANES))


def _mixer_in_kernel(x_ref, sh_ref, sc_ref, gpre_ref, w_ref, wvt_ref,
                     gq_ref, gkv_ref, wuq_ref, wuk_ref, wuvt_ref, bf_ref, cos_ref, slo_ref, shi_ref,
                     fqk_ref, fvt_ref, q_ref, k_ref, vt_ref, ck_ref, cq_ref, carry_ref, *, tiles_per_batch):
    @pl.when(pl.program_id(0) % tiles_per_batch == 0)
    def _():
        carry_ref[...] = jnp.zeros_like(carry_ref)

    tn = LATENT_WIDTH
    h = (_rms(x_ref[...], gpre_ref[...]) * (1.0 + sc_ref[0]) + sh_ref[0]).astype(BF16)
    lat = jnp.dot(h, w_ref[:, :tn], preferred_element_type=F32)
    fq = jnp.dot(h, w_ref[:, tn:2 * tn], preferred_element_type=F32) * FOX_Q_SCALE
    fqk_ref[:, :tn] = fq.astype(BF16)
    fqk_ref[:, tn:] = jnp.dot(h, w_ref[:, 2 * tn:], preferred_element_type=F32).astype(BF16)
    fvt_ref[0, 0] = lax.dot_general(wvt_ref[...], h, _NT, preferred_element_type=F32).astype(BF16)
    _mla_fox_prep(lat, gq_ref, gkv_ref, wuq_ref, wuk_ref, wuvt_ref, bf_ref, cos_ref, slo_ref, shi_ref,
                  q_ref, k_ref, vt_ref, ck_ref, cq_ref, carry_ref)


def _mixer_in(x, sh, sc, g_pre, w, wvt, g_q, g_kv, wuq, wuk, wuvt, b_forget, cos_t, sin_lo, sin_hi,
              *, batch, seq):
    t, d = x.shape
    tm = ATTN_TILE
    tn = LATENT_WIDTH
    tpb = seq // tm
    qk_w = MLA_HEADS * MLA_QK_PAD
    v_w = MLA_HEADS * MLA_V
    assert w.shape == (d, 3 * tn) and wvt.shape == (FOX_WIDTH, d)
    resident_bytes = (w.size + wvt.size + wuq.size + wuk.size + wuvt.size) * 2
    vmem = (2 * tm * d * 4 + tm * d * 2 + resident_bytes + 2 * tm * (2 * tn + FOX_WIDTH) * 2
            + 2 * tm * (2 * qk_w + v_w) * 2 + 2 * FOX_HEADS * tm * LANES * 4 + 6 * tm * LANES * 4
            + tm * (tn + qk_w) * 4)
    mod_spec = pl.BlockSpec((1, 1, d), lambda i: (i // tpb, 0, 0))
    resident = lambda a: pl.BlockSpec(a.shape, lambda i: (0, 0), pipeline_mode=pl.Buffered(1))
    small = lambda a: pl.BlockSpec(a.shape, lambda i: (0, 0))
    tab_spec = pl.BlockSpec((tm, LANES), lambda i: (i % tpb, 0))
    return pl.pallas_call(
        functools.partial(_mixer_in_kernel, tiles_per_batch=tpb),
        grid=(t // tm,),
        in_specs=[pl.BlockSpec((tm, d), lambda i: (i, 0)),
                  mod_spec, mod_spec, small(g_pre), resident(w), resident(wvt),
                  small(g_q), small(g_kv), resident(wuq), resident(wuk), resident(wuvt), small(b_forget),
                  tab_spec, tab_spec, tab_spec],
        out_specs=[pl.BlockSpec((tm, 2 * tn), lambda i: (i, 0)),
                   pl.BlockSpec((1, 1, FOX_WIDTH, tm), lambda i: (i // tpb, i % tpb, 0, 0)),
                   pl.BlockSpec((tm, qk_w), lambda i: (i, 0)),
                   pl.BlockSpec((tm, qk_w), lambda i: (i, 0)),
                   pl.BlockSpec((1, 1, v_w, tm), lambda i: (i // tpb, i % tpb, 0, 0)),
                   pl.BlockSpec((1, FOX_HEADS, tm, LANES), lambda i: (i // tpb, 0, i % tpb, 0)),
                   pl.BlockSpec((1, 1, FOX_HEADS, tm), lambda i: (i // tpb, i % tpb, 0, 0))],
        out_shape=[jax.ShapeDtypeStruct((t, 2 * tn), BF16),
                   jax.ShapeDtypeStruct((batch, tpb, FOX_WIDTH, tm), BF16),
                   jax.ShapeDtypeStruct((t, qk_w), BF16),
                   jax.ShapeDtypeStruct((t, qk_w), BF16),
                   jax.ShapeDtypeStruct((batch, tpb, v_w, tm), BF16),
                   jax.ShapeDtypeStruct((batch, FOX_HEADS, seq, LANES), F32),
                   jax.ShapeDtypeStruct((batch, tpb, FOX_HEADS, tm), F32)],
        scratch_shapes=[pltpu.VMEM((1, LANES), F32)],
        compiler_params=_params(vmem),
        name="mixer_in",
    )(x, sh, sc, g_pre, w, wvt, g_q, g_kv, wuq, wuk, wuvt, b_forget, cos_t, sin_lo, sin_hi)


def _attn_kernel(*refs, mask_chunk, fox):
    n_in = 5 if fox else 3
    n_cast = (len(refs) - n_in - 6) // 2
    cast_src = refs[n_in:n_in + n_cast]
    cast_dst = refs[n_in + n_cast + 1:n_in + 2 * n_cast + 1]
    refs = refs[:n_in] + (refs[n_in + n_cast],) + refs[n_in + 2 * n_cast + 1:]
    if fox:
        q_ref, k_ref, vt_ref, ck_ref, cq_ref, o_ref, s0_ref, s1_ref, m_ref, l_ref, acc_ref = refs
    else:
        q_ref, k_ref, vt_ref, o_ref, s0_ref, s1_ref, m_ref, l_ref, acc_ref = refs
    for src, dst in zip(cast_src, cast_dst):
        dst[...] = src[...].astype(BF16)
    head = pl.program_id(1)
    tq = ATTN_TILE
    nq = q_ref.shape[0] // tq
    n_low = nq * (nq - 1) // 2
    assert PHASE_B_UNROLL % 2 == 0 and n_low % PHASE_B_UNROLL == 0
    bufs = (s0_ref, s1_ref)

    def rows(t):
        return pl.ds(pl.multiple_of(t * tq, tq), tq)

    def scores(i, j, s_ref):
        s = lax.dot_general(k_ref[rows(j), :], q_ref[rows(i), :], _NT, preferred_element_type=F32)
        if fox:
            cum_q = cq_ref[0, i, pl.ds(head, 1), :]
            cum_k = ck_ref[0, 0, rows(j), :]
            s = s + (cum_q - jnp.concatenate([cum_k] * (tq // LANES), axis=1))
        s_ref[...] = s

    half = tq // 2

    def scores_diagonal(i, s_ref):
        lo, hi = i * tq, i * tq + half
        s_a = lax.dot_general(k_ref[lo:hi, :], q_ref[lo:lo + tq, :], _NT, preferred_element_type=F32)
        s_b = lax.dot_general(k_ref[hi:hi + half, :], q_ref[hi:hi + half, :], _NT,
                              preferred_element_type=F32)
        if fox:
            cum_q = cq_ref[0, i, pl.ds(head, 1), :]
            s_a = s_a + (cum_q - jnp.concatenate([ck_ref[0, 0, lo:hi, :]] * (tq // LANES), axis=1))
            s_b = s_b + (cum_q[:, half:]
                         - jnp.concatenate([ck_ref[0, 0, hi:hi + half, :]] * (half // LANES), axis=1))
        src = lax.broadcasted_iota(jnp.int32, (half, half), 0) // mask_chunk
        dst = lax.broadcasted_iota(jnp.int32, (half, half), 1) // mask_chunk
        s_ref[:half, :half] = jnp.where(src <= dst, s_a[:, :half], NEG_BIG)
        s_ref[:half, half:] = s_a[:, half:]
        s_ref[half:, half:] = jnp.where(src <= dst, s_b, NEG_BIG)

    def accumulate_diagonal(i, s_ref):
        s_a = s_ref[:half, :]
        s_b = s_ref[half:, half:]
        m_a = jnp.max(s_a, axis=0, keepdims=True)
        m_hi = jnp.maximum(m_a[:, half:], jnp.max(s_b, axis=0, keepdims=True))
        m = jnp.concatenate([m_a[:, :half], m_hi], axis=1)
        p_a = jnp.exp2(s_a - m)
        p_b = jnp.exp2(s_b - m_hi)
        l_a = jnp.sum(p_a, axis=0, keepdims=True)
        m_ref[i] = m
        l_ref[i] = jnp.concatenate([l_a[:, :half], l_a[:, half:] + jnp.sum(p_b, axis=0, keepdims=True)],
                                   axis=1)
        acc_a = jnp.dot(vt_ref[0, i, :, :half], p_a.astype(BF16), preferred_element_type=F32)
        acc_b = jnp.dot(vt_ref[0, i, :, half:], p_b.astype(BF16), preferred_element_type=F32)
        acc_ref[i, :, :half] = acc_a[:, :half]
        acc_ref[i, :, half:] = acc_a[:, half:] + acc_b

    def accumulate(i, j, s_ref):
        s = s_ref[...]
        m_prev = m_ref[i]
        m_new = jnp.maximum(m_prev, jnp.max(s, axis=0, keepdims=True))
        alpha = jnp.exp2(m_prev - m_new)
        p = jnp.exp2(s - m_new)
        l_ref[i] = alpha * l_ref[i] + jnp.sum(p, axis=0, keepdims=True)
        acc_ref[i] = alpha * acc_ref[i] + jnp.dot(vt_ref[0, j], p.astype(BF16),
                                                  preferred_element_type=F32)
        m_ref[i] = m_new

    scores_diagonal(0, bufs[0])
    for i in range(nq):
        if i + 1 < nq:
            scores_diagonal(i + 1, bufs[(i + 1) % 2])
        else:
            scores(1, 0, bufs[(i + 1) % 2])
        accumulate_diagonal(i, bufs[i % 2])

    def advance(i, j):
        wraps = j + 1 == i
        i_nxt = jnp.minimum(jnp.where(wraps, i + 1, i), nq - 1)
        return i_nxt, jnp.where(wraps, 0, j + 1)

    def body(_, carry):
        i, j = carry
        for u in range(PHASE_B_UNROLL):
            i_nxt, j_nxt = advance(i, j)
            scores(i_nxt, j_nxt, bufs[(nq + u + 1) % 2])
            accumulate(i, j, bufs[(nq + u) % 2])
            i, j = i_nxt, j_nxt
        return i, j

    lax.fori_loop(0, n_low // PHASE_B_UNROLL, body, (jnp.int32(1), jnp.int32(0)))

    for i in range(nq):
        o_ref[i * tq:(i + 1) * tq, :] = (acc_ref[i] / l_ref[i]).T.astype(o_ref.dtype)


def _attention(q_arr, k_arr, vt_arr, cum_k=None, cum_q=None, *, batch, seq, heads, dk, dv,
               q_col, k_col, mask_chunk, riders=()):
    fox = cum_k is not None
    tq = ATTN_TILE
    nq = seq // tq
    steps = batch * heads
    in_specs = [pl.BlockSpec((seq, dk), lambda b, h: (b, q_col + h)),
                pl.BlockSpec((seq, dk), lambda b, h: (b, k_col + h)),
                pl.BlockSpec((1, nq, dv, tq), lambda b, h: (b, 0, h, 0))]
    args = [q_arr, k_arr, vt_arr]
    if fox:
        in_specs += [pl.BlockSpec((1, 1, seq, LANES), lambda b, h: (b, h, 0, 0)),
                     pl.BlockSpec((1, nq, heads, tq), lambda b, h: (b, 0, 0, 0))]
        args += [cum_k, cum_q]
    slab_specs = [pl.BlockSpec((w.shape[0] // steps, w.shape[1]), lambda b, h: (b * heads + h, 0))
                  for w in riders]
    vmem = (2 * seq * (2 * dk + 2 * dv) * 2 + 2 * seq * LANES * 4 + 2 * seq * heads * 4
            + seq * dv * 4 + 8 * tq * tq * 4 + sum(2 * 6 * w.size // steps for w in riders))
    return pl.pallas_call(
        functools.partial(_attn_kernel, mask_chunk=mask_chunk, fox=fox),
        grid=(batch, heads),
        in_specs=in_specs + slab_specs,
        out_specs=[pl.BlockSpec((seq, dv), lambda b, h: (b, h))] + slab_specs,
        out_shape=[jax.ShapeDtypeStruct((batch * seq, heads * dv), BF16)]
                  + [jax.ShapeDtypeStruct(w.shape, BF16) for w in riders],
        scratch_shapes=[pltpu.VMEM((tq, tq), F32), pltpu.VMEM((tq, tq), F32),
                        pltpu.VMEM((nq, 1, tq), F32), pltpu.VMEM((nq, 1, tq), F32),
                        pltpu.VMEM((nq, dv, tq), F32)],
        compiler_params=_params(vmem),
        name="fox_attn" if fox else "mla_attn",
    )(*args, *riders)


def _outproj_kernel(x_ref, oa_ref, ob_ref, wa_ref, wb_ref, gpost_ref, gt_ref, o_ref):
    y = jnp.dot(oa_ref[...], wa_ref[...], preferred_element_type=F32)
    y = y + jnp.dot(ob_ref[...], wb_ref[...], preferred_element_type=F32)
    o_ref[...] = x_ref[...] + gt_ref[0] * _rms(y, gpost_ref[...])


def _outproj(x, o_a, o_b, w_a, w_b, g_post, gt, *, seq, tm):
    t, d = x.shape
    ka, kb = o_a.shape[1], o_b.shape[1]
    tpb = seq // tm
    vmem = 4 * tm * d * 4 + 2 * tm * (ka + kb) * 2 + 2 * (ka + kb) * d * 2 + 2 * tm * d * 4
    return pl.pallas_call(
        _outproj_kernel,
        grid=(t // tm,),
        in_specs=[pl.BlockSpec((tm, d), lambda i: (i, 0)),
                  pl.BlockSpec((tm, ka), lambda i: (i, 0)),
                  pl.BlockSpec((tm, kb), lambda i: (i, 0)),
                  pl.BlockSpec((ka, d), lambda i: (0, 0)),
                  pl.BlockSpec((kb, d), lambda i: (0, 0)),
                  pl.BlockSpec((1, d), lambda i: (0, 0)),
                  pl.BlockSpec((1, 1, d), lambda i: (i // tpb, 0, 0))],
        out_specs=pl.BlockSpec((tm, d), lambda i: (i, 0)),
        out_shape=jax.ShapeDtypeStruct((t, d), F32),
        compiler_params=_params(vmem),
        name="out_proj",
    )(x, o_a, o_b, w_a, w_b, g_post, gt)


def _rope_tables(seq):
    half = MLA_ROPE // 2
    inv = ROPE_THETA ** (-jnp.arange(half, dtype=F32) / half)
    ang = jnp.arange(seq).astype(F32)[:, None] * inv[None, :]
    cos, sin = jnp.cos(ang), jnp.sin(ang)
    zero = jnp.zeros_like(cos)
    pad = jnp.zeros((seq, LANES - 2 * half), F32)
    return (jnp.concatenate([cos, cos, pad], axis=1),
            jnp.concatenate([-sin, zero, pad], axis=1),
            jnp.concatenate([zero, sin, pad], axis=1))


def kernel(x, c, w_ada, b_ada, g_ffn1_pre, g_ffn1_post, w1_gate, w1_up, w1_down, g_mix_pre,
           g_mix_post, w_in, b_forget, g_q_a, w_uq, g_kv_a, w_ukv, w_o, g_ffn2_pre, g_ffn2_post,
           w2_gate, w2_up, w2_down):
    batch, seq, d = x.shape
    depth = w_ada.shape[0]
    t = batch * seq
    xt = x.reshape(t, d)
    c_pad = jnp.pad(c, ((0, 8 - batch), (0, 0)))
    cos_t, sin_lo, sin_hi = _rope_tables(seq)

    for l in range(depth):
        mod = _ada(c_pad, w_ada[l], b_ada[l:l + 1], tn=1024)[:batch]
        sh1, sc1, gt1, sh2, sc2, gt2, sh3, sc3, gt3 = [
            mod[:, n * d:(n + 1) * d].reshape(batch, 1, d) for n in range(9)]

        ffn1 = functools.partial(_ffn, xt, sh1, sc1, gt1, g_ffn1_pre[l:l + 1], g_ffn1_post[l:l + 1],
                                 seq=seq, tm=1024, res_weight=0.5)
        y_head, w1g, w1u, w1d = ffn1(w1_gate[l], w1_up[l], w1_down[l], tf=256, mode="head")
        xt = lax.dynamic_update_slice(ffn1(w1g, w1u, w1d, tf=512, mode="tail"), y_head, (0, 0))

        wi = w_in[l]
        fox0 = KPE_COL + MLA_ROPE
        fv0 = fox0 + 2 * FOX_WIDTH
        fl0 = fox0 + 3 * FOX_WIDTH
        w_in_p = jnp.concatenate(
            [wi[:, :fox0], jnp.zeros((d, LANES - MLA_ROPE), F32),
             wi[:, fl0:], jnp.zeros((d, LANES - FOX_HEADS), F32),
             wi[:, fox0:fv0]], axis=1).astype(BF16)
        w_fvt = wi[:, fv0:fl0].T.astype(BF16)

        wuq_p = jnp.pad(w_uq[l].reshape(MLA_Q_RANK, MLA_HEADS, MLA_NOPE + MLA_ROPE),
                        ((0, 0), (0, 0), (0, MLA_QK_PAD - MLA_NOPE - MLA_ROPE))
                        ).reshape(MLA_Q_RANK, MLA_HEADS * MLA_QK_PAD).astype(BF16)
        wukv = w_ukv[l].reshape(MLA_KV_RANK, MLA_HEADS, MLA_NOPE + MLA_V)
        wuk = wukv[:, :, :MLA_NOPE].reshape(MLA_KV_RANK, -1).astype(BF16)
        wuvt = wukv[:, :, MLA_NOPE:].reshape(MLA_KV_RANK, -1).T.astype(BF16)
        bf_pad = jnp.pad(b_forget[l:l + 1], ((0, 0), (0, LANES - FOX_HEADS)))
        fqk, fvt, q_mla, k_mla, vt_mla, cum_k, cum_q = _mixer_in(
            xt, sh2, sc2, g_mix_pre[l:l + 1], w_in_p, w_fvt, g_q_a[l:l + 1], g_kv_a[l:l + 1],
            wuq_p, wuk, wuvt, bf_pad, cos_t, sin_lo, sin_hi, batch=batch, seq=seq)

        o_mla, w2g, w2u = _attention(q_mla, k_mla, vt_mla, batch=batch, seq=seq, heads=MLA_HEADS,
                                     dk=MLA_QK_PAD, dv=MLA_V, q_col=0, k_col=0, mask_chunk=CHUNK,
                                     riders=(w2_gate[l], w2_up[l]))
        o_fox, w2d = _attention(fqk, fqk, fvt, cum_k, cum_q, batch=batch, seq=seq, heads=FOX_HEADS,
                                dk=FOX_DIM, dv=FOX_DIM, q_col=0, k_col=FOX_HEADS, mask_chunk=1,
                                riders=(w2_down[l],))

        wo = w_o[l].astype(BF16)
        xt = _outproj(xt, o_mla, o_fox, wo[:MLA_HEADS * MLA_V], wo[MLA_HEADS * MLA_V:],
                      g_mix_post[l:l + 1], gt2, seq=seq, tm=512)

        xt = _ffn(xt, sh3, sc3, gt3, g_ffn2_pre[l:l + 1], g_ffn2_post[l:l + 1],
                  w2g, w2u, w2d, seq=seq, tm=1024, tf=512, res_weight=0.5)

    return xt.reshape(batch, seq, d)
```

```python
import functools
import math

import jax
import jax.numpy as jnp
from jax import lax
from jax.experimental import pallas as pl
from jax.experimental.pallas import tpu as pltpu

F32 = jnp.float32
BF16 = jnp.bfloat16

V7X_VMEM_BYTES = 64 * 1024 * 1024
LANES = 128

EPS = 1e-6
ROPE_THETA = 10000.0
CHUNK = 64
MLA_HEADS = 8
MLA_Q_RANK = 512
MLA_KV_RANK = 256
MLA_NOPE = 128
MLA_ROPE = 64
MLA_V = 128
MLA_QK_PAD = 256
FOX_HEADS = 8
FOX_DIM = 128
FOX_WIDTH = FOX_HEADS * FOX_DIM
LATENT_WIDTH = 1024
KPE_COL = MLA_Q_RANK + MLA_KV_RANK
FLOGIT_COL = KPE_COL + LANES
ATTN_TILE = 512
PHASE_B_UNROLL = 14
LOG2E = math.log2(math.e)
MLA_Q_SCALE = LOG2E / math.sqrt(MLA_NOPE + MLA_ROPE)
FOX_Q_SCALE = LOG2E / math.sqrt(FOX_DIM)
NEG_BIG = -1e30

_NT = (((1,), (1,)), ((), ()))


def _params(vmem_bytes):
    limit = min(int(vmem_bytes * 1.25) + (4 << 20), V7X_VMEM_BYTES - (8 << 20))
    return pltpu.CompilerParams(vmem_limit_bytes=limit)


def _rms(x, g):
    return x * lax.rsqrt(jnp.mean(x * x, axis=-1, keepdims=True) + EPS) * g


STAT_ROWS = 8
APPLY_ROWS = 16


def _rms_stats(src_ref, stat_ref):
    tm, d = src_ref.shape

    def step(c, carry):
        rows = pl.ds(pl.multiple_of(c * STAT_ROWS, STAT_ROWS), STAT_ROWS)
        x = src_ref[rows, :]
        ms = jnp.sum(x * x, axis=-1, keepdims=True) * (1.0 / d)
        stat_ref[rows, :] = jnp.broadcast_to(lax.rsqrt(ms + EPS), (STAT_ROWS, LANES))
        return carry

    lax.fori_loop(0, tm // STAT_ROWS, step, 0, unroll=32)


def _for_row_blocks(tm, fn):
    def step(c, carry):
        fn(pl.ds(pl.multiple_of(c * APPLY_ROWS, APPLY_ROWS), APPLY_ROWS))
        return carry

    lax.fori_loop(0, tm // APPLY_ROWS, step, 0, unroll=4)


def _modulated_norm(x_ref, stat_ref, coef_ref, gpre_ref, sc_ref, sh_ref, h_ref):
    tm, d = x_ref.shape
    coef_ref[0:1, :] = gpre_ref[...] * (1.0 + sc_ref[0])
    coef_ref[1:2, :] = sh_ref[0]
    _rms_stats(x_ref, stat_ref)

    def apply(rows):
        r = stat_ref[rows, :]
        for k in range(d // LANES):
            cols = slice(k * LANES, (k + 1) * LANES)
            h = x_ref[rows, cols] * r * coef_ref[0:1, cols] + coef_ref[1:2, cols]
            h_ref[rows, cols] = h.astype(BF16)

    _for_row_blocks(tm, apply)


def _gated_norm_residual(x_ref, o_ref, stat_ref, coef_ref, gpost_ref, gt_ref, res_weight):
    tm, d = o_ref.shape
    coef_ref[0:1, :] = res_weight * gt_ref[0] * gpost_ref[...]
    _rms_stats(o_ref, stat_ref)

    def apply(rows):
        r = stat_ref[rows, :]
        for k in range(d // LANES):
            cols = slice(k * LANES, (k + 1) * LANES)
            o_ref[rows, cols] = x_ref[rows, cols] + o_ref[rows, cols] * r * coef_ref[0:1, cols]

    _for_row_blocks(tm, apply)


def _ada_kernel(c_ref, w_ref, b_ref, o_ref):
    c = c_ref[...]
    cond = (c * jax.nn.sigmoid(c)).astype(BF16)
    o_ref[...] = jnp.dot(cond, w_ref[...].astype(BF16), preferred_element_type=F32) + b_ref[...]


def _ada(c_pad, w, b, *, tn, n):
    m, d = c_pad.shape
    vmem = 2 * d * tn * 4 + d * tn * 2 + 4 * m * tn * 4
    return pl.pallas_call(
        _ada_kernel,
        grid=(n // tn,),
        in_specs=[pl.BlockSpec((m, d), lambda j: (0, 0)),
                  pl.BlockSpec((d, tn), lambda j: (0, j)),
                  pl.BlockSpec((1, tn), lambda j: (0, j))],
        out_specs=pl.BlockSpec((m, tn), lambda j: (0, j)),
        out_shape=jax.ShapeDtypeStruct((m, n), F32),
        compiler_params=_params(vmem),
        name="ada",
    )(c_pad, w, b)


def _ffn_kernel(*refs, res_weight, mode):
    x_ref, sh_ref, sc_ref, gt_ref, gpre_ref, gpost_ref, wg_ref, wu_ref, wd_ref = refs[:9]
    if mode == "head":
        o_ref, wg_out, wu_out, wd_out, h_ref, stat_ref, coef_ref = refs[9:]
    else:
        o_ref, h_ref, stat_ref, coef_ref = refs[9:]
    f = pl.program_id(1)
    active = pl.program_id(0) > 0 if mode == "tail" else True

    def swiglu_down(h):
        wg, wu, wd = wg_ref[...], wu_ref[...], wd_ref[...]
        if mode == "head":
            wg, wu, wd = wg.astype(BF16), wu.astype(BF16), wd.astype(BF16)
            wg_out[...], wu_out[...], wd_out[...] = wg, wu, wd
        g = jnp.dot(h, wg, preferred_element_type=F32)
        u = jnp.dot(h, wu, preferred_element_type=F32)
        a = (g * jax.nn.sigmoid(g) * u).astype(BF16)
        return jnp.dot(a, wd, preferred_element_type=F32)

    @pl.when(jnp.logical_and(active, f == 0))
    def _():
        h = (_rms(x_ref[...], gpre_ref[...]) * (1.0 + sc_ref[0]) + sh_ref[0]).astype(BF16)
        h_ref[...] = h
        o_ref[...] = swiglu_down(h)

    @pl.when(jnp.logical_and(active, f > 0))
    def _():
        o_ref[...] += swiglu_down(h_ref[...])

    @pl.when(jnp.logical_and(active, f == pl.num_programs(1) - 1))
    def _():
        _gated_norm_residual(x_ref, o_ref, stat_ref, coef_ref, gpost_ref, gt_ref, res_weight)

    if mode == "tail":
        @pl.when(jnp.logical_and(jnp.logical_not(active), f == 0))
        def _():
            o_ref[...] = jnp.zeros_like(o_ref)


def _ffn(x, sh, sc, gt, g_pre, g_post, wg, wu, wd, *, seq, tm, tf, res_weight, mode="all"):
    t, d = x.shape
    ff = wg.shape[1]
    tpb = seq // tm
    head = mode == "head"
    n_tiles = 1 if head else t // tm
    w_bytes = 4 if head else 2
    x_bufs = 1 if head else 2
    vmem = (2 * x_bufs * tm * d * 4 + tm * d * 2 + 6 * d * tf * w_bytes + 3 * tm * tf * 4
            + (9 * d * tf * 2 if head else 0))
    mod_spec = pl.BlockSpec((1, 1, d), lambda i, f: (i // tpb, 0, 0))
    gain_spec = pl.BlockSpec((1, d), lambda i, f: (0, 0))
    if mode == "tail":
        f_of = lambda i, f: jnp.where(i == 0, 0, f)
        x_spec = pl.BlockSpec((tm, d), lambda i, f: (jnp.maximum(i, 1), 0))
    else:
        f_of = lambda i, f: f
        x_spec = pl.BlockSpec((tm, d), lambda i, f: (i, 0), **(dict(pipeline_mode=pl.Buffered(1)) if head else {}))
    up_spec = pl.BlockSpec((d, tf), lambda i, f: (0, f_of(i, f)))
    down_spec = pl.BlockSpec((tf, d), lambda i, f: (f_of(i, f), 0))
    tile_kw = dict(pipeline_mode=pl.Buffered(1)) if head else {}
    out_specs = pl.BlockSpec((tm, d), lambda i, f: (i, 0), **tile_kw)
    out_shape = jax.ShapeDtypeStruct((n_tiles * tm, d), F32)
    if head:
        out_specs = [out_specs, up_spec, up_spec, down_spec]
        out_shape = [out_shape, jax.ShapeDtypeStruct(wg.shape, BF16), jax.ShapeDtypeStruct(wu.shape, BF16),
                     jax.ShapeDtypeStruct(wd.shape, BF16)]
    return pl.pallas_call(
        functools.partial(_ffn_kernel, res_weight=res_weight, mode=mode),
        grid=(n_tiles, ff // tf),
        in_specs=[x_spec, mod_spec, mod_spec, mod_spec, gain_spec, gain_spec, up_spec, up_spec, down_spec],
        out_specs=out_specs,
        out_shape=out_shape,
        scratch_shapes=[pltpu.VMEM((tm, d), BF16), pltpu.VMEM((tm, LANES), F32), pltpu.VMEM((8, d), F32)],
        compiler_params=_params(vmem),
        name="ffn_" + mode,
    )(x, sh, sc, gt, g_pre, g_post, wg, wu, wd)


def _rot(r, cos_t, sin_lo, sin_hi):
    return r * cos_t + pltpu.roll(r, 96, 1) * sin_lo + pltpu.roll(r, 32, 1) * sin_hi


def _mla_fox_prep(lat, gq_ref, gkv_ref, wuq_ref, wuk_ref, wuvt_ref, bf_ref, cos_ref, slo_ref, shi_ref,
                  q_ref, k_ref, vt_ref, ck_ref, cq_ref, carry_ref):
    tm = lat.shape[0]
    cos_t, sin_lo, sin_hi = cos_ref[...], slo_ref[...], shi_ref[...]

    qn = _rms(lat[:, :MLA_Q_RANK], gq_ref[...]).astype(BF16)
    q = jnp.dot(qn, wuq_ref[...], preferred_element_type=F32) * MLA_Q_SCALE
    kvn = _rms(lat[:, MLA_Q_RANK:KPE_COL], gkv_ref[...]).astype(BF16)
    k_nope = jnp.dot(kvn, wuk_ref[...], preferred_element_type=F32)
    vt_ref[0, 0] = lax.dot_general(wuvt_ref[...], kvn, _NT, preferred_element_type=F32).astype(BF16)
    k_rope = _rot(lat[:, KPE_COL:KPE_COL + LANES], cos_t, sin_lo, sin_hi).astype(BF16)
    for h in range(MLA_HEADS):
        c0 = h * MLA_QK_PAD
        q_ref[:, c0:c0 + MLA_NOPE] = q[:, c0:c0 + MLA_NOPE].astype(BF16)
        q_ref[:, c0 + MLA_NOPE:c0 + MLA_QK_PAD] = _rot(
            q[:, c0 + MLA_NOPE:c0 + MLA_QK_PAD], cos_t, sin_lo, sin_hi).astype(BF16)
        k_ref[:, c0:c0 + MLA_NOPE] = k_nope[:, h * MLA_NOPE:(h + 1) * MLA_NOPE].astype(BF16)
        k_ref[:, c0 + MLA_NOPE:c0 + MLA_QK_PAD] = k_rope

    z = lat[:, FLOGIT_COL:FLOGIT_COL + LANES] + bf_ref[...]
    lane = lax.broadcasted_iota(jnp.int32, (tm, LANES), 1)
    log_f = jnp.where(lane < FOX_HEADS, jnp.minimum(z, 0.0) - jnp.log1p(jnp.exp(-jnp.abs(z))), 0.0)
    hi = log_f.astype(BF16).astype(F32)
    mid = (log_f - hi).astype(BF16).astype(F32)
    lo = (log_f - hi - mid).astype(BF16).astype(F32)
    pieces = hi + pltpu.roll(mid, FOX_HEADS, 1) + pltpu.roll(lo, 2 * FOX_HEADS, 1)
    row = lax.broadcasted_iota(jnp.int32, (tm, tm), 0)
    col = lax.broadcasted_iota(jnp.int32, (tm, tm), 1)
    tri = (col <= row).astype(BF16)
    part = jnp.dot(tri, pieces.astype(BF16), preferred_element_type=F32)
    part = part + pltpu.roll(part, LANES - FOX_HEADS, 1) + pltpu.roll(part, LANES - 2 * FOX_HEADS, 1)
    cum = jnp.where(lane < FOX_HEADS, part, 0.0) + carry_ref[...]
    carry_ref[...] = cum[tm - 1:tm, :]
    cum2 = cum * LOG2E
    cq_ref[0, 0] = cum2.T[:FOX_HEADS, :]
    for h in range(FOX_HEADS):
        ck_ref[0, h] = jnp.broadcast_to(cum2[:, h:h + 1], (tm, LANES))


def _mixer_in_kernel(x_ref, sh_ref, sc_ref, gpre_ref, w_ref, wvt_ref,
                     gq_ref, gkv_ref, wuq_ref, wuk_ref, wuvt_ref, bf_ref, cos_ref, slo_ref, shi_ref,
                     fqk_ref, fvt_ref, q_ref, k_ref, vt_ref, ck_ref, cq_ref, carry_ref, *, tiles_per_batch):
    @pl.when(pl.program_id(0) % tiles_per_batch == 0)
    def _():
        carry_ref[...] = jnp.zeros_like(carry_ref)

    tn = LATENT_WIDTH
    h = (_rms(x_ref[...], gpre_ref[...]) * (1.0 + sc_ref[0]) + sh_ref[0]).astype(BF16)
    lat = jnp.dot(h, w_ref[:, :tn], preferred_element_type=F32)
    fq = jnp.dot(h, w_ref[:, tn:2 * tn], preferred_element_type=F32) * FOX_Q_SCALE
    fqk_ref[:, :tn] = fq.astype(BF16)
    fqk_ref[:, tn:] = jnp.dot(h, w_ref[:, 2 * tn:], preferred_element_type=F32).astype(BF16)
    fvt_ref[0, 0] = lax.dot_general(wvt_ref[...], h, _NT, preferred_element_type=F32).astype(BF16)
    _mla_fox_prep(lat, gq_ref, gkv_ref, wuq_ref, wuk_ref, wuvt_ref, bf_ref, cos_ref, slo_ref, shi_ref,
                  q_ref, k_ref, vt_ref, ck_ref, cq_ref, carry_ref)


def _mixer_in(x, sh, sc, g_pre, w, wvt, g_q, g_kv, wuq, wuk, wuvt, b_forget, cos_t, sin_lo, sin_hi,
              *, batch, seq):
    t, d = x.shape
    tm = ATTN_TILE
    tn = LATENT_WIDTH
    tpb = seq // tm
    qk_w = MLA_HEADS * MLA_QK_PAD
    v_w = MLA_HEADS * MLA_V
    assert w.shape == (d, 3 * tn) and wvt.shape == (FOX_WIDTH, d)
    resident_bytes = (w.size + wvt.size + wuq.size + wuk.size + wuvt.size) * 2
    vmem = (2 * tm * d * 4 + tm * d * 2 + resident_bytes + 2 * tm * (2 * tn + FOX_WIDTH) * 2
            + 2 * tm * (2 * qk_w + v_w) * 2 + 2 * FOX_HEADS * tm * LANES * 4 + 6 * tm * LANES * 4
            + tm * (tn + qk_w) * 4)
    mod_spec = pl.BlockSpec((1, 1, d), lambda i: (i // tpb, 0, 0))
    resident = lambda a: pl.BlockSpec(a.shape, lambda i: (0, 0), pipeline_mode=pl.Buffered(1))
    small = lambda a: pl.BlockSpec(a.shape, lambda i: (0, 0))
    tab_spec = pl.BlockSpec((tm, LANES), lambda i: (i % tpb, 0))
    return pl.pallas_call(
        functools.partial(_mixer_in_kernel, tiles_per_batch=tpb),
        grid=(t // tm,),
        in_specs=[pl.BlockSpec((tm, d), lambda i: (i, 0)),
                  mod_spec, mod_spec, small(g_pre), resident(w), resident(wvt),
                  small(g_q), small(g_kv), resident(wuq), resident(wuk), resident(wuvt), small(b_forget),
                  tab_spec, tab_spec, tab_spec],
        out_specs=[pl.BlockSpec((tm, 2 * tn), lambda i: (i, 0)),
                   pl.BlockSpec((1, 1, FOX_WIDTH, tm), lambda i: (i // tpb, i % tpb, 0, 0)),
                   pl.BlockSpec((tm, qk_w), lambda i: (i, 0)),
                   pl.BlockSpec((tm, qk_w), lambda i: (i, 0)),
                   pl.BlockSpec((1, 1, v_w, tm), lambda i: (i // tpb, i % tpb, 0, 0)),
                   pl.BlockSpec((1, FOX_HEADS, tm, LANES), lambda i: (i // tpb, 0, i % tpb, 0)),
                   pl.BlockSpec((1, 1, FOX_HEADS, tm), lambda i: (i // tpb, i % tpb, 0, 0))],
        out_shape=[jax.ShapeDtypeStruct((t, 2 * tn), BF16),
                   jax.ShapeDtypeStruct((batch, tpb, FOX_WIDTH, tm), BF16),
                   jax.ShapeDtypeStruct((t, qk_w), BF16),
                   jax.ShapeDtypeStruct((t, qk_w), BF16),
                   jax.ShapeDtypeStruct((batch, tpb, v_w, tm), BF16),
                   jax.ShapeDtypeStruct((batch, FOX_HEADS, seq, LANES), F32),
                   jax.ShapeDtypeStruct((batch, tpb, FOX_HEADS, tm), F32)],
        scratch_shapes=[pltpu.VMEM((1, LANES), F32)],
        compiler_params=_params(vmem),
        name="mixer_in",
    )(x, sh, sc, g_pre, w, wvt, g_q, g_kv, wuq, wuk, wuvt, b_forget, cos_t, sin_lo, sin_hi)


def _attn_kernel(*refs, mask_chunk, fox, n_cast, mod_rider):
    n_in = 5 if fox else 3
    n_mod = 3 if mod_rider else 0
    n_side_in = n_cast + n_mod
    side_in = refs[n_in:n_in + n_side_in]
    side_out = refs[n_in + n_side_in + 1:n_in + n_side_in + 1 + n_cast + (1 if mod_rider else 0)]
    refs = refs[:n_in] + (refs[n_in + n_side_in],) + refs[n_in + n_side_in + 1 + len(side_out):]
    if fox:
        q_ref, k_ref, vt_ref, ck_ref, cq_ref, o_ref, s0_ref, s1_ref, m_ref, l_ref, acc_ref = refs
    else:
        q_ref, k_ref, vt_ref, o_ref, s0_ref, s1_ref, m_ref, l_ref, acc_ref = refs
    for src, dst in zip(side_in[:n_cast], side_out[:n_cast]):
        dst[...] = src[...].astype(BF16)
    if mod_rider:
        c_ref, wada_ref, bada_ref = side_in[n_cast:]
        _ada_kernel(c_ref, wada_ref, bada_ref, side_out[n_cast])
    head = pl.program_id(1)
    tq = ATTN_TILE
    nq = q_ref.shape[0] // tq
    n_low = nq * (nq - 1) // 2
    assert PHASE_B_UNROLL % 2 == 0 and n_low % PHASE_B_UNROLL == 0
    bufs = (s0_ref, s1_ref)

    def rows(t):
        return pl.ds(pl.multiple_of(t * tq, tq), tq)

    def scores(i, j, s_ref):
        s = lax.dot_general(k_ref[rows(j), :], q_ref[rows(i), :], _NT, preferred_element_type=F32)
        if fox:
            cum_q = cq_ref[0, i, pl.ds(head, 1), :]
            cum_k = ck_ref[0, 0, rows(j), :]
            s = s + (cum_q - jnp.concatenate([cum_k] * (tq // LANES), axis=1))
        s_ref[...] = s

    half = tq // 2

    def scores_diagonal(i, s_ref):
        lo, hi = i * tq, i * tq + half
        s_a = lax.dot_general(k_ref[lo:hi, :], q_ref[lo:lo + tq, :], _NT, preferred_element_type=F32)
        s_b = lax.dot_general(k_ref[hi:hi + half, :], q_ref[hi:hi + half, :], _NT,
                              preferred_element_type=F32)
        if fox:
            cum_q = cq_ref[0, i, pl.ds(head, 1), :]
            s_a = s_a + (cum_q - jnp.concatenate([ck_ref[0, 0, lo:hi, :]] * (tq // LANES), axis=1))
            s_b = s_b + (cum_q[:, half:]
                         - jnp.concatenate([ck_ref[0, 0, hi:hi + half, :]] * (half // LANES), axis=1))
        src = lax.broadcasted_iota(jnp.int32, (half, half), 0) // mask_chunk
        dst = lax.broadcasted_iota(jnp.int32, (half, half), 1) // mask_chunk
        s_ref[:half, :half] = jnp.where(src <= dst, s_a[:, :half], NEG_BIG)
        s_ref[:half, half:] = s_a[:, half:]
        s_ref[half:, half:] = jnp.where(src <= dst, s_b, NEG_BIG)

    def accumulate_diagonal(i, s_ref):
        s_a = s_ref[:half, :]
        s_b = s_ref[half:, half:]
        m_a = jnp.max(s_a, axis=0, keepdims=True)
        m_hi = jnp.maximum(m_a[:, half:], jnp.max(s_b, axis=0, keepdims=True))
        m = jnp.concatenate([m_a[:, :half], m_hi], axis=1)
        p_a = jnp.exp2(s_a - m)
        p_b = jnp.exp2(s_b - m_hi)
        l_a = jnp.sum(p_a, axis=0, keepdims=True)
        m_ref[i] = m
        l_ref[i] = jnp.concatenate([l_a[:, :half], l_a[:, half:] + jnp.sum(p_b, axis=0, keepdims=True)],
                                   axis=1)
        acc_a = jnp.dot(vt_ref[0, i, :, :half], p_a.astype(BF16), preferred_element_type=F32)
        acc_b = jnp.dot(vt_ref[0, i, :, half:], p_b.astype(BF16), preferred_element_type=F32)
        acc_ref[i, :, :half] = acc_a[:, :half]
        acc_ref[i, :, half:] = acc_a[:, half:] + acc_b

    def accumulate(i, j, s_ref):
        s = s_ref[...]
        m_prev = m_ref[i]
        m_new = jnp.maximum(m_prev, jnp.max(s, axis=0, keepdims=True))
        alpha = jnp.exp2(m_prev - m_new)
        p = jnp.exp2(s - m_new)
        l_ref[i] = alpha * l_ref[i] + jnp.sum(p, axis=0, keepdims=True)
        acc_ref[i] = alpha * acc_ref[i] + jnp.dot(vt_ref[0, j], p.astype(BF16),
                                                  preferred_element_type=F32)
        m_ref[i] = m_new

    scores_diagonal(0, bufs[0])
    for i in range(nq):
        if i + 1 < nq:
            scores_diagonal(i + 1, bufs[(i + 1) % 2])
        else:
            scores(1, 0, bufs[(i + 1) % 2])
        accumulate_diagonal(i, bufs[i % 2])

    def advance(i, j):
        wraps = j + 1 == i
        i_nxt = jnp.minimum(jnp.where(wraps, i + 1, i), nq - 1)
        return i_nxt, jnp.where(wraps, 0, j + 1)

    def body(_, carry):
        i, j = carry
        for u in range(PHASE_B_UNROLL):
            i_nxt, j_nxt = advance(i, j)
            scores(i_nxt, j_nxt, bufs[(nq + u + 1) % 2])
            accumulate(i, j, bufs[(nq + u) % 2])
            i, j = i_nxt, j_nxt
        return i, j

    lax.fori_loop(0, n_low // PHASE_B_UNROLL, body, (jnp.int32(1), jnp.int32(0)))

    for i in range(nq):
        o_ref[i * tq:(i + 1) * tq, :] = (acc_ref[i] / l_ref[i]).T.astype(o_ref.dtype)


def _attention(q_arr, k_arr, vt_arr, cum_k=None, cum_q=None, *, batch, seq, heads, dk, dv,
               q_col, k_col, mask_chunk, riders=(), mod_rider=None):
    fox = cum_k is not None
    tq = ATTN_TILE
    nq = seq // tq
    steps = batch * heads
    in_specs = [pl.BlockSpec((seq, dk), lambda b, h: (b, q_col + h)),
                pl.BlockSpec((seq, dk), lambda b, h: (b, k_col + h)),
                pl.BlockSpec((1, nq, dv, tq), lambda b, h: (b, 0, h, 0))]
    args = [q_arr, k_arr, vt_arr]
    if fox:
        in_specs += [pl.BlockSpec((1, 1, seq, LANES), lambda b, h: (b, h, 0, 0)),
                     pl.BlockSpec((1, nq, heads, tq), lambda b, h: (b, 0, 0, 0))]
        args += [cum_k, cum_q]
    slab_specs = [pl.BlockSpec((w.shape[0] // steps, w.shape[1]), lambda b, h: (b * heads + h, 0))
                  for w in riders]
    vmem = (2 * seq * (2 * dk + 2 * dv) * 2 + 2 * seq * LANES * 4 + 2 * seq * heads * 4
            + seq * dv * 4 + 8 * tq * tq * 4 + sum(2 * 6 * w.size // steps for w in riders))
    side_in_specs, side_out_specs, side_args = list(slab_specs), list(slab_specs), list(riders)
    side_out_shape = [jax.ShapeDtypeStruct(w.shape, BF16) for w in riders]
    if mod_rider is not None:
        c_pad, w_ada, b_ada, first_col = mod_rider
        m, d_model = c_pad.shape
        tn = (w_ada.shape[1] - first_col) // steps
        assert first_col % tn == 0
        col = lambda b, h: (0, first_col // tn + b * heads + h)
        side_in_specs += [pl.BlockSpec((m, d_model), lambda b, h: (0, 0)),
                          pl.BlockSpec((d_model, tn), col), pl.BlockSpec((1, tn), col)]
        side_out_specs += [pl.BlockSpec((m, tn), lambda b, h: (0, b * heads + h))]
        side_out_shape += [jax.ShapeDtypeStruct((m, tn * steps), F32)]
        side_args += [c_pad, w_ada, b_ada]
        vmem += 2 * d_model * tn * 4 + d_model * tn * 2
    return pl.pallas_call(
        functools.partial(_attn_kernel, mask_chunk=mask_chunk, fox=fox, n_cast=len(riders),
                          mod_rider=mod_rider is not None),
        grid=(batch, heads),
        in_specs=in_specs + side_in_specs,
        out_specs=[pl.BlockSpec((seq, dv), lambda b, h: (b, h))] + side_out_specs,
        out_shape=[jax.ShapeDtypeStruct((batch * seq, heads * dv), BF16)] + side_out_shape,
        scratch_shapes=[pltpu.VMEM((tq, tq), F32), pltpu.VMEM((tq, tq), F32),
                        pltpu.VMEM((nq, 1, tq), F32), pltpu.VMEM((nq, 1, tq), F32),
                        pltpu.VMEM((nq, dv, tq), F32)],
        compiler_params=_params(vmem),
        name="fox_attn" if fox else "mla_attn",
    )(*args, *side_args)


def _outproj_kernel(x_ref, oa_ref, ob_ref, wa_ref, wb_ref, gpost_ref, gt_ref, o_ref):
    y = jnp.dot(oa_ref[...], wa_ref[...], preferred_element_type=F32)
    y = y + jnp.dot(ob_ref[...], wb_ref[...], preferred_element_type=F32)
    o_ref[...] = x_ref[...] + gt_ref[0] * _rms(y, gpost_ref[...])


def _outproj(x, o_a, o_b, w_a, w_b, g_post, gt, *, seq, tm):
    t, d = x.shape
    ka, kb = o_a.shape[1], o_b.shape[1]
    tpb = seq // tm
    vmem = 4 * tm * d * 4 + 2 * tm * (ka + kb) * 2 + 2 * (ka + kb) * d * 2 + 2 * tm * d * 4
    return pl.pallas_call(
        _outproj_kernel,
        grid=(t // tm,),
        in_specs=[pl.BlockSpec((tm, d), lambda i: (i, 0)),
                  pl.BlockSpec((tm, ka), lambda i: (i, 0)),
                  pl.BlockSpec((tm, kb), lambda i: (i, 0)),
                  pl.BlockSpec((ka, d), lambda i: (0, 0)),
                  pl.BlockSpec((kb, d), lambda i: (0, 0)),
                  pl.BlockSpec((1, d), lambda i: (0, 0)),
                  pl.BlockSpec((1, 1, d), lambda i: (i // tpb, 0, 0))],
        out_specs=pl.BlockSpec((tm, d), lambda i: (i, 0)),
        out_shape=jax.ShapeDtypeStruct((t, d), F32),
        compiler_params=_params(vmem),
        name="out_proj",
    )(x, o_a, o_b, w_a, w_b, g_post, gt)


def _rope_tables(seq):
    half = MLA_ROPE // 2
    inv = ROPE_THETA ** (-jnp.arange(half, dtype=F32) / half)
    ang = jnp.arange(seq).astype(F32)[:, None] * inv[None, :]
    cos, sin = jnp.cos(ang), jnp.sin(ang)
    zero = jnp.zeros_like(cos)
    pad = jnp.zeros((seq, LANES - 2 * half), F32)
    return (jnp.concatenate([cos, cos, pad], axis=1),
            jnp.concatenate([-sin, zero, pad], axis=1),
            jnp.concatenate([zero, sin, pad], axis=1))


def kernel(x, c, w_ada, b_ada, g_ffn1_pre, g_ffn1_post, w1_gate, w1_up, w1_down, g_mix_pre,
           g_mix_post, w_in, b_forget, g_q_a, w_uq, g_kv_a, w_ukv, w_o, g_ffn2_pre, g_ffn2_post,
           w2_gate, w2_up, w2_down):
    batch, seq, d = x.shape
    depth = w_ada.shape[0]
    t = batch * seq
    xt = x.reshape(t, d)
    c_pad = jnp.pad(c, ((0, 8 - batch), (0, 0)))
    cos_t, sin_lo, sin_hi = _rope_tables(seq)

    for l in range(depth):
        early = 5
        chunks = lambda mod: [mod[:batch, n * d:(n + 1) * d].reshape(batch, 1, d)
                              for n in range(mod.shape[1] // d)]
        sh1, sc1, gt1, sh2, sc2 = chunks(_ada(c_pad, w_ada[l], b_ada[l:l + 1], tn=1024, n=early * d))

        ffn1 = functools.partial(_ffn, xt, sh1, sc1, gt1, g_ffn1_pre[l:l + 1], g_ffn1_post[l:l + 1],
                                 seq=seq, tm=1024, res_weight=0.5)
        y_head, w1g, w1u, w1d = ffn1(w1_gate[l], w1_up[l], w1_down[l], tf=256, mode="head")
        xt = lax.dynamic_update_slice(ffn1(w1g, w1u, w1d, tf=512, mode="tail"), y_head, (0, 0))

        wi = w_in[l]
        fox0 = KPE_COL + MLA_ROPE
        fv0 = fox0 + 2 * FOX_WIDTH
        fl0 = fox0 + 3 * FOX_WIDTH
        w_in_p = jnp.concatenate(
            [wi[:, :fox0], jnp.zeros((d, LANES - MLA_ROPE), F32),
             wi[:, fl0:], jnp.zeros((d, LANES - FOX_HEADS), F32),
             wi[:, fox0:fv0]], axis=1).astype(BF16)
        w_fvt = wi[:, fv0:fl0].T.astype(BF16)

        wuq_p = jnp.pad(w_uq[l].reshape(MLA_Q_RANK, MLA_HEADS, MLA_NOPE + MLA_ROPE),
                        ((0, 0), (0, 0), (0, MLA_QK_PAD - MLA_NOPE - MLA_ROPE))
                        ).reshape(MLA_Q_RANK, MLA_HEADS * MLA_QK_PAD).astype(BF16)
        wukv = w_ukv[l].reshape(MLA_KV_RANK, MLA_HEADS, MLA_NOPE + MLA_V)
        wuk = wukv[:, :, :MLA_NOPE].reshape(MLA_KV_RANK, -1).astype(BF16)
        wuvt = wukv[:, :, MLA_NOPE:].reshape(MLA_KV_RANK, -1).T.astype(BF16)
        bf_pad = jnp.pad(b_forget[l:l + 1], ((0, 0), (0, LANES - FOX_HEADS)))
        fqk, fvt, q_mla, k_mla, vt_mla, cum_k, cum_q = _mixer_in(
            xt, sh2, sc2, g_mix_pre[l:l + 1], w_in_p, w_fvt, g_q_a[l:l + 1], g_kv_a[l:l + 1],
            wuq_p, wuk, wuvt, bf_pad, cos_t, sin_lo, sin_hi, batch=batch, seq=seq)

        o_mla, w2g, w2u = _attention(q_mla, k_mla, vt_mla, batch=batch, seq=seq, heads=MLA_HEADS,
                                     dk=MLA_QK_PAD, dv=MLA_V, q_col=0, k_col=0, mask_chunk=CHUNK,
                                     riders=(w2_gate[l], w2_up[l]))
        o_fox, w2d, wo, mod_late = _attention(
            fqk, fqk, fvt, cum_k, cum_q, batch=batch, seq=seq, heads=FOX_HEADS, dk=FOX_DIM, dv=FOX_DIM,
            q_col=0, k_col=FOX_HEADS, mask_chunk=1, riders=(w2_down[l], w_o[l]),
            mod_rider=(c_pad, w_ada[l], b_ada[l:l + 1], early * d))
        gt2, sh3, sc3, gt3 = chunks(mod_late)

        xt = _outproj(xt, o_mla, o_fox, wo[:MLA_HEADS * MLA_V], wo[MLA_HEADS * MLA_V:],
                      g_mix_post[l:l + 1], gt2, seq=seq, tm=512)

        xt = _ffn(xt, sh3, sc3, gt3, g_ffn2_pre[l:l + 1], g_ffn2_post[l:l + 1],
                  w2g, w2u, w2d, seq=seq, tm=1024, tf=512, res_weight=0.5)

    return xt.reshape(batch, seq, d)
```

```python
import functools
import math

import jax
import jax.numpy as jnp
import numpy as np
from jax import lax
from jax.experimental import pallas as pl
from jax.experimental.pallas import tpu as pltpu

F32 = jnp.float32
BF16 = jnp.bfloat16

V7X_VMEM_BYTES = 64 * 1024 * 1024
LANES = 128

EPS = 1e-6
ROPE_THETA = 10000.0
CHUNK = 64
MLA_HEADS = 8
MLA_Q_RANK = 512
MLA_KV_RANK = 256
MLA_NOPE = 128
MLA_ROPE = 64
MLA_V = 128
MLA_QK_PAD = 256
FOX_HEADS = 8
FOX_DIM = 128
FOX_WIDTH = FOX_HEADS * FOX_DIM
LATENT_WIDTH = 1024
KPE_COL = MLA_Q_RANK + MLA_KV_RANK
FLOGIT_COL = KPE_COL + LANES
ATTN_TILE = 512
PHASE_B_UNROLL = 14
LOG2E = math.log2(math.e)
MLA_Q_SCALE = LOG2E / math.sqrt(MLA_NOPE + MLA_ROPE)
FOX_Q_SCALE = LOG2E / math.sqrt(FOX_DIM)
NEG_BIG = -1e30

_NT = (((1,), (1,)), ((), ()))


def _params(vmem_bytes):
    limit = min(int(vmem_bytes * 1.25) + (4 << 20), V7X_VMEM_BYTES - (8 << 20))
    return pltpu.CompilerParams(vmem_limit_bytes=limit)


def _rms(x, g):
    return x * lax.rsqrt(jnp.mean(x * x, axis=-1, keepdims=True) + EPS) * g


STAT_ROWS = 8
APPLY_ROWS = 16


def _rms_stats(src_ref, stat_ref):
    tm, d = src_ref.shape

    def step(c, carry):
        rows = pl.ds(pl.multiple_of(c * STAT_ROWS, STAT_ROWS), STAT_ROWS)
        x = src_ref[rows, :]
        ms = jnp.sum(x * x, axis=-1, keepdims=True) * (1.0 / d)
        stat_ref[rows, :] = jnp.broadcast_to(lax.rsqrt(ms + EPS), (STAT_ROWS, LANES))
        return carry

    lax.fori_loop(0, tm // STAT_ROWS, step, 0, unroll=32)


def _for_row_blocks(tm, fn):
    def step(c, carry):
        fn(pl.ds(pl.multiple_of(c * APPLY_ROWS, APPLY_ROWS), APPLY_ROWS))
        return carry

    lax.fori_loop(0, tm // APPLY_ROWS, step, 0, unroll=4)


def _modulated_norm(x_ref, stat_ref, coef_ref, gpre_ref, sc_ref, sh_ref, h_ref):
    tm, d = x_ref.shape
    coef_ref[0:1, :] = gpre_ref[...] * (1.0 + sc_ref[0])
    coef_ref[1:2, :] = sh_ref[0]
    _rms_stats(x_ref, stat_ref)

    def apply(rows):
        r = stat_ref[rows, :]
        for k in range(d // LANES):
            cols = slice(k * LANES, (k + 1) * LANES)
            h = x_ref[rows, cols] * r * coef_ref[0:1, cols] + coef_ref[1:2, cols]
            h_ref[rows, cols] = h.astype(BF16)

    _for_row_blocks(tm, apply)


def _gated_norm_residual(x_ref, o_ref, stat_ref, coef_ref, gpost_ref, gt_ref, res_weight):
    tm, d = o_ref.shape
    coef_ref[0:1, :] = res_weight * gt_ref[0] * gpost_ref[...]
    _rms_stats(o_ref, stat_ref)

    def apply(rows):
        r = stat_ref[rows, :]
        for k in range(d // LANES):
            cols = slice(k * LANES, (k + 1) * LANES)
            o_ref[rows, cols] = x_ref[rows, cols] + o_ref[rows, cols] * r * coef_ref[0:1, cols]

    _for_row_blocks(tm, apply)


def _ada_kernel(c_ref, w_ref, b_ref, o_ref):
    c = c_ref[...]
    cond = (c * jax.nn.sigmoid(c)).astype(BF16)
    o_ref[...] = jnp.dot(cond, w_ref[...].astype(BF16), preferred_element_type=F32) + b_ref[...]


def _ada(c_pad, w, b, *, tn, n):
    m, d = c_pad.shape
    vmem = 2 * d * tn * 4 + d * tn * 2 + 4 * m * tn * 4
    return pl.pallas_call(
        _ada_kernel,
        grid=(n // tn,),
        in_specs=[pl.BlockSpec((m, d), lambda j: (0, 0)),
                  pl.BlockSpec((d, tn), lambda j: (0, j)),
                  pl.BlockSpec((1, tn), lambda j: (0, j))],
        out_specs=pl.BlockSpec((m, tn), lambda j: (0, j)),
        out_shape=jax.ShapeDtypeStruct((m, n), F32),
        compiler_params=_params(vmem),
        name="ada",
    )(c_pad, w, b)


def _ffn_kernel(*refs, res_weight, mode):
    x_ref, sh_ref, sc_ref, gt_ref, gpre_ref, gpost_ref, wg_ref, wu_ref, wd_ref = refs[:9]
    if mode == "head":
        o_ref, wg_out, wu_out, wd_out, h_ref, stat_ref, coef_ref = refs[9:]
    else:
        o_ref, h_ref, stat_ref, coef_ref = refs[9:]
    f = pl.program_id(1)
    active = pl.program_id(0) > 0 if mode == "tail" else True

    def swiglu_down(h):
        wg, wu, wd = wg_ref[...], wu_ref[...], wd_ref[...]
        if mode == "head":
            wg, wu, wd = wg.astype(BF16), wu.astype(BF16), wd.astype(BF16)
            wg_out[...], wu_out[...], wd_out[...] = wg, wu, wd
        g = jnp.dot(h, wg, preferred_element_type=F32)
        u = jnp.dot(h, wu, preferred_element_type=F32)
        a = (g * jax.nn.sigmoid(g) * u).astype(BF16)
        return jnp.dot(a, wd, preferred_element_type=F32)

    @pl.when(jnp.logical_and(active, f == 0))
    def _():
        h = (_rms(x_ref[...], gpre_ref[...]) * (1.0 + sc_ref[0]) + sh_ref[0]).astype(BF16)
        h_ref[...] = h
        o_ref[...] = swiglu_down(h)

    @pl.when(jnp.logical_and(active, f > 0))
    def _():
        o_ref[...] += swiglu_down(h_ref[...])

    @pl.when(jnp.logical_and(active, f == pl.num_programs(1) - 1))
    def _():
        _gated_norm_residual(x_ref, o_ref, stat_ref, coef_ref, gpost_ref, gt_ref, res_weight)

    if mode == "tail":
        @pl.when(jnp.logical_and(jnp.logical_not(active), f == 0))
        def _():
            o_ref[...] = jnp.zeros_like(o_ref)


def _ffn(x, sh, sc, gt, g_pre, g_post, wg, wu, wd, *, seq, tm, tf, res_weight, mode="all"):
    t, d = x.shape
    ff = wg.shape[1]
    tpb = seq // tm
    head = mode == "head"
    n_tiles = 1 if head else t // tm
    w_bytes = 4 if head else 2
    x_bufs = 1 if head else 2
    vmem = (2 * x_bufs * tm * d * 4 + tm * d * 2 + 6 * d * tf * w_bytes + 3 * tm * tf * 4
            + (9 * d * tf * 2 if head else 0))
    mod_spec = pl.BlockSpec((1, 1, d), lambda i, f: (i // tpb, 0, 0))
    gain_spec = pl.BlockSpec((1, d), lambda i, f: (0, 0))
    if mode == "tail":
        f_of = lambda i, f: jnp.where(i == 0, 0, f)
        x_spec = pl.BlockSpec((tm, d), lambda i, f: (jnp.maximum(i, 1), 0))
    else:
        f_of = lambda i, f: f
        x_spec = pl.BlockSpec((tm, d), lambda i, f: (i, 0), **(dict(pipeline_mode=pl.Buffered(1)) if head else {}))
    up_spec = pl.BlockSpec((d, tf), lambda i, f: (0, f_of(i, f)))
    down_spec = pl.BlockSpec((tf, d), lambda i, f: (f_of(i, f), 0))
    tile_kw = dict(pipeline_mode=pl.Buffered(1)) if head else {}
    out_specs = pl.BlockSpec((tm, d), lambda i, f: (i, 0), **tile_kw)
    out_shape = jax.ShapeDtypeStruct((n_tiles * tm, d), F32)
    if head:
        out_specs = [out_specs, up_spec, up_spec, down_spec]
        out_shape = [out_shape, jax.ShapeDtypeStruct(wg.shape, BF16), jax.ShapeDtypeStruct(wu.shape, BF16),
                     jax.ShapeDtypeStruct(wd.shape, BF16)]
    return pl.pallas_call(
        functools.partial(_ffn_kernel, res_weight=res_weight, mode=mode),
        grid=(n_tiles, ff // tf),
        in_specs=[x_spec, mod_spec, mod_spec, mod_spec, gain_spec, gain_spec, up_spec, up_spec, down_spec],
        out_specs=out_specs,
        out_shape=out_shape,
        scratch_shapes=[pltpu.VMEM((tm, d), BF16), pltpu.VMEM((tm, LANES), F32), pltpu.VMEM((8, d), F32)],
        compiler_params=_params(vmem),
        name="ffn_" + mode,
    )(x, sh, sc, gt, g_pre, g_post, wg, wu, wd)


def _rot(r, cos_t, sin_lo, sin_hi):
    return r * cos_t + pltpu.roll(r, 96, 1) * sin_lo + pltpu.roll(r, 32, 1) * sin_hi


def _mla_fox_prep(lat, gq_ref, gkv_ref, wuq_ref, wuk_ref, wuvt_ref, bf_ref, cos_ref, slo_ref, shi_ref,
                  q_ref, k_ref, vt_ref, ck_ref, cq_ref, carry_ref):
    tm = lat.shape[0]
    cos_t, sin_lo, sin_hi = cos_ref[...], slo_ref[...], shi_ref[...]

    qn = _rms(lat[:, :MLA_Q_RANK], gq_ref[...]).astype(BF16)
    q = jnp.dot(qn, wuq_ref[...], preferred_element_type=F32) * MLA_Q_SCALE
    kvn = _rms(lat[:, MLA_Q_RANK:KPE_COL], gkv_ref[...]).astype(BF16)
    k_nope = jnp.dot(kvn, wuk_ref[...], preferred_element_type=F32)
    vt_ref[0, 0] = lax.dot_general(wuvt_ref[...], kvn, _NT, preferred_element_type=F32).astype(BF16)
    k_rope = _rot(lat[:, KPE_COL:KPE_COL + LANES], cos_t, sin_lo, sin_hi).astype(BF16)
    for h in range(MLA_HEADS):
        c0 = h * MLA_QK_PAD
        q_ref[:, c0:c0 + MLA_NOPE] = q[:, c0:c0 + MLA_NOPE].astype(BF16)
        q_ref[:, c0 + MLA_NOPE:c0 + MLA_QK_PAD] = _rot(
            q[:, c0 + MLA_NOPE:c0 + MLA_QK_PAD], cos_t, sin_lo, sin_hi).astype(BF16)
        k_ref[:, c0:c0 + MLA_NOPE] = k_nope[:, h * MLA_NOPE:(h + 1) * MLA_NOPE].astype(BF16)
        k_ref[:, c0 + MLA_NOPE:c0 + MLA_QK_PAD] = k_rope

    z = lat[:, FLOGIT_COL:FLOGIT_COL + LANES] + bf_ref[...]
    lane = lax.broadcasted_iota(jnp.int32, (tm, LANES), 1)
    log_f = jnp.where(lane < FOX_HEADS, jnp.minimum(z, 0.0) - jnp.log1p(jnp.exp(-jnp.abs(z))), 0.0)
    hi = log_f.astype(BF16).astype(F32)
    mid = (log_f - hi).astype(BF16).astype(F32)
    lo = (log_f - hi - mid).astype(BF16).astype(F32)
    pieces = hi + pltpu.roll(mid, FOX_HEADS, 1) + pltpu.roll(lo, 2 * FOX_HEADS, 1)
    row = lax.broadcasted_iota(jnp.int32, (tm, tm), 0)
    col = lax.broadcasted_iota(jnp.int32, (tm, tm), 1)
    tri = (col <= row).astype(BF16)
    part = jnp.dot(tri, pieces.astype(BF16), preferred_element_type=F32)
    part = part + pltpu.roll(part, LANES - FOX_HEADS, 1) + pltpu.roll(part, LANES - 2 * FOX_HEADS, 1)
    cum = jnp.where(lane < FOX_HEADS, part, 0.0) + carry_ref[...]
    carry_ref[...] = cum[tm - 1:tm, :]
    cum2 = cum * LOG2E
    cq_ref[0, 0] = cum2.T[:FOX_HEADS, :]
    for h in range(FOX_HEADS):
        ck_ref[0, h] = jnp.broadcast_to(cum2[:, h:h + 1], (tm, LANES))


def _mixer_in_kernel(x_ref, sh_ref, sc_ref, gpre_ref, w_ref,
                     gq_ref, gkv_ref, wuq_ref, wuk_ref, wuvt_ref, bf_ref, cos_ref, slo_ref, shi_ref,
                     fqk_ref, fvt_ref, q_ref, k_ref, vt_ref, ck_ref, cq_ref, carry_ref, *, tiles_per_batch):
    @pl.when(pl.program_id(0) % tiles_per_batch == 0)
    def _():
        carry_ref[...] = jnp.zeros_like(carry_ref)

    tn = LATENT_WIDTH
    h = (_rms(x_ref[...], gpre_ref[...]) * (1.0 + sc_ref[0]) + sh_ref[0]).astype(BF16)
    lat = jnp.dot(h, w_ref[:, :tn], preferred_element_type=F32)
    fq = jnp.dot(h, w_ref[:, tn:2 * tn], preferred_element_type=F32) * FOX_Q_SCALE
    fqk_ref[:, :tn] = fq.astype(BF16)
    fqk_ref[:, tn:] = jnp.dot(h, w_ref[:, 2 * tn:3 * tn], preferred_element_type=F32).astype(BF16)
    fvt_ref[0, 0] = jnp.dot(h, w_ref[:, 3 * tn:], preferred_element_type=F32).T.astype(BF16)
    _mla_fox_prep(lat, gq_ref, gkv_ref, wuq_ref, wuk_ref, wuvt_ref, bf_ref, cos_ref, slo_ref, shi_ref,
                  q_ref, k_ref, vt_ref, ck_ref, cq_ref, carry_ref)


def _mixer_in(x, sh, sc, g_pre, w, g_q, g_kv, wuq, wuk, wuvt, b_forget, cos_t, sin_lo, sin_hi,
              *, batch, seq):
    t, d = x.shape
    tm = ATTN_TILE
    tn = LATENT_WIDTH
    tpb = seq // tm
    qk_w = MLA_HEADS * MLA_QK_PAD
    v_w = MLA_HEADS * MLA_V
    assert w.shape == (d, 3 * tn + FOX_WIDTH)
    resident_bytes = (w.size + wuq.size + wuk.size + wuvt.size) * 2
    vmem = (2 * tm * d * 4 + tm * d * 2 + resident_bytes + 2 * tm * (2 * tn + FOX_WIDTH) * 2
            + 2 * tm * (2 * qk_w + v_w) * 2 + 2 * FOX_HEADS * tm * LANES * 4 + 6 * tm * LANES * 4
            + tm * (tn + qk_w) * 4)
    mod_spec = pl.BlockSpec((1, 1, d), lambda i: (i // tpb, 0, 0))
    resident = lambda a: pl.BlockSpec(a.shape, lambda i: (0, 0), pipeline_mode=pl.Buffered(1))
    small = lambda a: pl.BlockSpec(a.shape, lambda i: (0, 0))
    tab_spec = pl.BlockSpec((tm, LANES), lambda i: (i % tpb, 0))
    return pl.pallas_call(
        functools.partial(_mixer_in_kernel, tiles_per_batch=tpb),
        grid=(t // tm,),
        in_specs=[pl.BlockSpec((tm, d), lambda i: (i, 0)),
                  mod_spec, mod_spec, small(g_pre), resident(w),
                  small(g_q), small(g_kv), resident(wuq), resident(wuk), resident(wuvt), small(b_forget),
                  tab_spec, tab_spec, tab_spec],
        out_specs=[pl.BlockSpec((tm, 2 * tn), lambda i: (i, 0)),
                   pl.BlockSpec((1, 1, FOX_WIDTH, tm), lambda i: (i // tpb, i % tpb, 0, 0)),
                   pl.BlockSpec((tm, qk_w), lambda i: (i, 0)),
                   pl.BlockSpec((tm, qk_w), lambda i: (i, 0)),
                   pl.BlockSpec((1, 1, v_w, tm), lambda i: (i // tpb, i % tpb, 0, 0)),
                   pl.BlockSpec((1, FOX_HEADS, tm, LANES), lambda i: (i // tpb, 0, i % tpb, 0)),
                   pl.BlockSpec((1, 1, FOX_HEADS, tm), lambda i: (i // tpb, i % tpb, 0, 0))],
        out_shape=[jax.ShapeDtypeStruct((t, 2 * tn), BF16),
                   jax.ShapeDtypeStruct((batch, tpb, FOX_WIDTH, tm), BF16),
                   jax.ShapeDtypeStruct((t, qk_w), BF16),
                   jax.ShapeDtypeStruct((t, qk_w), BF16),
                   jax.ShapeDtypeStruct((batch, tpb, v_w, tm), BF16),
                   jax.ShapeDtypeStruct((batch, FOX_HEADS, seq, LANES), F32),
                   jax.ShapeDtypeStruct((batch, tpb, FOX_HEADS, tm), F32)],
        scratch_shapes=[pltpu.VMEM((1, LANES), F32)],
        compiler_params=_params(vmem),
        name="mixer_in",
    )(x, sh, sc, g_pre, w, g_q, g_kv, wuq, wuk, wuvt, b_forget, cos_t, sin_lo, sin_hi)


def _attn_kernel(*refs, mask_chunk, fox, n_cast, mod_rider):
    n_in = 5 if fox else 3
    n_mod = 3 if mod_rider else 0
    n_side_in = n_cast + n_mod
    side_in = refs[n_in:n_in + n_side_in]
    side_out = refs[n_in + n_side_in + 1:n_in + n_side_in + 1 + n_cast + (1 if mod_rider else 0)]
    refs = refs[:n_in] + (refs[n_in + n_side_in],) + refs[n_in + n_side_in + 1 + len(side_out):]
    if fox:
        q_ref, k_ref, vt_ref, ck_ref, cq_ref, o_ref, s0_ref, s1_ref, m_ref, l_ref, acc_ref = refs
    else:
        q_ref, k_ref, vt_ref, o_ref, s0_ref, s1_ref, m_ref, l_ref, acc_ref = refs
    for src, dst in zip(side_in[:n_cast], side_out[:n_cast]):
        dst[...] = src[...].astype(BF16)
    if mod_rider:
        c_ref, wada_ref, bada_ref = side_in[n_cast:]
        _ada_kernel(c_ref, wada_ref, bada_ref, side_out[n_cast])
    head = pl.program_id(1)
    tq = ATTN_TILE
    nq = q_ref.shape[0] // tq
    n_low = nq * (nq - 1) // 2
    assert PHASE_B_UNROLL % 2 == 0 and n_low % PHASE_B_UNROLL == 0
    bufs = (s0_ref, s1_ref)

    def rows(t):
        return pl.ds(pl.multiple_of(t * tq, tq), tq)

    def scores(i, j, s_ref):
        s = lax.dot_general(k_ref[rows(j), :], q_ref[rows(i), :], _NT, preferred_element_type=F32)
        if fox:
            cum_q = cq_ref[0, i, pl.ds(head, 1), :]
            cum_k = ck_ref[0, 0, rows(j), :]
            s = s + (cum_q - jnp.concatenate([cum_k] * (tq // LANES), axis=1))
        s_ref[...] = s

    half = tq // 2

    def scores_diagonal(i, s_ref):
        lo, hi = i * tq, i * tq + half
        s_a = lax.dot_general(k_ref[lo:hi, :], q_ref[lo:lo + tq, :], _NT, preferred_element_type=F32)
        s_b = lax.dot_general(k_ref[hi:hi + half, :], q_ref[hi:hi + half, :], _NT,
                              preferred_element_type=F32)
        if fox:
            cum_q = cq_ref[0, i, pl.ds(head, 1), :]
            s_a = s_a + (cum_q - jnp.concatenate([ck_ref[0, 0, lo:hi, :]] * (tq // LANES), axis=1))
            s_b = s_b + (cum_q[:, half:]
                         - jnp.concatenate([ck_ref[0, 0, hi:hi + half, :]] * (half // LANES), axis=1))
        src = lax.broadcasted_iota(jnp.int32, (half, half), 0) // mask_chunk
        dst = lax.broadcasted_iota(jnp.int32, (half, half), 1) // mask_chunk
        s_ref[:half, :half] = jnp.where(src <= dst, s_a[:, :half], NEG_BIG)
        s_ref[:half, half:] = s_a[:, half:]
        s_ref[half:, half:] = jnp.where(src <= dst, s_b, NEG_BIG)

    def accumulate_diagonal(i, s_ref):
        s_a = s_ref[:half, :]
        s_b = s_ref[half:, half:]
        m_a = jnp.max(s_a, axis=0, keepdims=True)
        m_hi = jnp.maximum(m_a[:, half:], jnp.max(s_b, axis=0, keepdims=True))
        m = jnp.concatenate([m_a[:, :half], m_hi], axis=1)
        p_a = jnp.exp2(s_a - m)
        p_b = jnp.exp2(s_b - m_hi)
        l_a = jnp.sum(p_a, axis=0, keepdims=True)
        m_ref[i] = m
        l_ref[i] = jnp.concatenate([l_a[:, :half], l_a[:, half:] + jnp.sum(p_b, axis=0, keepdims=True)],
                                   axis=1)
        acc_a = jnp.dot(vt_ref[0, i, :, :half], p_a.astype(BF16), preferred_element_type=F32)
        acc_b = jnp.dot(vt_ref[0, i, :, half:], p_b.astype(BF16), preferred_element_type=F32)
        acc_ref[i, :, :half] = acc_a[:, :half]
        acc_ref[i, :, half:] = acc_a[:, half:] + acc_b

    def accumulate(i, j, s_ref):
        s = s_ref[...]
        m_prev = m_ref[i]
        m_new = jnp.maximum(m_prev, jnp.max(s, axis=0, keepdims=True))
        alpha = jnp.exp2(m_prev - m_new)
        p = jnp.exp2(s - m_new)
        l_ref[i] = alpha * l_ref[i] + jnp.sum(p, axis=0, keepdims=True)
        acc_ref[i] = alpha * acc_ref[i] + jnp.dot(vt_ref[0, j], p.astype(BF16),
                                                  preferred_element_type=F32)
        m_ref[i] = m_new

    scores_diagonal(0, bufs[0])
    for i in range(nq):
        if i + 1 < nq:
            scores_diagonal(i + 1, bufs[(i + 1) % 2])
        else:
            scores(1, 0, bufs[(i + 1) % 2])
        accumulate_diagonal(i, bufs[i % 2])

    def advance(i, j):
        wraps = j + 1 == i
        i_nxt = jnp.minimum(jnp.where(wraps, i + 1, i), nq - 1)
        return i_nxt, jnp.where(wraps, 0, j + 1)

    def body(_, carry):
        i, j = carry
        for u in range(PHASE_B_UNROLL):
            i_nxt, j_nxt = advance(i, j)
            scores(i_nxt, j_nxt, bufs[(nq + u + 1) % 2])
            accumulate(i, j, bufs[(nq + u) % 2])
            i, j = i_nxt, j_nxt
        return i, j

    lax.fori_loop(0, n_low // PHASE_B_UNROLL, body, (jnp.int32(1), jnp.int32(0)))

    for i in range(nq):
        o_ref[i * tq:(i + 1) * tq, :] = (acc_ref[i] / l_ref[i]).T.astype(o_ref.dtype)


def _attention(q_arr, k_arr, vt_arr, cum_k=None, cum_q=None, *, batch, seq, heads, dk, dv,
               q_col, k_col, mask_chunk, riders=(), mod_rider=None):
    fox = cum_k is not None
    tq = ATTN_TILE
    nq = seq // tq
    steps = batch * heads
    in_specs = [pl.BlockSpec((seq, dk), lambda b, h: (b, q_col + h)),
                pl.BlockSpec((seq, dk), lambda b, h: (b, k_col + h)),
                pl.BlockSpec((1, nq, dv, tq), lambda b, h: (b, 0, h, 0))]
    args = [q_arr, k_arr, vt_arr]
    if fox:
        in_specs += [pl.BlockSpec((1, 1, seq, LANES), lambda b, h: (b, h, 0, 0)),
                     pl.BlockSpec((1, nq, heads, tq), lambda b, h: (b, 0, 0, 0))]
        args += [cum_k, cum_q]
    slab_specs = [pl.BlockSpec((w.shape[0] // steps, w.shape[1]), lambda b, h: (b * heads + h, 0))
                  for w in riders]
    vmem = (2 * seq * (2 * dk + 2 * dv) * 2 + 2 * seq * LANES * 4 + 2 * seq * heads * 4
            + seq * dv * 4 + 8 * tq * tq * 4 + sum(2 * 6 * w.size // steps for w in riders))
    side_in_specs, side_out_specs, side_args = list(slab_specs), list(slab_specs), list(riders)
    side_out_shape = [jax.ShapeDtypeStruct(w.shape, BF16) for w in riders]
    if mod_rider is not None:
        c_pad, w_ada, b_ada, first_col = mod_rider
        m, d_model = c_pad.shape
        tn = (w_ada.shape[1] - first_col) // steps
        assert first_col % tn == 0
        col = lambda b, h: (0, first_col // tn + b * heads + h)
        side_in_specs += [pl.BlockSpec((m, d_model), lambda b, h: (0, 0)),
                          pl.BlockSpec((d_model, tn), col), pl.BlockSpec((1, tn), col)]
        side_out_specs += [pl.BlockSpec((m, tn), lambda b, h: (0, b * heads + h))]
        side_out_shape += [jax.ShapeDtypeStruct((m, tn * steps), F32)]
        side_args += [c_pad, w_ada, b_ada]
        vmem += 2 * d_model * tn * 4 + d_model * tn * 2
    return pl.pallas_call(
        functools.partial(_attn_kernel, mask_chunk=mask_chunk, fox=fox, n_cast=len(riders),
                          mod_rider=mod_rider is not None),
        grid=(batch, heads),
        in_specs=in_specs + side_in_specs,
        out_specs=[pl.BlockSpec((seq, dv), lambda b, h: (b, h))] + side_out_specs,
        out_shape=[jax.ShapeDtypeStruct((batch * seq, heads * dv), BF16)] + side_out_shape,
        scratch_shapes=[pltpu.VMEM((tq, tq), F32), pltpu.VMEM((tq, tq), F32),
                        pltpu.VMEM((nq, 1, tq), F32), pltpu.VMEM((nq, 1, tq), F32),
                        pltpu.VMEM((nq, dv, tq), F32)],
        compiler_params=_params(vmem),
        name="fox_attn" if fox else "mla_attn",
    )(*args, *side_args)


def _outproj_kernel(x_ref, oa_ref, ob_ref, wa_ref, wb_ref, gpost_ref, gt_ref, o_ref):
    y = jnp.dot(oa_ref[...], wa_ref[...], preferred_element_type=F32)
    y = y + jnp.dot(ob_ref[...], wb_ref[...], preferred_element_type=F32)
    o_ref[...] = x_ref[...] + gt_ref[0] * _rms(y, gpost_ref[...])


def _outproj(x, o_a, o_b, w_a, w_b, g_post, gt, *, seq, tm):
    t, d = x.shape
    ka, kb = o_a.shape[1], o_b.shape[1]
    tpb = seq // tm
    vmem = 4 * tm * d * 4 + 2 * tm * (ka + kb) * 2 + 2 * (ka + kb) * d * 2 + 2 * tm * d * 4
    return pl.pallas_call(
        _outproj_kernel,
        grid=(t // tm,),
        in_specs=[pl.BlockSpec((tm, d), lambda i: (i, 0)),
                  pl.BlockSpec((tm, ka), lambda i: (i, 0)),
                  pl.BlockSpec((tm, kb), lambda i: (i, 0)),
                  pl.BlockSpec((ka, d), lambda i: (0, 0)),
                  pl.BlockSpec((kb, d), lambda i: (0, 0)),
                  pl.BlockSpec((1, d), lambda i: (0, 0)),
                  pl.BlockSpec((1, 1, d), lambda i: (i // tpb, 0, 0))],
        out_specs=pl.BlockSpec((tm, d), lambda i: (i, 0)),
        out_shape=jax.ShapeDtypeStruct((t, d), F32),
        compiler_params=_params(vmem),
        name="out_proj",
    )(x, o_a, o_b, w_a, w_b, g_post, gt)


def _rope_tables(seq):
    half = MLA_ROPE // 2
    inv = ROPE_THETA ** (-np.arange(half, dtype=np.float64) / half)
    ang = np.arange(seq, dtype=np.float64)[:, None] * inv[None, :]
    cos, sin = np.cos(ang), np.sin(ang)
    zero = np.zeros_like(cos)
    pad = np.zeros((seq, LANES - 2 * half))
    return tuple(jnp.asarray(np.concatenate(parts, axis=1), F32)
                 for parts in ([cos, cos, pad], [-sin, zero, pad], [zero, sin, pad]))


def kernel(x, c, w_ada, b_ada, g_ffn1_pre, g_ffn1_post, w1_gate, w1_up, w1_down, g_mix_pre,
           g_mix_post, w_in, b_forget, g_q_a, w_uq, g_kv_a, w_ukv, w_o, g_ffn2_pre, g_ffn2_post,
           w2_gate, w2_up, w2_down):
    batch, seq, d = x.shape
    depth = w_ada.shape[0]
    t = batch * seq
    xt = x.reshape(t, d)
    c_pad = jnp.pad(c, ((0, 8 - batch), (0, 0)))
    cos_t, sin_lo, sin_hi = _rope_tables(seq)

    for l in range(depth):
        early = 5
        chunks = lambda mod: [mod[:batch, n * d:(n + 1) * d].reshape(batch, 1, d)
                              for n in range(mod.shape[1] // d)]
        sh1, sc1, gt1, sh2, sc2 = chunks(_ada(c_pad, w_ada[l], b_ada[l:l + 1], tn=1024, n=early * d))

        ffn1 = functools.partial(_ffn, xt, sh1, sc1, gt1, g_ffn1_pre[l:l + 1], g_ffn1_post[l:l + 1],
                                 seq=seq, tm=1024, res_weight=0.5)
        y_head, w1g, w1u, w1d = ffn1(w1_gate[l], w1_up[l], w1_down[l], tf=256, mode="head")
        xt = lax.dynamic_update_slice(ffn1(w1g, w1u, w1d, tf=512, mode="tail"), y_head, (0, 0))

        wi = w_in[l]
        fox0 = KPE_COL + MLA_ROPE
        fl0 = fox0 + 3 * FOX_WIDTH
        w_in_p = jnp.concatenate(
            [wi[:, :fox0], jnp.zeros((d, LANES - MLA_ROPE), F32),
             wi[:, fl0:], jnp.zeros((d, LANES - FOX_HEADS), F32),
             wi[:, fox0:fl0]], axis=1).astype(BF16)

        wuq_p = jnp.pad(w_uq[l].reshape(MLA_Q_RANK, MLA_HEADS, MLA_NOPE + MLA_ROPE),
                        ((0, 0), (0, 0), (0, MLA_QK_PAD - MLA_NOPE - MLA_ROPE))
                        ).reshape(MLA_Q_RANK, MLA_HEADS * MLA_QK_PAD).astype(BF16)
        wukv = w_ukv[l].reshape(MLA_KV_RANK, MLA_HEADS, MLA_NOPE + MLA_V)
        wuk = wukv[:, :, :MLA_NOPE].reshape(MLA_KV_RANK, -1).astype(BF16)
        wuvt = wukv[:, :, MLA_NOPE:].reshape(MLA_KV_RANK, -1).T.astype(BF16)
        bf_pad = jnp.pad(b_forget[l:l + 1], ((0, 0), (0, LANES - FOX_HEADS)))
        fqk, fvt, q_mla, k_mla, vt_mla, cum_k, cum_q = _mixer_in(
            xt, sh2, sc2, g_mix_pre[l:l + 1], w_in_p, g_q_a[l:l + 1], g_kv_a[l:l + 1],
            wuq_p, wuk, wuvt, bf_pad, cos_t, sin_lo, sin_hi, batch=batch, seq=seq)

        o_mla, w2g, w2u = _attention(q_mla, k_mla, vt_mla, batch=batch, seq=seq, heads=MLA_HEADS,
                                     dk=MLA_QK_PAD, dv=MLA_V, q_col=0, k_col=0, mask_chunk=CHUNK,
                                     riders=(w2_gate[l], w2_up[l]))
        o_fox, w2d, wo, mod_late = _attention(
            fqk, fqk, fvt, cum_k, cum_q, batch=batch, seq=seq, heads=FOX_HEADS, dk=FOX_DIM, dv=FOX_DIM,
            q_col=0, k_col=FOX_HEADS, mask_chunk=1, riders=(w2_down[l], w_o[l]),
            mod_rider=(c_pad, w_ada[l], b_ada[l:l + 1], early * d))
        gt2, sh3, sc3, gt3 = chunks(mod_late)

        xt = _outproj(xt, o_mla, o_fox, wo[:MLA_HEADS * MLA_V], wo[MLA_HEADS * MLA_V:],
                      g_mix_post[l:l + 1], gt2, seq=seq, tm=512)

        xt = _ffn(xt, sh3, sc3, gt3, g_ffn2_pre[l:l + 1], g_ffn2_post[l:l + 1],
                  w2g, w2u, w2d, seq=seq, tm=1024, tf=512, res_weight=0.5)

    return xt.reshape(batch, seq, d)
```

```python
import functools
import math

import jax
import jax.numpy as jnp
import numpy as np
from jax import lax
from jax.experimental import pallas as pl
from jax.experimental.pallas import tpu as pltpu

F32 = jnp.float32
BF16 = jnp.bfloat16

V7X_VMEM_BYTES = 64 * 1024 * 1024
LANES = 128

EPS = 1e-6
ROPE_THETA = 10000.0
CHUNK = 64
MLA_HEADS = 8
MLA_Q_RANK = 512
MLA_KV_RANK = 256
MLA_NOPE = 128
MLA_ROPE = 64
MLA_V = 128
MLA_QK_PAD = 256
FOX_HEADS = 8
FOX_DIM = 128
FOX_WIDTH = FOX_HEADS * FOX_DIM
LATENT_WIDTH = 1024
KPE_COL = MLA_Q_RANK + MLA_KV_RANK
FLOGIT_COL = KPE_COL + LANES
ATTN_TILE = 512
PHASE_B_UNROLL = 14
LOG2E = math.log2(math.e)
MLA_Q_SCALE = LOG2E / math.sqrt(MLA_NOPE + MLA_ROPE)
FOX_Q_SCALE = LOG2E / math.sqrt(FOX_DIM)
NEG_BIG = -1e30

_NT = (((1,), (1,)), ((), ()))


def _params(vmem_bytes):
    limit = min(int(vmem_bytes * 1.25) + (4 << 20), V7X_VMEM_BYTES - (8 << 20))
    return pltpu.CompilerParams(vmem_limit_bytes=limit)


def _rms(x, g):
    return x * lax.rsqrt(jnp.mean(x * x, axis=-1, keepdims=True) + EPS) * g


STAT_ROWS = 8
APPLY_ROWS = 16


def _rms_stats(src_ref, stat_ref):
    tm, d = src_ref.shape

    def step(c, carry):
        rows = pl.ds(pl.multiple_of(c * STAT_ROWS, STAT_ROWS), STAT_ROWS)
        x = src_ref[rows, :]
        ms = jnp.sum(x * x, axis=-1, keepdims=True) * (1.0 / d)
        stat_ref[rows, :] = jnp.broadcast_to(lax.rsqrt(ms + EPS), (STAT_ROWS, LANES))
        return carry

    lax.fori_loop(0, tm // STAT_ROWS, step, 0, unroll=32)


def _for_row_blocks(tm, fn):
    def step(c, carry):
        fn(pl.ds(pl.multiple_of(c * APPLY_ROWS, APPLY_ROWS), APPLY_ROWS))
        return carry

    lax.fori_loop(0, tm // APPLY_ROWS, step, 0, unroll=4)


def _modulated_norm(x_ref, stat_ref, coef_ref, gpre_ref, sc_ref, sh_ref, h_ref):
    tm, d = x_ref.shape
    coef_ref[0:1, :] = gpre_ref[...] * (1.0 + sc_ref[0])
    coef_ref[1:2, :] = sh_ref[0]
    _rms_stats(x_ref, stat_ref)

    def apply(rows):
        r = stat_ref[rows, :]
        for k in range(d // LANES):
            cols = slice(k * LANES, (k + 1) * LANES)
            h = x_ref[rows, cols] * r * coef_ref[0:1, cols] + coef_ref[1:2, cols]
            h_ref[rows, cols] = h.astype(BF16)

    _for_row_blocks(tm, apply)


def _gated_norm_residual(x_ref, o_ref, stat_ref, coef_ref, gpost_ref, gt_ref, res_weight):
    tm, d = o_ref.shape
    coef_ref[0:1, :] = res_weight * gt_ref[0] * gpost_ref[...]
    _rms_stats(o_ref, stat_ref)

    def apply(rows):
        r = stat_ref[rows, :]
        for k in range(d // LANES):
            cols = slice(k * LANES, (k + 1) * LANES)
            o_ref[rows, cols] = x_ref[rows, cols] + o_ref[rows, cols] * r * coef_ref[0:1, cols]

    _for_row_blocks(tm, apply)


def _ada_kernel(c_ref, w_ref, b_ref, o_ref):
    c = c_ref[...]
    cond = (c * jax.nn.sigmoid(c)).astype(BF16)
    o_ref[...] = jnp.dot(cond, w_ref[...].astype(BF16), preferred_element_type=F32) + b_ref[...]


def _ada(c_pad, w, b, *, tn, n):
    m, d = c_pad.shape
    vmem = 2 * d * tn * 4 + d * tn * 2 + 4 * m * tn * 4
    return pl.pallas_call(
        _ada_kernel,
        grid=(n // tn,),
        in_specs=[pl.BlockSpec((m, d), lambda j: (0, 0)),
                  pl.BlockSpec((d, tn), lambda j: (0, j)),
                  pl.BlockSpec((1, tn), lambda j: (0, j))],
        out_specs=pl.BlockSpec((m, tn), lambda j: (0, j)),
        out_shape=jax.ShapeDtypeStruct((m, n), F32),
        compiler_params=_params(vmem),
        name="ada",
    )(c_pad, w, b)


def _ffn_kernel(*refs, res_weight, mode):
    x_ref, sh_ref, sc_ref, gt_ref, gpre_ref, gpost_ref, wg_ref, wu_ref, wd_ref = refs[:9]
    if mode == "head":
        o_ref, wg_out, wu_out, wd_out, h_ref, stat_ref, coef_ref = refs[9:]
    else:
        o_ref, h_ref, stat_ref, coef_ref = refs[9:]
    f = pl.program_id(1)
    active = pl.program_id(0) > 0 if mode == "tail" else True

    def swiglu_down(h):
        wg, wu, wd = wg_ref[...], wu_ref[...], wd_ref[...]
        if mode == "head":
            wg, wu, wd = wg.astype(BF16), wu.astype(BF16), wd.astype(BF16)
            wg_out[...], wu_out[...], wd_out[...] = wg, wu, wd
        g = jnp.dot(h, wg, preferred_element_type=F32)
        u = jnp.dot(h, wu, preferred_element_type=F32)
        a = (g * jax.nn.sigmoid(g) * u).astype(BF16)
        return jnp.dot(a, wd, preferred_element_type=F32)

    @pl.when(jnp.logical_and(active, f == 0))
    def _():
        h = (_rms(x_ref[...], gpre_ref[...]) * (1.0 + sc_ref[0]) + sh_ref[0]).astype(BF16)
        h_ref[...] = h
        o_ref[...] = swiglu_down(h)

    @pl.when(jnp.logical_and(active, f > 0))
    def _():
        o_ref[...] += swiglu_down(h_ref[...])

    @pl.when(jnp.logical_and(active, f == pl.num_programs(1) - 1))
    def _():
        _gated_norm_residual(x_ref, o_ref, stat_ref, coef_ref, gpost_ref, gt_ref, res_weight)

    if mode == "tail":
        @pl.when(jnp.logical_and(jnp.logical_not(active), f == 0))
        def _():
            o_ref[...] = jnp.zeros_like(o_ref)


def _ffn(x, sh, sc, gt, g_pre, g_post, wg, wu, wd, *, seq, tm, tf, res_weight, mode="all"):
    t, d = x.shape
    ff = wg.shape[1]
    tpb = seq // tm
    head = mode == "head"
    n_tiles = 1 if head else t // tm
    w_bytes = 4 if head else 2
    x_bufs = 1 if head else 2
    vmem = (2 * x_bufs * tm * d * 4 + tm * d * 2 + 6 * d * tf * w_bytes + 3 * tm * tf * 4
            + (9 * d * tf * 2 if head else 0))
    mod_spec = pl.BlockSpec((1, 1, d), lambda i, f: (i // tpb, 0, 0))
    gain_spec = pl.BlockSpec((1, d), lambda i, f: (0, 0))
    if mode == "tail":
        f_of = lambda i, f: jnp.where(i == 0, 0, f)
        x_spec = pl.BlockSpec((tm, d), lambda i, f: (jnp.maximum(i, 1), 0))
    else:
        f_of = lambda i, f: f
        x_spec = pl.BlockSpec((tm, d), lambda i, f: (i, 0), **(dict(pipeline_mode=pl.Buffered(1)) if head else {}))
    up_spec = pl.BlockSpec((d, tf), lambda i, f: (0, f_of(i, f)))
    down_spec = pl.BlockSpec((tf, d), lambda i, f: (f_of(i, f), 0))
    tile_kw = dict(pipeline_mode=pl.Buffered(1)) if head else {}
    out_specs = pl.BlockSpec((tm, d), lambda i, f: (i, 0), **tile_kw)
    out_shape = jax.ShapeDtypeStruct((n_tiles * tm, d), F32)
    if head:
        out_specs = [out_specs, up_spec, up_spec, down_spec]
        out_shape = [out_shape, jax.ShapeDtypeStruct(wg.shape, BF16), jax.ShapeDtypeStruct(wu.shape, BF16),
                     jax.ShapeDtypeStruct(wd.shape, BF16)]
    return pl.pallas_call(
        functools.partial(_ffn_kernel, res_weight=res_weight, mode=mode),
        grid=(n_tiles, ff // tf),
        in_specs=[x_spec, mod_spec, mod_spec, mod_spec, gain_spec, gain_spec, up_spec, up_spec, down_spec],
        out_specs=out_specs,
        out_shape=out_shape,
        scratch_shapes=[pltpu.VMEM((tm, d), BF16), pltpu.VMEM((tm, LANES), F32), pltpu.VMEM((8, d), F32)],
        compiler_params=_params(vmem),
        name="ffn_" + mode,
    )(x, sh, sc, gt, g_pre, g_post, wg, wu, wd)


def _rot(r, cos_t, sin_lo, sin_hi):
    return r * cos_t + pltpu.roll(r, 96, 1) * sin_lo + pltpu.roll(r, 32, 1) * sin_hi


def _mla_fox_prep(lat, gq_ref, gkv_ref, wuq_ref, wuk_ref, wuvt_ref, bf_ref, cos_ref, slo_ref, shi_ref,
                  q_ref, k_ref, vt_ref, ck_ref, cq_ref, carry_ref):
    tm = lat.shape[0]
    cos_t, sin_lo, sin_hi = cos_ref[...], slo_ref[...], shi_ref[...]

    qn = _rms(lat[:, :MLA_Q_RANK], gq_ref[...]).astype(BF16)
    q = jnp.dot(qn, wuq_ref[...], preferred_element_type=F32) * MLA_Q_SCALE
    kvn = _rms(lat[:, MLA_Q_RANK:KPE_COL], gkv_ref[...]).astype(BF16)
    k_nope = jnp.dot(kvn, wuk_ref[...], preferred_element_type=F32)
    vt_ref[0, 0] = lax.dot_general(wuvt_ref[...], kvn, _NT, preferred_element_type=F32).astype(BF16)
    k_rope = _rot(lat[:, KPE_COL:KPE_COL + LANES], cos_t, sin_lo, sin_hi).astype(BF16)
    for h in range(MLA_HEADS):
        c0 = h * MLA_QK_PAD
        q_ref[:, c0:c0 + MLA_NOPE] = q[:, c0:c0 + MLA_NOPE].astype(BF16)
        q_ref[:, c0 + MLA_NOPE:c0 + MLA_QK_PAD] = _rot(
            q[:, c0 + MLA_NOPE:c0 + MLA_QK_PAD], cos_t, sin_lo, sin_hi).astype(BF16)
        k_ref[:, c0:c0 + MLA_NOPE] = k_nope[:, h * MLA_NOPE:(h + 1) * MLA_NOPE].astype(BF16)
        k_ref[:, c0 + MLA_NOPE:c0 + MLA_QK_PAD] = k_rope

    z = lat[:, FLOGIT_COL:FLOGIT_COL + LANES] + bf_ref[...]
    lane = lax.broadcasted_iota(jnp.int32, (tm, LANES), 1)
    log_f = jnp.where(lane < FOX_HEADS, jnp.minimum(z, 0.0) - jnp.log1p(jnp.exp(-jnp.abs(z))), 0.0)
    hi = log_f.astype(BF16).astype(F32)
    mid = (log_f - hi).astype(BF16).astype(F32)
    lo = (log_f - hi - mid).astype(BF16).astype(F32)
    pieces = hi + pltpu.roll(mid, FOX_HEADS, 1) + pltpu.roll(lo, 2 * FOX_HEADS, 1)
    row = lax.broadcasted_iota(jnp.int32, (tm, tm), 0)
    col = lax.broadcasted_iota(jnp.int32, (tm, tm), 1)
    tri = (col <= row).astype(BF16)
    part = jnp.dot(tri, pieces.astype(BF16), preferred_element_type=F32)
    part = part + pltpu.roll(part, LANES - FOX_HEADS, 1) + pltpu.roll(part, LANES - 2 * FOX_HEADS, 1)
    cum = jnp.where(lane < FOX_HEADS, part, 0.0) + carry_ref[...]
    carry_ref[...] = cum[tm - 1:tm, :]
    cum2 = cum * LOG2E
    cq_ref[0, 0] = cum2.T[:FOX_HEADS, :]
    for h in range(FOX_HEADS):
        ck_ref[0, h] = jnp.broadcast_to(cum2[:, h:h + 1], (tm, LANES))


def _mixer_in_kernel(x_ref, sh_ref, sc_ref, gpre_ref, wlat_ref, wfox_ref,
                     gq_ref, gkv_ref, wuq_ref, wuk_ref, wuvt_ref, bf_ref, cos_ref, slo_ref, shi_ref,
                     fqk_ref, fvt_ref, q_ref, k_ref, vt_ref, ck_ref, cq_ref, carry_ref, *, tiles_per_batch):
    @pl.when(pl.program_id(0) % tiles_per_batch == 0)
    def _():
        carry_ref[...] = jnp.zeros_like(carry_ref)

    tn = LATENT_WIDTH
    h = (_rms(x_ref[...], gpre_ref[...]) * (1.0 + sc_ref[0]) + sh_ref[0]).astype(BF16)
    fw = FOX_WIDTH
    lat = jnp.dot(h, wlat_ref[...], preferred_element_type=F32)
    fq = jnp.dot(h, wfox_ref[:, :fw], preferred_element_type=F32) * FOX_Q_SCALE
    fqk_ref[:, :fw] = fq.astype(BF16)
    fqk_ref[:, fw:] = jnp.dot(h, wfox_ref[:, fw:2 * fw], preferred_element_type=F32).astype(BF16)
    fvt_ref[0, 0] = jnp.dot(h, wfox_ref[:, 2 * fw:], preferred_element_type=F32).T.astype(BF16)
    _mla_fox_prep(lat, gq_ref, gkv_ref, wuq_ref, wuk_ref, wuvt_ref, bf_ref, cos_ref, slo_ref, shi_ref,
                  q_ref, k_ref, vt_ref, ck_ref, cq_ref, carry_ref)


def _mixer_in(x, sh, sc, g_pre, w_lat, w_fox, g_q, g_kv, wuq, wuk, wuvt, b_forget, cos_t, sin_lo, sin_hi,
              *, batch, seq):
    t, d = x.shape
    tm = ATTN_TILE
    tn = LATENT_WIDTH
    tpb = seq // tm
    qk_w = MLA_HEADS * MLA_QK_PAD
    v_w = MLA_HEADS * MLA_V
    assert tn == FOX_WIDTH and w_lat.shape == (d, tn) and w_fox.shape == (d, 3 * FOX_WIDTH)
    resident_bytes = (w_lat.size + w_fox.size + wuq.size + wuk.size + wuvt.size) * 2
    vmem = (2 * tm * d * 4 + tm * d * 2 + resident_bytes + 2 * tm * (2 * tn + FOX_WIDTH) * 2
            + 2 * tm * (2 * qk_w + v_w) * 2 + 2 * FOX_HEADS * tm * LANES * 4 + 6 * tm * LANES * 4
            + tm * (tn + qk_w) * 4)
    mod_spec = pl.BlockSpec((1, 1, d), lambda i: (i // tpb, 0, 0))
    resident = lambda a: pl.BlockSpec(a.shape, lambda i: (0, 0), pipeline_mode=pl.Buffered(1))
    small = lambda a: pl.BlockSpec(a.shape, lambda i: (0, 0))
    tab_spec = pl.BlockSpec((tm, LANES), lambda i: (i % tpb, 0))
    return pl.pallas_call(
        functools.partial(_mixer_in_kernel, tiles_per_batch=tpb),
        grid=(t // tm,),
        in_specs=[pl.BlockSpec((tm, d), lambda i: (i, 0)),
                  mod_spec, mod_spec, small(g_pre), resident(w_lat), resident(w_fox),
                  small(g_q), small(g_kv), resident(wuq), resident(wuk), resident(wuvt), small(b_forget),
                  tab_spec, tab_spec, tab_spec],
        out_specs=[pl.BlockSpec((tm, 2 * tn), lambda i: (i, 0)),
                   pl.BlockSpec((1, 1, FOX_WIDTH, tm), lambda i: (i // tpb, i % tpb, 0, 0)),
                   pl.BlockSpec((tm, qk_w), lambda i: (i, 0)),
                   pl.BlockSpec((tm, qk_w), lambda i: (i, 0)),
                   pl.BlockSpec((1, 1, v_w, tm), lambda i: (i // tpb, i % tpb, 0, 0)),
                   pl.BlockSpec((1, FOX_HEADS, tm, LANES), lambda i: (i // tpb, 0, i % tpb, 0)),
                   pl.BlockSpec((1, 1, FOX_HEADS, tm), lambda i: (i // tpb, i % tpb, 0, 0))],
        out_shape=[jax.ShapeDtypeStruct((t, 2 * tn), BF16),
                   jax.ShapeDtypeStruct((batch, tpb, FOX_WIDTH, tm), BF16),
                   jax.ShapeDtypeStruct((t, qk_w), BF16),
                   jax.ShapeDtypeStruct((t, qk_w), BF16),
                   jax.ShapeDtypeStruct((batch, tpb, v_w, tm), BF16),
                   jax.ShapeDtypeStruct((batch, FOX_HEADS, seq, LANES), F32),
                   jax.ShapeDtypeStruct((batch, tpb, FOX_HEADS, tm), F32)],
        scratch_shapes=[pltpu.VMEM((1, LANES), F32)],
        compiler_params=_params(vmem),
        name="mixer_in",
    )(x, sh, sc, g_pre, w_lat, w_fox, g_q, g_kv, wuq, wuk, wuvt, b_forget, cos_t, sin_lo, sin_hi)


def _attn_kernel(*refs, mask_chunk, fox, n_cast, mod_rider):
    n_in = 5 if fox else 3
    n_mod = 3 if mod_rider else 0
    n_side_in = n_cast + n_mod
    side_in = refs[n_in:n_in + n_side_in]
    side_out = refs[n_in + n_side_in + 1:n_in + n_side_in + 1 + n_cast + (1 if mod_rider else 0)]
    refs = refs[:n_in] + (refs[n_in + n_side_in],) + refs[n_in + n_side_in + 1 + len(side_out):]
    if fox:
        q_ref, k_ref, vt_ref, ck_ref, cq_ref, o_ref, s0_ref, s1_ref, m_ref, l_ref, acc_ref = refs
    else:
        q_ref, k_ref, vt_ref, o_ref, s0_ref, s1_ref, m_ref, l_ref, acc_ref = refs
    for src, dst in zip(side_in[:n_cast], side_out[:n_cast]):
        dst[...] = src[...].astype(BF16)
    if mod_rider:
        c_ref, wada_ref, bada_ref = side_in[n_cast:]
        _ada_kernel(c_ref, wada_ref, bada_ref, side_out[n_cast])
    head = pl.program_id(1)
    tq = ATTN_TILE
    nq = q_ref.shape[0] // tq
    n_low = nq * (nq - 1) // 2
    assert PHASE_B_UNROLL % 2 == 0 and n_low % PHASE_B_UNROLL == 0
    bufs = (s0_ref, s1_ref)

    def rows(t):
        return pl.ds(pl.multiple_of(t * tq, tq), tq)

    def scores(i, j, s_ref):
        s = lax.dot_general(k_ref[rows(j), :], q_ref[rows(i), :], _NT, preferred_element_type=F32)
        if fox:
            cum_q = cq_ref[0, i, pl.ds(head, 1), :]
            cum_k = ck_ref[0, 0, rows(j), :]
            s = s + (cum_q - jnp.concatenate([cum_k] * (tq // LANES), axis=1))
        s_ref[...] = s

    half = tq // 2

    def scores_diagonal(i, s_ref):
        lo, hi = i * tq, i * tq + half
        s_a = lax.dot_general(k_ref[lo:hi, :], q_ref[lo:lo + tq, :], _NT, preferred_element_type=F32)
        s_b = lax.dot_general(k_ref[hi:hi + half, :], q_ref[hi:hi + half, :], _NT,
                              preferred_element_type=F32)
        if fox:
            cum_q = cq_ref[0, i, pl.ds(head, 1), :]
            s_a = s_a + (cum_q - jnp.concatenate([ck_ref[0, 0, lo:hi, :]] * (tq // LANES), axis=1))
            s_b = s_b + (cum_q[:, half:]
                         - jnp.concatenate([ck_ref[0, 0, hi:hi + half, :]] * (half // LANES), axis=1))
        src = lax.broadcasted_iota(jnp.int32, (half, half), 0) // mask_chunk
        dst = lax.broadcasted_iota(jnp.int32, (half, half), 1) // mask_chunk
        s_ref[:half, :half] = jnp.where(src <= dst, s_a[:, :half], NEG_BIG)
        s_ref[:half, half:] = s_a[:, half:]
        s_ref[half:, half:] = jnp.where(src <= dst, s_b, NEG_BIG)

    def accumulate_diagonal(i, s_ref):
        s_a = s_ref[:half, :]
        s_b = s_ref[half:, half:]
        m_a = jnp.max(s_a, axis=0, keepdims=True)
        m_hi = jnp.maximum(m_a[:, half:], jnp.max(s_b, axis=0, keepdims=True))
        m = jnp.concatenate([m_a[:, :half], m_hi], axis=1)
        p_a = jnp.exp2(s_a - m)
        p_b = jnp.exp2(s_b - m_hi)
        l_a = jnp.sum(p_a, axis=0, keepdims=True)
        m_ref[i] = m
        l_ref[i] = jnp.concatenate([l_a[:, :half], l_a[:, half:] + jnp.sum(p_b, axis=0, keepdims=True)],
                                   axis=1)
        acc_a = jnp.dot(vt_ref[0, i, :, :half], p_a.astype(BF16), preferred_element_type=F32)
        acc_b = jnp.dot(vt_ref[0, i, :, half:], p_b.astype(BF16), preferred_element_type=F32)
        acc_ref[i, :, :half] = acc_a[:, :half]
        acc_ref[i, :, half:] = acc_a[:, half:] + acc_b

    def accumulate(i, j, s_ref):
        s = s_ref[...]
        m_prev = m_ref[i]
        m_new = jnp.maximum(m_prev, jnp.max(s, axis=0, keepdims=True))
        alpha = jnp.exp2(m_prev - m_new)
        p = jnp.exp2(s - m_new)
        l_ref[i] = alpha * l_ref[i] + jnp.sum(p, axis=0, keepdims=True)
        acc_ref[i] = alpha * acc_ref[i] + jnp.dot(vt_ref[0, j], p.astype(BF16),
                                                  preferred_element_type=F32)
        m_ref[i] = m_new

    scores_diagonal(0, bufs[0])
    for i in range(nq):
        if i + 1 < nq:
            scores_diagonal(i + 1, bufs[(i + 1) % 2])
        else:
            scores(1, 0, bufs[(i + 1) % 2])
        accumulate_diagonal(i, bufs[i % 2])

    def advance(i, j):
        wraps = j + 1 == i
        i_nxt = jnp.minimum(jnp.where(wraps, i + 1, i), nq - 1)
        return i_nxt, jnp.where(wraps, 0, j + 1)

    def body(_, carry):
        i, j = carry
        for u in range(PHASE_B_UNROLL):
            i_nxt, j_nxt = advance(i, j)
            scores(i_nxt, j_nxt, bufs[(nq + u + 1) % 2])
            accumulate(i, j, bufs[(nq + u) % 2])
            i, j = i_nxt, j_nxt
        return i, j

    lax.fori_loop(0, n_low // PHASE_B_UNROLL, body, (jnp.int32(1), jnp.int32(0)))

    for i in range(nq):
        o_ref[i * tq:(i + 1) * tq, :] = (acc_ref[i] / l_ref[i]).T.astype(o_ref.dtype)


def _attention(q_arr, k_arr, vt_arr, cum_k=None, cum_q=None, *, batch, seq, heads, dk, dv,
               q_col, k_col, mask_chunk, riders=(), mod_rider=None):
    fox = cum_k is not None
    tq = ATTN_TILE
    nq = seq // tq
    steps = batch * heads
    in_specs = [pl.BlockSpec((seq, dk), lambda b, h: (b, q_col + h)),
                pl.BlockSpec((seq, dk), lambda b, h: (b, k_col + h)),
                pl.BlockSpec((1, nq, dv, tq), lambda b, h: (b, 0, h, 0))]
    args = [q_arr, k_arr, vt_arr]
    if fox:
        in_specs += [pl.BlockSpec((1, 1, seq, LANES), lambda b, h: (b, h, 0, 0)),
                     pl.BlockSpec((1, nq, heads, tq), lambda b, h: (b, 0, 0, 0))]
        args += [cum_k, cum_q]
    slab_specs = [pl.BlockSpec((w.shape[0] // steps, w.shape[1]), lambda b, h: (b * heads + h, 0))
                  for w in riders]
    vmem = (2 * seq * (2 * dk + 2 * dv) * 2 + 2 * seq * LANES * 4 + 2 * seq * heads * 4
            + seq * dv * 4 + 8 * tq * tq * 4 + sum(2 * 6 * w.size // steps for w in riders))
    side_in_specs, side_out_specs, side_args = list(slab_specs), list(slab_specs), list(riders)
    side_out_shape = [jax.ShapeDtypeStruct(w.shape, BF16) for w in riders]
    if mod_rider is not None:
        c_pad, w_ada, b_ada, first_col = mod_rider
        m, d_model = c_pad.shape
        tn = (w_ada.shape[1] - first_col) // steps
        assert first_col % tn == 0
        col = lambda b, h: (0, first_col // tn + b * heads + h)
        side_in_specs += [pl.BlockSpec((m, d_model), lambda b, h: (0, 0)),
                          pl.BlockSpec((d_model, tn), col), pl.BlockSpec((1, tn), col)]
        side_out_specs += [pl.BlockSpec((m, tn), lambda b, h: (0, b * heads + h))]
        side_out_shape += [jax.ShapeDtypeStruct((m, tn * steps), F32)]
        side_args += [c_pad, w_ada, b_ada]
        vmem += 2 * d_model * tn * 4 + d_model * tn * 2
    return pl.pallas_call(
        functools.partial(_attn_kernel, mask_chunk=mask_chunk, fox=fox, n_cast=len(riders),
                          mod_rider=mod_rider is not None),
        grid=(batch, heads),
        in_specs=in_specs + side_in_specs,
        out_specs=[pl.BlockSpec((seq, dv), lambda b, h: (b, h))] + side_out_specs,
        out_shape=[jax.ShapeDtypeStruct((batch * seq, heads * dv), BF16)] + side_out_shape,
        scratch_shapes=[pltpu.VMEM((tq, tq), F32), pltpu.VMEM((tq, tq), F32),
                        pltpu.VMEM((nq, 1, tq), F32), pltpu.VMEM((nq, 1, tq), F32),
                        pltpu.VMEM((nq, dv, tq), F32)],
        compiler_params=_params(vmem),
        name="fox_attn" if fox else "mla_attn",
    )(*args, *side_args)


def _outproj_kernel(x_ref, oa_ref, ob_ref, wa_ref, wb_ref, gpost_ref, gt_ref, o_ref):
    y = jnp.dot(oa_ref[...], wa_ref[...], preferred_element_type=F32)
    y = y + jnp.dot(ob_ref[...], wb_ref[...], preferred_element_type=F32)
    o_ref[...] = x_ref[...] + gt_ref[0] * _rms(y, gpost_ref[...])


def _outproj(x, o_a, o_b, w_a, w_b, g_post, gt, *, seq, tm):
    t, d = x.shape
    ka, kb = o_a.shape[1], o_b.shape[1]
    tpb = seq // tm
    vmem = 4 * tm * d * 4 + 2 * tm * (ka + kb) * 2 + 2 * (ka + kb) * d * 2 + 2 * tm * d * 4
    return pl.pallas_call(
        _outproj_kernel,
        grid=(t // tm,),
        in_specs=[pl.BlockSpec((tm, d), lambda i: (i, 0)),
                  pl.BlockSpec((tm, ka), lambda i: (i, 0)),
                  pl.BlockSpec((tm, kb), lambda i: (i, 0)),
                  pl.BlockSpec((ka, d), lambda i: (0, 0)),
                  pl.BlockSpec((kb, d), lambda i: (0, 0)),
                  pl.BlockSpec((1, d), lambda i: (0, 0)),
                  pl.BlockSpec((1, 1, d), lambda i: (i // tpb, 0, 0))],
        out_specs=pl.BlockSpec((tm, d), lambda i: (i, 0)),
        out_shape=jax.ShapeDtypeStruct((t, d), F32),
        compiler_params=_params(vmem),
        name="out_proj",
    )(x, o_a, o_b, w_a, w_b, g_post, gt)


def _rope_tables(seq):
    half = MLA_ROPE // 2
    inv = ROPE_THETA ** (-np.arange(half, dtype=np.float64) / half)
    ang = np.arange(seq, dtype=np.float64)[:, None] * inv[None, :]
    cos, sin = np.cos(ang), np.sin(ang)
    zero = np.zeros_like(cos)
    pad = np.zeros((seq, LANES - 2 * half))
    return tuple(jnp.asarray(np.concatenate(parts, axis=1), F32)
                 for parts in ([cos, cos, pad], [-sin, zero, pad], [zero, sin, pad]))


def kernel(x, c, w_ada, b_ada, g_ffn1_pre, g_ffn1_post, w1_gate, w1_up, w1_down, g_mix_pre,
           g_mix_post, w_in, b_forget, g_q_a, w_uq, g_kv_a, w_ukv, w_o, g_ffn2_pre, g_ffn2_post,
           w2_gate, w2_up, w2_down):
    batch, seq, d = x.shape
    depth = w_ada.shape[0]
    t = batch * seq
    xt = x.reshape(t, d)
    c_pad = jnp.pad(c, ((0, 8 - batch), (0, 0)))
    cos_t, sin_lo, sin_hi = _rope_tables(seq)

    for l in range(depth):
        early = 5
        chunks = lambda mod: [mod[:batch, n * d:(n + 1) * d].reshape(batch, 1, d)
                              for n in range(mod.shape[1] // d)]
        sh1, sc1, gt1, sh2, sc2 = chunks(_ada(c_pad, w_ada[l], b_ada[l:l + 1], tn=1024, n=early * d))

        ffn1 = functools.partial(_ffn, xt, sh1, sc1, gt1, g_ffn1_pre[l:l + 1], g_ffn1_post[l:l + 1],
                                 seq=seq, tm=1024, res_weight=0.5)
        y_head, w1g, w1u, w1d = ffn1(w1_gate[l], w1_up[l], w1_down[l], tf=256, mode="head")
        xt = lax.dynamic_update_slice(ffn1(w1g, w1u, w1d, tf=512, mode="tail"), y_head, (0, 0))

        wi = w_in[l]
        fox0 = KPE_COL + MLA_ROPE
        fl0 = fox0 + 3 * FOX_WIDTH
        w_lat = jnp.concatenate(
            [wi[:, :fox0], jnp.zeros((d, LANES - MLA_ROPE), F32),
             wi[:, fl0:], jnp.zeros((d, LANES - FOX_HEADS), F32)], axis=1).astype(BF16)
        w_fox = wi[:, fox0:fl0].astype(BF16)

        wuq_p = jnp.pad(w_uq[l].reshape(MLA_Q_RANK, MLA_HEADS, MLA_NOPE + MLA_ROPE),
                        ((0, 0), (0, 0), (0, MLA_QK_PAD - MLA_NOPE - MLA_ROPE))
                        ).reshape(MLA_Q_RANK, MLA_HEADS * MLA_QK_PAD).astype(BF16)
        wukv = w_ukv[l].reshape(MLA_KV_RANK, MLA_HEADS, MLA_NOPE + MLA_V)
        wuk = wukv[:, :, :MLA_NOPE].reshape(MLA_KV_RANK, -1).astype(BF16)
        wuvt = wukv[:, :, MLA_NOPE:].reshape(MLA_KV_RANK, -1).T.astype(BF16)
        bf_pad = jnp.pad(b_forget[l:l + 1], ((0, 0), (0, LANES - FOX_HEADS)))
        fqk, fvt, q_mla, k_mla, vt_mla, cum_k, cum_q = _mixer_in(
            xt, sh2, sc2, g_mix_pre[l:l + 1], w_lat, w_fox, g_q_a[l:l + 1], g_kv_a[l:l + 1],
            wuq_p, wuk, wuvt, bf_pad, cos_t, sin_lo, sin_hi, batch=batch, seq=seq)

        o_mla, w2g, w2u = _attention(q_mla, k_mla, vt_mla, batch=batch, seq=seq, heads=MLA_HEADS,
                                     dk=MLA_QK_PAD, dv=MLA_V, q_col=0, k_col=0, mask_chunk=CHUNK,
                                     riders=(w2_gate[l], w2_up[l]))
        o_fox, w2d, wo, mod_late = _attention(
            fqk, fqk, fvt, cum_k, cum_q, batch=batch, seq=seq, heads=FOX_HEADS, dk=FOX_DIM, dv=FOX_DIM,
            q_col=0, k_col=FOX_HEADS, mask_chunk=1, riders=(w2_down[l], w_o[l]),
            mod_rider=(c_pad, w_ada[l], b_ada[l:l + 1], early * d))
        gt2, sh3, sc3, gt3 = chunks(mod_late)

        xt = _outproj(xt, o_mla, o_fox, wo[:MLA_HEADS * MLA_V], wo[MLA_HEADS * MLA_V:],
                      g_mix_post[l:l + 1], gt2, seq=seq, tm=512)

        xt = _ffn(xt, sh3, sc3, gt3, g_ffn2_pre[l:l + 1], g_ffn2_post[l:l + 1],
                  w2g, w2u, w2d, seq=seq, tm=1024, tf=512, res_weight=0.5)

    return xt.reshape(batch, seq, d)
```

```python
import functools
import math

import jax
import jax.numpy as jnp
import numpy as np
from jax import lax
from jax.experimental import pallas as pl
from jax.experimental.pallas import tpu as pltpu

F32 = jnp.float32
BF16 = jnp.bfloat16

V7X_VMEM_BYTES = 64 * 1024 * 1024
LANES = 128

EPS = 1e-6
ROPE_THETA = 10000.0
CHUNK = 64
MLA_HEADS = 8
MLA_Q_RANK = 512
MLA_KV_RANK = 256
MLA_NOPE = 128
MLA_ROPE = 64
MLA_V = 128
MLA_QK_PAD = 256
FOX_HEADS = 8
FOX_DIM = 128
FOX_WIDTH = FOX_HEADS * FOX_DIM
LATENT_WIDTH = 1024
KPE_COL = MLA_Q_RANK + MLA_KV_RANK
FLOGIT_COL = KPE_COL + LANES
ATTN_TILE = 512
PHASE_B_UNROLL = 14
LOG2E = math.log2(math.e)
MLA_Q_SCALE = LOG2E / math.sqrt(MLA_NOPE + MLA_ROPE)
FOX_Q_SCALE = LOG2E / math.sqrt(FOX_DIM)
NEG_BIG = -1e30

_NT = (((1,), (1,)), ((), ()))


def _params(vmem_bytes):
    limit = min(int(vmem_bytes * 1.25) + (4 << 20), V7X_VMEM_BYTES - (8 << 20))
    return pltpu.CompilerParams(vmem_limit_bytes=limit)


def _rms(x, g):
    return x * lax.rsqrt(jnp.mean(x * x, axis=-1, keepdims=True) + EPS) * g


APPLY_ROWS = 16


def _for_row_blocks(tm, fn):
    def step(c, carry):
        fn(pl.ds(pl.multiple_of(c * APPLY_ROWS, APPLY_ROWS), APPLY_ROWS))
        return carry

    lax.fori_loop(0, tm // APPLY_ROWS, step, 0, unroll=4)


def _gated_norm_residual(x_ref, o_ref, stat_ref, coef_ref, gpost_ref, gt_ref, res_weight):
    tm, d = o_ref.shape
    coef_ref[0:1, :] = res_weight * gt_ref[0] * gpost_ref[...]

    def apply(rows):
        r = stat_ref[rows, :]
        for k in range(d // LANES):
            cols = slice(k * LANES, (k + 1) * LANES)
            o_ref[rows, cols] = x_ref[rows, cols] + o_ref[rows, cols] * r * coef_ref[0:1, cols]

    _for_row_blocks(tm, apply)


def _ada_kernel(c_ref, w_ref, b_ref, o_ref):
    c = c_ref[...]
    cond = (c * jax.nn.sigmoid(c)).astype(BF16)
    o_ref[...] = jnp.dot(cond, w_ref[...].astype(BF16), preferred_element_type=F32) + b_ref[...]


def _ada(c_pad, w, b, *, tn, n):
    m, d = c_pad.shape
    vmem = 2 * d * tn * 4 + d * tn * 2 + 4 * m * tn * 4
    return pl.pallas_call(
        _ada_kernel,
        grid=(n // tn,),
        in_specs=[pl.BlockSpec((m, d), lambda j: (0, 0)),
                  pl.BlockSpec((d, tn), lambda j: (0, j)),
                  pl.BlockSpec((1, tn), lambda j: (0, j))],
        out_specs=pl.BlockSpec((m, tn), lambda j: (0, j)),
        out_shape=jax.ShapeDtypeStruct((m, n), F32),
        compiler_params=_params(vmem),
        name="ada",
    )(c_pad, w, b)


def _ffn_kernel(*refs, res_weight, mode):
    x_ref, sh_ref, sc_ref, gt_ref, gpre_ref, gpost_ref, wg_ref, wu_ref, wd_ref = refs[:9]
    if mode == "head":
        o_ref, wg_out, wu_out, wd_out, h_ref, stat_ref, coef_ref = refs[9:]
    else:
        o_ref, h_ref, stat_ref, coef_ref = refs[9:]
    f = pl.program_id(1)
    active = pl.program_id(0) > 0 if mode == "tail" else True

    def swiglu_down(h):
        wg, wu, wd = wg_ref[...], wu_ref[...], wd_ref[...]
        if mode == "head":
            wg, wu, wd = wg.astype(BF16), wu.astype(BF16), wd.astype(BF16)
            wg_out[...], wu_out[...], wd_out[...] = wg, wu, wd
        g = jnp.dot(h, wg, preferred_element_type=F32)
        u = jnp.dot(h, wu, preferred_element_type=F32)
        a = (g * jax.nn.sigmoid(g) * u).astype(BF16)
        return jnp.dot(a, wd, preferred_element_type=F32)

    @pl.when(jnp.logical_and(active, f == 0))
    def _():
        h = (_rms(x_ref[...], gpre_ref[...]) * (1.0 + sc_ref[0]) + sh_ref[0]).astype(BF16)
        h_ref[...] = h
        o_ref[...] = swiglu_down(h)

    last = pl.num_programs(1) - 1

    @pl.when(jnp.logical_and(active, jnp.logical_and(f > 0, f < last)))
    def _():
        o_ref[...] += swiglu_down(h_ref[...])

    @pl.when(jnp.logical_and(active, f == last))
    def _():
        y = o_ref[...] + swiglu_down(h_ref[...])
        o_ref[...] = y
        ms = jnp.mean(y * y, axis=-1, keepdims=True)
        stat_ref[...] = jnp.broadcast_to(lax.rsqrt(ms + EPS), stat_ref.shape)
        _gated_norm_residual(x_ref, o_ref, stat_ref, coef_ref, gpost_ref, gt_ref, res_weight)

    if mode == "tail":
        @pl.when(jnp.logical_and(jnp.logical_not(active), f == 0))
        def _():
            o_ref[...] = jnp.zeros_like(o_ref)


def _ffn(x, sh, sc, gt, g_pre, g_post, wg, wu, wd, *, seq, tm, tf, res_weight, mode="all"):
    t, d = x.shape
    ff = wg.shape[1]
    tpb = seq // tm
    head = mode == "head"
    n_tiles = 1 if head else t // tm
    w_bytes = 4 if head else 2
    x_bufs = 1 if head else 2
    vmem = (2 * x_bufs * tm * d * 4 + tm * d * 2 + 6 * d * tf * w_bytes + 3 * tm * tf * 4
            + (9 * d * tf * 2 if head else 0))
    mod_spec = pl.BlockSpec((1, 1, d), lambda i, f: (i // tpb, 0, 0))
    gain_spec = pl.BlockSpec((1, d), lambda i, f: (0, 0))
    if mode == "tail":
        f_of = lambda i, f: jnp.where(i == 0, 0, f)
        x_spec = pl.BlockSpec((tm, d), lambda i, f: (jnp.maximum(i, 1), 0))
    else:
        f_of = lambda i, f: f
        x_spec = pl.BlockSpec((tm, d), lambda i, f: (i, 0), **(dict(pipeline_mode=pl.Buffered(1)) if head else {}))
    up_spec = pl.BlockSpec((d, tf), lambda i, f: (0, f_of(i, f)))
    down_spec = pl.BlockSpec((tf, d), lambda i, f: (f_of(i, f), 0))
    tile_kw = dict(pipeline_mode=pl.Buffered(1)) if head else {}
    out_specs = pl.BlockSpec((tm, d), lambda i, f: (i, 0), **tile_kw)
    out_shape = jax.ShapeDtypeStruct((n_tiles * tm, d), F32)
    if head:
        out_specs = [out_specs, up_spec, up_spec, down_spec]
        out_shape = [out_shape, jax.ShapeDtypeStruct(wg.shape, BF16), jax.ShapeDtypeStruct(wu.shape, BF16),
                     jax.ShapeDtypeStruct(wd.shape, BF16)]
    return pl.pallas_call(
        functools.partial(_ffn_kernel, res_weight=res_weight, mode=mode),
        grid=(n_tiles, ff // tf),
        in_specs=[x_spec, mod_spec, mod_spec, mod_spec, gain_spec, gain_spec, up_spec, up_spec, down_spec],
        out_specs=out_specs,
        out_shape=out_shape,
        scratch_shapes=[pltpu.VMEM((tm, d), BF16), pltpu.VMEM((tm, LANES), F32), pltpu.VMEM((8, d), F32)],
        compiler_params=_params(vmem),
        name="ffn_" + mode,
    )(x, sh, sc, gt, g_pre, g_post, wg, wu, wd)


def _rot(r, cos_t, sin_lo, sin_hi):
    return r * cos_t + pltpu.roll(r, 96, 1) * sin_lo + pltpu.roll(r, 32, 1) * sin_hi


def _mla_fox_prep(lat, gq_ref, gkv_ref, wuq_ref, wuk_ref, wuvt_ref, bf_ref, cos_ref, slo_ref, shi_ref,
                  q_ref, k_ref, vt_ref, ck_ref, cq_ref, carry_ref):
    tm = lat.shape[0]
    cos_t, sin_lo, sin_hi = cos_ref[...], slo_ref[...], shi_ref[...]

    qn = _rms(lat[:, :MLA_Q_RANK], gq_ref[...]).astype(BF16)
    q = jnp.dot(qn, wuq_ref[...], preferred_element_type=F32) * MLA_Q_SCALE
    kvn = _rms(lat[:, MLA_Q_RANK:KPE_COL], gkv_ref[...]).astype(BF16)
    k_nope = jnp.dot(kvn, wuk_ref[...], preferred_element_type=F32)
    vt_ref[0, 0] = lax.dot_general(wuvt_ref[...], kvn, _NT, preferred_element_type=F32).astype(BF16)
    k_rope = _rot(lat[:, KPE_COL:KPE_COL + LANES], cos_t, sin_lo, sin_hi).astype(BF16)
    for h in range(MLA_HEADS):
        c0 = h * MLA_QK_PAD
        q_ref[:, c0:c0 + MLA_NOPE] = q[:, c0:c0 + MLA_NOPE].astype(BF16)
        q_ref[:, c0 + MLA_NOPE:c0 + MLA_QK_PAD] = _rot(
            q[:, c0 + MLA_NOPE:c0 + MLA_QK_PAD], cos_t, sin_lo, sin_hi).astype(BF16)
        k_ref[:, c0:c0 + MLA_NOPE] = k_nope[:, h * MLA_NOPE:(h + 1) * MLA_NOPE].astype(BF16)
        k_ref[:, c0 + MLA_NOPE:c0 + MLA_QK_PAD] = k_rope

    z = lat[:, FLOGIT_COL:FLOGIT_COL + LANES] + bf_ref[...]
    lane = lax.broadcasted_iota(jnp.int32, (tm, LANES), 1)
    log_f = jnp.where(lane < FOX_HEADS, jnp.minimum(z, 0.0) - jnp.log1p(jnp.exp(-jnp.abs(z))), 0.0)
    hi = log_f.astype(BF16).astype(F32)
    mid = (log_f - hi).astype(BF16).astype(F32)
    lo = (log_f - hi - mid).astype(BF16).astype(F32)
    pieces = hi + pltpu.roll(mid, FOX_HEADS, 1) + pltpu.roll(lo, 2 * FOX_HEADS, 1)
    row = lax.broadcasted_iota(jnp.int32, (tm, tm), 0)
    col = lax.broadcasted_iota(jnp.int32, (tm, tm), 1)
    tri = (col <= row).astype(BF16)
    part = jnp.dot(tri, pieces.astype(BF16), preferred_element_type=F32)
    part = part + pltpu.roll(part, LANES - FOX_HEADS, 1) + pltpu.roll(part, LANES - 2 * FOX_HEADS, 1)
    cum = jnp.where(lane < FOX_HEADS, part, 0.0) + carry_ref[...]
    carry_ref[...] = cum[tm - 1:tm, :]
    cum2 = cum * LOG2E
    cq_ref[0, 0] = cum2.T[:FOX_HEADS, :]
    for h in range(FOX_HEADS):
        ck_ref[0, h] = jnp.broadcast_to(cum2[:, h:h + 1], (tm, LANES))


def _mixer_in_kernel(x_ref, sh_ref, sc_ref, gpre_ref, wlat_ref, wfox_ref,
                     gq_ref, gkv_ref, wuq_ref, wuk_ref, wuvt_ref, bf_ref, cos_ref, slo_ref, shi_ref,
                     fqk_ref, fvt_ref, q_ref, k_ref, vt_ref, ck_ref, cq_ref, carry_ref, *, tiles_per_batch):
    @pl.when(pl.program_id(0) % tiles_per_batch == 0)
    def _():
        carry_ref[...] = jnp.zeros_like(carry_ref)

    tn = LATENT_WIDTH
    h = (_rms(x_ref[...], gpre_ref[...]) * (1.0 + sc_ref[0]) + sh_ref[0]).astype(BF16)
    fw = FOX_WIDTH
    lat = jnp.dot(h, wlat_ref[...], preferred_element_type=F32)
    fq = jnp.dot(h, wfox_ref[:, :fw], preferred_element_type=F32) * FOX_Q_SCALE
    fqk_ref[:, :fw] = fq.astype(BF16)
    fqk_ref[:, fw:] = jnp.dot(h, wfox_ref[:, fw:2 * fw], preferred_element_type=F32).astype(BF16)
    fvt_ref[0, 0] = jnp.dot(h, wfox_ref[:, 2 * fw:], preferred_element_type=F32).T.astype(BF16)
    _mla_fox_prep(lat, gq_ref, gkv_ref, wuq_ref, wuk_ref, wuvt_ref, bf_ref, cos_ref, slo_ref, shi_ref,
                  q_ref, k_ref, vt_ref, ck_ref, cq_ref, carry_ref)


def _mixer_in(x, sh, sc, g_pre, w_lat, w_fox, g_q, g_kv, wuq, wuk, wuvt, b_forget, cos_t, sin_lo, sin_hi,
              *, batch, seq):
    t, d = x.shape
    tm = ATTN_TILE
    tn = LATENT_WIDTH
    tpb = seq // tm
    qk_w = MLA_HEADS * MLA_QK_PAD
    v_w = MLA_HEADS * MLA_V
    assert tn == FOX_WIDTH and w_lat.shape == (d, tn) and w_fox.shape == (d, 3 * FOX_WIDTH)
    resident_bytes = (w_lat.size + w_fox.size + wuq.size + wuk.size + wuvt.size) * 2
    vmem = (2 * tm * d * 4 + tm * d * 2 + resident_bytes + 2 * tm * (2 * tn + FOX_WIDTH) * 2
            + 2 * tm * (2 * qk_w + v_w) * 2 + 2 * FOX_HEADS * tm * LANES * 4 + 6 * tm * LANES * 4
            + tm * (tn + qk_w) * 4)
    mod_spec = pl.BlockSpec((1, 1, d), lambda i: (i // tpb, 0, 0))
    resident = lambda a: pl.BlockSpec(a.shape, lambda i: (0, 0), pipeline_mode=pl.Buffered(1))
    small = lambda a: pl.BlockSpec(a.shape, lambda i: (0, 0))
    tab_spec = pl.BlockSpec((tm, LANES), lambda i: (i % tpb, 0))
    return pl.pallas_call(
        functools.partial(_mixer_in_kernel, tiles_per_batch=tpb),
        grid=(t // tm,),
        in_specs=[pl.BlockSpec((tm, d), lambda i: (i, 0)),
                  mod_spec, mod_spec, small(g_pre), resident(w_lat), resident(w_fox),
                  small(g_q), small(g_kv), resident(wuq), resident(wuk), resident(wuvt), small(b_forget),
                  tab_spec, tab_spec, tab_spec],
        out_specs=[pl.BlockSpec((tm, 2 * tn), lambda i: (i, 0)),
                   pl.BlockSpec((1, 1, FOX_WIDTH, tm), lambda i: (i // tpb, i % tpb, 0, 0)),
                   pl.BlockSpec((tm, qk_w), lambda i: (i, 0)),
                   pl.BlockSpec((tm, qk_w), lambda i: (i, 0)),
                   pl.BlockSpec((1, 1, v_w, tm), lambda i: (i // tpb, i % tpb, 0, 0)),
                   pl.BlockSpec((1, FOX_HEADS, tm, LANES), lambda i: (i // tpb, 0, i % tpb, 0)),
                   pl.BlockSpec((1, 1, FOX_HEADS, tm), lambda i: (i // tpb, i % tpb, 0, 0))],
        out_shape=[jax.ShapeDtypeStruct((t, 2 * tn), BF16),
                   jax.ShapeDtypeStruct((batch, tpb, FOX_WIDTH, tm), BF16),
                   jax.ShapeDtypeStruct((t, qk_w), BF16),
                   jax.ShapeDtypeStruct((t, qk_w), BF16),
                   jax.ShapeDtypeStruct((batch, tpb, v_w, tm), BF16),
                   jax.ShapeDtypeStruct((batch, FOX_HEADS, seq, LANES), F32),
                   jax.ShapeDtypeStruct((batch, tpb, FOX_HEADS, tm), F32)],
        scratch_shapes=[pltpu.VMEM((1, LANES), F32)],
        compiler_params=_params(vmem),
        name="mixer_in",
    )(x, sh, sc, g_pre, w_lat, w_fox, g_q, g_kv, wuq, wuk, wuvt, b_forget, cos_t, sin_lo, sin_hi)


def _attn_kernel(*refs, mask_chunk, fox, n_cast, mod_rider):
    n_in = 5 if fox else 3
    n_mod = 3 if mod_rider else 0
    n_side_in = n_cast + n_mod
    side_in = refs[n_in:n_in + n_side_in]
    side_out = refs[n_in + n_side_in + 1:n_in + n_side_in + 1 + n_cast + (1 if mod_rider else 0)]
    refs = refs[:n_in] + (refs[n_in + n_side_in],) + refs[n_in + n_side_in + 1 + len(side_out):]
    if fox:
        q_ref, k_ref, vt_ref, ck_ref, cq_ref, o_ref, s0_ref, s1_ref, m_ref, l_ref, acc_ref = refs
    else:
        q_ref, k_ref, vt_ref, o_ref, s0_ref, s1_ref, m_ref, l_ref, acc_ref = refs
    for src, dst in zip(side_in[:n_cast], side_out[:n_cast]):
        dst[...] = src[...].astype(BF16)
    if mod_rider:
        c_ref, wada_ref, bada_ref = side_in[n_cast:]
        _ada_kernel(c_ref, wada_ref, bada_ref, side_out[n_cast])
    head = pl.program_id(1)
    tq = ATTN_TILE
    nq = q_ref.shape[0] // tq
    n_low = nq * (nq - 1) // 2
    assert PHASE_B_UNROLL % 2 == 0 and n_low % PHASE_B_UNROLL == 0
    bufs = (s0_ref, s1_ref)

    def rows(t):
        return pl.ds(pl.multiple_of(t * tq, tq), tq)

    def scores(i, j, s_ref):
        s = lax.dot_general(k_ref[rows(j), :], q_ref[rows(i), :], _NT, preferred_element_type=F32)
        if fox:
            cum_q = cq_ref[0, i, pl.ds(head, 1), :]
            cum_k = ck_ref[0, 0, rows(j), :]
            s = s + (cum_q - jnp.concatenate([cum_k] * (tq // LANES), axis=1))
        s_ref[...] = s

    half = tq // 2

    def scores_diagonal(i, s_ref):
        lo, hi = i * tq, i * tq + half
        s_a = lax.dot_general(k_ref[lo:hi, :], q_ref[lo:lo + tq, :], _NT, preferred_element_type=F32)
        s_b = lax.dot_general(k_ref[hi:hi + half, :], q_ref[hi:hi + half, :], _NT,
                              preferred_element_type=F32)
        if fox:
            cum_q = cq_ref[0, i, pl.ds(head, 1), :]
            s_a = s_a + (cum_q - jnp.concatenate([ck_ref[0, 0, lo:hi, :]] * (tq // LANES), axis=1))
            s_b = s_b + (cum_q[:, half:]
                         - jnp.concatenate([ck_ref[0, 0, hi:hi + half, :]] * (half // LANES), axis=1))
        src = lax.broadcasted_iota(jnp.int32, (half, half), 0) // mask_chunk
        dst = lax.broadcasted_iota(jnp.int32, (half, half), 1) // mask_chunk
        s_ref[:half, :half] = jnp.where(src <= dst, s_a[:, :half], NEG_BIG)
        s_ref[:half, half:] = s_a[:, half:]
        s_ref[half:, half:] = jnp.where(src <= dst, s_b, NEG_BIG)

    def accumulate_diagonal(i, s_ref):
        s_a = s_ref[:half, :]
        s_b = s_ref[half:, half:]
        m_a = jnp.max(s_a, axis=0, keepdims=True)
        m_hi = jnp.maximum(m_a[:, half:], jnp.max(s_b, axis=0, keepdims=True))
        m = jnp.concatenate([m_a[:, :half], m_hi], axis=1)
        p_a = jnp.exp2(s_a - m)
        p_b = jnp.exp2(s_b - m_hi)
        l_a = jnp.sum(p_a, axis=0, keepdims=True)
        m_ref[i] = m
        l_ref[i] = jnp.concatenate([l_a[:, :half], l_a[:, half:] + jnp.sum(p_b, axis=0, keepdims=True)],
                                   axis=1)
        acc_a = jnp.dot(vt_ref[0, i, :, :half], p_a.astype(BF16), preferred_element_type=F32)
        acc_b = jnp.dot(vt_ref[0, i, :, half:], p_b.astype(BF16), preferred_element_type=F32)
        acc_ref[i, :, :half] = acc_a[:, :half]
        acc_ref[i, :, half:] = acc_a[:, half:] + acc_b

    def accumulate(i, j, s_ref):
        s = s_ref[...]
        m_prev = m_ref[i]
        m_new = jnp.maximum(m_prev, jnp.max(s, axis=0, keepdims=True))
        alpha = jnp.exp2(m_prev - m_new)
        p = jnp.exp2(s - m_new)
        l_ref[i] = alpha * l_ref[i] + jnp.sum(p, axis=0, keepdims=True)
        acc_ref[i] = alpha * acc_ref[i] + jnp.dot(vt_ref[0, j], p.astype(BF16),
                                                  preferred_element_type=F32)
        m_ref[i] = m_new

    scores_diagonal(0, bufs[0])
    for i in range(nq):
        if i + 1 < nq:
            scores_diagonal(i + 1, bufs[(i + 1) % 2])
        else:
            scores(1, 0, bufs[(i + 1) % 2])
        accumulate_diagonal(i, bufs[i % 2])

    def advance(i, j):
        wraps = j + 1 == i
        i_nxt = jnp.minimum(jnp.where(wraps, i + 1, i), nq - 1)
        return i_nxt, jnp.where(wraps, 0, j + 1)

    def body(_, carry):
        i, j = carry
        for u in range(PHASE_B_UNROLL):
            i_nxt, j_nxt = advance(i, j)
            scores(i_nxt, j_nxt, bufs[(nq + u + 1) % 2])
            accumulate(i, j, bufs[(nq + u) % 2])
            i, j = i_nxt, j_nxt
        return i, j

    lax.fori_loop(0, n_low // PHASE_B_UNROLL, body, (jnp.int32(1), jnp.int32(0)))

    for i in range(nq):
        o_ref[i * tq:(i + 1) * tq, :] = (acc_ref[i] / l_ref[i]).T.astype(o_ref.dtype)


def _attention(q_arr, k_arr, vt_arr, cum_k=None, cum_q=None, *, batch, seq, heads, dk, dv,
               q_col, k_col, mask_chunk, riders=(), mod_rider=None):
    fox = cum_k is not None
    tq = ATTN_TILE
    nq = seq // tq
    steps = batch * heads
    in_specs = [pl.BlockSpec((seq, dk), lambda b, h: (b, q_col + h)),
                pl.BlockSpec((seq, dk), lambda b, h: (b, k_col + h)),
                pl.BlockSpec((1, nq, dv, tq), lambda b, h: (b, 0, h, 0))]
    args = [q_arr, k_arr, vt_arr]
    if fox:
        in_specs += [pl.BlockSpec((1, 1, seq, LANES), lambda b, h: (b, h, 0, 0)),
                     pl.BlockSpec((1, nq, heads, tq), lambda b, h: (b, 0, 0, 0))]
        args += [cum_k, cum_q]
    slab_specs = [pl.BlockSpec((w.shape[0] // steps, w.shape[1]), lambda b, h: (b * heads + h, 0))
                  for w in riders]
    vmem = (2 * seq * (2 * dk + 2 * dv) * 2 + 2 * seq * LANES * 4 + 2 * seq * heads * 4
            + seq * dv * 4 + 8 * tq * tq * 4 + sum(2 * 6 * w.size // steps for w in riders))
    side_in_specs, side_out_specs, side_args = list(slab_specs), list(slab_specs), list(riders)
    side_out_shape = [jax.ShapeDtypeStruct(w.shape, BF16) for w in riders]
    if mod_rider is not None:
        c_pad, w_ada, b_ada, first_col = mod_rider
        m, d_model = c_pad.shape
        tn = (w_ada.shape[1] - first_col) // steps
        assert first_col % tn == 0
        col = lambda b, h: (0, first_col // tn + b * heads + h)
        side_in_specs += [pl.BlockSpec((m, d_model), lambda b, h: (0, 0)),
                          pl.BlockSpec((d_model, tn), col), pl.BlockSpec((1, tn), col)]
        side_out_specs += [pl.BlockSpec((m, tn), lambda b, h: (0, b * heads + h))]
        side_out_shape += [jax.ShapeDtypeStruct((m, tn * steps), F32)]
        side_args += [c_pad, w_ada, b_ada]
        vmem += 2 * d_model * tn * 4 + d_model * tn * 2
    return pl.pallas_call(
        functools.partial(_attn_kernel, mask_chunk=mask_chunk, fox=fox, n_cast=len(riders),
                          mod_rider=mod_rider is not None),
        grid=(batch, heads),
        in_specs=in_specs + side_in_specs,
        out_specs=[pl.BlockSpec((seq, dv), lambda b, h: (b, h))] + side_out_specs,
        out_shape=[jax.ShapeDtypeStruct((batch * seq, heads * dv), BF16)] + side_out_shape,
        scratch_shapes=[pltpu.VMEM((tq, tq), F32), pltpu.VMEM((tq, tq), F32),
                        pltpu.VMEM((nq, 1, tq), F32), pltpu.VMEM((nq, 1, tq), F32),
                        pltpu.VMEM((nq, dv, tq), F32)],
        compiler_params=_params(vmem),
        name="fox_attn" if fox else "mla_attn",
    )(*args, *side_args)


def _outproj_kernel(x_ref, oa_ref, ob_ref, wa_ref, wb_ref, gpost_ref, gt_ref, o_ref):
    y = jnp.dot(oa_ref[...], wa_ref[...], preferred_element_type=F32)
    y = y + jnp.dot(ob_ref[...], wb_ref[...], preferred_element_type=F32)
    o_ref[...] = x_ref[...] + gt_ref[0] * _rms(y, gpost_ref[...])


def _outproj(x, o_a, o_b, w_a, w_b, g_post, gt, *, seq, tm):
    t, d = x.shape
    ka, kb = o_a.shape[1], o_b.shape[1]
    tpb = seq // tm
    vmem = 4 * tm * d * 4 + 2 * tm * (ka + kb) * 2 + 2 * (ka + kb) * d * 2 + 2 * tm * d * 4
    return pl.pallas_call(
        _outproj_kernel,
        grid=(t // tm,),
        in_specs=[pl.BlockSpec((tm, d), lambda i: (i, 0)),
                  pl.BlockSpec((tm, ka), lambda i: (i, 0)),
                  pl.BlockSpec((tm, kb), lambda i: (i, 0)),
                  pl.BlockSpec((ka, d), lambda i: (0, 0)),
                  pl.BlockSpec((kb, d), lambda i: (0, 0)),
                  pl.BlockSpec((1, d), lambda i: (0, 0)),
                  pl.BlockSpec((1, 1, d), lambda i: (i // tpb, 0, 0))],
        out_specs=pl.BlockSpec((tm, d), lambda i: (i, 0)),
        out_shape=jax.ShapeDtypeStruct((t, d), F32),
        compiler_params=_params(vmem),
        name="out_proj",
    )(x, o_a, o_b, w_a, w_b, g_post, gt)


def _rope_tables(seq):
    half = MLA_ROPE // 2
    inv = ROPE_THETA ** (-np.arange(half, dtype=np.float64) / half)
    ang = np.arange(seq, dtype=np.float64)[:, None] * inv[None, :]
    cos, sin = np.cos(ang), np.sin(ang)
    zero = np.zeros_like(cos)
    pad = np.zeros((seq, LANES - 2 * half))
    return tuple(jnp.asarray(np.concatenate(parts, axis=1), F32)
                 for parts in ([cos, cos, pad], [-sin, zero, pad], [zero, sin, pad]))


def kernel(x, c, w_ada, b_ada, g_ffn1_pre, g_ffn1_post, w1_gate, w1_up, w1_down, g_mix_pre,
           g_mix_post, w_in, b_forget, g_q_a, w_uq, g_kv_a, w_ukv, w_o, g_ffn2_pre, g_ffn2_post,
           w2_gate, w2_up, w2_down):
    batch, seq, d = x.shape
    depth = w_ada.shape[0]
    t = batch * seq
    xt = x.reshape(t, d)
    c_pad = jnp.pad(c, ((0, 8 - batch), (0, 0)))
    cos_t, sin_lo, sin_hi = _rope_tables(seq)

    for l in range(depth):
        early = 5
        chunks = lambda mod: [mod[:batch, n * d:(n + 1) * d].reshape(batch, 1, d)
                              for n in range(mod.shape[1] // d)]
        sh1, sc1, gt1, sh2, sc2 = chunks(_ada(c_pad, w_ada[l], b_ada[l:l + 1], tn=1024, n=early * d))

        ffn1 = functools.partial(_ffn, xt, sh1, sc1, gt1, g_ffn1_pre[l:l + 1], g_ffn1_post[l:l + 1],
                                 seq=seq, tm=1024, res_weight=0.5)
        y_head, w1g, w1u, w1d = ffn1(w1_gate[l], w1_up[l], w1_down[l], tf=256, mode="head")
        xt = lax.dynamic_update_slice(ffn1(w1g, w1u, w1d, tf=512, mode="tail"), y_head, (0, 0))

        wi = w_in[l]
        fox0 = KPE_COL + MLA_ROPE
        fl0 = fox0 + 3 * FOX_WIDTH
        w_lat = jnp.concatenate(
            [wi[:, :fox0], jnp.zeros((d, LANES - MLA_ROPE), F32),
             wi[:, fl0:], jnp.zeros((d, LANES - FOX_HEADS), F32)], axis=1).astype(BF16)
        w_fox = wi[:, fox0:fl0].astype(BF16)

        wuq_p = jnp.pad(w_uq[l].reshape(MLA_Q_RANK, MLA_HEADS, MLA_NOPE + MLA_ROPE),
                        ((0, 0), (0, 0), (0, MLA_QK_PAD - MLA_NOPE - MLA_ROPE))
                        ).reshape(MLA_Q_RANK, MLA_HEADS * MLA_QK_PAD).astype(BF16)
        wukv = w_ukv[l].reshape(MLA_KV_RANK, MLA_HEADS, MLA_NOPE + MLA_V)
        wuk = wukv[:, :, :MLA_NOPE].reshape(MLA_KV_RANK, -1).astype(BF16)
        wuvt = wukv[:, :, MLA_NOPE:].reshape(MLA_KV_RANK, -1).T.astype(BF16)
        bf_pad = jnp.pad(b_forget[l:l + 1], ((0, 0), (0, LANES - FOX_HEADS)))
        fqk, fvt, q_mla, k_mla, vt_mla, cum_k, cum_q = _mixer_in(
            xt, sh2, sc2, g_mix_pre[l:l + 1], w_lat, w_fox, g_q_a[l:l + 1], g_kv_a[l:l + 1],
            wuq_p, wuk, wuvt, bf_pad, cos_t, sin_lo, sin_hi, batch=batch, seq=seq)

        o_mla, w2g, w2u = _attention(q_mla, k_mla, vt_mla, batch=batch, seq=seq, heads=MLA_HEADS,
                                     dk=MLA_QK_PAD, dv=MLA_V, q_col=0, k_col=0, mask_chunk=CHUNK,
                                     riders=(w2_gate[l], w2_up[l]))
        o_fox, w2d, wo, mod_late = _attention(
            fqk, fqk, fvt, cum_k, cum_q, batch=batch, seq=seq, heads=FOX_HEADS, dk=FOX_DIM, dv=FOX_DIM,
            q_col=0, k_col=FOX_HEADS, mask_chunk=1, riders=(w2_down[l], w_o[l]),
            mod_rider=(c_pad, w_ada[l], b_ada[l:l + 1], early * d))
        gt2, sh3, sc3, gt3 = chunks(mod_late)

        xt = _outproj(xt, o_mla, o_fox, wo[:MLA_HEADS * MLA_V], wo[MLA_HEADS * MLA_V:],
                      g_mix_post[l:l + 1], gt2, seq=seq, tm=512)

        xt = _ffn(xt, sh3, sc3, gt3, g_ffn2_pre[l:l + 1], g_ffn2_post[l:l + 1],
                  w2g, w2u, w2d, seq=seq, tm=1024, tf=512, res_weight=0.5)

    return xt.reshape(batch, seq, d)
```

```python
import functools
import math

import jax
import jax.numpy as jnp
import numpy as np
from jax import lax
from jax.experimental import pallas as pl
from jax.experimental.pallas import tpu as pltpu

F32 = jnp.float32
BF16 = jnp.bfloat16

V7X_VMEM_BYTES = 64 * 1024 * 1024
LANES = 128

EPS = 1e-6
ROPE_THETA = 10000.0
CHUNK = 64
MLA_HEADS = 8
MLA_Q_RANK = 512
MLA_KV_RANK = 256
MLA_NOPE = 128
MLA_ROPE = 64
MLA_V = 128
MLA_QK_PAD = 256
FOX_HEADS = 8
FOX_DIM = 128
FOX_WIDTH = FOX_HEADS * FOX_DIM
LATENT_WIDTH = 1024
KPE_COL = MLA_Q_RANK + MLA_KV_RANK
FLOGIT_COL = KPE_COL + LANES
ATTN_TILE = 512
PHASE_B_UNROLL = 14
LOG2E = math.log2(math.e)
MLA_Q_SCALE = LOG2E / math.sqrt(MLA_NOPE + MLA_ROPE)
FOX_Q_SCALE = LOG2E / math.sqrt(FOX_DIM)
NEG_BIG = -1e30

_NT = (((1,), (1,)), ((), ()))


def _params(vmem_bytes):
    limit = min(int(vmem_bytes * 1.25) + (4 << 20), V7X_VMEM_BYTES - (8 << 20))
    return pltpu.CompilerParams(vmem_limit_bytes=limit)


def _rms(x, g):
    return x * lax.rsqrt(jnp.mean(x * x, axis=-1, keepdims=True) + EPS) * g


STAT_ROWS = 8
APPLY_ROWS = 16


def _rms_stats(src_ref, stat_ref):
    tm, d = src_ref.shape

    def step(c, carry):
        rows = pl.ds(pl.multiple_of(c * STAT_ROWS, STAT_ROWS), STAT_ROWS)
        x = src_ref[rows, :]
        ms = jnp.sum(x * x, axis=-1, keepdims=True) * (1.0 / d)
        stat_ref[rows, :] = jnp.broadcast_to(lax.rsqrt(ms + EPS), (STAT_ROWS, LANES))
        return carry

    lax.fori_loop(0, tm // STAT_ROWS, step, 0, unroll=32)


def _for_row_blocks(tm, fn):
    def step(c, carry):
        fn(pl.ds(pl.multiple_of(c * APPLY_ROWS, APPLY_ROWS), APPLY_ROWS))
        return carry

    lax.fori_loop(0, tm // APPLY_ROWS, step, 0, unroll=4)


def _gated_norm_residual(x_ref, o_ref, stat_ref, coef_ref, gpost_ref, gt_ref, res_weight):
    tm, d = o_ref.shape
    coef_ref[0:1, :] = res_weight * gt_ref[0] * gpost_ref[...]
    _rms_stats(o_ref, stat_ref)

    def apply(rows):
        r = stat_ref[rows, :]
        for k in range(d // LANES):
            cols = slice(k * LANES, (k + 1) * LANES)
            o_ref[rows, cols] = x_ref[rows, cols] + o_ref[rows, cols] * r * coef_ref[0:1, cols]

    _for_row_blocks(tm, apply)


def _ada_kernel(c_ref, w_ref, b_ref, o_ref):
    c = c_ref[...]
    cond = (c * jax.nn.sigmoid(c)).astype(BF16)
    o_ref[...] = jnp.dot(cond, w_ref[...].astype(BF16), preferred_element_type=F32) + b_ref[...]


def _ada(c_pad, w, b, *, tn, n):
    m, d = c_pad.shape
    vmem = 2 * d * tn * 4 + d * tn * 2 + 4 * m * tn * 4
    return pl.pallas_call(
        _ada_kernel,
        grid=(n // tn,),
        in_specs=[pl.BlockSpec((m, d), lambda j: (0, 0)),
                  pl.BlockSpec((d, tn), lambda j: (0, j)),
                  pl.BlockSpec((1, tn), lambda j: (0, j))],
        out_specs=pl.BlockSpec((m, tn), lambda j: (0, j)),
        out_shape=jax.ShapeDtypeStruct((m, n), F32),
        compiler_params=_params(vmem),
        name="ada",
    )(c_pad, w, b)


def _ffn_kernel(*refs, res_weight, mode):
    x_ref, sh_ref, sc_ref, gt_ref, gpre_ref, gpost_ref, wg_ref, wu_ref, wd_ref = refs[:9]
    if mode == "head":
        o_ref, wg_out, wu_out, wd_out, h_ref, stat_ref, coef_ref = refs[9:]
    else:
        o_ref, h_ref, stat_ref, coef_ref = refs[9:]
    f = pl.program_id(1)
    active = pl.program_id(0) > 0 if mode == "tail" else True

    def swiglu_down(h):
        wg, wu, wd = wg_ref[...], wu_ref[...], wd_ref[...]
        if mode == "head":
            wg, wu, wd = wg.astype(BF16), wu.astype(BF16), wd.astype(BF16)
            wg_out[...], wu_out[...], wd_out[...] = wg, wu, wd
        g = jnp.dot(h, wg, preferred_element_type=F32)
        u = jnp.dot(h, wu, preferred_element_type=F32)
        a = (g * jax.nn.sigmoid(g) * u).astype(BF16)
        return jnp.dot(a, wd, preferred_element_type=F32)

    @pl.when(jnp.logical_and(active, f == 0))
    def _():
        h = (_rms(x_ref[...], gpre_ref[...]) * (1.0 + sc_ref[0]) + sh_ref[0]).astype(BF16)
        h_ref[...] = h
        o_ref[...] = swiglu_down(h)

    @pl.when(jnp.logical_and(active, f > 0))
    def _():
        o_ref[...] += swiglu_down(h_ref[...])

    @pl.when(jnp.logical_and(active, f == pl.num_programs(1) - 1))
    def _():
        _gated_norm_residual(x_ref, o_ref, stat_ref, coef_ref, gpost_ref, gt_ref, res_weight)

    if mode == "tail":
        @pl.when(jnp.logical_and(jnp.logical_not(active), f == 0))
        def _():
            o_ref[...] = jnp.zeros_like(o_ref)


def _ffn(x, sh, sc, gt, g_pre, g_post, wg, wu, wd, *, seq, tm, tf, res_weight, mode="all"):
    t, d = x.shape
    ff = wg.shape[1]
    tpb = seq // tm
    head = mode == "head"
    n_tiles = 1 if head else t // tm
    w_bytes = 4 if head else 2
    x_bufs = 1 if head else 2
    vmem = (2 * x_bufs * tm * d * 4 + tm * d * 2 + 6 * d * tf * w_bytes + 3 * tm * tf * 4
            + (9 * d * tf * 2 if head else 0))
    mod_spec = pl.BlockSpec((1, 1, d), lambda i, f: (i // tpb, 0, 0))
    gain_spec = pl.BlockSpec((1, d), lambda i, f: (0, 0))
    tile_kw = dict(pipeline_mode=pl.Buffered(1)) if head else {}
    if mode == "tail":
        f_of = lambda i, f: jnp.where(i == 0, 0, f)
        x_spec = pl.BlockSpec((tm, d), lambda i, f: (jnp.maximum(i, 1), 0))
    else:
        f_of = lambda i, f: f
        x_spec = pl.BlockSpec((tm, d), lambda i, f: (i, 0), **tile_kw)
    up_spec = pl.BlockSpec((d, tf), lambda i, f: (0, f_of(i, f)))
    down_spec = pl.BlockSpec((tf, d), lambda i, f: (f_of(i, f), 0))
    out_specs = pl.BlockSpec((tm, d), lambda i, f: (i, 0), **tile_kw)
    out_shape = jax.ShapeDtypeStruct((n_tiles * tm, d), F32)
    if head:
        out_specs = [out_specs, up_spec, up_spec, down_spec]
        out_shape = [out_shape, jax.ShapeDtypeStruct(wg.shape, BF16), jax.ShapeDtypeStruct(wu.shape, BF16),
                     jax.ShapeDtypeStruct(wd.shape, BF16)]
    return pl.pallas_call(
        functools.partial(_ffn_kernel, res_weight=res_weight, mode=mode),
        grid=(n_tiles, ff // tf),
        in_specs=[x_spec, mod_spec, mod_spec, mod_spec, gain_spec, gain_spec, up_spec, up_spec, down_spec],
        out_specs=out_specs,
        out_shape=out_shape,
        scratch_shapes=[pltpu.VMEM((tm, d), BF16), pltpu.VMEM((tm, LANES), F32), pltpu.VMEM((8, d), F32)],
        compiler_params=_params(vmem),
        name="ffn_" + mode,
    )(x, sh, sc, gt, g_pre, g_post, wg, wu, wd)


def _rot(r, cos_t, sin_lo, sin_hi):
    return r * cos_t + pltpu.roll(r, 96, 1) * sin_lo + pltpu.roll(r, 32, 1) * sin_hi


def _mla_fox_prep(lat, gq_ref, gkv_ref, wuq_ref, wuk_ref, wuvt_ref, bf_ref, cos_ref, slo_ref, shi_ref,
                  q_ref, k_ref, vt_ref, ck_ref, cq_ref, carry_ref):
    tm = lat.shape[0]
    cos_t, sin_lo, sin_hi = cos_ref[...], slo_ref[...], shi_ref[...]

    qn = _rms(lat[:, :MLA_Q_RANK], gq_ref[...]).astype(BF16)
    q = jnp.dot(qn, wuq_ref[...], preferred_element_type=F32) * MLA_Q_SCALE
    kvn = _rms(lat[:, MLA_Q_RANK:KPE_COL], gkv_ref[...]).astype(BF16)
    k_nope = jnp.dot(kvn, wuk_ref[...], preferred_element_type=F32)
    vt_ref[0, 0] = lax.dot_general(wuvt_ref[...], kvn, _NT, preferred_element_type=F32).astype(BF16)
    k_rope = _rot(lat[:, KPE_COL:KPE_COL + LANES], cos_t, sin_lo, sin_hi).astype(BF16)
    for h in range(MLA_HEADS):
        c0 = h * MLA_QK_PAD
        q_ref[:, c0:c0 + MLA_NOPE] = q[:, c0:c0 + MLA_NOPE].astype(BF16)
        q_ref[:, c0 + MLA_NOPE:c0 + MLA_QK_PAD] = _rot(
            q[:, c0 + MLA_NOPE:c0 + MLA_QK_PAD], cos_t, sin_lo, sin_hi).astype(BF16)
        k_ref[:, c0:c0 + MLA_NOPE] = k_nope[:, h * MLA_NOPE:(h + 1) * MLA_NOPE].astype(BF16)
        k_ref[:, c0 + MLA_NOPE:c0 + MLA_QK_PAD] = k_rope

    z = lat[:, FLOGIT_COL:FLOGIT_COL + LANES] + bf_ref[...]
    lane = lax.broadcasted_iota(jnp.int32, (tm, LANES), 1)
    log_f = jnp.where(lane < FOX_HEADS, jnp.minimum(z, 0.0) - jnp.log1p(jnp.exp(-jnp.abs(z))), 0.0)
    hi = log_f.astype(BF16).astype(F32)
    mid = (log_f - hi).astype(BF16).astype(F32)
    lo = (log_f - hi - mid).astype(BF16).astype(F32)
    pieces = hi + pltpu.roll(mid, FOX_HEADS, 1) + pltpu.roll(lo, 2 * FOX_HEADS, 1)
    row = lax.broadcasted_iota(jnp.int32, (tm, tm), 0)
    col = lax.broadcasted_iota(jnp.int32, (tm, tm), 1)
    tri = (col <= row).astype(BF16)
    part = jnp.dot(tri, pieces.astype(BF16), preferred_element_type=F32)
    part = part + pltpu.roll(part, LANES - FOX_HEADS, 1) + pltpu.roll(part, LANES - 2 * FOX_HEADS, 1)
    cum = jnp.where(lane < FOX_HEADS, part, 0.0) + carry_ref[...]
    carry_ref[...] = cum[tm - 1:tm, :]
    cum2 = cum * LOG2E
    cq_ref[0, 0] = cum2.T[:FOX_HEADS, :]
    for h in range(FOX_HEADS):
        ck_ref[0, h] = jnp.broadcast_to(cum2[:, h:h + 1], (tm, LANES))


def _mixer_in_kernel(x_ref, sh_ref, sc_ref, gpre_ref, wlat_ref, wfox_ref,
                     gq_ref, gkv_ref, wuq_ref, wuk_ref, wuvt_ref, bf_ref, cos_ref, slo_ref, shi_ref,
                     fqk_ref, fvt_ref, q_ref, k_ref, vt_ref, ck_ref, cq_ref, carry_ref, *, tiles_per_batch):
    @pl.when(pl.program_id(0) % tiles_per_batch == 0)
    def _():
        carry_ref[...] = jnp.zeros_like(carry_ref)

    h = (_rms(x_ref[...], gpre_ref[...]) * (1.0 + sc_ref[0]) + sh_ref[0]).astype(BF16)
    fw = FOX_WIDTH
    lat = jnp.dot(h, wlat_ref[...], preferred_element_type=F32)
    fq = jnp.dot(h, wfox_ref[:, :fw], preferred_element_type=F32) * FOX_Q_SCALE
    fqk_ref[:, :fw] = fq.astype(BF16)
    fqk_ref[:, fw:] = jnp.dot(h, wfox_ref[:, fw:2 * fw], preferred_element_type=F32).astype(BF16)
    fvt_ref[0, 0] = jnp.dot(h, wfox_ref[:, 2 * fw:], preferred_element_type=F32).T.astype(BF16)
    _mla_fox_prep(lat, gq_ref, gkv_ref, wuq_ref, wuk_ref, wuvt_ref, bf_ref, cos_ref, slo_ref, shi_ref,
                  q_ref, k_ref, vt_ref, ck_ref, cq_ref, carry_ref)


def _mixer_in(x, sh, sc, g_pre, w_lat, w_fox, g_q, g_kv, wuq, wuk, wuvt, b_forget, cos_t, sin_lo, sin_hi,
              *, batch, seq):
    t, d = x.shape
    tm = ATTN_TILE
    tn = LATENT_WIDTH
    tpb = seq // tm
    qk_w = MLA_HEADS * MLA_QK_PAD
    v_w = MLA_HEADS * MLA_V
    assert tn == FOX_WIDTH and w_lat.shape == (d, tn) and w_fox.shape == (d, 3 * FOX_WIDTH)
    resident_bytes = (w_lat.size + w_fox.size + wuq.size + wuk.size + wuvt.size) * 2
    vmem = (2 * tm * d * 4 + tm * d * 2 + resident_bytes + 2 * tm * (2 * tn + FOX_WIDTH) * 2
            + 2 * tm * (2 * qk_w + v_w) * 2 + 2 * FOX_HEADS * tm * LANES * 4 + 6 * tm * LANES * 4
            + tm * (tn + qk_w) * 4)
    mod_spec = pl.BlockSpec((1, 1, d), lambda i: (i // tpb, 0, 0))
    resident = lambda a: pl.BlockSpec(a.shape, lambda i: (0, 0), pipeline_mode=pl.Buffered(1))
    small = lambda a: pl.BlockSpec(a.shape, lambda i: (0, 0))
    tab_spec = pl.BlockSpec((tm, LANES), lambda i: (i % tpb, 0))
    return pl.pallas_call(
        functools.partial(_mixer_in_kernel, tiles_per_batch=tpb),
        grid=(t // tm,),
        in_specs=[pl.BlockSpec((tm, d), lambda i: (i, 0)),
                  mod_spec, mod_spec, small(g_pre), resident(w_lat), resident(w_fox),
                  small(g_q), small(g_kv), resident(wuq), resident(wuk), resident(wuvt), small(b_forget),
                  tab_spec, tab_spec, tab_spec],
        out_specs=[pl.BlockSpec((tm, 2 * tn), lambda i: (i, 0)),
                   pl.BlockSpec((1, 1, FOX_WIDTH, tm), lambda i: (i // tpb, i % tpb, 0, 0)),
                   pl.BlockSpec((tm, qk_w), lambda i: (i, 0)),
                   pl.BlockSpec((tm, qk_w), lambda i: (i, 0)),
                   pl.BlockSpec((1, 1, v_w, tm), lambda i: (i // tpb, i % tpb, 0, 0)),
                   pl.BlockSpec((1, FOX_HEADS, tm, LANES), lambda i: (i // tpb, 0, i % tpb, 0)),
                   pl.BlockSpec((1, 1, FOX_HEADS, tm), lambda i: (i // tpb, i % tpb, 0, 0))],
        out_shape=[jax.ShapeDtypeStruct((t, 2 * tn), BF16),
                   jax.ShapeDtypeStruct((batch, tpb, FOX_WIDTH, tm), BF16),
                   jax.ShapeDtypeStruct((t, qk_w), BF16),
                   jax.ShapeDtypeStruct((t, qk_w), BF16),
                   jax.ShapeDtypeStruct((batch, tpb, v_w, tm), BF16),
                   jax.ShapeDtypeStruct((batch, FOX_HEADS, seq, LANES), F32),
                   jax.ShapeDtypeStruct((batch, tpb, FOX_HEADS, tm), F32)],
        scratch_shapes=[pltpu.VMEM((1, LANES), F32)],
        compiler_params=_params(vmem),
        name="mixer_in",
    )(x, sh, sc, g_pre, w_lat, w_fox, g_q, g_kv, wuq, wuk, wuvt, b_forget, cos_t, sin_lo, sin_hi)


def _attn_kernel(*refs, mask_chunk, fox, n_cast, mod_rider):
    n_in = 5 if fox else 3
    n_mod = 3 if mod_rider else 0
    n_side_in = n_cast + n_mod
    side_in = refs[n_in:n_in + n_side_in]
    side_out = refs[n_in + n_side_in + 1:n_in + n_side_in + 1 + n_cast + (1 if mod_rider else 0)]
    refs = refs[:n_in] + (refs[n_in + n_side_in],) + refs[n_in + n_side_in + 1 + len(side_out):]
    if fox:
        q_ref, k_ref, vt_ref, ck_ref, cq_ref, o_ref, s0_ref, s1_ref, m_ref, l_ref, acc_ref = refs
    else:
        q_ref, k_ref, vt_ref, o_ref, s0_ref, s1_ref, m_ref, l_ref, acc_ref = refs
    for src, dst in zip(side_in[:n_cast], side_out[:n_cast]):
        dst[...] = src[...].astype(BF16)
    if mod_rider:
        c_ref, wada_ref, bada_ref = side_in[n_cast:]
        _ada_kernel(c_ref, wada_ref, bada_ref, side_out[n_cast])
    head = pl.program_id(1)
    tq = ATTN_TILE
    nq = q_ref.shape[0] // tq
    n_low = nq * (nq - 1) // 2
    assert PHASE_B_UNROLL % 2 == 0 and n_low % PHASE_B_UNROLL == 0
    bufs = (s0_ref, s1_ref)

    def rows(t):
        return pl.ds(pl.multiple_of(t * tq, tq), tq)

    def scores(i, j, s_ref):
        s = lax.dot_general(k_ref[rows(j), :], q_ref[rows(i), :], _NT, preferred_element_type=F32)
        if fox:
            cum_q = cq_ref[0, i, pl.ds(head, 1), :]
            cum_k = ck_ref[0, 0, rows(j), :]
            s = s + (cum_q - jnp.concatenate([cum_k] * (tq // LANES), axis=1))
        s_ref[...] = s

    half = tq // 2

    def scores_diagonal(i, s_ref):
        lo, hi = i * tq, i * tq + half
        s_a = lax.dot_general(k_ref[lo:hi, :], q_ref[lo:lo + tq, :], _NT, preferred_element_type=F32)
        s_b = lax.dot_general(k_ref[hi:hi + half, :], q_ref[hi:hi + half, :], _NT,
                              preferred_element_type=F32)
        if fox:
            cum_q = cq_ref[0, i, pl.ds(head, 1), :]
            s_a = s_a + (cum_q - jnp.concatenate([ck_ref[0, 0, lo:hi, :]] * (tq // LANES), axis=1))
            s_b = s_b + (cum_q[:, half:]
                         - jnp.concatenate([ck_ref[0, 0, hi:hi + half, :]] * (half // LANES), axis=1))
        src = lax.broadcasted_iota(jnp.int32, (half, half), 0) // mask_chunk
        dst = lax.broadcasted_iota(jnp.int32, (half, half), 1) // mask_chunk
        s_ref[:half, :half] = jnp.where(src <= dst, s_a[:, :half], NEG_BIG)
        s_ref[:half, half:] = s_a[:, half:]
        s_ref[half:, half:] = jnp.where(src <= dst, s_b, NEG_BIG)

    def accumulate_diagonal(i, s_ref):
        s_a = s_ref[:half, :]
        s_b = s_ref[half:, half:]
        m_a = jnp.max(s_a, axis=0, keepdims=True)
        m_hi = jnp.maximum(m_a[:, half:], jnp.max(s_b, axis=0, keepdims=True))
        m = jnp.concatenate([m_a[:, :half], m_hi], axis=1)
        p_a = jnp.exp2(s_a - m)
        p_b = jnp.exp2(s_b - m_hi)
        l_a = jnp.sum(p_a, axis=0, keepdims=True)
        m_ref[i] = m
        l_ref[i] = jnp.concatenate([l_a[:, :half], l_a[:, half:] + jnp.sum(p_b, axis=0, keepdims=True)],
                                   axis=1)
        acc_a = jnp.dot(vt_ref[0, i, :, :half], p_a.astype(BF16), preferred_element_type=F32)
        acc_b = jnp.dot(vt_ref[0, i, :, half:], p_b.astype(BF16), preferred_element_type=F32)
        acc_ref[i, :, :half] = acc_a[:, :half]
        acc_ref[i, :, half:] = acc_a[:, half:] + acc_b

    def accumulate(i, j, s_ref):
        s = s_ref[...]
        m_prev = m_ref[i]
        m_new = jnp.maximum(m_prev, jnp.max(s, axis=0, keepdims=True))
        alpha = jnp.exp2(m_prev - m_new)
        p = jnp.exp2(s - m_new)
        l_ref[i] = alpha * l_ref[i] + jnp.sum(p, axis=0, keepdims=True)
        acc_ref[i] = alpha * acc_ref[i] + jnp.dot(vt_ref[0, j], p.astype(BF16),
                                                  preferred_element_type=F32)
        m_ref[i] = m_new

    scores_diagonal(0, bufs[0])
    for i in range(nq):
        if i + 1 < nq:
            scores_diagonal(i + 1, bufs[(i + 1) % 2])
        else:
            scores(1, 0, bufs[(i + 1) % 2])
        accumulate_diagonal(i, bufs[i % 2])

    def advance(i, j):
        wraps = j + 1 == i
        i_nxt = jnp.minimum(jnp.where(wraps, i + 1, i), nq - 1)
        return i_nxt, jnp.where(wraps, 0, j + 1)

    def body(_, carry):
        i, j = carry
        for u in range(PHASE_B_UNROLL):
            i_nxt, j_nxt = advance(i, j)
            scores(i_nxt, j_nxt, bufs[(nq + u + 1) % 2])
            accumulate(i, j, bufs[(nq + u) % 2])
            i, j = i_nxt, j_nxt
        return i, j

    lax.fori_loop(0, n_low // PHASE_B_UNROLL, body, (jnp.int32(1), jnp.int32(0)))

    for i in range(nq):
        o_ref[i * tq:(i + 1) * tq, :] = (acc_ref[i] / l_ref[i]).T.astype(o_ref.dtype)


def _attention(q_arr, k_arr, vt_arr, cum_k=None, cum_q=None, *, batch, seq, heads, dk, dv,
               q_col, k_col, mask_chunk, riders=(), mod_rider=None):
    fox = cum_k is not None
    tq = ATTN_TILE
    nq = seq // tq
    steps = batch * heads
    in_specs = [pl.BlockSpec((seq, dk), lambda b, h: (b, q_col + h)),
                pl.BlockSpec((seq, dk), lambda b, h: (b, k_col + h)),
                pl.BlockSpec((1, nq, dv, tq), lambda b, h: (b, 0, h, 0))]
    args = [q_arr, k_arr, vt_arr]
    if fox:
        in_specs += [pl.BlockSpec((1, 1, seq, LANES), lambda b, h: (b, h, 0, 0)),
                     pl.BlockSpec((1, nq, heads, tq), lambda b, h: (b, 0, 0, 0))]
        args += [cum_k, cum_q]
    slab_specs = [pl.BlockSpec((w.shape[0] // steps, w.shape[1]), lambda b, h: (b * heads + h, 0))
                  for w in riders]
    vmem = (2 * seq * (2 * dk + 2 * dv) * 2 + 2 * seq * LANES * 4 + 2 * seq * heads * 4
            + seq * dv * 4 + 8 * tq * tq * 4 + sum(2 * 6 * w.size // steps for w in riders))
    side_in_specs, side_out_specs, side_args = list(slab_specs), list(slab_specs), list(riders)
    side_out_shape = [jax.ShapeDtypeStruct(w.shape, BF16) for w in riders]
    if mod_rider is not None:
        c_pad, w_ada, b_ada, first_col = mod_rider
        m, d_model = c_pad.shape
        tn = (w_ada.shape[1] - first_col) // steps
        assert first_col % tn == 0
        col = lambda b, h: (0, first_col // tn + b * heads + h)
        side_in_specs += [pl.BlockSpec((m, d_model), lambda b, h: (0, 0)),
                          pl.BlockSpec((d_model, tn), col), pl.BlockSpec((1, tn), col)]
        side_out_specs += [pl.BlockSpec((m, tn), lambda b, h: (0, b * heads + h))]
        side_out_shape += [jax.ShapeDtypeStruct((m, tn * steps), F32)]
        side_args += [c_pad, w_ada, b_ada]
        vmem += 2 * d_model * tn * 4 + d_model * tn * 2
    return pl.pallas_call(
        functools.partial(_attn_kernel, mask_chunk=mask_chunk, fox=fox, n_cast=len(riders),
                          mod_rider=mod_rider is not None),
        grid=(batch, heads),
        in_specs=in_specs + side_in_specs,
        out_specs=[pl.BlockSpec((seq, dv), lambda b, h: (b, h))] + side_out_specs,
        out_shape=[jax.ShapeDtypeStruct((batch * seq, heads * dv), BF16)] + side_out_shape,
        scratch_shapes=[pltpu.VMEM((tq, tq), F32), pltpu.VMEM((tq, tq), F32),
                        pltpu.VMEM((nq, 1, tq), F32), pltpu.VMEM((nq, 1, tq), F32),
                        pltpu.VMEM((nq, dv, tq), F32)],
        compiler_params=_params(vmem),
        name="fox_attn" if fox else "mla_attn",
    )(*args, *side_args)


def _outproj_kernel(x_ref, oa_ref, ob_ref, wa_ref, wb_ref, gpost_ref, gt_ref, o_ref):
    y = jnp.dot(oa_ref[...], wa_ref[...], preferred_element_type=F32)
    y = y + jnp.dot(ob_ref[...], wb_ref[...], preferred_element_type=F32)
    o_ref[...] = x_ref[...] + gt_ref[0] * _rms(y, gpost_ref[...])


def _outproj(x, o_a, o_b, w, g_post, gt, *, seq, tm):
    t, d = x.shape
    ka, kb = o_a.shape[1], o_b.shape[1]
    assert ka == kb and w.shape == (ka + kb, d)
    tpb = seq // tm
    vmem = 4 * tm * d * 4 + 2 * tm * (ka + kb) * 2 + 2 * (ka + kb) * d * 2 + 2 * tm * d * 4
    return pl.pallas_call(
        _outproj_kernel,
        grid=(t // tm,),
        in_specs=[pl.BlockSpec((tm, d), lambda i: (i, 0)),
                  pl.BlockSpec((tm, ka), lambda i: (i, 0)),
                  pl.BlockSpec((tm, kb), lambda i: (i, 0)),
                  pl.BlockSpec((ka, d), lambda i: (0, 0)),
                  pl.BlockSpec((kb, d), lambda i: (1, 0)),
                  pl.BlockSpec((1, d), lambda i: (0, 0)),
                  pl.BlockSpec((1, 1, d), lambda i: (i // tpb, 0, 0))],
        out_specs=pl.BlockSpec((tm, d), lambda i: (i, 0)),
        out_shape=jax.ShapeDtypeStruct((t, d), F32),
        compiler_params=_params(vmem),
        name="out_proj",
    )(x, o_a, o_b, w, w, g_post, gt)


def _rope_tables(seq):
    half = MLA_ROPE // 2
    inv = ROPE_THETA ** (-np.arange(half, dtype=np.float64) / half)
    ang = np.arange(seq, dtype=np.float64)[:, None] * inv[None, :]
    cos, sin = np.cos(ang), np.sin(ang)
    zero = np.zeros_like(cos)
    pad = np.zeros((seq, LANES - 2 * half))
    return tuple(jnp.asarray(np.concatenate(parts, axis=1), F32)
                 for parts in ([cos, cos, pad], [-sin, zero, pad], [zero, sin, pad]))


def kernel(x, c, w_ada, b_ada, g_ffn1_pre, g_ffn1_post, w1_gate, w1_up, w1_down, g_mix_pre,
           g_mix_post, w_in, b_forget, g_q_a, w_uq, g_kv_a, w_ukv, w_o, g_ffn2_pre, g_ffn2_post,
           w2_gate, w2_up, w2_down):
    batch, seq, d = x.shape
    depth = w_ada.shape[0]
    t = batch * seq
    xt = x.reshape(t, d)
    c_pad = jnp.pad(c, ((0, 8 - batch), (0, 0)))
    cos_t, sin_lo, sin_hi = _rope_tables(seq)

    for l in range(depth):
        early = 5
        chunks = lambda mod: [mod[:batch, n * d:(n + 1) * d].reshape(batch, 1, d)
                              for n in range(mod.shape[1] // d)]
        sh1, sc1, gt1, sh2, sc2 = chunks(_ada(c_pad, w_ada[l], b_ada[l:l + 1], tn=1024, n=early * d))

        ffn1 = functools.partial(_ffn, xt, sh1, sc1, gt1, g_ffn1_pre[l:l + 1], g_ffn1_post[l:l + 1],
                                 seq=seq, tm=1024, res_weight=0.5)
        y_head, w1g, w1u, w1d = ffn1(w1_gate[l], w1_up[l], w1_down[l], tf=256, mode="head")
        xt = lax.dynamic_update_slice(ffn1(w1g, w1u, w1d, tf=512, mode="tail"), y_head, (0, 0))

        wi = w_in[l]
        fox0 = KPE_COL + MLA_ROPE
        fl0 = fox0 + 3 * FOX_WIDTH
        w_lat = jnp.concatenate(
            [wi[:, :fox0], jnp.zeros((d, LANES - MLA_ROPE), F32),
             wi[:, fl0:], jnp.zeros((d, LANES - FOX_HEADS), F32)], axis=1).astype(BF16)
        w_fox = wi[:, fox0:fl0].astype(BF16)

        wuq_p = jnp.pad(w_uq[l].reshape(MLA_Q_RANK, MLA_HEADS, MLA_NOPE + MLA_ROPE),
                        ((0, 0), (0, 0), (0, MLA_QK_PAD - MLA_NOPE - MLA_ROPE))
                        ).reshape(MLA_Q_RANK, MLA_HEADS * MLA_QK_PAD).astype(BF16)
        wukv = w_ukv[l].reshape(MLA_KV_RANK, MLA_HEADS, MLA_NOPE + MLA_V)
        wuk = wukv[:, :, :MLA_NOPE].reshape(MLA_KV_RANK, -1).astype(BF16)
        wuvt = wukv[:, :, MLA_NOPE:].reshape(MLA_KV_RANK, -1).T.astype(BF16)
        bf_pad = jnp.pad(b_forget[l:l + 1], ((0, 0), (0, LANES - FOX_HEADS)))
        fqk, fvt, q_mla, k_mla, vt_mla, cum_k, cum_q = _mixer_in(
            xt, sh2, sc2, g_mix_pre[l:l + 1], w_lat, w_fox, g_q_a[l:l + 1], g_kv_a[l:l + 1],
            wuq_p, wuk, wuvt, bf_pad, cos_t, sin_lo, sin_hi, batch=batch, seq=seq)

        o_mla, w2g, w2u = _attention(q_mla, k_mla, vt_mla, batch=batch, seq=seq, heads=MLA_HEADS,
                                     dk=MLA_QK_PAD, dv=MLA_V, q_col=0, k_col=0, mask_chunk=CHUNK,
                                     riders=(w2_gate[l], w2_up[l]))
        o_fox, w2d, wo, mod_late = _attention(
            fqk, fqk, fvt, cum_k, cum_q, batch=batch, seq=seq, heads=FOX_HEADS, dk=FOX_DIM, dv=FOX_DIM,
            q_col=0, k_col=FOX_HEADS, mask_chunk=1, riders=(w2_down[l], w_o[l]),
            mod_rider=(c_pad, w_ada[l], b_ada[l:l + 1], early * d))
        gt2, sh3, sc3, gt3 = chunks(mod_late)

        xt = _outproj(xt, o_mla, o_fox, wo, g_mix_post[l:l + 1], gt2, seq=seq, tm=512)

        xt = _ffn(xt, sh3, sc3, gt3, g_ffn2_pre[l:l + 1], g_ffn2_post[l:l + 1],
                  w2g, w2u, w2d, seq=seq, tm=1024, tf=512, res_weight=0.5)

    return xt.reshape(batch, seq, d)
```

```python
import functools
import math

import jax
import jax.numpy as jnp
import numpy as np
from jax import lax
from jax.experimental import pallas as pl
from jax.experimental.pallas import tpu as pltpu

F32 = jnp.float32
BF16 = jnp.bfloat16

V7X_VMEM_BYTES = 64 * 1024 * 1024
LANES = 128

EPS = 1e-6
ROPE_THETA = 10000.0
CHUNK = 64
MLA_HEADS = 8
MLA_Q_RANK = 512
MLA_KV_RANK = 256
MLA_NOPE = 128
MLA_ROPE = 64
MLA_V = 128
MLA_QK_PAD = 256
FOX_HEADS = 8
FOX_DIM = 128
FOX_WIDTH = FOX_HEADS * FOX_DIM
LATENT_WIDTH = 1024
KPE_COL = MLA_Q_RANK + MLA_KV_RANK
FLOGIT_COL = KPE_COL + LANES
ATTN_TILE = 512
PHASE_B_UNROLL = 14
LOG2E = math.log2(math.e)
MLA_Q_SCALE = LOG2E / math.sqrt(MLA_NOPE + MLA_ROPE)
FOX_Q_SCALE = LOG2E / math.sqrt(FOX_DIM)
NEG_BIG = -1e30

_NT = (((1,), (1,)), ((), ()))


def _params(vmem_bytes):
    limit = min(int(vmem_bytes * 1.25) + (4 << 20), V7X_VMEM_BYTES - (8 << 20))
    return pltpu.CompilerParams(vmem_limit_bytes=limit)


def _rms(x, g):
    return x * lax.rsqrt(jnp.mean(x * x, axis=-1, keepdims=True) + EPS) * g


STAT_ROWS = 8
APPLY_ROWS = 16


def _rms_stats(src_ref, stat_ref):
    tm, d = src_ref.shape

    def step(c, carry):
        rows = pl.ds(pl.multiple_of(c * STAT_ROWS, STAT_ROWS), STAT_ROWS)
        x = src_ref[rows, :]
        ms = jnp.sum(x * x, axis=-1, keepdims=True) * (1.0 / d)
        stat_ref[rows, :] = jnp.broadcast_to(lax.rsqrt(ms + EPS), (STAT_ROWS, LANES))
        return carry

    lax.fori_loop(0, tm // STAT_ROWS, step, 0, unroll=32)


def _for_row_blocks(tm, fn):
    def step(c, carry):
        fn(pl.ds(pl.multiple_of(c * APPLY_ROWS, APPLY_ROWS), APPLY_ROWS))
        return carry

    lax.fori_loop(0, tm // APPLY_ROWS, step, 0, unroll=4)


def _gated_norm_residual(x_ref, o_ref, stat_ref, coef_ref, gpost_ref, gt_ref, res_weight):
    tm, d = o_ref.shape
    coef_ref[0:1, :] = res_weight * gt_ref[0] * gpost_ref[...]
    _rms_stats(o_ref, stat_ref)

    def apply(rows):
        r = stat_ref[rows, :]
        for k in range(d // LANES):
            cols = slice(k * LANES, (k + 1) * LANES)
            o_ref[rows, cols] = x_ref[rows, cols] + o_ref[rows, cols] * r * coef_ref[0:1, cols]

    _for_row_blocks(tm, apply)


def _ada_kernel(c_ref, w_ref, b_ref, o_ref):
    c = c_ref[...]
    cond = (c * jax.nn.sigmoid(c)).astype(BF16)
    o_ref[...] = jnp.dot(cond, w_ref[...].astype(BF16), preferred_element_type=F32) + b_ref[...]


def _ada(c_pad, w, b, *, tn, n):
    m, d = c_pad.shape
    vmem = 2 * d * tn * 4 + d * tn * 2 + 4 * m * tn * 4
    return pl.pallas_call(
        _ada_kernel,
        grid=(n // tn,),
        in_specs=[pl.BlockSpec((m, d), lambda j: (0, 0)),
                  pl.BlockSpec((d, tn), lambda j: (0, j)),
                  pl.BlockSpec((1, tn), lambda j: (0, j))],
        out_specs=pl.BlockSpec((m, tn), lambda j: (0, j)),
        out_shape=jax.ShapeDtypeStruct((m, n), F32),
        compiler_params=_params(vmem),
        name="ada",
    )(c_pad, w, b)


def _ffn_kernel(*refs, res_weight, mode):
    x_ref, sh_ref, sc_ref, gt_ref, gpre_ref, gpost_ref, wg_ref, wu_ref, wd_ref = refs[:9]
    if mode == "head":
        o_ref, wg_out, wu_out, wd_out, h_ref, stat_ref, coef_ref = refs[9:]
    else:
        o_ref, h_ref, stat_ref, coef_ref = refs[9:]
    f = pl.program_id(1)
    active = pl.program_id(0) > 0 if mode == "tail" else True

    def swiglu_down(h):
        wg, wu, wd = wg_ref[...], wu_ref[...], wd_ref[...]
        if mode == "head":
            wg, wu, wd = wg.astype(BF16), wu.astype(BF16), wd.astype(BF16)
            wg_out[...], wu_out[...], wd_out[...] = wg, wu, wd
        g = jnp.dot(h, wg, preferred_element_type=F32)
        u = jnp.dot(h, wu, preferred_element_type=F32)
        a = (g * jax.nn.sigmoid(g) * u).astype(BF16)
        return jnp.dot(a, wd, preferred_element_type=F32)

    @pl.when(jnp.logical_and(active, f == 0))
    def _():
        h = (_rms(x_ref[...], gpre_ref[...]) * (1.0 + sc_ref[0]) + sh_ref[0]).astype(BF16)
        h_ref[...] = h
        o_ref[...] = swiglu_down(h)

    @pl.when(jnp.logical_and(active, f > 0))
    def _():
        o_ref[...] += swiglu_down(h_ref[...])

    @pl.when(jnp.logical_and(active, f == pl.num_programs(1) - 1))
    def _():
        _gated_norm_residual(x_ref, o_ref, stat_ref, coef_ref, gpost_ref, gt_ref, res_weight)

    if mode == "tail":
        @pl.when(jnp.logical_and(jnp.logical_not(active), f == 0))
        def _():
            o_ref[...] = jnp.zeros_like(o_ref)


def _ffn(x, sh, sc, gt, g_pre, g_post, wg, wu, wd, *, seq, tm, tf, res_weight, mode="all"):
    t, d = x.shape
    ff = wg.shape[1]
    tpb = seq // tm
    head = mode == "head"
    n_tiles = 1 if head else t // tm
    w_bytes = 4 if head else 2
    x_bufs = 1 if head else 2
    vmem = (2 * x_bufs * tm * d * 4 + tm * d * 2 + 6 * d * tf * w_bytes + 3 * tm * tf * 4
            + (9 * d * tf * 2 if head else 0))
    mod_spec = pl.BlockSpec((1, 1, d), lambda i, f: (i // tpb, 0, 0))
    gain_spec = pl.BlockSpec((1, d), lambda i, f: (0, 0))
    tile_kw = dict(pipeline_mode=pl.Buffered(1)) if head else {}
    if mode == "tail":
        f_of = lambda i, f: jnp.where(i == 0, 0, f)
        x_spec = pl.BlockSpec((tm, d), lambda i, f: (jnp.maximum(i, 1), 0))
    else:
        f_of = lambda i, f: f
        x_spec = pl.BlockSpec((tm, d), lambda i, f: (i, 0), **tile_kw)
    up_spec = pl.BlockSpec((d, tf), lambda i, f: (0, f_of(i, f)))
    down_spec = pl.BlockSpec((tf, d), lambda i, f: (f_of(i, f), 0))
    out_specs = pl.BlockSpec((tm, d), lambda i, f: (i, 0), **tile_kw)
    out_shape = jax.ShapeDtypeStruct((n_tiles * tm, d), F32)
    if head:
        out_specs = [out_specs, up_spec, up_spec, down_spec]
        out_shape = [out_shape, jax.ShapeDtypeStruct(wg.shape, BF16), jax.ShapeDtypeStruct(wu.shape, BF16),
                     jax.ShapeDtypeStruct(wd.shape, BF16)]
    return pl.pallas_call(
        functools.partial(_ffn_kernel, res_weight=res_weight, mode=mode),
        grid=(n_tiles, ff // tf),
        in_specs=[x_spec, mod_spec, mod_spec, mod_spec, gain_spec, gain_spec, up_spec, up_spec, down_spec],
        out_specs=out_specs,
        out_shape=out_shape,
        scratch_shapes=[pltpu.VMEM((tm, d), BF16), pltpu.VMEM((tm, LANES), F32), pltpu.VMEM((8, d), F32)],
        compiler_params=_params(vmem),
        name="ffn_" + mode,
    )(x, sh, sc, gt, g_pre, g_post, wg, wu, wd)


def _rot(r, cos_t, sin_lo, sin_hi):
    return r * cos_t + pltpu.roll(r, 96, 1) * sin_lo + pltpu.roll(r, 32, 1) * sin_hi


def _mla_fox_prep(lat, gq_ref, gkv_ref, wuq_ref, wuk_ref, wuvt_ref, bf_ref, cos_ref, slo_ref, shi_ref,
                  q_ref, k_ref, vt_ref, ck_ref, cq_ref, carry_ref):
    tm = lat.shape[0]
    cos_t, sin_lo, sin_hi = cos_ref[...], slo_ref[...], shi_ref[...]

    qn = _rms(lat[:, :MLA_Q_RANK], gq_ref[...]).astype(BF16)
    q = jnp.dot(qn, wuq_ref[...], preferred_element_type=F32) * MLA_Q_SCALE
    kvn = _rms(lat[:, MLA_Q_RANK:KPE_COL], gkv_ref[...]).astype(BF16)
    k_nope = jnp.dot(kvn, wuk_ref[...], preferred_element_type=F32)
    vt_ref[0, 0] = lax.dot_general(wuvt_ref[...], kvn, _NT, preferred_element_type=F32).astype(BF16)
    k_rope = _rot(lat[:, KPE_COL:KPE_COL + LANES], cos_t, sin_lo, sin_hi).astype(BF16)
    for h in range(MLA_HEADS):
        c0 = h * MLA_QK_PAD
        q_ref[:, c0:c0 + MLA_NOPE] = q[:, c0:c0 + MLA_NOPE].astype(BF16)
        q_ref[:, c0 + MLA_NOPE:c0 + MLA_QK_PAD] = _rot(
            q[:, c0 + MLA_NOPE:c0 + MLA_QK_PAD], cos_t, sin_lo, sin_hi).astype(BF16)
        k_ref[:, c0:c0 + MLA_NOPE] = k_nope[:, h * MLA_NOPE:(h + 1) * MLA_NOPE].astype(BF16)
        k_ref[:, c0 + MLA_NOPE:c0 + MLA_QK_PAD] = k_rope

    z = lat[:, FLOGIT_COL:FLOGIT_COL + LANES] + bf_ref[...]
    lane = lax.broadcasted_iota(jnp.int32, (tm, LANES), 1)
    log_f = jnp.where(lane < FOX_HEADS, jnp.minimum(z, 0.0) - jnp.log1p(jnp.exp(-jnp.abs(z))), 0.0)
    hi = log_f.astype(BF16).astype(F32)
    mid = (log_f - hi).astype(BF16).astype(F32)
    lo = (log_f - hi - mid).astype(BF16).astype(F32)
    pieces = hi + pltpu.roll(mid, FOX_HEADS, 1) + pltpu.roll(lo, 2 * FOX_HEADS, 1)
    row = lax.broadcasted_iota(jnp.int32, (tm, tm), 0)
    col = lax.broadcasted_iota(jnp.int32, (tm, tm), 1)
    tri = (col <= row).astype(BF16)
    part = jnp.dot(tri, pieces.astype(BF16), preferred_element_type=F32)
    part = part + pltpu.roll(part, LANES - FOX_HEADS, 1) + pltpu.roll(part, LANES - 2 * FOX_HEADS, 1)
    cum = jnp.where(lane < FOX_HEADS, part, 0.0) + carry_ref[...]
    carry_ref[...] = cum[tm - 1:tm, :]
    cum2 = cum * LOG2E
    cq_ref[0, 0] = cum2.T[:FOX_HEADS, :]
    for h in range(FOX_HEADS):
        ck_ref[0, h] = jnp.broadcast_to(cum2[:, h:h + 1], (tm, LANES))


def _mixer_in_kernel(x_ref, sh_ref, sc_ref, gpre_ref, wlat_ref, wfox_ref,
                     gq_ref, gkv_ref, wuq_ref, wuk_ref, wuvt_ref, bf_ref, cos_ref, slo_ref, shi_ref,
                     fqk_ref, fvt_ref, q_ref, k_ref, vt_ref, ck_ref, cq_ref, carry_ref, *, tiles_per_batch):
    @pl.when(pl.program_id(0) % tiles_per_batch == 0)
    def _():
        carry_ref[...] = jnp.zeros_like(carry_ref)

    h = (_rms(x_ref[...], gpre_ref[...]) * (1.0 + sc_ref[0]) + sh_ref[0]).astype(BF16)
    fw = FOX_WIDTH
    lat = jnp.dot(h, wlat_ref[...], preferred_element_type=F32)
    p = jnp.dot(h, wfox_ref[...], preferred_element_type=F32)
    p = pltpu.roll(p, p.shape[1] - MLA_ROPE, 1)
    fqk_ref[:, :fw] = (p[:, :fw] * FOX_Q_SCALE).astype(BF16)
    fqk_ref[:, fw:] = p[:, fw:2 * fw].astype(BF16)
    fvt_ref[0, 0] = p[:, 2 * fw:3 * fw].T.astype(BF16)
    _mla_fox_prep(lat, gq_ref, gkv_ref, wuq_ref, wuk_ref, wuvt_ref, bf_ref, cos_ref, slo_ref, shi_ref,
                  q_ref, k_ref, vt_ref, ck_ref, cq_ref, carry_ref)


def _mixer_in(x, sh, sc, g_pre, w_lat, w_fox, g_q, g_kv, wuq, wuk, wuvt, b_forget, cos_t, sin_lo, sin_hi,
              *, batch, seq):
    t, d = x.shape
    tm = ATTN_TILE
    tn = LATENT_WIDTH
    tpb = seq // tm
    qk_w = MLA_HEADS * MLA_QK_PAD
    v_w = MLA_HEADS * MLA_V
    assert tn == FOX_WIDTH and w_lat.shape == (d, tn) and w_fox.shape[1] >= MLA_ROPE + 3 * FOX_WIDTH
    resident_bytes = (w_lat.size + w_fox.size + wuq.size + wuk.size + wuvt.size) * 2
    vmem = (2 * tm * d * 4 + tm * d * 2 + resident_bytes + 2 * tm * (2 * tn + FOX_WIDTH) * 2
            + 2 * tm * (2 * qk_w + v_w) * 2 + 2 * FOX_HEADS * tm * LANES * 4 + 6 * tm * LANES * 4
            + tm * (tn + qk_w) * 4)
    mod_spec = pl.BlockSpec((1, 1, d), lambda i: (i // tpb, 0, 0))
    resident = lambda a: pl.BlockSpec(a.shape, lambda i: (0, 0), pipeline_mode=pl.Buffered(1))
    small = lambda a: pl.BlockSpec(a.shape, lambda i: (0, 0))
    tab_spec = pl.BlockSpec((tm, LANES), lambda i: (i % tpb, 0))
    return pl.pallas_call(
        functools.partial(_mixer_in_kernel, tiles_per_batch=tpb),
        grid=(t // tm,),
        in_specs=[pl.BlockSpec((tm, d), lambda i: (i, 0)),
                  mod_spec, mod_spec, small(g_pre), resident(w_lat), resident(w_fox),
                  small(g_q), small(g_kv), resident(wuq), resident(wuk), resident(wuvt), small(b_forget),
                  tab_spec, tab_spec, tab_spec],
        out_specs=[pl.BlockSpec((tm, 2 * tn), lambda i: (i, 0)),
                   pl.BlockSpec((1, 1, FOX_WIDTH, tm), lambda i: (i // tpb, i % tpb, 0, 0)),
                   pl.BlockSpec((tm, qk_w), lambda i: (i, 0)),
                   pl.BlockSpec((tm, qk_w), lambda i: (i, 0)),
                   pl.BlockSpec((1, 1, v_w, tm), lambda i: (i // tpb, i % tpb, 0, 0)),
                   pl.BlockSpec((1, FOX_HEADS, tm, LANES), lambda i: (i // tpb, 0, i % tpb, 0)),
                   pl.BlockSpec((1, 1, FOX_HEADS, tm), lambda i: (i // tpb, i % tpb, 0, 0))],
        out_shape=[jax.ShapeDtypeStruct((t, 2 * tn), BF16),
                   jax.ShapeDtypeStruct((batch, tpb, FOX_WIDTH, tm), BF16),
                   jax.ShapeDtypeStruct((t, qk_w), BF16),
                   jax.ShapeDtypeStruct((t, qk_w), BF16),
                   jax.ShapeDtypeStruct((batch, tpb, v_w, tm), BF16),
                   jax.ShapeDtypeStruct((batch, FOX_HEADS, seq, LANES), F32),
                   jax.ShapeDtypeStruct((batch, tpb, FOX_HEADS, tm), F32)],
        scratch_shapes=[pltpu.VMEM((1, LANES), F32)],
        compiler_params=_params(vmem),
        name="mixer_in",
    )(x, sh, sc, g_pre, w_lat, w_fox, g_q, g_kv, wuq, wuk, wuvt, b_forget, cos_t, sin_lo, sin_hi)


def _attn_kernel(*refs, mask_chunk, fox, n_cast, mod_rider):
    n_in = 5 if fox else 3
    n_mod = 3 if mod_rider else 0
    n_side_in = n_cast + n_mod
    side_in = refs[n_in:n_in + n_side_in]
    side_out = refs[n_in + n_side_in + 1:n_in + n_side_in + 1 + n_cast + (1 if mod_rider else 0)]
    refs = refs[:n_in] + (refs[n_in + n_side_in],) + refs[n_in + n_side_in + 1 + len(side_out):]
    if fox:
        q_ref, k_ref, vt_ref, ck_ref, cq_ref, o_ref, s0_ref, s1_ref, m_ref, l_ref, acc_ref = refs
    else:
        q_ref, k_ref, vt_ref, o_ref, s0_ref, s1_ref, m_ref, l_ref, acc_ref = refs
    for src, dst in zip(side_in[:n_cast], side_out[:n_cast]):
        dst[...] = src[...].astype(BF16)
    if mod_rider:
        c_ref, wada_ref, bada_ref = side_in[n_cast:]
        _ada_kernel(c_ref, wada_ref, bada_ref, side_out[n_cast])
    head = pl.program_id(1)
    tq = ATTN_TILE
    nq = q_ref.shape[0] // tq
    n_low = nq * (nq - 1) // 2
    assert PHASE_B_UNROLL % 2 == 0 and n_low % PHASE_B_UNROLL == 0
    bufs = (s0_ref, s1_ref)

    def rows(t):
        return pl.ds(pl.multiple_of(t * tq, tq), tq)

    def scores(i, j, s_ref):
        s = lax.dot_general(k_ref[rows(j), :], q_ref[rows(i), :], _NT, preferred_element_type=F32)
        if fox:
            cum_q = cq_ref[0, i, pl.ds(head, 1), :]
            cum_k = ck_ref[0, 0, rows(j), :]
            s = s + (cum_q - jnp.concatenate([cum_k] * (tq // LANES), axis=1))
        s_ref[...] = s

    half = tq // 2

    def scores_diagonal(i, s_ref):
        lo, hi = i * tq, i * tq + half
        s_a = lax.dot_general(k_ref[lo:hi, :], q_ref[lo:lo + tq, :], _NT, preferred_element_type=F32)
        s_b = lax.dot_general(k_ref[hi:hi + half, :], q_ref[hi:hi + half, :], _NT,
                              preferred_element_type=F32)
        if fox:
            cum_q = cq_ref[0, i, pl.ds(head, 1), :]
            s_a = s_a + (cum_q - jnp.concatenate([ck_ref[0, 0, lo:hi, :]] * (tq // LANES), axis=1))
            s_b = s_b + (cum_q[:, half:]
                         - jnp.concatenate([ck_ref[0, 0, hi:hi + half, :]] * (half // LANES), axis=1))
        src = lax.broadcasted_iota(jnp.int32, (half, half), 0) // mask_chunk
        dst = lax.broadcasted_iota(jnp.int32, (half, half), 1) // mask_chunk
        s_ref[:half, :half] = jnp.where(src <= dst, s_a[:, :half], NEG_BIG)
        s_ref[:half, half:] = s_a[:, half:]
        s_ref[half:, half:] = jnp.where(src <= dst, s_b, NEG_BIG)

    def accumulate_diagonal(i, s_ref):
        s_a = s_ref[:half, :]
        s_b = s_ref[half:, half:]
        m_a = jnp.max(s_a, axis=0, keepdims=True)
        m_hi = jnp.maximum(m_a[:, half:], jnp.max(s_b, axis=0, keepdims=True))
        m = jnp.concatenate([m_a[:, :half], m_hi], axis=1)
        p_a = jnp.exp2(s_a - m)
        p_b = jnp.exp2(s_b - m_hi)
        l_a = jnp.sum(p_a, axis=0, keepdims=True)
        m_ref[i] = m
        l_ref[i] = jnp.concatenate([l_a[:, :half], l_a[:, half:] + jnp.sum(p_b, axis=0, keepdims=True)],
                                   axis=1)
        acc_a = jnp.dot(vt_ref[0, i, :, :half], p_a.astype(BF16), preferred_element_type=F32)
        acc_b = jnp.dot(vt_ref[0, i, :, half:], p_b.astype(BF16), preferred_element_type=F32)
        acc_ref[i, :, :half] = acc_a[:, :half]
        acc_ref[i, :, half:] = acc_a[:, half:] + acc_b

    def accumulate(i, j, s_ref):
        s = s_ref[...]
        m_prev = m_ref[i]
        m_new = jnp.maximum(m_prev, jnp.max(s, axis=0, keepdims=True))
        alpha = jnp.exp2(m_prev - m_new)
        p = jnp.exp2(s - m_new)
        l_ref[i] = alpha * l_ref[i] + jnp.sum(p, axis=0, keepdims=True)
        acc_ref[i] = alpha * acc_ref[i] + jnp.dot(vt_ref[0, j], p.astype(BF16),
                                                  preferred_element_type=F32)
        m_ref[i] = m_new

    scores_diagonal(0, bufs[0])
    for i in range(nq):
        if i + 1 < nq:
            scores_diagonal(i + 1, bufs[(i + 1) % 2])
        else:
            scores(1, 0, bufs[(i + 1) % 2])
        accumulate_diagonal(i, bufs[i % 2])

    def advance(i, j):
        wraps = j + 1 == i
        i_nxt = jnp.minimum(jnp.where(wraps, i + 1, i), nq - 1)
        return i_nxt, jnp.where(wraps, 0, j + 1)

    def body(_, carry):
        i, j = carry
        for u in range(PHASE_B_UNROLL):
            i_nxt, j_nxt = advance(i, j)
            scores(i_nxt, j_nxt, bufs[(nq + u + 1) % 2])
            accumulate(i, j, bufs[(nq + u) % 2])
            i, j = i_nxt, j_nxt
        return i, j

    lax.fori_loop(0, n_low // PHASE_B_UNROLL, body, (jnp.int32(1), jnp.int32(0)))

    for i in range(nq):
        o_ref[i * tq:(i + 1) * tq, :] = (acc_ref[i] / l_ref[i]).T.astype(o_ref.dtype)


def _attention(q_arr, k_arr, vt_arr, cum_k=None, cum_q=None, *, batch, seq, heads, dk, dv,
               q_col, k_col, mask_chunk, riders=(), mod_rider=None):
    fox = cum_k is not None
    tq = ATTN_TILE
    nq = seq // tq
    steps = batch * heads
    in_specs = [pl.BlockSpec((seq, dk), lambda b, h: (b, q_col + h)),
                pl.BlockSpec((seq, dk), lambda b, h: (b, k_col + h)),
                pl.BlockSpec((1, nq, dv, tq), lambda b, h: (b, 0, h, 0))]
    args = [q_arr, k_arr, vt_arr]
    if fox:
        in_specs += [pl.BlockSpec((1, 1, seq, LANES), lambda b, h: (b, h, 0, 0)),
                     pl.BlockSpec((1, nq, heads, tq), lambda b, h: (b, 0, 0, 0))]
        args += [cum_k, cum_q]
    slab_specs = [pl.BlockSpec((w.shape[0] // steps, w.shape[1]), lambda b, h: (b * heads + h, 0))
                  for w in riders]
    vmem = (2 * seq * (2 * dk + 2 * dv) * 2 + 2 * seq * LANES * 4 + 2 * seq * heads * 4
            + seq * dv * 4 + 8 * tq * tq * 4 + sum(2 * 6 * w.size // steps for w in riders))
    side_in_specs, side_out_specs, side_args = list(slab_specs), list(slab_specs), list(riders)
    side_out_shape = [jax.ShapeDtypeStruct(w.shape, BF16) for w in riders]
    if mod_rider is not None:
        c_pad, w_ada, b_ada, first_col = mod_rider
        m, d_model = c_pad.shape
        tn = (w_ada.shape[1] - first_col) // steps
        assert first_col % tn == 0
        col = lambda b, h: (0, first_col // tn + b * heads + h)
        side_in_specs += [pl.BlockSpec((m, d_model), lambda b, h: (0, 0)),
                          pl.BlockSpec((d_model, tn), col), pl.BlockSpec((1, tn), col)]
        side_out_specs += [pl.BlockSpec((m, tn), lambda b, h: (0, b * heads + h))]
        side_out_shape += [jax.ShapeDtypeStruct((m, tn * steps), F32)]
        side_args += [c_pad, w_ada, b_ada]
        vmem += 2 * d_model * tn * 4 + d_model * tn * 2
    return pl.pallas_call(
        functools.partial(_attn_kernel, mask_chunk=mask_chunk, fox=fox, n_cast=len(riders),
                          mod_rider=mod_rider is not None),
        grid=(batch, heads),
        in_specs=in_specs + side_in_specs,
        out_specs=[pl.BlockSpec((seq, dv), lambda b, h: (b, h))] + side_out_specs,
        out_shape=[jax.ShapeDtypeStruct((batch * seq, heads * dv), BF16)] + side_out_shape,
        scratch_shapes=[pltpu.VMEM((tq, tq), F32), pltpu.VMEM((tq, tq), F32),
                        pltpu.VMEM((nq, 1, tq), F32), pltpu.VMEM((nq, 1, tq), F32),
                        pltpu.VMEM((nq, dv, tq), F32)],
        compiler_params=_params(vmem),
        name="fox_attn" if fox else "mla_attn",
    )(*args, *side_args)


def _outproj_kernel(x_ref, oa_ref, ob_ref, wa_ref, wb_ref, gpost_ref, gt_ref, o_ref):
    y = jnp.dot(oa_ref[...], wa_ref[...], preferred_element_type=F32)
    y = y + jnp.dot(ob_ref[...], wb_ref[...], preferred_element_type=F32)
    o_ref[...] = x_ref[...] + gt_ref[0] * _rms(y, gpost_ref[...])


def _outproj(x, o_a, o_b, w, g_post, gt, *, seq, tm):
    t, d = x.shape
    ka, kb = o_a.shape[1], o_b.shape[1]
    assert ka == kb and w.shape == (ka + kb, d)
    tpb = seq // tm
    vmem = 4 * tm * d * 4 + 2 * tm * (ka + kb) * 2 + 2 * (ka + kb) * d * 2 + 2 * tm * d * 4
    return pl.pallas_call(
        _outproj_kernel,
        grid=(t // tm,),
        in_specs=[pl.BlockSpec((tm, d), lambda i: (i, 0)),
                  pl.BlockSpec((tm, ka), lambda i: (i, 0)),
                  pl.BlockSpec((tm, kb), lambda i: (i, 0)),
                  pl.BlockSpec((ka, d), lambda i: (0, 0)),
                  pl.BlockSpec((kb, d), lambda i: (1, 0)),
                  pl.BlockSpec((1, d), lambda i: (0, 0)),
                  pl.BlockSpec((1, 1, d), lambda i: (i // tpb, 0, 0))],
        out_specs=pl.BlockSpec((tm, d), lambda i: (i, 0)),
        out_shape=jax.ShapeDtypeStruct((t, d), F32),
        compiler_params=_params(vmem),
        name="out_proj",
    )(x, o_a, o_b, w, w, g_post, gt)


def _rope_tables(seq):
    half = MLA_ROPE // 2
    inv = ROPE_THETA ** (-np.arange(half, dtype=np.float64) / half)
    ang = np.arange(seq, dtype=np.float64)[:, None] * inv[None, :]
    cos, sin = np.cos(ang), np.sin(ang)
    zero = np.zeros_like(cos)
    pad = np.zeros((seq, LANES - 2 * half))
    return tuple(jnp.asarray(np.concatenate(parts, axis=1), F32)
                 for parts in ([cos, cos, pad], [-sin, zero, pad], [zero, sin, pad]))


def kernel(x, c, w_ada, b_ada, g_ffn1_pre, g_ffn1_post, w1_gate, w1_up, w1_down, g_mix_pre,
           g_mix_post, w_in, b_forget, g_q_a, w_uq, g_kv_a, w_ukv, w_o, g_ffn2_pre, g_ffn2_post,
           w2_gate, w2_up, w2_down):
    batch, seq, d = x.shape
    depth = w_ada.shape[0]
    t = batch * seq
    xt = x.reshape(t, d)
    c_pad = jnp.pad(c, ((0, 8 - batch), (0, 0)))
    cos_t, sin_lo, sin_hi = _rope_tables(seq)

    for l in range(depth):
        early = 5
        chunks = lambda mod: [mod[:batch, n * d:(n + 1) * d].reshape(batch, 1, d)
                              for n in range(mod.shape[1] // d)]
        sh1, sc1, gt1, sh2, sc2 = chunks(_ada(c_pad, w_ada[l], b_ada[l:l + 1], tn=1024, n=early * d))

        ffn1 = functools.partial(_ffn, xt, sh1, sc1, gt1, g_ffn1_pre[l:l + 1], g_ffn1_post[l:l + 1],
                                 seq=seq, tm=1024, res_weight=0.5)
        y_head, w1g, w1u, w1d = ffn1(w1_gate[l], w1_up[l], w1_down[l], tf=256, mode="head")
        xt = lax.dynamic_update_slice(ffn1(w1g, w1u, w1d, tf=512, mode="tail"), y_head, (0, 0))

        wi = w_in[l]
        fox0 = KPE_COL + MLA_ROPE
        fl0 = fox0 + 3 * FOX_WIDTH
        w_lat = jnp.concatenate(
            [wi[:, :fox0], jnp.zeros((d, LANES - MLA_ROPE), F32),
             wi[:, fl0:], jnp.zeros((d, LANES - FOX_HEADS), F32)], axis=1).astype(BF16)
        w_fox = jnp.pad(wi[:, KPE_COL:], ((0, 0), (0, -(wi.shape[1] - KPE_COL) % LANES))).astype(BF16)

        wuq_p = jnp.pad(w_uq[l].reshape(MLA_Q_RANK, MLA_HEADS, MLA_NOPE + MLA_ROPE),
                        ((0, 0), (0, 0), (0, MLA_QK_PAD - MLA_NOPE - MLA_ROPE))
                        ).reshape(MLA_Q_RANK, MLA_HEADS * MLA_QK_PAD).astype(BF16)
        wukv = w_ukv[l].reshape(MLA_KV_RANK, MLA_HEADS, MLA_NOPE + MLA_V)
        wuk = wukv[:, :, :MLA_NOPE].reshape(MLA_KV_RANK, -1).astype(BF16)
        wuvt = wukv[:, :, MLA_NOPE:].reshape(MLA_KV_RANK, -1).T.astype(BF16)
        bf_pad = jnp.pad(b_forget[l:l + 1], ((0, 0), (0, LANES - FOX_HEADS)))
        fqk, fvt, q_mla, k_mla, vt_mla, cum_k, cum_q = _mixer_in(
            xt, sh2, sc2, g_mix_pre[l:l + 1], w_lat, w_fox, g_q_a[l:l + 1], g_kv_a[l:l + 1],
            wuq_p, wuk, wuvt, bf_pad, cos_t, sin_lo, sin_hi, batch=batch, seq=seq)

        o_mla, w2g, w2u = _attention(q_mla, k_mla, vt_mla, batch=batch, seq=seq, heads=MLA_HEADS,
                                     dk=MLA_QK_PAD, dv=MLA_V, q_col=0, k_col=0, mask_chunk=CHUNK,
                                     riders=(w2_gate[l], w2_up[l]))
        o_fox, w2d, wo, mod_late = _attention(
            fqk, fqk, fvt, cum_k, cum_q, batch=batch, seq=seq, heads=FOX_HEADS, dk=FOX_DIM, dv=FOX_DIM,
            q_col=0, k_col=FOX_HEADS, mask_chunk=1, riders=(w2_down[l], w_o[l]),
            mod_rider=(c_pad, w_ada[l], b_ada[l:l + 1], early * d))
        gt2, sh3, sc3, gt3 = chunks(mod_late)

        xt = _outproj(xt, o_mla, o_fox, wo, g_mix_post[l:l + 1], gt2, seq=seq, tm=512)

        xt = _ffn(xt, sh3, sc3, gt3, g_ffn2_pre[l:l + 1], g_ffn2_post[l:l + 1],
                  w2g, w2u, w2d, seq=seq, tm=1024, tf=512, res_weight=0.5)

    return xt.reshape(batch, seq, d)
```

```python
import functools
import math

import jax
import jax.numpy as jnp
import numpy as np
from jax import lax
from jax.experimental import pallas as pl
from jax.experimental.pallas import tpu as pltpu

F32 = jnp.float32
BF16 = jnp.bfloat16

V7X_VMEM_BYTES = 64 * 1024 * 1024
LANES = 128

EPS = 1e-6
ROPE_THETA = 10000.0
CHUNK = 64
MLA_HEADS = 8
MLA_Q_RANK = 512
MLA_KV_RANK = 256
MLA_NOPE = 128
MLA_ROPE = 64
MLA_V = 128
MLA_QK_PAD = 256
FOX_HEADS = 8
FOX_DIM = 128
FOX_WIDTH = FOX_HEADS * FOX_DIM
LATENT_WIDTH = 1024
KPE_COL = MLA_Q_RANK + MLA_KV_RANK
FLOGIT_COL = KPE_COL + LANES
ATTN_TILE = 512
PHASE_B_UNROLL = 14
RIDER_SLABS = 64
LOG2E = math.log2(math.e)
MLA_Q_SCALE = LOG2E / math.sqrt(MLA_NOPE + MLA_ROPE)
FOX_Q_SCALE = LOG2E / math.sqrt(FOX_DIM)
NEG_BIG = -1e30

_NT = (((1,), (1,)), ((), ()))


def _params(vmem_bytes):
    limit = min(int(vmem_bytes * 1.25) + (4 << 20), V7X_VMEM_BYTES - (8 << 20))
    return pltpu.CompilerParams(vmem_limit_bytes=limit)


def _rms(x, g):
    return x * lax.rsqrt(jnp.mean(x * x, axis=-1, keepdims=True) + EPS) * g


STAT_ROWS = 8
APPLY_ROWS = 16


def _rms_stats(src_ref, stat_ref):
    tm, d = src_ref.shape

    def step(c, carry):
        rows = pl.ds(pl.multiple_of(c * STAT_ROWS, STAT_ROWS), STAT_ROWS)
        x = src_ref[rows, :]
        ms = jnp.sum(x * x, axis=-1, keepdims=True) * (1.0 / d)
        stat_ref[rows, :] = jnp.broadcast_to(lax.rsqrt(ms + EPS), (STAT_ROWS, LANES))
        return carry

    lax.fori_loop(0, tm // STAT_ROWS, step, 0, unroll=32)


def _for_row_blocks(tm, fn):
    def step(c, carry):
        fn(pl.ds(pl.multiple_of(c * APPLY_ROWS, APPLY_ROWS), APPLY_ROWS))
        return carry

    lax.fori_loop(0, tm // APPLY_ROWS, step, 0, unroll=4)


def _gated_norm_residual(x_ref, o_ref, stat_ref, coef_ref, gpost_ref, gt_ref, res_weight):
    tm, d = o_ref.shape
    coef_ref[0:1, :] = res_weight * gt_ref[0] * gpost_ref[...]
    _rms_stats(o_ref, stat_ref)

    def apply(rows):
        r = stat_ref[rows, :]
        for k in range(d // LANES):
            cols = slice(k * LANES, (k + 1) * LANES)
            o_ref[rows, cols] = x_ref[rows, cols] + o_ref[rows, cols] * r * coef_ref[0:1, cols]

    _for_row_blocks(tm, apply)


def _ada_kernel(c_ref, w_ref, b_ref, o_ref):
    c = c_ref[...]
    cond = (c * jax.nn.sigmoid(c)).astype(BF16)
    o_ref[...] = jnp.dot(cond, w_ref[...].astype(BF16), preferred_element_type=F32) + b_ref[...]


def _ada(c_pad, w, b, *, tn, n):
    m, d = c_pad.shape
    vmem = 2 * d * tn * 4 + d * tn * 2 + 4 * m * tn * 4
    return pl.pallas_call(
        _ada_kernel,
        grid=(n // tn,),
        in_specs=[pl.BlockSpec((m, d), lambda j: (0, 0)),
                  pl.BlockSpec((d, tn), lambda j: (0, j)),
                  pl.BlockSpec((1, tn), lambda j: (0, j))],
        out_specs=pl.BlockSpec((m, tn), lambda j: (0, j)),
        out_shape=jax.ShapeDtypeStruct((m, n), F32),
        compiler_params=_params(vmem),
        name="ada",
    )(c_pad, w, b)


def _ffn_kernel(*refs, res_weight, mode):
    x_ref, sh_ref, sc_ref, gt_ref, gpre_ref, gpost_ref, wg_ref, wu_ref, wd_ref = refs[:9]
    if mode == "head":
        o_ref, wg_out, wu_out, wd_out, h_ref, stat_ref, coef_ref = refs[9:]
    elif mode == "tail":
        rider_in, o_ref, rider_out, h_ref, stat_ref, coef_ref = refs[9:]
        rider_out[...] = rider_in[...].astype(BF16)
    else:
        o_ref, h_ref, stat_ref, coef_ref = refs[9:]
    f = pl.program_id(1)
    active = pl.program_id(0) > 0 if mode == "tail" else True

    def swiglu_down(h):
        wg, wu, wd = wg_ref[...], wu_ref[...], wd_ref[...]
        if mode == "head":
            wg, wu, wd = wg.astype(BF16), wu.astype(BF16), wd.astype(BF16)
            wg_out[...], wu_out[...], wd_out[...] = wg, wu, wd
        g = jnp.dot(h, wg, preferred_element_type=F32)
        u = jnp.dot(h, wu, preferred_element_type=F32)
        a = (g * jax.nn.sigmoid(g) * u).astype(BF16)
        return jnp.dot(a, wd, preferred_element_type=F32)

    @pl.when(jnp.logical_and(active, f == 0))
    def _():
        h = (_rms(x_ref[...], gpre_ref[...]) * (1.0 + sc_ref[0]) + sh_ref[0]).astype(BF16)
        h_ref[...] = h
        o_ref[...] = swiglu_down(h)

    @pl.when(jnp.logical_and(active, f > 0))
    def _():
        o_ref[...] += swiglu_down(h_ref[...])

    @pl.when(jnp.logical_and(active, f == pl.num_programs(1) - 1))
    def _():
        _gated_norm_residual(x_ref, o_ref, stat_ref, coef_ref, gpost_ref, gt_ref, res_weight)

    if mode == "tail":
        @pl.when(jnp.logical_and(jnp.logical_not(active), f == 0))
        def _():
            o_ref[...] = jnp.zeros_like(o_ref)


def _ffn(x, sh, sc, gt, g_pre, g_post, wg, wu, wd, *, seq, tm, tf, res_weight, mode="all", rider=None):
    t, d = x.shape
    ff = wg.shape[1]
    tpb = seq // tm
    head = mode == "head"
    n_tiles = 1 if head else t // tm
    w_bytes = 4 if head else 2
    x_bufs = 1 if head else 2
    vmem = (2 * x_bufs * tm * d * 4 + tm * d * 2 + 6 * d * tf * w_bytes + 3 * tm * tf * 4
            + (9 * d * tf * 2 if head else 0))
    mod_spec = pl.BlockSpec((1, 1, d), lambda i, f: (i // tpb, 0, 0))
    gain_spec = pl.BlockSpec((1, d), lambda i, f: (0, 0))
    tile_kw = dict(pipeline_mode=pl.Buffered(1)) if head else {}
    if mode == "tail":
        f_of = lambda i, f: jnp.where(i == 0, 0, f)
        x_spec = pl.BlockSpec((tm, d), lambda i, f: (jnp.maximum(i, 1), 0))
    else:
        f_of = lambda i, f: f
        x_spec = pl.BlockSpec((tm, d), lambda i, f: (i, 0), **tile_kw)
    up_spec = pl.BlockSpec((d, tf), lambda i, f: (0, f_of(i, f)))
    down_spec = pl.BlockSpec((tf, d), lambda i, f: (f_of(i, f), 0))
    out_specs = pl.BlockSpec((tm, d), lambda i, f: (i, 0), **tile_kw)
    out_shape = jax.ShapeDtypeStruct((n_tiles * tm, d), F32)
    in_specs = [x_spec, mod_spec, mod_spec, mod_spec, gain_spec, gain_spec, up_spec, up_spec, down_spec]
    args = [x, sh, sc, gt, g_pre, g_post, wg, wu, wd]
    if mode == "tail":
        per_tile = RIDER_SLABS // n_tiles
        slab_spec = pl.BlockSpec((rider.shape[0] // RIDER_SLABS, rider.shape[1]),
                                 lambda i, f: (i * per_tile + jnp.minimum(f, per_tile - 1), 0))
        in_specs.append(slab_spec)
        args.append(rider)
        out_specs = [out_specs, slab_spec]
        out_shape = [out_shape, jax.ShapeDtypeStruct(rider.shape, BF16)]
    if head:
        out_specs = [out_specs, up_spec, up_spec, down_spec]
        out_shape = [out_shape, jax.ShapeDtypeStruct(wg.shape, BF16), jax.ShapeDtypeStruct(wu.shape, BF16),
                     jax.ShapeDtypeStruct(wd.shape, BF16)]
    return pl.pallas_call(
        functools.partial(_ffn_kernel, res_weight=res_weight, mode=mode),
        grid=(n_tiles, ff // tf),
        in_specs=in_specs,
        out_specs=out_specs,
        out_shape=out_shape,
        scratch_shapes=[pltpu.VMEM((tm, d), BF16), pltpu.VMEM((tm, LANES), F32), pltpu.VMEM((8, d), F32)],
        compiler_params=_params(vmem),
        name="ffn_" + mode,
    )(*args)


def _rot(r, cos_t, sin_lo, sin_hi):
    return r * cos_t + pltpu.roll(r, 96, 1) * sin_lo + pltpu.roll(r, 32, 1) * sin_hi


def _mla_fox_prep(lat, gq_ref, gkv_ref, wuq_ref, wuk_ref, wuvt_ref, bf_ref, cos_ref, slo_ref, shi_ref,
                  q_ref, k_ref, vt_ref, ck_ref, cq_ref, carry_ref):
    tm = lat.shape[0]
    cos_t, sin_lo, sin_hi = cos_ref[...], slo_ref[...], shi_ref[...]

    qn = _rms(lat[:, :MLA_Q_RANK], gq_ref[...]).astype(BF16)
    q = jnp.dot(qn, wuq_ref[...], preferred_element_type=F32) * MLA_Q_SCALE
    kvn = _rms(lat[:, MLA_Q_RANK:KPE_COL], gkv_ref[...]).astype(BF16)
    k_nope = jnp.dot(kvn, wuk_ref[...], preferred_element_type=F32)
    vt_ref[0, 0] = lax.dot_general(wuvt_ref[...], kvn, _NT, preferred_element_type=F32).astype(BF16)
    k_rope = _rot(lat[:, KPE_COL:KPE_COL + LANES], cos_t, sin_lo, sin_hi).astype(BF16)
    for h in range(MLA_HEADS):
        c0 = h * MLA_QK_PAD
        q_ref[:, c0:c0 + MLA_NOPE] = q[:, c0:c0 + MLA_NOPE].astype(BF16)
        q_ref[:, c0 + MLA_NOPE:c0 + MLA_QK_PAD] = _rot(
            q[:, c0 + MLA_NOPE:c0 + MLA_QK_PAD], cos_t, sin_lo, sin_hi).astype(BF16)
        k_ref[:, c0:c0 + MLA_NOPE] = k_nope[:, h * MLA_NOPE:(h + 1) * MLA_NOPE].astype(BF16)
        k_ref[:, c0 + MLA_NOPE:c0 + MLA_QK_PAD] = k_rope

    z = lat[:, FLOGIT_COL:FLOGIT_COL + LANES] + bf_ref[...]
    lane = lax.broadcasted_iota(jnp.int32, (tm, LANES), 1)
    log_f = jnp.where(lane < FOX_HEADS, jnp.minimum(z, 0.0) - jnp.log1p(jnp.exp(-jnp.abs(z))), 0.0)
    hi = log_f.astype(BF16).astype(F32)
    mid = (log_f - hi).astype(BF16).astype(F32)
    lo = (log_f - hi - mid).astype(BF16).astype(F32)
    pieces = hi + pltpu.roll(mid, FOX_HEADS, 1) + pltpu.roll(lo, 2 * FOX_HEADS, 1)
    row = lax.broadcasted_iota(jnp.int32, (tm, tm), 0)
    col = lax.broadcasted_iota(jnp.int32, (tm, tm), 1)
    tri = (col <= row).astype(BF16)
    part = jnp.dot(tri, pieces.astype(BF16), preferred_element_type=F32)
    part = part + pltpu.roll(part, LANES - FOX_HEADS, 1) + pltpu.roll(part, LANES - 2 * FOX_HEADS, 1)
    cum = jnp.where(lane < FOX_HEADS, part, 0.0) + carry_ref[...]
    carry_ref[...] = cum[tm - 1:tm, :]
    cum2 = cum * LOG2E
    cq_ref[0, 0] = cum2.T[:FOX_HEADS, :]
    for h in range(FOX_HEADS):
        ck_ref[0, h] = jnp.broadcast_to(cum2[:, h:h + 1], (tm, LANES))


def _mixer_in_kernel(x_ref, sh_ref, sc_ref, gpre_ref, wlat_ref, wfox_ref,
                     gq_ref, gkv_ref, wuq_ref, wuk_ref, wuvt_ref, bf_ref, cos_ref, slo_ref, shi_ref,
                     fqk_ref, fvt_ref, q_ref, k_ref, vt_ref, ck_ref, cq_ref, carry_ref, *, tiles_per_batch):
    @pl.when(pl.program_id(0) % tiles_per_batch == 0)
    def _():
        carry_ref[...] = jnp.zeros_like(carry_ref)

    h = (_rms(x_ref[...], gpre_ref[...]) * (1.0 + sc_ref[0]) + sh_ref[0]).astype(BF16)
    fw = FOX_WIDTH
    lat = jnp.dot(h, wlat_ref[...], preferred_element_type=F32)
    fq = jnp.dot(h, wfox_ref[:, :fw], preferred_element_type=F32) * FOX_Q_SCALE
    fqk_ref[:, :fw] = fq.astype(BF16)
    fqk_ref[:, fw:] = jnp.dot(h, wfox_ref[:, fw:2 * fw], preferred_element_type=F32).astype(BF16)
    fvt_ref[0, 0] = jnp.dot(h, wfox_ref[:, 2 * fw:], preferred_element_type=F32).T.astype(BF16)
    _mla_fox_prep(lat, gq_ref, gkv_ref, wuq_ref, wuk_ref, wuvt_ref, bf_ref, cos_ref, slo_ref, shi_ref,
                  q_ref, k_ref, vt_ref, ck_ref, cq_ref, carry_ref)


def _mixer_in(x, sh, sc, g_pre, w_lat, w_fox, g_q, g_kv, wuq, wuk, wuvt, b_forget, cos_t, sin_lo, sin_hi,
              *, batch, seq):
    t, d = x.shape
    tm = ATTN_TILE
    tn = LATENT_WIDTH
    tpb = seq // tm
    qk_w = MLA_HEADS * MLA_QK_PAD
    v_w = MLA_HEADS * MLA_V
    assert tn == FOX_WIDTH and w_lat.shape == (d, tn) and w_fox.shape == (d, 3 * FOX_WIDTH)
    resident_bytes = (w_lat.size + w_fox.size + wuq.size + wuk.size + wuvt.size) * 2
    vmem = (2 * tm * d * 4 + tm * d * 2 + resident_bytes + 2 * tm * (2 * tn + FOX_WIDTH) * 2
            + 2 * tm * (2 * qk_w + v_w) * 2 + 2 * FOX_HEADS * tm * LANES * 4 + 6 * tm * LANES * 4
            + tm * (tn + qk_w) * 4)
    mod_spec = pl.BlockSpec((1, 1, d), lambda i: (i // tpb, 0, 0))
    resident = lambda a: pl.BlockSpec(a.shape, lambda i: (0, 0), pipeline_mode=pl.Buffered(1))
    small = lambda a: pl.BlockSpec(a.shape, lambda i: (0, 0))
    tab_spec = pl.BlockSpec((tm, LANES), lambda i: (i % tpb, 0))
    return pl.pallas_call(
        functools.partial(_mixer_in_kernel, tiles_per_batch=tpb),
        grid=(t // tm,),
        in_specs=[pl.BlockSpec((tm, d), lambda i: (i, 0)),
                  mod_spec, mod_spec, small(g_pre), resident(w_lat), resident(w_fox),
                  small(g_q), small(g_kv), resident(wuq), resident(wuk), resident(wuvt), small(b_forget),
                  tab_spec, tab_spec, tab_spec],
        out_specs=[pl.BlockSpec((tm, 2 * tn), lambda i: (i, 0)),
                   pl.BlockSpec((1, 1, FOX_WIDTH, tm), lambda i: (i // tpb, i % tpb, 0, 0)),
                   pl.BlockSpec((tm, qk_w), lambda i: (i, 0)),
                   pl.BlockSpec((tm, qk_w), lambda i: (i, 0)),
                   pl.BlockSpec((1, 1, v_w, tm), lambda i: (i // tpb, i % tpb, 0, 0)),
                   pl.BlockSpec((1, FOX_HEADS, tm, LANES), lambda i: (i // tpb, 0, i % tpb, 0)),
                   pl.BlockSpec((1, 1, FOX_HEADS, tm), lambda i: (i // tpb, i % tpb, 0, 0))],
        out_shape=[jax.ShapeDtypeStruct((t, 2 * tn), BF16),
                   jax.ShapeDtypeStruct((batch, tpb, FOX_WIDTH, tm), BF16),
                   jax.ShapeDtypeStruct((t, qk_w), BF16),
                   jax.ShapeDtypeStruct((t, qk_w), BF16),
                   jax.ShapeDtypeStruct((batch, tpb, v_w, tm), BF16),
                   jax.ShapeDtypeStruct((batch, FOX_HEADS, seq, LANES), F32),
                   jax.ShapeDtypeStruct((batch, tpb, FOX_HEADS, tm), F32)],
        scratch_shapes=[pltpu.VMEM((1, LANES), F32)],
        compiler_params=_params(vmem),
        name="mixer_in",
    )(x, sh, sc, g_pre, w_lat, w_fox, g_q, g_kv, wuq, wuk, wuvt, b_forget, cos_t, sin_lo, sin_hi)


def _attn_kernel(*refs, mask_chunk, fox, n_cast, mod_rider):
    n_in = 5 if fox else 3
    n_mod = 3 if mod_rider else 0
    n_side_in = n_cast + n_mod
    side_in = refs[n_in:n_in + n_side_in]
    side_out = refs[n_in + n_side_in + 1:n_in + n_side_in + 1 + n_cast + (1 if mod_rider else 0)]
    refs = refs[:n_in] + (refs[n_in + n_side_in],) + refs[n_in + n_side_in + 1 + len(side_out):]
    if fox:
        q_ref, k_ref, vt_ref, ck_ref, cq_ref, o_ref, s0_ref, s1_ref, m_ref, l_ref, acc_ref = refs
    else:
        q_ref, k_ref, vt_ref, o_ref, s0_ref, s1_ref, m_ref, l_ref, acc_ref = refs
    for src, dst in zip(side_in[:n_cast], side_out[:n_cast]):
        dst[...] = src[...].astype(BF16)
    if mod_rider:
        c_ref, wada_ref, bada_ref = side_in[n_cast:]
        _ada_kernel(c_ref, wada_ref, bada_ref, side_out[n_cast])
    head = pl.program_id(1)
    tq = ATTN_TILE
    nq = q_ref.shape[0] // tq
    n_low = nq * (nq - 1) // 2
    assert PHASE_B_UNROLL % 2 == 0 and n_low % PHASE_B_UNROLL == 0
    bufs = (s0_ref, s1_ref)

    def rows(t):
        return pl.ds(pl.multiple_of(t * tq, tq), tq)

    def scores(i, j, s_ref):
        s = lax.dot_general(k_ref[rows(j), :], q_ref[rows(i), :], _NT, preferred_element_type=F32)
        if fox:
            cum_q = cq_ref[0, i, pl.ds(head, 1), :]
            cum_k = ck_ref[0, 0, rows(j), :]
            s = s + (cum_q - jnp.concatenate([cum_k] * (tq // LANES), axis=1))
        s_ref[...] = s

    half = tq // 2

    def scores_diagonal(i, s_ref):
        lo, hi = i * tq, i * tq + half
        s_a = lax.dot_general(k_ref[lo:hi, :], q_ref[lo:lo + tq, :], _NT, preferred_element_type=F32)
        s_b = lax.dot_general(k_ref[hi:hi + half, :], q_ref[hi:hi + half, :], _NT,
                              preferred_element_type=F32)
        if fox:
            cum_q = cq_ref[0, i, pl.ds(head, 1), :]
            s_a = s_a + (cum_q - jnp.concatenate([ck_ref[0, 0, lo:hi, :]] * (tq // LANES), axis=1))
            s_b = s_b + (cum_q[:, half:]
                         - jnp.concatenate([ck_ref[0, 0, hi:hi + half, :]] * (half // LANES), axis=1))
        src = lax.broadcasted_iota(jnp.int32, (half, half), 0) // mask_chunk
        dst = lax.broadcasted_iota(jnp.int32, (half, half), 1) // mask_chunk
        s_ref[:half, :half] = jnp.where(src <= dst, s_a[:, :half], NEG_BIG)
        s_ref[:half, half:] = s_a[:, half:]
        s_ref[half:, half:] = jnp.where(src <= dst, s_b, NEG_BIG)

    def accumulate_diagonal(i, s_ref):
        s_a = s_ref[:half, :]
        s_b = s_ref[half:, half:]
        m_a = jnp.max(s_a, axis=0, keepdims=True)
        m_hi = jnp.maximum(m_a[:, half:], jnp.max(s_b, axis=0, keepdims=True))
        m = jnp.concatenate([m_a[:, :half], m_hi], axis=1)
        p_a = jnp.exp2(s_a - m)
        p_b = jnp.exp2(s_b - m_hi)
        l_a = jnp.sum(p_a, axis=0, keepdims=True)
        m_ref[i] = m
        l_ref[i] = jnp.concatenate([l_a[:, :half], l_a[:, half:] + jnp.sum(p_b, axis=0, keepdims=True)],
                                   axis=1)
        acc_a = jnp.dot(vt_ref[0, i, :, :half], p_a.astype(BF16), preferred_element_type=F32)
        acc_b = jnp.dot(vt_ref[0, i, :, half:], p_b.astype(BF16), preferred_element_type=F32)
        acc_ref[i, :, :half] = acc_a[:, :half]
        acc_ref[i, :, half:] = acc_a[:, half:] + acc_b

    def accumulate(i, j, s_ref):
        s = s_ref[...]
        m_prev = m_ref[i]
        m_new = jnp.maximum(m_prev, jnp.max(s, axis=0, keepdims=True))
        alpha = jnp.exp2(m_prev - m_new)
        p = jnp.exp2(s - m_new)
        l_ref[i] = alpha * l_ref[i] + jnp.sum(p, axis=0, keepdims=True)
        acc_ref[i] = alpha * acc_ref[i] + jnp.dot(vt_ref[0, j], p.astype(BF16),
                                                  preferred_element_type=F32)
        m_ref[i] = m_new

    scores_diagonal(0, bufs[0])
    for i in range(nq):
        if i + 1 < nq:
            scores_diagonal(i + 1, bufs[(i + 1) % 2])
        else:
            scores(1, 0, bufs[(i + 1) % 2])
        accumulate_diagonal(i, bufs[i % 2])

    def advance(i, j):
        wraps = j + 1 == i
        i_nxt = jnp.minimum(jnp.where(wraps, i + 1, i), nq - 1)
        return i_nxt, jnp.where(wraps, 0, j + 1)

    def body(_, carry):
        i, j = carry
        for u in range(PHASE_B_UNROLL):
            i_nxt, j_nxt = advance(i, j)
            scores(i_nxt, j_nxt, bufs[(nq + u + 1) % 2])
            accumulate(i, j, bufs[(nq + u) % 2])
            i, j = i_nxt, j_nxt
        return i, j

    lax.fori_loop(0, n_low // PHASE_B_UNROLL, body, (jnp.int32(1), jnp.int32(0)))

    for i in range(nq):
        o_ref[i * tq:(i + 1) * tq, :] = (acc_ref[i] / l_ref[i]).T.astype(o_ref.dtype)


def _attention(q_arr, k_arr, vt_arr, cum_k=None, cum_q=None, *, batch, seq, heads, dk, dv,
               q_col, k_col, mask_chunk, riders=(), mod_rider=None):
    fox = cum_k is not None
    tq = ATTN_TILE
    nq = seq // tq
    steps = batch * heads
    in_specs = [pl.BlockSpec((seq, dk), lambda b, h: (b, q_col + h)),
                pl.BlockSpec((seq, dk), lambda b, h: (b, k_col + h)),
                pl.BlockSpec((1, nq, dv, tq), lambda b, h: (b, 0, h, 0))]
    args = [q_arr, k_arr, vt_arr]
    if fox:
        in_specs += [pl.BlockSpec((1, 1, seq, LANES), lambda b, h: (b, h, 0, 0)),
                     pl.BlockSpec((1, nq, heads, tq), lambda b, h: (b, 0, 0, 0))]
        args += [cum_k, cum_q]
    slab_specs = [pl.BlockSpec((w.shape[0] // steps, w.shape[1]), lambda b, h: (b * heads + h, 0))
                  for w in riders]
    vmem = (2 * seq * (2 * dk + 2 * dv) * 2 + 2 * seq * LANES * 4 + 2 * seq * heads * 4
            + seq * dv * 4 + 8 * tq * tq * 4 + sum(2 * 6 * w.size // steps for w in riders))
    side_in_specs, side_out_specs, side_args = list(slab_specs), list(slab_specs), list(riders)
    side_out_shape = [jax.ShapeDtypeStruct(w.shape, BF16) for w in riders]
    if mod_rider is not None:
        c_pad, w_ada, b_ada, first_col = mod_rider
        m, d_model = c_pad.shape
        tn = (w_ada.shape[1] - first_col) // steps
        assert first_col % tn == 0
        col = lambda b, h: (0, first_col // tn + b * heads + h)
        side_in_specs += [pl.BlockSpec((m, d_model), lambda b, h: (0, 0)),
                          pl.BlockSpec((d_model, tn), col), pl.BlockSpec((1, tn), col)]
        side_out_specs += [pl.BlockSpec((m, tn), lambda b, h: (0, b * heads + h))]
        side_out_shape += [jax.ShapeDtypeStruct((m, tn * steps), F32)]
        side_args += [c_pad, w_ada, b_ada]
        vmem += 2 * d_model * tn * 4 + d_model * tn * 2
    return pl.pallas_call(
        functools.partial(_attn_kernel, mask_chunk=mask_chunk, fox=fox, n_cast=len(riders),
                          mod_rider=mod_rider is not None),
        grid=(batch, heads),
        in_specs=in_specs + side_in_specs,
        out_specs=[pl.BlockSpec((seq, dv), lambda b, h: (b, h))] + side_out_specs,
        out_shape=[jax.ShapeDtypeStruct((batch * seq, heads * dv), BF16)] + side_out_shape,
        scratch_shapes=[pltpu.VMEM((tq, tq), F32), pltpu.VMEM((tq, tq), F32),
                        pltpu.VMEM((nq, 1, tq), F32), pltpu.VMEM((nq, 1, tq), F32),
                        pltpu.VMEM((nq, dv, tq), F32)],
        compiler_params=_params(vmem),
        name="fox_attn" if fox else "mla_attn",
    )(*args, *side_args)


def _outproj_kernel(x_ref, oa_ref, ob_ref, wa_ref, wb_ref, gpost_ref, gt_ref, o_ref):
    y = jnp.dot(oa_ref[...], wa_ref[...], preferred_element_type=F32)
    y = y + jnp.dot(ob_ref[...], wb_ref[...], preferred_element_type=F32)
    o_ref[...] = x_ref[...] + gt_ref[0] * _rms(y, gpost_ref[...])


def _outproj(x, o_a, o_b, w, g_post, gt, *, seq, tm):
    t, d = x.shape
    ka, kb = o_a.shape[1], o_b.shape[1]
    assert ka == kb and w.shape == (ka + kb, d)
    tpb = seq // tm
    vmem = 4 * tm * d * 4 + 2 * tm * (ka + kb) * 2 + 2 * (ka + kb) * d * 2 + 2 * tm * d * 4
    return pl.pallas_call(
        _outproj_kernel,
        grid=(t // tm,),
        in_specs=[pl.BlockSpec((tm, d), lambda i: (i, 0)),
                  pl.BlockSpec((tm, ka), lambda i: (i, 0)),
                  pl.BlockSpec((tm, kb), lambda i: (i, 0)),
                  pl.BlockSpec((ka, d), lambda i: (0, 0)),
                  pl.BlockSpec((kb, d), lambda i: (1, 0)),
                  pl.BlockSpec((1, d), lambda i: (0, 0)),
                  pl.BlockSpec((1, 1, d), lambda i: (i // tpb, 0, 0))],
        out_specs=pl.BlockSpec((tm, d), lambda i: (i, 0)),
        out_shape=jax.ShapeDtypeStruct((t, d), F32),
        compiler_params=_params(vmem),
        name="out_proj",
    )(x, o_a, o_b, w, w, g_post, gt)


def _rope_tables(seq):
    half = MLA_ROPE // 2
    inv = ROPE_THETA ** (-np.arange(half, dtype=np.float64) / half)
    ang = np.arange(seq, dtype=np.float64)[:, None] * inv[None, :]
    cos, sin = np.cos(ang), np.sin(ang)
    zero = np.zeros_like(cos)
    pad = np.zeros((seq, LANES - 2 * half))
    return tuple(jnp.asarray(np.concatenate(parts, axis=1), F32)
                 for parts in ([cos, cos, pad], [-sin, zero, pad], [zero, sin, pad]))


def kernel(x, c, w_ada, b_ada, g_ffn1_pre, g_ffn1_post, w1_gate, w1_up, w1_down, g_mix_pre,
           g_mix_post, w_in, b_forget, g_q_a, w_uq, g_kv_a, w_ukv, w_o, g_ffn2_pre, g_ffn2_post,
           w2_gate, w2_up, w2_down):
    batch, seq, d = x.shape
    depth = w_ada.shape[0]
    t = batch * seq
    xt = x.reshape(t, d)
    c_pad = jnp.pad(c, ((0, 8 - batch), (0, 0)))
    cos_t, sin_lo, sin_hi = _rope_tables(seq)

    for l in range(depth):
        early = 5
        chunks = lambda mod: [mod[:batch, n * d:(n + 1) * d].reshape(batch, 1, d)
                              for n in range(mod.shape[1] // d)]
        sh1, sc1, gt1, sh2, sc2 = chunks(_ada(c_pad, w_ada[l], b_ada[l:l + 1], tn=1024, n=early * d))

        ffn1 = functools.partial(_ffn, xt, sh1, sc1, gt1, g_ffn1_pre[l:l + 1], g_ffn1_post[l:l + 1],
                                 seq=seq, tm=1024, res_weight=0.5)
        y_head, w1g, w1u, w1d = ffn1(w1_gate[l], w1_up[l], w1_down[l], tf=256, mode="head")
        y_tail, wi = ffn1(w1g, w1u, w1d, tf=512, mode="tail", rider=w_in[l])
        xt = lax.dynamic_update_slice(y_tail, y_head, (0, 0))

        fox0 = KPE_COL + MLA_ROPE
        fl0 = fox0 + 3 * FOX_WIDTH
        w_lat = jnp.concatenate(
            [wi[:, :fox0], jnp.zeros((d, LANES - MLA_ROPE), BF16),
             wi[:, fl0:], jnp.zeros((d, LANES - FOX_HEADS), BF16)], axis=1)
        w_fox = wi[:, fox0:fl0]

        wuq_p = jnp.pad(w_uq[l].reshape(MLA_Q_RANK, MLA_HEADS, MLA_NOPE + MLA_ROPE),
                        ((0, 0), (0, 0), (0, MLA_QK_PAD - MLA_NOPE - MLA_ROPE))
                        ).reshape(MLA_Q_RANK, MLA_HEADS * MLA_QK_PAD).astype(BF16)
        wukv = w_ukv[l].reshape(MLA_KV_RANK, MLA_HEADS, MLA_NOPE + MLA_V)
        wuk = wukv[:, :, :MLA_NOPE].reshape(MLA_KV_RANK, -1).astype(BF16)
        wuvt = wukv[:, :, MLA_NOPE:].reshape(MLA_KV_RANK, -1).T.astype(BF16)
        bf_pad = jnp.pad(b_forget[l:l + 1], ((0, 0), (0, LANES - FOX_HEADS)))
        fqk, fvt, q_mla, k_mla, vt_mla, cum_k, cum_q = _mixer_in(
            xt, sh2, sc2, g_mix_pre[l:l + 1], w_lat, w_fox, g_q_a[l:l + 1], g_kv_a[l:l + 1],
            wuq_p, wuk, wuvt, bf_pad, cos_t, sin_lo, sin_hi, batch=batch, seq=seq)

        o_mla, w2g, w2u = _attention(q_mla, k_mla, vt_mla, batch=batch, seq=seq, heads=MLA_HEADS,
                                     dk=MLA_QK_PAD, dv=MLA_V, q_col=0, k_col=0, mask_chunk=CHUNK,
                                     riders=(w2_gate[l], w2_up[l]))
        o_fox, w2d, wo, mod_late = _attention(
            fqk, fqk, fvt, cum_k, cum_q, batch=batch, seq=seq, heads=FOX_HEADS, dk=FOX_DIM, dv=FOX_DIM,
            q_col=0, k_col=FOX_HEADS, mask_chunk=1, riders=(w2_down[l], w_o[l]),
            mod_rider=(c_pad, w_ada[l], b_ada[l:l + 1], early * d))
        gt2, sh3, sc3, gt3 = chunks(mod_late)

        xt = _outproj(xt, o_mla, o_fox, wo, g_mix_post[l:l + 1], gt2, seq=seq, tm=512)

        xt = _ffn(xt, sh3, sc3, gt3, g_ffn2_pre[l:l + 1], g_ffn2_post[l:l + 1],
                  w2g, w2u, w2d, seq=seq, tm=1024, tf=512, res_weight=0.5)

    return xt.reshape(batch, seq, d)
```

```python
import functools
import math

import jax
import jax.numpy as jnp
import numpy as np
from jax import lax
from jax.experimental import pallas as pl
from jax.experimental.pallas import tpu as pltpu

F32 = jnp.float32
BF16 = jnp.bfloat16

V7X_VMEM_BYTES = 64 * 1024 * 1024
LANES = 128

EPS = 1e-6
ROPE_THETA = 10000.0
CHUNK = 64
MLA_HEADS = 8
MLA_Q_RANK = 512
MLA_KV_RANK = 256
MLA_NOPE = 128
MLA_ROPE = 64
MLA_V = 128
MLA_QK_PAD = 256
FOX_HEADS = 8
FOX_DIM = 128
FOX_WIDTH = FOX_HEADS * FOX_DIM
LATENT_WIDTH = 1024
KPE_COL = MLA_Q_RANK + MLA_KV_RANK
FLOGIT_COL = KPE_COL + LANES
ATTN_TILE = 512
PHASE_B_UNROLL = 14
LOG2E = math.log2(math.e)
MLA_Q_SCALE = LOG2E / math.sqrt(MLA_NOPE + MLA_ROPE)
FOX_Q_SCALE = LOG2E / math.sqrt(FOX_DIM)
NEG_BIG = -1e30

_NT = (((1,), (1,)), ((), ()))


def _params(vmem_bytes):
    limit = min(int(vmem_bytes * 1.25) + (4 << 20), V7X_VMEM_BYTES - (8 << 20))
    return pltpu.CompilerParams(vmem_limit_bytes=limit)


def _rms(x, g):
    return x * lax.rsqrt(jnp.mean(x * x, axis=-1, keepdims=True) + EPS) * g


STAT_ROWS = 8
APPLY_ROWS = 16


def _rms_stats(src_ref, stat_ref):
    tm, d = src_ref.shape

    def step(c, carry):
        rows = pl.ds(pl.multiple_of(c * STAT_ROWS, STAT_ROWS), STAT_ROWS)
        x = src_ref[rows, :]
        ms = jnp.sum(x * x, axis=-1, keepdims=True) * (1.0 / d)
        stat_ref[rows, :] = jnp.broadcast_to(lax.rsqrt(ms + EPS), (STAT_ROWS, LANES))
        return carry

    lax.fori_loop(0, tm // STAT_ROWS, step, 0, unroll=32)


def _for_row_blocks(tm, fn):
    def step(c, carry):
        fn(pl.ds(pl.multiple_of(c * APPLY_ROWS, APPLY_ROWS), APPLY_ROWS))
        return carry

    lax.fori_loop(0, tm // APPLY_ROWS, step, 0, unroll=4)


def _gated_norm_residual(x_ref, o_ref, stat_ref, coef_ref, gpost_ref, gt_ref, res_weight):
    tm, d = o_ref.shape
    coef_ref[0:1, :] = res_weight * gt_ref[0] * gpost_ref[...]
    _rms_stats(o_ref, stat_ref)

    def apply(rows):
        r = stat_ref[rows, :]
        for k in range(d // LANES):
            cols = slice(k * LANES, (k + 1) * LANES)
            o_ref[rows, cols] = x_ref[rows, cols] + o_ref[rows, cols] * r * coef_ref[0:1, cols]

    _for_row_blocks(tm, apply)


def _ada_kernel(c_ref, w_ref, b_ref, o_ref):
    c = c_ref[...]
    cond = (c * jax.nn.sigmoid(c)).astype(BF16)
    o_ref[...] = jnp.dot(cond, w_ref[...].astype(BF16), preferred_element_type=F32) + b_ref[...]


def _ada(c_pad, w, b, *, tn, n):
    m, d = c_pad.shape
    vmem = 2 * d * tn * 4 + d * tn * 2 + 4 * m * tn * 4
    return pl.pallas_call(
        _ada_kernel,
        grid=(n // tn,),
        in_specs=[pl.BlockSpec((m, d), lambda j: (0, 0)),
                  pl.BlockSpec((d, tn), lambda j: (0, j)),
                  pl.BlockSpec((1, tn), lambda j: (0, j))],
        out_specs=pl.BlockSpec((m, tn), lambda j: (0, j)),
        out_shape=jax.ShapeDtypeStruct((m, n), F32),
        compiler_params=_params(vmem),
        name="ada",
    )(c_pad, w, b)


def _ffn_kernel(*refs, res_weight, mode):
    x_ref, sh_ref, sc_ref, gt_ref, gpre_ref, gpost_ref, wg_ref, wu_ref, wd_ref = refs[:9]
    if mode == "head":
        o_ref, wg_out, wu_out, wd_out, h_ref, stat_ref, coef_ref = refs[9:]
    else:
        o_ref, h_ref, stat_ref, coef_ref = refs[9:]
    f = pl.program_id(1)
    active = pl.program_id(0) > 0 if mode == "tail" else True

    def swiglu_down(h):
        wg, wu, wd = wg_ref[...], wu_ref[...], wd_ref[...]
        if mode == "head":
            wg, wu, wd = wg.astype(BF16), wu.astype(BF16), wd.astype(BF16)
            wg_out[...], wu_out[...], wd_out[...] = wg, wu, wd
        g = jnp.dot(h, wg, preferred_element_type=F32)
        u = jnp.dot(h, wu, preferred_element_type=F32)
        a = (g * jax.nn.sigmoid(g) * u).astype(BF16)
        return jnp.dot(a, wd, preferred_element_type=F32)

    @pl.when(jnp.logical_and(active, f == 0))
    def _():
        h = (_rms(x_ref[...], gpre_ref[...]) * (1.0 + sc_ref[0]) + sh_ref[0]).astype(BF16)
        h_ref[...] = h
        o_ref[...] = swiglu_down(h)

    @pl.when(jnp.logical_and(active, f > 0))
    def _():
        o_ref[...] += swiglu_down(h_ref[...])

    @pl.when(jnp.logical_and(active, f == pl.num_programs(1) - 1))
    def _():
        _gated_norm_residual(x_ref, o_ref, stat_ref, coef_ref, gpost_ref, gt_ref, res_weight)

    if mode == "tail":
        @pl.when(jnp.logical_and(jnp.logical_not(active), f == 0))
        def _():
            o_ref[...] = jnp.zeros_like(o_ref)


def _ffn(x, sh, sc, gt, g_pre, g_post, wg, wu, wd, *, seq, tm, tf, res_weight, mode="all"):
    t, d = x.shape
    ff = wg.shape[1]
    tpb = seq // tm
    head = mode == "head"
    n_tiles = 1 if head else t // tm
    w_bytes = 4 if head else 2
    x_bufs = 1 if head else 2
    vmem = (2 * x_bufs * tm * d * 4 + tm * d * 2 + 6 * d * tf * w_bytes + 3 * tm * tf * 4
            + (9 * d * tf * 2 if head else 0))
    mod_spec = pl.BlockSpec((1, 1, d), lambda i, f: (i // tpb, 0, 0))
    gain_spec = pl.BlockSpec((1, d), lambda i, f: (0, 0))
    tile_kw = dict(pipeline_mode=pl.Buffered(1)) if head else {}
    if mode == "tail":
        f_of = lambda i, f: jnp.where(i == 0, 0, f)
        x_spec = pl.BlockSpec((tm, d), lambda i, f: (jnp.maximum(i, 1), 0))
    else:
        f_of = lambda i, f: f
        x_spec = pl.BlockSpec((tm, d), lambda i, f: (i, 0), **tile_kw)
    up_spec = pl.BlockSpec((d, tf), lambda i, f: (0, f_of(i, f)))
    down_spec = pl.BlockSpec((tf, d), lambda i, f: (f_of(i, f), 0))
    out_specs = pl.BlockSpec((tm, d), lambda i, f: (i, 0), **tile_kw)
    out_shape = jax.ShapeDtypeStruct((n_tiles * tm, d), F32)
    if head:
        out_specs = [out_specs, up_spec, up_spec, down_spec]
        out_shape = [out_shape, jax.ShapeDtypeStruct(wg.shape, BF16), jax.ShapeDtypeStruct(wu.shape, BF16),
                     jax.ShapeDtypeStruct(wd.shape, BF16)]
    return pl.pallas_call(
        functools.partial(_ffn_kernel, res_weight=res_weight, mode=mode),
        grid=(n_tiles, ff // tf),
        in_specs=[x_spec, mod_spec, mod_spec, mod_spec, gain_spec, gain_spec, up_spec, up_spec, down_spec],
        out_specs=out_specs,
        out_shape=out_shape,
        scratch_shapes=[pltpu.VMEM((tm, d), BF16), pltpu.VMEM((tm, LANES), F32), pltpu.VMEM((8, d), F32)],
        compiler_params=_params(vmem),
        name="ffn_" + mode,
    )(x, sh, sc, gt, g_pre, g_post, wg, wu, wd)


def _rot(r, cos_t, sin_lo, sin_hi):
    return r * cos_t + pltpu.roll(r, 96, 1) * sin_lo + pltpu.roll(r, 32, 1) * sin_hi


def _mla_fox_prep(lat, gq_ref, gkv_ref, wuq_ref, wuk_ref, wuvt_ref, bf_ref, cos_ref, slo_ref, shi_ref,
                  q_ref, k_ref, vt_ref, ck_ref, cq_ref, carry_ref):
    tm = lat.shape[0]
    cos_t, sin_lo, sin_hi = cos_ref[...], slo_ref[...], shi_ref[...]

    qn = _rms(lat[:, :MLA_Q_RANK], gq_ref[...]).astype(BF16)
    q = jnp.dot(qn, wuq_ref[...], preferred_element_type=F32) * MLA_Q_SCALE
    kvn = _rms(lat[:, MLA_Q_RANK:KPE_COL], gkv_ref[...]).astype(BF16)
    k_nope = jnp.dot(kvn, wuk_ref[...], preferred_element_type=F32)
    vt_ref[0, 0] = lax.dot_general(wuvt_ref[...], kvn, _NT, preferred_element_type=F32).astype(BF16)
    k_rope = _rot(lat[:, KPE_COL:KPE_COL + LANES], cos_t, sin_lo, sin_hi).astype(BF16)
    for h in range(MLA_HEADS):
        c0 = h * MLA_QK_PAD
        q_ref[:, c0:c0 + MLA_NOPE] = q[:, c0:c0 + MLA_NOPE].astype(BF16)
        q_ref[:, c0 + MLA_NOPE:c0 + MLA_QK_PAD] = _rot(
            q[:, c0 + MLA_NOPE:c0 + MLA_QK_PAD], cos_t, sin_lo, sin_hi).astype(BF16)
        k_ref[:, c0:c0 + MLA_NOPE] = k_nope[:, h * MLA_NOPE:(h + 1) * MLA_NOPE].astype(BF16)
        k_ref[:, c0 + MLA_NOPE:c0 + MLA_QK_PAD] = k_rope

    z = lat[:, FLOGIT_COL:FLOGIT_COL + LANES] + bf_ref[...]
    lane = lax.broadcasted_iota(jnp.int32, (tm, LANES), 1)
    log_f = jnp.where(lane < FOX_HEADS, jnp.minimum(z, 0.0) - jnp.log1p(jnp.exp(-jnp.abs(z))), 0.0)
    hi = log_f.astype(BF16).astype(F32)
    mid = (log_f - hi).astype(BF16).astype(F32)
    lo = (log_f - hi - mid).astype(BF16).astype(F32)
    pieces = hi + pltpu.roll(mid, FOX_HEADS, 1) + pltpu.roll(lo, 2 * FOX_HEADS, 1)
    row = lax.broadcasted_iota(jnp.int32, (tm, tm), 0)
    col = lax.broadcasted_iota(jnp.int32, (tm, tm), 1)
    tri = (col <= row).astype(BF16)
    part = jnp.dot(tri, pieces.astype(BF16), preferred_element_type=F32)
    part = part + pltpu.roll(part, LANES - FOX_HEADS, 1) + pltpu.roll(part, LANES - 2 * FOX_HEADS, 1)
    cum = jnp.where(lane < FOX_HEADS, part, 0.0) + carry_ref[...]
    carry_ref[...] = cum[tm - 1:tm, :]
    cum2 = cum * LOG2E
    cq_ref[0, 0] = cum2.T[:FOX_HEADS, :]
    for h in range(FOX_HEADS):
        ck_ref[0, h] = jnp.broadcast_to(cum2[:, h:h + 1], (tm, LANES))


def _mixer_in_kernel(x_ref, sh_ref, sc_ref, gpre_ref, wlat_ref, wfox_ref,
                     gq_ref, gkv_ref, wuq_ref, wuk_ref, wuvt_ref, bf_ref, cos_ref, slo_ref, shi_ref,
                     fqk_ref, fvt_ref, q_ref, k_ref, vt_ref, ck_ref, cq_ref, carry_ref, *, tiles_per_batch):
    @pl.when(pl.program_id(0) % tiles_per_batch == 0)
    def _():
        carry_ref[...] = jnp.zeros_like(carry_ref)

    h = (_rms(x_ref[...], gpre_ref[...]) * (1.0 + sc_ref[0]) + sh_ref[0]).astype(BF16)
    fw = FOX_WIDTH
    lat = jnp.dot(h, wlat_ref[...], preferred_element_type=F32)
    fq = jnp.dot(h, wfox_ref[:, :fw], preferred_element_type=F32) * FOX_Q_SCALE
    fqk_ref[:, :fw] = fq.astype(BF16)
    fqk_ref[:, fw:] = jnp.dot(h, wfox_ref[:, fw:2 * fw], preferred_element_type=F32).astype(BF16)
    fvt_ref[0, 0] = jnp.dot(h, wfox_ref[:, 2 * fw:], preferred_element_type=F32).T.astype(BF16)
    _mla_fox_prep(lat, gq_ref, gkv_ref, wuq_ref, wuk_ref, wuvt_ref, bf_ref, cos_ref, slo_ref, shi_ref,
                  q_ref, k_ref, vt_ref, ck_ref, cq_ref, carry_ref)


def _mixer_in(x, sh, sc, g_pre, w_lat, w_fox, g_q, g_kv, wuq, wuk, wuvt, b_forget, cos_t, sin_lo, sin_hi,
              *, batch, seq):
    t, d = x.shape
    tm = ATTN_TILE
    tn = LATENT_WIDTH
    tpb = seq // tm
    qk_w = MLA_HEADS * MLA_QK_PAD
    v_w = MLA_HEADS * MLA_V
    assert tn == FOX_WIDTH and w_lat.shape == (d, tn) and w_fox.shape == (d, 3 * FOX_WIDTH)
    resident_bytes = (w_lat.size + w_fox.size + wuq.size + wuk.size + wuvt.size) * 2
    vmem = (2 * tm * d * 4 + tm * d * 2 + resident_bytes + 2 * tm * (2 * tn + FOX_WIDTH) * 2
            + 2 * tm * (2 * qk_w + v_w) * 2 + 2 * FOX_HEADS * tm * LANES * 4 + 6 * tm * LANES * 4
            + tm * (tn + qk_w) * 4)
    mod_spec = pl.BlockSpec((1, 1, d), lambda i: (i // tpb, 0, 0))
    resident = lambda a: pl.BlockSpec(a.shape, lambda i: (0, 0), pipeline_mode=pl.Buffered(1))
    small = lambda a: pl.BlockSpec(a.shape, lambda i: (0, 0))
    tab_spec = pl.BlockSpec((tm, LANES), lambda i: (i % tpb, 0))
    return pl.pallas_call(
        functools.partial(_mixer_in_kernel, tiles_per_batch=tpb),
        grid=(t // tm,),
        in_specs=[pl.BlockSpec((tm, d), lambda i: (i, 0)),
                  mod_spec, mod_spec, small(g_pre), resident(w_lat), resident(w_fox),
                  small(g_q), small(g_kv), resident(wuq), resident(wuk), resident(wuvt), small(b_forget),
                  tab_spec, tab_spec, tab_spec],
        out_specs=[pl.BlockSpec((tm, 2 * tn), lambda i: (i, 0)),
                   pl.BlockSpec((1, 1, FOX_WIDTH, tm), lambda i: (i // tpb, i % tpb, 0, 0)),
                   pl.BlockSpec((tm, qk_w), lambda i: (i, 0)),
                   pl.BlockSpec((tm, qk_w), lambda i: (i, 0)),
                   pl.BlockSpec((1, 1, v_w, tm), lambda i: (i // tpb, i % tpb, 0, 0)),
                   pl.BlockSpec((1, FOX_HEADS, tm, LANES), lambda i: (i // tpb, 0, i % tpb, 0)),
                   pl.BlockSpec((1, 1, FOX_HEADS, tm), lambda i: (i // tpb, i % tpb, 0, 0))],
        out_shape=[jax.ShapeDtypeStruct((t, 2 * tn), BF16),
                   jax.ShapeDtypeStruct((batch, tpb, FOX_WIDTH, tm), BF16),
                   jax.ShapeDtypeStruct((t, qk_w), BF16),
                   jax.ShapeDtypeStruct((t, qk_w), BF16),
                   jax.ShapeDtypeStruct((batch, tpb, v_w, tm), BF16),
                   jax.ShapeDtypeStruct((batch, FOX_HEADS, seq, LANES), F32),
                   jax.ShapeDtypeStruct((batch, tpb, FOX_HEADS, tm), F32)],
        scratch_shapes=[pltpu.VMEM((1, LANES), F32)],
        compiler_params=_params(vmem),
        name="mixer_in",
    )(x, sh, sc, g_pre, w_lat, w_fox, g_q, g_kv, wuq, wuk, wuvt, b_forget, cos_t, sin_lo, sin_hi)


def _attn_kernel(*refs, mask_chunk, fox, n_cast, mod_rider):
    n_in = 5 if fox else 3
    n_mod = 3 if mod_rider else 0
    n_side_in = n_cast + n_mod
    side_in = refs[n_in:n_in + n_side_in]
    side_out = refs[n_in + n_side_in + 1:n_in + n_side_in + 1 + n_cast + (1 if mod_rider else 0)]
    refs = refs[:n_in] + (refs[n_in + n_side_in],) + refs[n_in + n_side_in + 1 + len(side_out):]
    if fox:
        q_ref, k_ref, vt_ref, ck_ref, cq_ref, o_ref, s0_ref, s1_ref, m_ref, l_ref, acc_ref = refs
    else:
        q_ref, k_ref, vt_ref, o_ref, s0_ref, s1_ref, m_ref, l_ref, acc_ref = refs
    for src, dst in zip(side_in[:n_cast], side_out[:n_cast]):
        dst[...] = src[...].astype(BF16)
    if mod_rider:
        c_ref, wada_ref, bada_ref = side_in[n_cast:]
        _ada_kernel(c_ref, wada_ref, bada_ref, side_out[n_cast])
    head = pl.program_id(1)
    tq = ATTN_TILE
    nq = q_ref.shape[0] // tq
    n_low = nq * (nq - 1) // 2
    assert PHASE_B_UNROLL % 2 == 0 and n_low % PHASE_B_UNROLL == 0
    bufs = (s0_ref, s1_ref)

    def rows(t):
        return pl.ds(pl.multiple_of(t * tq, tq), tq)

    def scores(i, j, s_ref):
        s = lax.dot_general(k_ref[rows(j), :], q_ref[rows(i), :], _NT, preferred_element_type=F32)
        if fox:
            cum_k = ck_ref[0, 0, rows(j), :]
            s = s - jnp.concatenate([cum_k] * (tq // LANES), axis=1)
        s_ref[...] = s

    half = tq // 2

    def scores_diagonal(i, s_ref):
        lo, hi = i * tq, i * tq + half
        s_a = lax.dot_general(k_ref[lo:hi, :], q_ref[lo:lo + tq, :], _NT, preferred_element_type=F32)
        s_b = lax.dot_general(k_ref[hi:hi + half, :], q_ref[hi:hi + half, :], _NT,
                              preferred_element_type=F32)
        if fox:
            cum_q = cq_ref[0, i, pl.ds(head, 1), :]
            s_a = s_a + (cum_q - jnp.concatenate([ck_ref[0, 0, lo:hi, :]] * (tq // LANES), axis=1))
            s_b = s_b + (cum_q[:, half:]
                         - jnp.concatenate([ck_ref[0, 0, hi:hi + half, :]] * (half // LANES), axis=1))
        src = lax.broadcasted_iota(jnp.int32, (half, half), 0) // mask_chunk
        dst = lax.broadcasted_iota(jnp.int32, (half, half), 1) // mask_chunk
        s_ref[:half, :half] = jnp.where(src <= dst, s_a[:, :half], NEG_BIG)
        s_ref[:half, half:] = s_a[:, half:]
        s_ref[half:, half:] = jnp.where(src <= dst, s_b, NEG_BIG)

    def accumulate_diagonal(i, s_ref):
        s_a = s_ref[:half, :]
        s_b = s_ref[half:, half:]
        m_a = jnp.max(s_a, axis=0, keepdims=True)
        m_hi = jnp.maximum(m_a[:, half:], jnp.max(s_b, axis=0, keepdims=True))
        m = jnp.concatenate([m_a[:, :half], m_hi], axis=1)
        p_a = jnp.exp2(s_a - m)
        p_b = jnp.exp2(s_b - m_hi)
        l_a = jnp.sum(p_a, axis=0, keepdims=True)
        m_ref[i] = m
        l_ref[i] = jnp.concatenate([l_a[:, :half], l_a[:, half:] + jnp.sum(p_b, axis=0, keepdims=True)],
                                   axis=1)
        acc_a = jnp.dot(vt_ref[0, i, :, :half], p_a.astype(BF16), preferred_element_type=F32)
        acc_b = jnp.dot(vt_ref[0, i, :, half:], p_b.astype(BF16), preferred_element_type=F32)
        acc_ref[i, :, :half] = acc_a[:, :half]
        acc_ref[i, :, half:] = acc_a[:, half:] + acc_b

    def accumulate(i, j, s_ref):
        s = s_ref[...]
        m_prev = m_ref[i]
        m_tile = jnp.max(s, axis=0, keepdims=True)
        if fox:
            cum_q = cq_ref[0, i, pl.ds(head, 1), :]
            m_tile = m_tile + cum_q
        m_new = jnp.maximum(m_prev, m_tile)
        alpha = jnp.exp2(m_prev - m_new)
        p = jnp.exp2(s - (m_new - cum_q if fox else m_new))
        l_ref[i] = alpha * l_ref[i] + jnp.sum(p, axis=0, keepdims=True)
        acc_ref[i] = alpha * acc_ref[i] + jnp.dot(vt_ref[0, j], p.astype(BF16),
                                                  preferred_element_type=F32)
        m_ref[i] = m_new

    scores_diagonal(0, bufs[0])
    for i in range(nq):
        if i + 1 < nq:
            scores_diagonal(i + 1, bufs[(i + 1) % 2])
        else:
            scores(1, 0, bufs[(i + 1) % 2])
        accumulate_diagonal(i, bufs[i % 2])

    def advance(i, j):
        wraps = j + 1 == i
        i_nxt = jnp.minimum(jnp.where(wraps, i + 1, i), nq - 1)
        return i_nxt, jnp.where(wraps, 0, j + 1)

    def body(_, carry):
        i, j = carry
        for u in range(PHASE_B_UNROLL):
            i_nxt, j_nxt = advance(i, j)
            scores(i_nxt, j_nxt, bufs[(nq + u + 1) % 2])
            accumulate(i, j, bufs[(nq + u) % 2])
            i, j = i_nxt, j_nxt
        return i, j

    lax.fori_loop(0, n_low // PHASE_B_UNROLL, body, (jnp.int32(1), jnp.int32(0)))

    for i in range(nq):
        o_ref[i * tq:(i + 1) * tq, :] = (acc_ref[i] / l_ref[i]).T.astype(o_ref.dtype)


def _attention(q_arr, k_arr, vt_arr, cum_k=None, cum_q=None, *, batch, seq, heads, dk, dv,
               q_col, k_col, mask_chunk, riders=(), mod_rider=None):
    fox = cum_k is not None
    tq = ATTN_TILE
    nq = seq // tq
    steps = batch * heads
    in_specs = [pl.BlockSpec((seq, dk), lambda b, h: (b, q_col + h)),
                pl.BlockSpec((seq, dk), lambda b, h: (b, k_col + h)),
                pl.BlockSpec((1, nq, dv, tq), lambda b, h: (b, 0, h, 0))]
    args = [q_arr, k_arr, vt_arr]
    if fox:
        in_specs += [pl.BlockSpec((1, 1, seq, LANES), lambda b, h: (b, h, 0, 0)),
                     pl.BlockSpec((1, nq, heads, tq), lambda b, h: (b, 0, 0, 0))]
        args += [cum_k, cum_q]
    slab_specs = [pl.BlockSpec((w.shape[0] // steps, w.shape[1]), lambda b, h: (b * heads + h, 0))
                  for w in riders]
    vmem = (2 * seq * (2 * dk + 2 * dv) * 2 + 2 * seq * LANES * 4 + 2 * seq * heads * 4
            + seq * dv * 4 + 8 * tq * tq * 4 + sum(2 * 6 * w.size // steps for w in riders))
    side_in_specs, side_out_specs, side_args = list(slab_specs), list(slab_specs), list(riders)
    side_out_shape = [jax.ShapeDtypeStruct(w.shape, BF16) for w in riders]
    if mod_rider is not None:
        c_pad, w_ada, b_ada, first_col = mod_rider
        m, d_model = c_pad.shape
        tn = (w_ada.shape[1] - first_col) // steps
        assert first_col % tn == 0
        col = lambda b, h: (0, first_col // tn + b * heads + h)
        side_in_specs += [pl.BlockSpec((m, d_model), lambda b, h: (0, 0)),
                          pl.BlockSpec((d_model, tn), col), pl.BlockSpec((1, tn), col)]
        side_out_specs += [pl.BlockSpec((m, tn), lambda b, h: (0, b * heads + h))]
        side_out_shape += [jax.ShapeDtypeStruct((m, tn * steps), F32)]
        side_args += [c_pad, w_ada, b_ada]
        vmem += 2 * d_model * tn * 4 + d_model * tn * 2
    return pl.pallas_call(
        functools.partial(_attn_kernel, mask_chunk=mask_chunk, fox=fox, n_cast=len(riders),
                          mod_rider=mod_rider is not None),
        grid=(batch, heads),
        in_specs=in_specs + side_in_specs,
        out_specs=[pl.BlockSpec((seq, dv), lambda b, h: (b, h))] + side_out_specs,
        out_shape=[jax.ShapeDtypeStruct((batch * seq, heads * dv), BF16)] + side_out_shape,
        scratch_shapes=[pltpu.VMEM((tq, tq), F32), pltpu.VMEM((tq, tq), F32),
                        pltpu.VMEM((nq, 1, tq), F32), pltpu.VMEM((nq, 1, tq), F32),
                        pltpu.VMEM((nq, dv, tq), F32)],
        compiler_params=_params(vmem),
        name="fox_attn" if fox else "mla_attn",
    )(*args, *side_args)


def _outproj_kernel(x_ref, oa_ref, ob_ref, wa_ref, wb_ref, gpost_ref, gt_ref, o_ref):
    y = jnp.dot(oa_ref[...], wa_ref[...], preferred_element_type=F32)
    y = y + jnp.dot(ob_ref[...], wb_ref[...], preferred_element_type=F32)
    o_ref[...] = x_ref[...] + gt_ref[0] * _rms(y, gpost_ref[...])


def _outproj(x, o_a, o_b, w, g_post, gt, *, seq, tm):
    t, d = x.shape
    ka, kb = o_a.shape[1], o_b.shape[1]
    assert ka == kb and w.shape == (ka + kb, d)
    tpb = seq // tm
    vmem = 4 * tm * d * 4 + 2 * tm * (ka + kb) * 2 + 2 * (ka + kb) * d * 2 + 2 * tm * d * 4
    return pl.pallas_call(
        _outproj_kernel,
        grid=(t // tm,),
        in_specs=[pl.BlockSpec((tm, d), lambda i: (i, 0)),
                  pl.BlockSpec((tm, ka), lambda i: (i, 0)),
                  pl.BlockSpec((tm, kb), lambda i: (i, 0)),
                  pl.BlockSpec((ka, d), lambda i: (0, 0)),
                  pl.BlockSpec((kb, d), lambda i: (1, 0)),
                  pl.BlockSpec((1, d), lambda i: (0, 0)),
                  pl.BlockSpec((1, 1, d), lambda i: (i // tpb, 0, 0))],
        out_specs=pl.BlockSpec((tm, d), lambda i: (i, 0)),
        out_shape=jax.ShapeDtypeStruct((t, d), F32),
        compiler_params=_params(vmem),
        name="out_proj",
    )(x, o_a, o_b, w, w, g_post, gt)


def _rope_tables(seq):
    half = MLA_ROPE // 2
    inv = ROPE_THETA ** (-np.arange(half, dtype=np.float64) / half)
    ang = np.arange(seq, dtype=np.float64)[:, None] * inv[None, :]
    cos, sin = np.cos(ang), np.sin(ang)
    zero = np.zeros_like(cos)
    pad = np.zeros((seq, LANES - 2 * half))
    return tuple(jnp.asarray(np.concatenate(parts, axis=1), F32)
                 for parts in ([cos, cos, pad], [-sin, zero, pad], [zero, sin, pad]))


def kernel(x, c, w_ada, b_ada, g_ffn1_pre, g_ffn1_post, w1_gate, w1_up, w1_down, g_mix_pre,
           g_mix_post, w_in, b_forget, g_q_a, w_uq, g_kv_a, w_ukv, w_o, g_ffn2_pre, g_ffn2_post,
           w2_gate, w2_up, w2_down):
    batch, seq, d = x.shape
    depth = w_ada.shape[0]
    t = batch * seq
    xt = x.reshape(t, d)
    c_pad = jnp.pad(c, ((0, 8 - batch), (0, 0)))
    cos_t, sin_lo, sin_hi = _rope_tables(seq)

    for l in range(depth):
        early = 5
        chunks = lambda mod: [mod[:batch, n * d:(n + 1) * d].reshape(batch, 1, d)
                              for n in range(mod.shape[1] // d)]
        sh1, sc1, gt1, sh2, sc2 = chunks(_ada(c_pad, w_ada[l], b_ada[l:l + 1], tn=1024, n=early * d))

        ffn1 = functools.partial(_ffn, xt, sh1, sc1, gt1, g_ffn1_pre[l:l + 1], g_ffn1_post[l:l + 1],
                                 seq=seq, tm=1024, res_weight=0.5)
        y_head, w1g, w1u, w1d = ffn1(w1_gate[l], w1_up[l], w1_down[l], tf=256, mode="head")
        xt = lax.dynamic_update_slice(ffn1(w1g, w1u, w1d, tf=512, mode="tail"), y_head, (0, 0))

        wi = w_in[l]
        fox0 = KPE_COL + MLA_ROPE
        fl0 = fox0 + 3 * FOX_WIDTH
        w_lat = jnp.concatenate(
            [wi[:, :fox0], jnp.zeros((d, LANES - MLA_ROPE), F32),
             wi[:, fl0:], jnp.zeros((d, LANES - FOX_HEADS), F32)], axis=1).astype(BF16)
        w_fox = wi[:, fox0:fl0].astype(BF16)

        wuq_p = jnp.pad(w_uq[l].reshape(MLA_Q_RANK, MLA_HEADS, MLA_NOPE + MLA_ROPE),
                        ((0, 0), (0, 0), (0, MLA_QK_PAD - MLA_NOPE - MLA_ROPE))
                        ).reshape(MLA_Q_RANK, MLA_HEADS * MLA_QK_PAD).astype(BF16)
        wukv = w_ukv[l].reshape(MLA_KV_RANK, MLA_HEADS, MLA_NOPE + MLA_V)
        wuk = wukv[:, :, :MLA_NOPE].reshape(MLA_KV_RANK, -1).astype(BF16)
        wuvt = wukv[:, :, MLA_NOPE:].reshape(MLA_KV_RANK, -1).T.astype(BF16)
        bf_pad = jnp.pad(b_forget[l:l + 1], ((0, 0), (0, LANES - FOX_HEADS)))
        fqk, fvt, q_mla, k_mla, vt_mla, cum_k, cum_q = _mixer_in(
            xt, sh2, sc2, g_mix_pre[l:l + 1], w_lat, w_fox, g_q_a[l:l + 1], g_kv_a[l:l + 1],
            wuq_p, wuk, wuvt, bf_pad, cos_t, sin_lo, sin_hi, batch=batch, seq=seq)

        o_mla, w2g, w2u = _attention(q_mla, k_mla, vt_mla, batch=batch, seq=seq, heads=MLA_HEADS,
                                     dk=MLA_QK_PAD, dv=MLA_V, q_col=0, k_col=0, mask_chunk=CHUNK,
                                     riders=(w2_gate[l], w2_up[l]))
        o_fox, w2d, wo, mod_late = _attention(
            fqk, fqk, fvt, cum_k, cum_q, batch=batch, seq=seq, heads=FOX_HEADS, dk=FOX_DIM, dv=FOX_DIM,
            q_col=0, k_col=FOX_HEADS, mask_chunk=1, riders=(w2_down[l], w_o[l]),
            mod_rider=(c_pad, w_ada[l], b_ada[l:l + 1], early * d))
        gt2, sh3, sc3, gt3 = chunks(mod_late)

        xt = _outproj(xt, o_mla, o_fox, wo, g_mix_post[l:l + 1], gt2, seq=seq, tm=512)

        xt = _ffn(xt, sh3, sc3, gt3, g_ffn2_pre[l:l + 1], g_ffn2_post[l:l + 1],
                  w2g, w2u, w2d, seq=seq, tm=1024, tf=512, res_weight=0.5)

    return xt.reshape(batch, seq, d)
```

```python
import functools
import math

import jax
import jax.numpy as jnp
import numpy as np
from jax import lax
from jax.experimental import pallas as pl
from jax.experimental.pallas import tpu as pltpu

F32 = jnp.float32
BF16 = jnp.bfloat16

V7X_VMEM_BYTES = 64 * 1024 * 1024
LANES = 128

EPS = 1e-6
ROPE_THETA = 10000.0
CHUNK = 64
MLA_HEADS = 8
MLA_Q_RANK = 512
MLA_KV_RANK = 256
MLA_NOPE = 128
MLA_ROPE = 64
MLA_V = 128
MLA_QK_PAD = 256
FOX_HEADS = 8
FOX_DIM = 128
FOX_WIDTH = FOX_HEADS * FOX_DIM
LATENT_WIDTH = 1024
KPE_COL = MLA_Q_RANK + MLA_KV_RANK
FLOGIT_COL = KPE_COL + LANES
ATTN_TILE = 512
PHASE_B_UNROLL = 14
LOG2E = math.log2(math.e)
MLA_Q_SCALE = LOG2E / math.sqrt(MLA_NOPE + MLA_ROPE)
FOX_Q_SCALE = LOG2E / math.sqrt(FOX_DIM)
NEG_BIG = -1e30

_NT = (((1,), (1,)), ((), ()))


def _params(vmem_bytes):
    limit = min(int(vmem_bytes * 1.25) + (4 << 20), V7X_VMEM_BYTES - (8 << 20))
    return pltpu.CompilerParams(vmem_limit_bytes=limit)


def _rms(x, g):
    return x * lax.rsqrt(jnp.mean(x * x, axis=-1, keepdims=True) + EPS) * g


STAT_ROWS = 8
APPLY_ROWS = 16


def _rms_stats(src_ref, stat_ref):
    tm, d = src_ref.shape

    def step(c, carry):
        rows = pl.ds(pl.multiple_of(c * STAT_ROWS, STAT_ROWS), STAT_ROWS)
        x = src_ref[rows, :]
        ms = jnp.sum(x * x, axis=-1, keepdims=True) * (1.0 / d)
        stat_ref[rows, :] = jnp.broadcast_to(lax.rsqrt(ms + EPS), (STAT_ROWS, LANES))
        return carry

    lax.fori_loop(0, tm // STAT_ROWS, step, 0, unroll=32)


def _for_row_blocks(tm, fn):
    def step(c, carry):
        fn(pl.ds(pl.multiple_of(c * APPLY_ROWS, APPLY_ROWS), APPLY_ROWS))
        return carry

    lax.fori_loop(0, tm // APPLY_ROWS, step, 0, unroll=4)


def _gated_norm_residual(x_ref, o_ref, stat_ref, coef_ref, gpost_ref, gt_ref, res_weight):
    tm, d = o_ref.shape
    coef_ref[0:1, :] = res_weight * gt_ref[0] * gpost_ref[...]
    _rms_stats(o_ref, stat_ref)

    def apply(rows):
        r = stat_ref[rows, :]
        for k in range(d // LANES):
            cols = slice(k * LANES, (k + 1) * LANES)
            o_ref[rows, cols] = x_ref[rows, cols] + o_ref[rows, cols] * r * coef_ref[0:1, cols]

    _for_row_blocks(tm, apply)


def _ada_kernel(c_ref, w_ref, b_ref, o_ref):
    c = c_ref[...]
    cond = (c * jax.nn.sigmoid(c)).astype(BF16)
    o_ref[...] = jnp.dot(cond, w_ref[...].astype(BF16), preferred_element_type=F32) + b_ref[...]


def _ada(c_pad, w, b, *, tn, n):
    m, d = c_pad.shape
    vmem = 2 * d * tn * 4 + d * tn * 2 + 4 * m * tn * 4
    return pl.pallas_call(
        _ada_kernel,
        grid=(n // tn,),
        in_specs=[pl.BlockSpec((m, d), lambda j: (0, 0)),
                  pl.BlockSpec((d, tn), lambda j: (0, j)),
                  pl.BlockSpec((1, tn), lambda j: (0, j))],
        out_specs=pl.BlockSpec((m, tn), lambda j: (0, j)),
        out_shape=jax.ShapeDtypeStruct((m, n), F32),
        compiler_params=_params(vmem),
        name="ada",
    )(c_pad, w, b)


def _ffn_kernel(*refs, res_weight, mode):
    x_ref, sh_ref, sc_ref, gt_ref, gpre_ref, gpost_ref, wg_ref, wu_ref, wd_ref = refs[:9]
    if mode == "head":
        o_ref, wg_out, wu_out, wd_out, h_ref, stat_ref, coef_ref = refs[9:]
    else:
        o_ref, h_ref, stat_ref, coef_ref = refs[9:]
    f = pl.program_id(1)
    active = pl.program_id(0) > 0 if mode == "tail" else True

    def swiglu_down(h):
        wg, wu, wd = wg_ref[...], wu_ref[...], wd_ref[...]
        if mode == "head":
            wg, wu, wd = wg.astype(BF16), wu.astype(BF16), wd.astype(BF16)
            wg_out[...], wu_out[...], wd_out[...] = wg, wu, wd
        g = jnp.dot(h, wg, preferred_element_type=F32)
        u = jnp.dot(h, wu, preferred_element_type=F32)
        a = (g * jax.nn.sigmoid(g) * u).astype(BF16)
        return jnp.dot(a, wd, preferred_element_type=F32)

    @pl.when(jnp.logical_and(active, f == 0))
    def _():
        h = (_rms(x_ref[...], gpre_ref[...]) * (1.0 + sc_ref[0]) + sh_ref[0]).astype(BF16)
        h_ref[...] = h
        o_ref[...] = swiglu_down(h)

    @pl.when(jnp.logical_and(active, f > 0))
    def _():
        o_ref[...] += swiglu_down(h_ref[...])

    @pl.when(jnp.logical_and(active, f == pl.num_programs(1) - 1))
    def _():
        _gated_norm_residual(x_ref, o_ref, stat_ref, coef_ref, gpost_ref, gt_ref, res_weight)

    if mode == "tail":
        @pl.when(jnp.logical_and(jnp.logical_not(active), f == 0))
        def _():
            o_ref[...] = jnp.zeros_like(o_ref)


def _ffn(x, sh, sc, gt, g_pre, g_post, wg, wu, wd, *, seq, tm, tf, res_weight, mode="all"):
    t, d = x.shape
    ff = wg.shape[1]
    tpb = seq // tm
    head = mode == "head"
    n_tiles = 1 if head else t // tm
    w_bytes = 4 if head else 2
    x_bufs = 1 if head else 2
    vmem = (2 * x_bufs * tm * d * 4 + tm * d * 2 + 6 * d * tf * w_bytes + 3 * tm * tf * 4
            + (9 * d * tf * 2 if head else 0))
    mod_spec = pl.BlockSpec((1, 1, d), lambda i, f: (i // tpb, 0, 0))
    gain_spec = pl.BlockSpec((1, d), lambda i, f: (0, 0))
    tile_kw = dict(pipeline_mode=pl.Buffered(1)) if head else {}
    if mode == "tail":
        f_of = lambda i, f: jnp.where(i == 0, 0, f)
        x_spec = pl.BlockSpec((tm, d), lambda i, f: (jnp.maximum(i, 1), 0))
    else:
        f_of = lambda i, f: f
        x_spec = pl.BlockSpec((tm, d), lambda i, f: (i, 0), **tile_kw)
    up_spec = pl.BlockSpec((d, tf), lambda i, f: (0, f_of(i, f)))
    down_spec = pl.BlockSpec((tf, d), lambda i, f: (f_of(i, f), 0))
    out_specs = pl.BlockSpec((tm, d), lambda i, f: (i, 0), **tile_kw)
    out_shape = jax.ShapeDtypeStruct((n_tiles * tm, d), F32)
    if head:
        out_specs = [out_specs, up_spec, up_spec, down_spec]
        out_shape = [out_shape, jax.ShapeDtypeStruct(wg.shape, BF16), jax.ShapeDtypeStruct(wu.shape, BF16),
                     jax.ShapeDtypeStruct(wd.shape, BF16)]
    return pl.pallas_call(
        functools.partial(_ffn_kernel, res_weight=res_weight, mode=mode),
        grid=(n_tiles, ff // tf),
        in_specs=[x_spec, mod_spec, mod_spec, mod_spec, gain_spec, gain_spec, up_spec, up_spec, down_spec],
        out_specs=out_specs,
        out_shape=out_shape,
        scratch_shapes=[pltpu.VMEM((tm, d), BF16), pltpu.VMEM((tm, LANES), F32), pltpu.VMEM((8, d), F32)],
        compiler_params=_params(vmem),
        name="ffn_" + mode,
    )(x, sh, sc, gt, g_pre, g_post, wg, wu, wd)


def _rot(r, cos_t, sin_lo, sin_hi):
    return r * cos_t + pltpu.roll(r, 96, 1) * sin_lo + pltpu.roll(r, 32, 1) * sin_hi


def _mla_fox_prep(lat, gq_ref, gkv_ref, wuq_ref, wuk_ref, wuvt_ref, bf_ref, cos_ref, slo_ref, shi_ref,
                  q_ref, k_ref, vt_ref, ck_ref, cq_ref, carry_ref):
    tm = lat.shape[0]
    cos_t, sin_lo, sin_hi = cos_ref[...], slo_ref[...], shi_ref[...]

    qn = _rms(lat[:, :MLA_Q_RANK], gq_ref[...]).astype(BF16)
    q = jnp.dot(qn, wuq_ref[...], preferred_element_type=F32) * MLA_Q_SCALE
    kvn = _rms(lat[:, MLA_Q_RANK:KPE_COL], gkv_ref[...]).astype(BF16)
    k_nope = jnp.dot(kvn, wuk_ref[...], preferred_element_type=F32)
    vt_ref[0, 0] = lax.dot_general(wuvt_ref[...], kvn, _NT, preferred_element_type=F32).astype(BF16)
    k_rope = _rot(lat[:, KPE_COL:KPE_COL + LANES], cos_t, sin_lo, sin_hi).astype(BF16)
    for h in range(MLA_HEADS):
        c0 = h * MLA_QK_PAD
        q_ref[:, c0:c0 + MLA_NOPE] = q[:, c0:c0 + MLA_NOPE].astype(BF16)
        q_ref[:, c0 + MLA_NOPE:c0 + MLA_QK_PAD] = _rot(
            q[:, c0 + MLA_NOPE:c0 + MLA_QK_PAD], cos_t, sin_lo, sin_hi).astype(BF16)
        k_ref[:, c0:c0 + MLA_NOPE] = k_nope[:, h * MLA_NOPE:(h + 1) * MLA_NOPE].astype(BF16)
        k_ref[:, c0 + MLA_NOPE:c0 + MLA_QK_PAD] = k_rope

    z = lat[:, FLOGIT_COL:FLOGIT_COL + LANES] + bf_ref[...]
    lane = lax.broadcasted_iota(jnp.int32, (tm, LANES), 1)
    log_f = jnp.where(lane < FOX_HEADS, jnp.minimum(z, 0.0) - jnp.log1p(jnp.exp(-jnp.abs(z))), 0.0)
    hi = log_f.astype(BF16).astype(F32)
    mid = (log_f - hi).astype(BF16).astype(F32)
    lo = (log_f - hi - mid).astype(BF16).astype(F32)
    pieces = hi + pltpu.roll(mid, FOX_HEADS, 1) + pltpu.roll(lo, 2 * FOX_HEADS, 1)
    row = lax.broadcasted_iota(jnp.int32, (tm, tm), 0)
    col = lax.broadcasted_iota(jnp.int32, (tm, tm), 1)
    tri = (col <= row).astype(BF16)
    part = jnp.dot(tri, pieces.astype(BF16), preferred_element_type=F32)
    part = part + pltpu.roll(part, LANES - FOX_HEADS, 1) + pltpu.roll(part, LANES - 2 * FOX_HEADS, 1)
    cum = jnp.where(lane < FOX_HEADS, part, 0.0) + carry_ref[...]
    carry_ref[...] = cum[tm - 1:tm, :]
    cum2 = cum * LOG2E
    cq_ref[0, 0] = cum2.T[:FOX_HEADS, :]
    for h in range(FOX_HEADS):
        ck_ref[0, h] = jnp.broadcast_to(cum2[:, h:h + 1], (tm, LANES))


def _mixer_in_kernel(x_ref, sh_ref, sc_ref, gpre_ref, wlat_ref, wfox_ref,
                     gq_ref, gkv_ref, wuq_ref, wuk_ref, wuvt_ref, bf_ref, cos_ref, slo_ref, shi_ref,
                     fqk_ref, fvt_ref, q_ref, k_ref, vt_ref, ck_ref, cq_ref, carry_ref, *, tiles_per_batch):
    @pl.when(pl.program_id(0) % tiles_per_batch == 0)
    def _():
        carry_ref[...] = jnp.zeros_like(carry_ref)

    h = (_rms(x_ref[...], gpre_ref[...]) * (1.0 + sc_ref[0]) + sh_ref[0]).astype(BF16)
    fw = FOX_WIDTH
    lat = jnp.dot(h, wlat_ref[...], preferred_element_type=F32)
    fq = jnp.dot(h, wfox_ref[:, :fw], preferred_element_type=F32) * FOX_Q_SCALE
    fqk_ref[:, :fw] = fq.astype(BF16)
    fqk_ref[:, fw:] = jnp.dot(h, wfox_ref[:, fw:2 * fw], preferred_element_type=F32).astype(BF16)
    fvt_ref[0, 0] = jnp.dot(h, wfox_ref[:, 2 * fw:], preferred_element_type=F32).T.astype(BF16)
    _mla_fox_prep(lat, gq_ref, gkv_ref, wuq_ref, wuk_ref, wuvt_ref, bf_ref, cos_ref, slo_ref, shi_ref,
                  q_ref, k_ref, vt_ref, ck_ref, cq_ref, carry_ref)


def _mixer_in(x, sh, sc, g_pre, w_lat, w_fox, g_q, g_kv, wuq, wuk, wuvt, b_forget, cos_t, sin_lo, sin_hi,
              *, batch, seq):
    t, d = x.shape
    tm = ATTN_TILE
    tn = LATENT_WIDTH
    tpb = seq // tm
    qk_w = MLA_HEADS * MLA_QK_PAD
    v_w = MLA_HEADS * MLA_V
    assert tn == FOX_WIDTH and w_lat.shape == (d, tn) and w_fox.shape == (d, 3 * FOX_WIDTH)
    resident_bytes = (w_lat.size + w_fox.size + wuq.size + wuk.size + wuvt.size) * 2
    vmem = (2 * tm * d * 4 + tm * d * 2 + resident_bytes + 2 * tm * (2 * tn + FOX_WIDTH) * 2
            + 2 * tm * (2 * qk_w + v_w) * 2 + 2 * FOX_HEADS * tm * LANES * 4 + 6 * tm * LANES * 4
            + tm * (tn + qk_w) * 4)
    mod_spec = pl.BlockSpec((1, 1, d), lambda i: (i // tpb, 0, 0))
    resident = lambda a: pl.BlockSpec(a.shape, lambda i: (0, 0), pipeline_mode=pl.Buffered(1))
    small = lambda a: pl.BlockSpec(a.shape, lambda i: (0, 0))
    tab_spec = pl.BlockSpec((tm, LANES), lambda i: (i % tpb, 0))
    return pl.pallas_call(
        functools.partial(_mixer_in_kernel, tiles_per_batch=tpb),
        grid=(t // tm,),
        in_specs=[pl.BlockSpec((tm, d), lambda i: (i, 0)),
                  mod_spec, mod_spec, small(g_pre), resident(w_lat), resident(w_fox),
                  small(g_q), small(g_kv), resident(wuq), resident(wuk), resident(wuvt), small(b_forget),
                  tab_spec, tab_spec, tab_spec],
        out_specs=[pl.BlockSpec((tm, 2 * tn), lambda i: (i, 0)),
                   pl.BlockSpec((1, 1, FOX_WIDTH, tm), lambda i: (i // tpb, i % tpb, 0, 0)),
                   pl.BlockSpec((tm, qk_w), lambda i: (i, 0)),
                   pl.BlockSpec((tm, qk_w), lambda i: (i, 0)),
                   pl.BlockSpec((1, 1, v_w, tm), lambda i: (i // tpb, i % tpb, 0, 0)),
                   pl.BlockSpec((1, FOX_HEADS, tm, LANES), lambda i: (i // tpb, 0, i % tpb, 0)),
                   pl.BlockSpec((1, 1, FOX_HEADS, tm), lambda i: (i // tpb, i % tpb, 0, 0))],
        out_shape=[jax.ShapeDtypeStruct((t, 2 * tn), BF16),
                   jax.ShapeDtypeStruct((batch, tpb, FOX_WIDTH, tm), BF16),
                   jax.ShapeDtypeStruct((t, qk_w), BF16),
                   jax.ShapeDtypeStruct((t, qk_w), BF16),
                   jax.ShapeDtypeStruct((batch, tpb, v_w, tm), BF16),
                   jax.ShapeDtypeStruct((batch, FOX_HEADS, seq, LANES), F32),
                   jax.ShapeDtypeStruct((batch, tpb, FOX_HEADS, tm), F32)],
        scratch_shapes=[pltpu.VMEM((1, LANES), F32)],
        compiler_params=_params(vmem),
        name="mixer_in",
    )(x, sh, sc, g_pre, w_lat, w_fox, g_q, g_kv, wuq, wuk, wuvt, b_forget, cos_t, sin_lo, sin_hi)


def _attn_kernel(*refs, mask_chunk, fox, n_cast, mod_rider):
    n_in = 5 if fox else 3
    n_mod = 3 if mod_rider else 0
    n_side_in = n_cast + n_mod
    side_in = refs[n_in:n_in + n_side_in]
    side_out = refs[n_in + n_side_in + 1:n_in + n_side_in + 1 + n_cast + (1 if mod_rider else 0)]
    refs = refs[:n_in] + (refs[n_in + n_side_in],) + refs[n_in + n_side_in + 1 + len(side_out):]
    if fox:
        q_ref, k_ref, vt_ref, ck_ref, cq_ref, o_ref, s0_ref, s1_ref, m_ref, l_ref, acc_ref = refs
    else:
        q_ref, k_ref, vt_ref, o_ref, s0_ref, s1_ref, m_ref, l_ref, acc_ref = refs
    for src, dst in zip(side_in[:n_cast], side_out[:n_cast]):
        dst[...] = src[...].astype(BF16)
    if mod_rider:
        c_ref, wada_ref, bada_ref = side_in[n_cast:]
        _ada_kernel(c_ref, wada_ref, bada_ref, side_out[n_cast])
    head = pl.program_id(1)
    tq = ATTN_TILE
    nq = q_ref.shape[0] // tq
    n_low = nq * (nq - 1) // 2
    assert PHASE_B_UNROLL % 2 == 0 and n_low % PHASE_B_UNROLL == 0
    bufs = (s0_ref, s1_ref)

    def rows(t):
        return pl.ds(pl.multiple_of(t * tq, tq), tq)

    def scores(i, j, s_ref):
        s = lax.dot_general(k_ref[rows(j), :], q_ref[rows(i), :], _NT, preferred_element_type=F32)
        if fox:
            cum_k = ck_ref[0, 0, rows(j), :]
            s = s - jnp.concatenate([cum_k] * (tq // LANES), axis=1)
        s_ref[...] = s

    half = tq // 2

    def scores_diagonal(i, s_ref):
        lo, hi = i * tq, i * tq + half
        s_a = lax.dot_general(k_ref[lo:hi, :], q_ref[lo:lo + tq, :], _NT, preferred_element_type=F32)
        s_b = lax.dot_general(k_ref[hi:hi + half, :], q_ref[hi:hi + half, :], _NT,
                              preferred_element_type=F32)
        if fox:
            s_a = s_a - jnp.concatenate([ck_ref[0, 0, lo:hi, :]] * (tq // LANES), axis=1)
            s_b = s_b - jnp.concatenate([ck_ref[0, 0, hi:hi + half, :]] * (half // LANES), axis=1)
        src = lax.broadcasted_iota(jnp.int32, (half, half), 0) // mask_chunk
        dst = lax.broadcasted_iota(jnp.int32, (half, half), 1) // mask_chunk
        s_ref[:half, :half] = jnp.where(src <= dst, s_a[:, :half], NEG_BIG)
        s_ref[:half, half:] = s_a[:, half:]
        s_ref[half:, half:] = jnp.where(src <= dst, s_b, NEG_BIG)

    def accumulate_diagonal(i, s_ref):
        s_a = s_ref[:half, :]
        s_b = s_ref[half:, half:]
        m_a = jnp.max(s_a, axis=0, keepdims=True)
        m_hi = jnp.maximum(m_a[:, half:], jnp.max(s_b, axis=0, keepdims=True))
        m = jnp.concatenate([m_a[:, :half], m_hi], axis=1)
        p_a = jnp.exp2(s_a - m)
        p_b = jnp.exp2(s_b - m_hi)
        l_a = jnp.sum(p_a, axis=0, keepdims=True)
        m_ref[i] = m + cq_ref[0, i, pl.ds(head, 1), :] if fox else m
        l_ref[i] = jnp.concatenate([l_a[:, :half], l_a[:, half:] + jnp.sum(p_b, axis=0, keepdims=True)],
                                   axis=1)
        acc_a = jnp.dot(vt_ref[0, i, :, :half], p_a.astype(BF16), preferred_element_type=F32)
        acc_b = jnp.dot(vt_ref[0, i, :, half:], p_b.astype(BF16), preferred_element_type=F32)
        acc_ref[i, :, :half] = acc_a[:, :half]
        acc_ref[i, :, half:] = acc_a[:, half:] + acc_b

    def accumulate(i, j, s_ref):
        s = s_ref[...]
        m_prev = m_ref[i]
        m_tile = jnp.max(s, axis=0, keepdims=True)
        if fox:
            cum_q = cq_ref[0, i, pl.ds(head, 1), :]
            m_tile = m_tile + cum_q
        m_new = jnp.maximum(m_prev, m_tile)
        alpha = jnp.exp2(m_prev - m_new)
        p = jnp.exp2(s - (m_new - cum_q if fox else m_new))
        l_ref[i] = alpha * l_ref[i] + jnp.sum(p, axis=0, keepdims=True)
        acc_ref[i] = alpha * acc_ref[i] + jnp.dot(vt_ref[0, j], p.astype(BF16),
                                                  preferred_element_type=F32)
        m_ref[i] = m_new

    scores_diagonal(0, bufs[0])
    for i in range(nq):
        if i + 1 < nq:
            scores_diagonal(i + 1, bufs[(i + 1) % 2])
        else:
            scores(1, 0, bufs[(i + 1) % 2])
        accumulate_diagonal(i, bufs[i % 2])

    def advance(i, j):
        wraps = j + 1 == i
        i_nxt = jnp.minimum(jnp.where(wraps, i + 1, i), nq - 1)
        return i_nxt, jnp.where(wraps, 0, j + 1)

    def body(_, carry):
        i, j = carry
        for u in range(PHASE_B_UNROLL):
            i_nxt, j_nxt = advance(i, j)
            scores(i_nxt, j_nxt, bufs[(nq + u + 1) % 2])
            accumulate(i, j, bufs[(nq + u) % 2])
            i, j = i_nxt, j_nxt
        return i, j

    lax.fori_loop(0, n_low // PHASE_B_UNROLL, body, (jnp.int32(1), jnp.int32(0)))

    for i in range(nq):
        o_ref[i * tq:(i + 1) * tq, :] = (acc_ref[i] / l_ref[i]).T.astype(o_ref.dtype)


def _attention(q_arr, k_arr, vt_arr, cum_k=None, cum_q=None, *, batch, seq, heads, dk, dv,
               q_col, k_col, mask_chunk, riders=(), mod_rider=None):
    fox = cum_k is not None
    tq = ATTN_TILE
    nq = seq // tq
    steps = batch * heads
    in_specs = [pl.BlockSpec((seq, dk), lambda b, h: (b, q_col + h)),
                pl.BlockSpec((seq, dk), lambda b, h: (b, k_col + h)),
                pl.BlockSpec((1, nq, dv, tq), lambda b, h: (b, 0, h, 0))]
    args = [q_arr, k_arr, vt_arr]
    if fox:
        in_specs += [pl.BlockSpec((1, 1, seq, LANES), lambda b, h: (b, h, 0, 0)),
                     pl.BlockSpec((1, nq, heads, tq), lambda b, h: (b, 0, 0, 0))]
        args += [cum_k, cum_q]
    slab_specs = [pl.BlockSpec((w.shape[0] // steps, w.shape[1]), lambda b, h: (b * heads + h, 0))
                  for w in riders]
    vmem = (2 * seq * (2 * dk + 2 * dv) * 2 + 2 * seq * LANES * 4 + 2 * seq * heads * 4
            + seq * dv * 4 + 8 * tq * tq * 4 + sum(2 * 6 * w.size // steps for w in riders))
    side_in_specs, side_out_specs, side_args = list(slab_specs), list(slab_specs), list(riders)
    side_out_shape = [jax.ShapeDtypeStruct(w.shape, BF16) for w in riders]
    if mod_rider is not None:
        c_pad, w_ada, b_ada, first_col = mod_rider
        m, d_model = c_pad.shape
        tn = (w_ada.shape[1] - first_col) // steps
        assert first_col % tn == 0
        col = lambda b, h: (0, first_col // tn + b * heads + h)
        side_in_specs += [pl.BlockSpec((m, d_model), lambda b, h: (0, 0)),
                          pl.BlockSpec((d_model, tn), col), pl.BlockSpec((1, tn), col)]
        side_out_specs += [pl.BlockSpec((m, tn), lambda b, h: (0, b * heads + h))]
        side_out_shape += [jax.ShapeDtypeStruct((m, tn * steps), F32)]
        side_args += [c_pad, w_ada, b_ada]
        vmem += 2 * d_model * tn * 4 + d_model * tn * 2
    return pl.pallas_call(
        functools.partial(_attn_kernel, mask_chunk=mask_chunk, fox=fox, n_cast=len(riders),
                          mod_rider=mod_rider is not None),
        grid=(batch, heads),
        in_specs=in_specs + side_in_specs,
        out_specs=[pl.BlockSpec((seq, dv), lambda b, h: (b, h))] + side_out_specs,
        out_shape=[jax.ShapeDtypeStruct((batch * seq, heads * dv), BF16)] + side_out_shape,
        scratch_shapes=[pltpu.VMEM((tq, tq), F32), pltpu.VMEM((tq, tq), F32),
                        pltpu.VMEM((nq, 1, tq), F32), pltpu.VMEM((nq, 1, tq), F32),
                        pltpu.VMEM((nq, dv, tq), F32)],
        compiler_params=_params(vmem),
        name="fox_attn" if fox else "mla_attn",
    )(*args, *side_args)


def _outproj_kernel(x_ref, oa_ref, ob_ref, wa_ref, wb_ref, gpost_ref, gt_ref, o_ref):
    y = jnp.dot(oa_ref[...], wa_ref[...], preferred_element_type=F32)
    y = y + jnp.dot(ob_ref[...], wb_ref[...], preferred_element_type=F32)
    o_ref[...] = x_ref[...] + gt_ref[0] * _rms(y, gpost_ref[...])


def _outproj(x, o_a, o_b, w, g_post, gt, *, seq, tm):
    t, d = x.shape
    ka, kb = o_a.shape[1], o_b.shape[1]
    assert ka == kb and w.shape == (ka + kb, d)
    tpb = seq // tm
    vmem = 4 * tm * d * 4 + 2 * tm * (ka + kb) * 2 + 2 * (ka + kb) * d * 2 + 2 * tm * d * 4
    return pl.pallas_call(
        _outproj_kernel,
        grid=(t // tm,),
        in_specs=[pl.BlockSpec((tm, d), lambda i: (i, 0)),
                  pl.BlockSpec((tm, ka), lambda i: (i, 0)),
                  pl.BlockSpec((tm, kb), lambda i: (i, 0)),
                  pl.BlockSpec((ka, d), lambda i: (0, 0)),
                  pl.BlockSpec((kb, d), lambda i: (1, 0)),
                  pl.BlockSpec((1, d), lambda i: (0, 0)),
                  pl.BlockSpec((1, 1, d), lambda i: (i // tpb, 0, 0))],
        out_specs=pl.BlockSpec((tm, d), lambda i: (i, 0)),
        out_shape=jax.ShapeDtypeStruct((t, d), F32),
        compiler_params=_params(vmem),
        name="out_proj",
    )(x, o_a, o_b, w, w, g_post, gt)


def _rope_tables(seq):
    half = MLA_ROPE // 2
    inv = ROPE_THETA ** (-np.arange(half, dtype=np.float64) / half)
    ang = np.arange(seq, dtype=np.float64)[:, None] * inv[None, :]
    cos, sin = np.cos(ang), np.sin(ang)
    zero = np.zeros_like(cos)
    pad = np.zeros((seq, LANES - 2 * half))
    return tuple(jnp.asarray(np.concatenate(parts, axis=1), F32)
                 for parts in ([cos, cos, pad], [-sin, zero, pad], [zero, sin, pad]))


def kernel(x, c, w_ada, b_ada, g_ffn1_pre, g_ffn1_post, w1_gate, w1_up, w1_down, g_mix_pre,
           g_mix_post, w_in, b_forget, g_q_a, w_uq, g_kv_a, w_ukv, w_o, g_ffn2_pre, g_ffn2_post,
           w2_gate, w2_up, w2_down):
    batch, seq, d = x.shape
    depth = w_ada.shape[0]
    t = batch * seq
    xt = x.reshape(t, d)
    c_pad = jnp.pad(c, ((0, 8 - batch), (0, 0)))
    cos_t, sin_lo, sin_hi = _rope_tables(seq)

    for l in range(depth):
        early = 5
        chunks = lambda mod: [mod[:batch, n * d:(n + 1) * d].reshape(batch, 1, d)
                              for n in range(mod.shape[1] // d)]
        sh1, sc1, gt1, sh2, sc2 = chunks(_ada(c_pad, w_ada[l], b_ada[l:l + 1], tn=1024, n=early * d))

        ffn1 = functools.partial(_ffn, xt, sh1, sc1, gt1, g_ffn1_pre[l:l + 1], g_ffn1_post[l:l + 1],
                                 seq=seq, tm=1024, res_weight=0.5)
        y_head, w1g, w1u, w1d = ffn1(w1_gate[l], w1_up[l], w1_down[l], tf=256, mode="head")
        xt = lax.dynamic_update_slice(ffn1(w1g, w1u, w1d, tf=512, mode="tail"), y_head, (0, 0))

        wi = w_in[l]
        fox0 = KPE_COL + MLA_ROPE
        fl0 = fox0 + 3 * FOX_WIDTH
        w_lat = jnp.concatenate(
            [wi[:, :fox0], jnp.zeros((d, LANES - MLA_ROPE), F32),
             wi[:, fl0:], jnp.zeros((d, LANES - FOX_HEADS), F32)], axis=1).astype(BF16)
        w_fox = wi[:, fox0:fl0].astype(BF16)

        wuq_p = jnp.pad(w_uq[l].reshape(MLA_Q_RANK, MLA_HEADS, MLA_NOPE + MLA_ROPE),
                        ((0, 0), (0, 0), (0, MLA_QK_PAD - MLA_NOPE - MLA_ROPE))
                        ).reshape(MLA_Q_RANK, MLA_HEADS * MLA_QK_PAD).astype(BF16)
        wukv = w_ukv[l].reshape(MLA_KV_RANK, MLA_HEADS, MLA_NOPE + MLA_V)
        wuk = wukv[:, :, :MLA_NOPE].reshape(MLA_KV_RANK, -1).astype(BF16)
        wuvt = wukv[:, :, MLA_NOPE:].reshape(MLA_KV_RANK, -1).T.astype(BF16)
        bf_pad = jnp.pad(b_forget[l:l + 1], ((0, 0), (0, LANES - FOX_HEADS)))
        fqk, fvt, q_mla, k_mla, vt_mla, cum_k, cum_q = _mixer_in(
            xt, sh2, sc2, g_mix_pre[l:l + 1], w_lat, w_fox, g_q_a[l:l + 1], g_kv_a[l:l + 1],
            wuq_p, wuk, wuvt, bf_pad, cos_t, sin_lo, sin_hi, batch=batch, seq=seq)

        o_mla, w2g, w2u = _attention(q_mla, k_mla, vt_mla, batch=batch, seq=seq, heads=MLA_HEADS,
                                     dk=MLA_QK_PAD, dv=MLA_V, q_col=0, k_col=0, mask_chunk=CHUNK,
                                     riders=(w2_gate[l], w2_up[l]))
        o_fox, w2d, wo, mod_late = _attention(
            fqk, fqk, fvt, cum_k, cum_q, batch=batch, seq=seq, heads=FOX_HEADS, dk=FOX_DIM, dv=FOX_DIM,
            q_col=0, k_col=FOX_HEADS, mask_chunk=1, riders=(w2_down[l], w_o[l]),
            mod_rider=(c_pad, w_ada[l], b_ada[l:l + 1], early * d))
        gt2, sh3, sc3, gt3 = chunks(mod_late)

        xt = _outproj(xt, o_mla, o_fox, wo, g_mix_post[l:l + 1], gt2, seq=seq, tm=512)

        xt = _ffn(xt, sh3, sc3, gt3, g_ffn2_pre[l:l + 1], g_ffn2_post[l:l + 1],
                  w2g, w2u, w2d, seq=seq, tm=1024, tf=512, res_weight=0.5)

    return xt.reshape(batch, seq, d)
```
